```python
import math
import jax, jax.numpy as jnp
from jax import lax
import numpy as np

D_MODEL = 1024
BATCH = 4
SEQ = 8192
DEPTH = 4

CHUNK = 64
QBLOCK = 128
D_FF = 4 * D_MODEL
EPS = 1e-6

A_HEADS = 8
A_KV_HEADS = 2
A_HEAD_DIM = 64
A_WINDOW = 128
A_WINDOW_CHUNKS = A_WINDOW // CHUNK
A_PREV_BLOCKS = -(-(A_WINDOW_CHUNKS * CHUNK) // QBLOCK)
B_HEADS = 8
B_Q_RANK = 256
B_KV_RANK = 128
B_NOPE_DIM = 64
B_ROPE_DIM = 32
B_V_DIM = 64
ROPE_THETA = 10000.0
C_HEADS = 8
C_HEAD_DIM = 64
D_HEADS = 4
D_HEAD_DIM = 64
D_V_DIM = 2 * D_HEAD_DIM

EVEN_SPLITS = [A_HEADS * A_HEAD_DIM, A_KV_HEADS * A_HEAD_DIM, A_KV_HEADS * A_HEAD_DIM,
               B_Q_RANK, B_KV_RANK, B_ROPE_DIM]
EVEN_IN = sum(EVEN_SPLITS)
EVEN_MIX = A_HEADS * A_HEAD_DIM + B_HEADS * B_V_DIM
ODD_SPLITS = [C_HEADS * C_HEAD_DIM] * 3 + [D_HEADS * 2 * D_HEAD_DIM] * 2 + [D_HEADS * D_V_DIM]
ODD_IN = sum(ODD_SPLITS)
ODD_MIX = C_HEADS * C_HEAD_DIM + D_HEADS * D_V_DIM
N_EVEN = (DEPTH + 1) // 2
N_ODD = DEPTH // 2

kernel_name = "hybrid_chunk_causal_encoder"


def rms_norm(x, g):
    xf = x.astype(jnp.float32)
    y = xf * lax.rsqrt(jnp.mean(xf * xf, axis=-1, keepdims=True) + EPS)
    return (y * g.astype(jnp.float32)).astype(x.dtype)


def head_rms(x, g):
    xf = x.astype(jnp.float32)
    return xf * lax.rsqrt(jnp.mean(xf * xf, axis=-1, keepdims=True) + EPS) * g.astype(jnp.float32)


def alibi_slopes(n):
    return 2.0 ** (-8.0 * jnp.arange(1, n + 1, dtype=jnp.float32) / n)


def apply_rope(x, positions):
    half = x.shape[-1] // 2
    inv = ROPE_THETA ** (-jnp.arange(half, dtype=jnp.float32) / half)
    ang = positions.astype(jnp.float32)[:, :, None, None] * inv
    cos, sin = jnp.cos(ang), jnp.sin(ang)
    x1, x2 = x[..., :half], x[..., half:]
    return jnp.concatenate([x1 * cos - x2 * sin, x1 * sin + x2 * cos], axis=-1)


def sweep_query_blocks(block_fn, *q_arrays):
    B, S = q_arrays[0].shape[:2]
    nb = S // QBLOCK
    xs = tuple(jnp.swapaxes(a.reshape((B, nb, QBLOCK) + a.shape[2:]), 0, 1) for a in q_arrays)
    out = lax.map(lambda t: block_fn(t[0], *t[1:]), (jnp.arange(nb),) + xs)
    return jnp.swapaxes(out, 0, 1).reshape((B, S) + out.shape[3:])


def banded_sink_attention(q, k, v, sinks):
    B, S, H, Dh = q.shape
    G = k.shape[2]
    R = H // G
    nb = S // QBLOCK
    qb = q.reshape(B, nb, QBLOCK, G, R, Dh)

    def band(t):
        tb = t.reshape(B, nb, QBLOCK, G, Dh)
        prev = [jnp.pad(tb, ((0, 0), (p, 0), (0, 0), (0, 0), (0, 0)))[:, :nb]
                for p in range(A_PREV_BLOCKS, 0, -1)]
        return jnp.concatenate(prev + [tb], axis=2)

    kk, vv = band(k), band(v)
    nk = kk.shape[2]
    blk = jnp.arange(nb)[:, None]
    qpos = blk * QBLOCK + jnp.arange(QBLOCK)[None]
    kpos = (blk - A_PREV_BLOCKS) * QBLOCK + jnp.arange(nk)[None]
    dchunk = qpos[:, :, None] // CHUNK - kpos[:, None, :] // CHUNK
    allowed = (kpos[:, None, :] >= 0) & (dchunk >= 0) & (dchunk <= A_WINDOW_CHUNKS)
    dist = jnp.abs(qpos[:, :, None] - kpos[:, None, :]).astype(jnp.float32)
    slopes = alibi_slopes(H).reshape(G, R)
    s = jnp.einsum('bnqgrd,bnkgd->bngrqk', qb, kk) * (Dh ** -0.5)
    s = s - slopes[None, None, :, :, None, None] * dist[None, :, None, None]
    s = jnp.where(allowed[None, :, None, None], s, -jnp.inf)
    sink = sinks.astype(jnp.float32).reshape(1, 1, G, R, 1, 1)
    m = jnp.maximum(jnp.max(s, axis=-1, keepdims=True), sink)
    p = jnp.exp(s - m)
    w = p / (jnp.sum(p, axis=-1, keepdims=True) + jnp.exp(sink - m))
    o = jnp.einsum('bngrqk,bnkgd->bnqgrd', w, vv)
    return o.reshape(B, S, H, Dh)


def chunk_causal_softmax_attention(q, k, v):
    S = q.shape[1]
    scale = q.shape[-1] ** -0.5
    kchunk = jnp.arange(S) // CHUNK

    def block(i, qb):
        qchunk = (i * QBLOCK + jnp.arange(QBLOCK)) // CHUNK
        allowed = kchunk[None, :] <= qchunk[:, None]
        s = jnp.einsum('bqhd,bkhd->bhqk', qb, k) * scale
        p = jax.nn.softmax(jnp.where(allowed, s, -jnp.inf), axis=-1)
        return jnp.einsum('bhqk,bkhd->bqhd', p, v)

    return sweep_query_blocks(block, q)


def stick_breaking_attention(q, k, v):
    S, Dh = q.shape[1], q.shape[-1]
    kpos = jnp.arange(S)

    def block(i, qb):
        qpos = i * QBLOCK + jnp.arange(QBLOCK)
        strict = kpos[None, :] < qpos[:, None]
        z = jnp.einsum('bqhd,bkhd->bhqk', qb, k) * (Dh ** -0.5)
        log1m = jnp.where(strict, -jax.nn.softplus(z), 0.0)
        between = lax.cumsum(log1m, axis=3, reverse=True) - log1m
        a = jnp.where(strict, jnp.exp(jax.nn.log_sigmoid(z) + between), 0.0)
        return jnp.einsum('bhqk,bkhd->bqhd', a, v)

    return sweep_query_blocks(block, q)


def differential_attention(q1, q2, k1, k2, v, lam):
    S, H, Dh = q1.shape[1], q1.shape[2], q1.shape[-1]
    scale = Dh ** -0.5
    kpos = jnp.arange(S)
    slopes = alibi_slopes(H)

    def block(i, q1b, q2b):
        qpos = i * QBLOCK + jnp.arange(QBLOCK)
        allowed = (kpos[None, :] // CHUNK) <= (qpos[:, None] // CHUNK)
        dist = jnp.abs(qpos[:, None] - kpos[None, :]).astype(jnp.float32)
        bias = -slopes[:, None, None] * dist[None]
        s1 = jnp.einsum('bqhd,bkhd->bhqk', q1b, k1) * scale + bias
        s2 = jnp.einsum('bqhd,bkhd->bhqk', q2b, k2) * scale + bias
        p1 = jax.nn.softmax(jnp.where(allowed, s1, -jnp.inf), axis=-1)
        p2 = jax.nn.softmax(jnp.where(allowed, s2, -jnp.inf), axis=-1)
        return jnp.einsum('bhqk,bkhd->bqhd', p1 - lam * p2, v)

    return sweep_query_blocks(block, q1, q2)


def even_mixer(h, positions, w_in, w_out, a_qg, a_kg, a_sinks, b_cqg, b_ckvg, b_w_uq, b_w_ukv, b_qg, b_kg):
    B, S, _ = h.shape
    proj = h @ w_in
    aq, ak, av, cq, ckv, krope = jnp.split(proj, np.cumsum(EVEN_SPLITS)[:-1].tolist(), axis=-1)
    qa = head_rms(aq.reshape(B, S, A_HEADS, A_HEAD_DIM), a_qg)
    ka = head_rms(ak.reshape(B, S, A_KV_HEADS, A_HEAD_DIM), a_kg)
    va = av.reshape(B, S, A_KV_HEADS, A_HEAD_DIM).astype(jnp.float32)
    oa = banded_sink_attention(qa, ka, va, a_sinks)
    cq = rms_norm(cq, b_cqg)
    ckv = rms_norm(ckv, b_ckvg)
    qb = (cq @ b_w_uq).reshape(B, S, B_HEADS, B_NOPE_DIM + B_ROPE_DIM)
    kv = (ckv @ b_w_ukv).reshape(B, S, B_HEADS, B_NOPE_DIM + B_V_DIM)
    kb = jnp.concatenate([kv[..., :B_NOPE_DIM],
                          jnp.broadcast_to(krope[:, :, None, :], (B, S, B_HEADS, B_ROPE_DIM))], axis=-1)
    vb = kv[..., B_NOPE_DIM:].astype(jnp.float32)
    qb = head_rms(qb, b_qg)
    kb = head_rms(kb, b_kg)
    qb = jnp.concatenate([qb[..., :B_NOPE_DIM], apply_rope(qb[..., B_NOPE_DIM:], positions)], axis=-1)
    kb = jnp.concatenate([kb[..., :B_NOPE_DIM], apply_rope(kb[..., B_NOPE_DIM:], positions)], axis=-1)
    ob = chunk_causal_softmax_attention(qb, kb, vb)
    mix = jnp.concatenate([oa.reshape(B, S, -1), ob.reshape(B, S, -1)], axis=-1).astype(h.dtype)
    return mix @ w_out


def odd_mixer(h, w_in, w_out, d_qg, d_kg, d_lam, d_subln, lambda_init):
    B, S, _ = h.shape
    proj = h @ w_in
    cq, ck, cv, dq, dk, dv = jnp.split(proj, np.cumsum(ODD_SPLITS)[:-1].tolist(), axis=-1)
    f32 = jnp.float32
    oc = stick_breaking_attention(cq.reshape(B, S, C_HEADS, C_HEAD_DIM).astype(f32),
                                  ck.reshape(B, S, C_HEADS, C_HEAD_DIM).astype(f32),
                                  cv.reshape(B, S, C_HEADS, C_HEAD_DIM).astype(f32))
    q = head_rms(dq.reshape(B, S, D_HEADS, 2, D_HEAD_DIM), d_qg)
    k = head_rms(dk.reshape(B, S, D_HEADS, 2, D_HEAD_DIM), d_kg)
    v = dv.reshape(B, S, D_HEADS, D_V_DIM).astype(f32)
    lf = d_lam.astype(f32)
    lam = jnp.exp(jnp.sum(lf[0] * lf[1])) - jnp.exp(jnp.sum(lf[2] * lf[3])) + lambda_init
    od = differential_attention(q[:, :, :, 0], q[:, :, :, 1], k[:, :, :, 0], k[:, :, :, 1], v, lam)
    od = head_rms(od, d_subln) * (1.0 - lambda_init)
    mix = jnp.concatenate([oc.reshape(B, S, -1), od.reshape(B, S, -1)], axis=-1).astype(h.dtype)
    return mix @ w_out


def squared_relu_mlp(h, w_up, w_down):
    return jnp.square(jax.nn.relu(h @ w_up)) @ w_down


def setup_inputs(seed: int = 0) -> dict:
    key = jax.random.key(seed)
    ks = jax.random.split(key, 23)
    f32 = jnp.float32

    def nrm(i, shape, scale):
        return jax.random.normal(ks[i], shape, f32) * scale

    def gain(i, shape):
        return 1.0 + 0.1 * jax.random.normal(ks[i], shape, f32)

    x = jax.random.normal(ks[0], (BATCH, SEQ, D_MODEL), f32)
    offset = jax.random.randint(ks[1], (BATCH, 1), 0, 4096, dtype=jnp.int32)
    positions = jnp.arange(SEQ, dtype=jnp.int32)[None, :] + offset
    return {
        "x": x,
        "positions": positions,
        "norm_mix_g": gain(2, (DEPTH, D_MODEL)),
        "norm_ffn_g": gain(3, (DEPTH, D_MODEL)),
        "mlp_w_up": nrm(4, (DEPTH, D_MODEL, D_FF), D_MODEL ** -0.5),
        "mlp_w_down": nrm(5, (DEPTH, D_FF, D_MODEL), D_FF ** -0.5),
        "ev_w_in": nrm(6, (N_EVEN, D_MODEL, EVEN_IN), D_MODEL ** -0.5),
        "ev_w_out": nrm(7, (N_EVEN, EVEN_MIX, D_MODEL), EVEN_MIX ** -0.5),
        "a_q_norm": gain(8, (N_EVEN, A_HEAD_DIM)),
        "a_k_norm": gain(9, (N_EVEN, A_HEAD_DIM)),
        "a_sinks": nrm(10, (N_EVEN, A_HEADS), 0.5),
        "b_cq_norm": gain(11, (N_EVEN, B_Q_RANK)),
        "b_ckv_norm": gain(12, (N_EVEN, B_KV_RANK)),
        "b_w_uq": nrm(13, (N_EVEN, B_Q_RANK, B_HEADS * (B_NOPE_DIM + B_ROPE_DIM)), B_Q_RANK ** -0.5),
        "b_w_ukv": nrm(14, (N_EVEN, B_KV_RANK, B_HEADS * (B_NOPE_DIM + B_V_DIM)), B_KV_RANK ** -0.5),
        "b_q_norm": gain(15, (N_EVEN, B_NOPE_DIM + B_ROPE_DIM)),
        "b_k_norm": gain(16, (N_EVEN, B_NOPE_DIM + B_ROPE_DIM)),
        "od_w_in": nrm(17, (N_ODD, D_MODEL, ODD_IN), D_MODEL ** -0.5),
        "od_w_out": nrm(18, (N_ODD, ODD_MIX, D_MODEL), ODD_MIX ** -0.5),
        "d_q_norm": gain(19, (N_ODD, 2, D_HEAD_DIM)),
        "d_k_norm": gain(20, (N_ODD, 2, D_HEAD_DIM)),
        "d_lambda": nrm(21, (N_ODD, 4, D_HEAD_DIM), 0.1),
        "d_subln": gain(22, (N_ODD, D_V_DIM)),
    }


def reference(x, positions, norm_mix_g, norm_ffn_g, mlp_w_up, mlp_w_down, ev_w_in, ev_w_out,
              a_q_norm, a_k_norm, a_sinks, b_cq_norm, b_ckv_norm, b_w_uq, b_w_ukv, b_q_norm, b_k_norm,
              od_w_in, od_w_out, d_q_norm, d_k_norm, d_lambda, d_subln):
    for layer in range(DEPTH):
        j = layer // 2
        h = rms_norm(x, norm_mix_g[layer])
        if layer % 2 == 0:
            mix = even_mixer(h, positions, ev_w_in[j], ev_w_out[j], a_q_norm[j], a_k_norm[j], a_sinks[j],
                             b_cq_norm[j], b_ckv_norm[j], b_w_uq[j], b_w_ukv[j], b_q_norm[j], b_k_norm[j])
        else:
            lambda_init = 0.8 - 0.6 * math.exp(-0.3 * layer)
            mix = odd_mixer(h, od_w_in[j], od_w_out[j], d_q_norm[j], d_k_norm[j], d_lambda[j], d_subln[j],
                            lambda_init)
        x = x + mix
        h = rms_norm(x, norm_ffn_g[layer])
        x = x + squared_relu_mlp(h, mlp_w_up[layer], mlp_w_down[layer])
    return x
```

```python
import functools
import math

import jax
import jax.numpy as jnp
from jax import lax
from jax.experimental import pallas as pl
from jax.experimental.pallas import tpu as pltpu

F32 = jnp.float32
BF16 = jnp.bfloat16

EPS = 1e-6
CHUNK = 64
CHUNK_SHIFT = 6
LANES = 128
HEAD = 64
A_HEADS = 8
A_GROUP = 4
A_WINDOW = 128
A_WINDOW_CHUNKS = A_WINDOW // CHUNK
B_HEADS = 8
B_QK = 96
B_ROPE_HALF = 16
ROPE_THETA = 10000.0
C_HEADS = 8
D_HEADS = 4
NEG = -1e30
VMEM_LIMIT = 56 * 1024 * 1024


def _dot(a, b):
    return jnp.dot(a, b, preferred_element_type=F32)


def _dot_t(a, b):
    return lax.dot_general(a, b, (((1,), (1,)), ((), ())), preferred_element_type=F32)


def _rms(x, denom):
    return x * lax.rsqrt(jnp.sum(x * x, axis=-1, keepdims=True) * (1.0 / denom) + EPS)


def _pair_rms(x, lo):
    xx = x * x
    s_lo = jnp.sum(jnp.where(lo, xx, 0.0), axis=-1, keepdims=True)
    s_hi = jnp.sum(jnp.where(lo, 0.0, xx), axis=-1, keepdims=True)
    r = jnp.where(lo, lax.rsqrt(s_lo * (1.0 / HEAD) + EPS), lax.rsqrt(s_hi * (1.0 / HEAD) + EPS))
    return x * r


def _lane_lo(shape):
    return lax.broadcasted_iota(jnp.int32, shape, len(shape) - 1) < HEAD


def _even_proj_kernel(x_ref, pos_ref, g_ref, w_in_ref, aqg_ref, akg_ref, cqg_ref, ckvg_ref,
                      wuq_ref, wukv_ref, bqg_ref, bkg_ref, inv_ref,
                      qa_ref, ka_ref, va_ref, qb_ref, kb_ref, vb_ref):
    x = x_ref[...]
    h = _rms(x, x.shape[-1]) * g_ref[...]
    proj = _dot(h.astype(BF16), w_in_ref[...])
    lo = _lane_lo((1, LANES))
    for p in range(A_HEADS // 2):
        seg = proj[:, p * LANES:(p + 1) * LANES]
        qa_ref[:, p * LANES:(p + 1) * LANES] = (_pair_rms(seg, lo) * aqg_ref[...] * 0.125).astype(BF16)
    ka_ref[...] = (_pair_rms(proj[:, 512:640], lo) * akg_ref[...]).astype(BF16)
    va_ref[...] = proj[:, 640:768].astype(BF16)
    cq = _rms(proj[:, 768:1024], 256) * cqg_ref[...]
    ckv = _rms(proj[:, 1024:1152], 128) * ckvg_ref[...]
    krope = proj[:, 1152:1280]
    qall = _dot(cq.astype(BF16), wuq_ref[...])
    kvall = _dot(ckv.astype(BF16), wukv_ref[...])
    ang = pos_ref[...].astype(F32) * inv_ref[...]
    cosf = jnp.cos(ang)
    sinf = jnp.sin(ang)
    lane = lax.broadcasted_iota(jnp.int32, (1, LANES), 1)
    s_first = jnp.where((lane >= HEAD) & (lane < HEAD + B_ROPE_HALF), -sinf, 0.0)
    s_second = jnp.where((lane >= HEAD + B_ROPE_HALF) & (lane < B_QK), sinf, 0.0)

    def rope(t):
        return (t * cosf + pltpu.roll(t, LANES - B_ROPE_HALF, 1) * s_first
                + pltpu.roll(t, B_ROPE_HALF, 1) * s_second)

    scale_b = B_QK ** -0.5
    for p in range(B_HEADS // 2):
        vpair = []
        for hh in (2 * p, 2 * p + 1):
            qh = qall[:, hh * LANES:(hh + 1) * LANES]
            qn = _rms(qh, B_QK) * bqg_ref[...]
            qb_ref[:, hh * LANES:(hh + 1) * LANES] = (rope(qn) * scale_b).astype(BF16)
            kvh = kvall[:, hh * LANES:(hh + 1) * LANES]
            kpre = jnp.where(lo, kvh, krope)
            kn = _rms(kpre, B_QK) * bkg_ref[...]
            kb_ref[:, hh * LANES:(hh + 1) * LANES] = rope(kn).astype(BF16)
            vpair.append(kvh)
        vb_ref[:, p * LANES:(p + 1) * LANES] = jnp.where(
            lo, pltpu.roll(vpair[0], HEAD, 1), vpair[1]).astype(BF16)


def _even_proj(x, pos, g, w_in, aqg, akg, cqg, ckvg, wuq, wukv, bqg, bkg, inv, tm):
    T, D = x.shape
    full = lambda a: pl.BlockSpec(a.shape, lambda i: (0,) * a.ndim)
    row = lambda c: pl.BlockSpec((tm, c), lambda i: (i, 0))
    outs = (512, 128, 128, 1024, 1024, 512)
    return pl.pallas_call(
        _even_proj_kernel,
        grid=(T // tm,),
        in_specs=[row(D), row(1)] + [full(a) for a in (g, w_in, aqg, akg, cqg, ckvg, wuq, wukv, bqg, bkg, inv)],
        out_specs=[row(c) for c in outs],
        out_shape=[jax.ShapeDtypeStruct((T, c), BF16) for c in outs],
        compiler_params=pltpu.CompilerParams(dimension_semantics=("arbitrary",), vmem_limit_bytes=VMEM_LIMIT),
        name="even_proj",
    )(x, pos, g, w_in, aqg, akg, cqg, ckvg, wuq, wukv, bqg, bkg, inv)


def _odd_proj_kernel(x_ref, g_ref, w_in_ref, dqg_ref, dkg_ref,
                     cq_ref, ck_ref, cv_ref, dq_ref, dk_ref, dv_ref):
    x = x_ref[...]
    h = _rms(x, x.shape[-1]) * g_ref[...]
    proj = _dot(h.astype(BF16), w_in_ref[...])
    lo = _lane_lo((1, LANES))
    cq_ref[...] = (proj[:, 0:512] * 0.125).astype(BF16)
    ck_ref[...] = proj[:, 512:1024].astype(BF16)
    cv_ref[...] = proj[:, 1024:1536].astype(BF16)
    for hh in range(D_HEADS):
        sl = slice(hh * LANES, (hh + 1) * LANES)
        dq_ref[:, sl] = (_pair_rms(proj[:, 1536 + hh * LANES:1536 + (hh + 1) * LANES], lo)
                         * dqg_ref[...] * 0.125).astype(BF16)
        dk_ref[:, sl] = (_pair_rms(proj[:, 2048 + hh * LANES:2048 + (hh + 1) * LANES], lo)
                         * dkg_ref[...]).astype(BF16)
    dv_ref[...] = proj[:, 2560:3072].astype(BF16)


def _odd_proj(x, g, w_in, dqg, dkg, tm):
    T, D = x.shape
    full = lambda a: pl.BlockSpec(a.shape, lambda i: (0,) * a.ndim)
    row = lambda c: pl.BlockSpec((tm, c), lambda i: (i, 0))
    return pl.pallas_call(
        _odd_proj_kernel,
        grid=(T // tm,),
        in_specs=[row(D)] + [full(a) for a in (g, w_in, dqg, dkg)],
        out_specs=[row(512)] * 6,
        out_shape=[jax.ShapeDtypeStruct((T, 512), BF16)] * 6,
        compiler_params=pltpu.CompilerParams(dimension_semantics=("arbitrary",), vmem_limit_bytes=VMEM_LIMIT),
        name="odd_proj",
    )(x, g, w_in, dqg, dkg)


def _attn_a_kernel(slope_ref, sink_ref, q_ref, k_ref, v_ref, o_ref, *, tq, win):
    p = pl.program_id(1)
    qi = pl.program_id(2)
    q0 = qi * tq
    ks = pl.multiple_of(jnp.maximum(q0 - A_WINDOW, 0), A_WINDOW)
    grp = p // (A_GROUP // 2)
    lane_half = lax.broadcasted_iota(jnp.int32, (1, LANES), 1) // HEAD
    in_grp = lane_half == grp
    k = k_ref[0, pl.ds(ks, win), :]
    v = v_ref[0, pl.ds(ks, win), :]
    qq = q_ref[0].astype(F32)
    qsw = pltpu.roll(qq, HEAD, 1)
    qpos = q0 + lax.broadcasted_iota(jnp.int32, (tq, win), 0)
    kpos = ks + lax.broadcasted_iota(jnp.int32, (tq, win), 1)
    dch = (qpos >> CHUNK_SHIFT) - (kpos >> CHUNK_SHIFT)
    allowed = (dch >= 0) & (dch <= A_WINDOW_CHUNKS)
    dist = jnp.abs(qpos - kpos).astype(F32)
    outs = []
    for e in range(2):
        hidx = 2 * p + e
        qh = jnp.where(in_grp, jnp.where(grp == e, qq, qsw), 0.0).astype(BF16)
        s = _dot_t(qh, k) - slope_ref[hidx] * dist
        s = jnp.where(allowed, s, NEG)
        sink = sink_ref[hidx]
        m = jnp.maximum(jnp.max(s, axis=-1, keepdims=True), sink)
        pr = jnp.exp(s - m)
        den = jnp.sum(pr, axis=-1, keepdims=True) + jnp.exp(sink - m)
        o = _dot(pr.astype(BF16), v) / den
        outs.append(jnp.where(grp == e, o, pltpu.roll(o, HEAD, 1)))
    o_ref[0] = jnp.where(lane_half == 0, outs[0], outs[1]).astype(BF16)


def _attn_a(slopes, sinks, qa, ka, va, tq):
    B, S, _ = qa.shape
    win = tq + A_WINDOW
    smem = pl.BlockSpec(memory_space=pltpu.SMEM)
    return pl.pallas_call(
        functools.partial(_attn_a_kernel, tq=tq, win=win),
        grid=(B, A_HEADS // 2, S // tq),
        in_specs=[smem, smem,
                  pl.BlockSpec((1, tq, LANES), lambda b, p, i: (b, i, p)),
                  pl.BlockSpec((1, S, LANES), lambda b, p, i: (b, 0, 0)),
                  pl.BlockSpec((1, S, LANES), lambda b, p, i: (b, 0, 0))],
        out_specs=pl.BlockSpec((1, tq, LANES), lambda b, p, i: (b, i, p)),
        out_shape=jax.ShapeDtypeStruct((B, S, 512), BF16),
        compiler_params=pltpu.CompilerParams(dimension_semantics=("arbitrary",) * 3, vmem_limit_bytes=VMEM_LIMIT),
        name="attn_a",
    )(slopes, sinks, qa, ka, va)


def _attn_b_kernel(q_ref, k_ref, v_ref, o_ref, m_sc, l_sc, acc_sc, *, t):
    qi = pl.program_id(2)
    row = lax.broadcasted_iota(jnp.int32, (t, t), 0)
    col = lax.broadcasted_iota(jnp.int32, (t, t), 1)
    allowed = (col >> CHUNK_SHIFT) <= (row >> CHUNK_SHIFT)
    for e in range(2):
        q = q_ref[0, :, e * LANES:(e + 1) * LANES]
        m_sc[e] = jnp.full((t, 1), NEG, F32)
        l_sc[e] = jnp.zeros((t, 1), F32)
        acc_sc[e] = jnp.zeros((t, LANES), F32)

        def step(j, diag, e=e, q=q):
            k = k_ref[0, pl.ds(pl.multiple_of(j * t, t), t), e * LANES:(e + 1) * LANES]
            v = v_ref[0, pl.ds(pl.multiple_of(j * t, t), t), :]
            s = _dot_t(q, k)
            if diag:
                s = jnp.where(allowed, s, NEG)
            m_prev = m_sc[e]
            m_new = jnp.maximum(m_prev, jnp.max(s, axis=-1, keepdims=True))
            alpha = jnp.exp(m_prev - m_new)
            pr = jnp.exp(s - m_new)
            l_sc[e] = alpha * l_sc[e] + jnp.sum(pr, axis=-1, keepdims=True)
            acc_sc[e] = alpha * acc_sc[e] + _dot(pr.astype(BF16), v)
            m_sc[e] = m_new

        def body(j, c):
            step(j, False)
            return c

        lax.fori_loop(0, qi, body, 0)
        step(qi, True)
    o_lo = acc_sc[0] / l_sc[0]
    o_hi = acc_sc[1] / l_sc[1]
    o_ref[0] = jnp.where(_lane_lo((1, LANES)), o_lo, o_hi).astype(BF16)


def _attn_b(qb, kb, vb, t):
    B, S, _ = qb.shape
    return pl.pallas_call(
        functools.partial(_attn_b_kernel, t=t),
        grid=(B, B_HEADS // 2, S // t),
        in_specs=[pl.BlockSpec((1, t, 2 * LANES), lambda b, p, i: (b, i, p)),
                  pl.BlockSpec((1, S, 2 * LANES), lambda b, p, i: (b, 0, p)),
                  pl.BlockSpec((1, S, LANES), lambda b, p, i: (b, 0, p))],
        out_specs=pl.BlockSpec((1, t, LANES), lambda b, p, i: (b, i, p)),
        out_shape=jax.ShapeDtypeStruct((B, S, 512), BF16),
        scratch_shapes=[pltpu.VMEM((2, t, 1), F32), pltpu.VMEM((2, t, 1), F32), pltpu.VMEM((2, t, LANES), F32)],
        compiler_params=pltpu.CompilerParams(dimension_semantics=("arbitrary",) * 3, vmem_limit_bytes=VMEM_LIMIT),
        name="attn_b",
    )(qb, kb, vb)


def _attn_c_kernel(q_ref, k_ref, v_ref, o_ref, r_sc, acc_sc, *, t):
    qi = pl.program_id(2)
    row = lax.broadcasted_iota(jnp.int32, (t, t), 0)
    col = lax.broadcasted_iota(jnp.int32, (t, t), 1)
    strict = col < row
    upper = jnp.where(row > col, 1.0, 0.0).astype(BF16)
    lane_lo = _lane_lo((1, LANES))
    qq = q_ref[0]
    for e in range(2):
        q = jnp.where(lane_lo == (e == 0), qq, jnp.zeros_like(qq))
        r_sc[e] = jnp.zeros((t, 1), F32)
        acc_sc[e] = jnp.zeros((t, LANES), F32)

        def step(j, diag, e=e, q=q):
            k = k_ref[0, pl.ds(pl.multiple_of(j * t, t), t), :]
            v = v_ref[0, pl.ds(pl.multiple_of(j * t, t), t), :]
            z = _dot_t(q, k)
            sp = jnp.maximum(z, 0.0) + jnp.log(1.0 + jnp.exp(-jnp.abs(z)))
            spm = jnp.where(strict, sp, 0.0) if diag else sp
            hi = spm.astype(BF16)
            lo = (spm - hi.astype(F32)).astype(BF16)
            suffix = _dot(hi, upper) + _dot(lo, upper)
            a = jnp.exp(z - sp - suffix - r_sc[e])
            if diag:
                a = jnp.where(strict, a, 0.0)
            acc_sc[e] = acc_sc[e] + _dot(a.astype(BF16), v)
            r_sc[e] = r_sc[e] + jnp.sum(spm, axis=-1, keepdims=True)

        step(qi, True)

        def body(i, c):
            step(qi - 1 - i, False)
            return c

        lax.fori_loop(0, qi, body, 0)
    o_ref[0] = jnp.where(lane_lo, acc_sc[0], acc_sc[1]).astype(BF16)


def _attn_c(cq, ck, cv, t):
    B, S, _ = cq.shape
    return pl.pallas_call(
        functools.partial(_attn_c_kernel, t=t),
        grid=(B, C_HEADS // 2, S // t),
        in_specs=[pl.BlockSpec((1, t, LANES), lambda b, p, i: (b, i, p)),
                  pl.BlockSpec((1, S, LANES), lambda b, p, i: (b, 0, p)),
                  pl.BlockSpec((1, S, LANES), lambda b, p, i: (b, 0, p))],
        out_specs=pl.BlockSpec((1, t, LANES), lambda b, p, i: (b, i, p)),
        out_shape=jax.ShapeDtypeStruct((B, S, 512), BF16),
        scratch_shapes=[pltpu.VMEM((2, t, 1), F32), pltpu.VMEM((2, t, LANES), F32)],
        compiler_params=pltpu.CompilerParams(dimension_semantics=("arbitrary",) * 3, vmem_limit_bytes=VMEM_LIMIT),
        name="attn_c",
    )(cq, ck, cv)


def _attn_d_kernel(slope_ref, q_ref, k_ref, v_ref, lam_ref, subln_ref, o_ref, m_sc, l_sc, acc_sc,
                   *, t, lambda_init):
    h = pl.program_id(1)
    qi = pl.program_id(2)
    slope = slope_ref[h]
    row = lax.broadcasted_iota(jnp.int32, (t, t), 0)
    col = lax.broadcasted_iota(jnp.int32, (t, t), 1)
    allowed = (col >> CHUNK_SHIFT) <= (row >> CHUNK_SHIFT)
    rel = row - col
    lane_lo = _lane_lo((1, LANES))
    qq = q_ref[0]
    qs = [jnp.where(lane_lo == (e == 0), qq, jnp.zeros_like(qq)) for e in range(2)]
    for e in range(2):
        m_sc[e] = jnp.full((t, 1), NEG, F32)
        l_sc[e] = jnp.zeros((t, 1), F32)
        acc_sc[e] = jnp.zeros((t, LANES), F32)

    def step(j, diag):
        k = k_ref[0, pl.ds(pl.multiple_of(j * t, t), t), :]
        v = v_ref[0, pl.ds(pl.multiple_of(j * t, t), t), :]
        bias = -slope * jnp.abs(rel + (qi - j) * t).astype(F32)
        for e in range(2):
            s = _dot_t(qs[e], k) + bias
            if diag:
                s = jnp.where(allowed, s, NEG)
            m_prev = m_sc[e]
            m_new = jnp.maximum(m_prev, jnp.max(s, axis=-1, keepdims=True))
            alpha = jnp.exp(m_prev - m_new)
            pr = jnp.exp(s - m_new)
            l_sc[e] = alpha * l_sc[e] + jnp.sum(pr, axis=-1, keepdims=True)
            acc_sc[e] = alpha * acc_sc[e] + _dot(pr.astype(BF16), v)
            m_sc[e] = m_new

    def body(j, c):
        step(j, False)
        return c

    lax.fori_loop(0, qi, body, 0)
    step(qi, True)
    lf = lam_ref[...]
    lam = (jnp.exp(jnp.sum(lf[0:1] * lf[1:2], axis=-1, keepdims=True))
           - jnp.exp(jnp.sum(lf[2:3] * lf[3:4], axis=-1, keepdims=True)) + lambda_init)
    o = acc_sc[0] / l_sc[0] - lam * (acc_sc[1] / l_sc[1])
    o = _rms(o, LANES) * subln_ref[...] * (1.0 - lambda_init)
    o_ref[0] = o.astype(BF16)


def _attn_d(slopes, dq, dk, dv, lam, subln, lambda_init, t):
    B, S, _ = dq.shape
    smem = pl.BlockSpec(memory_space=pltpu.SMEM)
    return pl.pallas_call(
        functools.partial(_attn_d_kernel, t=t, lambda_init=lambda_init),
        grid=(B, D_HEADS, S // t),
        in_specs=[smem,
                  pl.BlockSpec((1, t, LANES), lambda b, h, i: (b, i, h)),
                  pl.BlockSpec((1, S, LANES), lambda b, h, i: (b, 0, h)),
                  pl.BlockSpec((1, S, LANES), lambda b, h, i: (b, 0, h)),
                  pl.BlockSpec(lam.shape, lambda b, h, i: (0, 0)),
                  pl.BlockSpec(subln.shape, lambda b, h, i: (0, 0))],
        out_specs=pl.BlockSpec((1, t, LANES), lambda b, h, i: (b, i, h)),
        out_shape=jax.ShapeDtypeStruct((B, S, 512), BF16),
        scratch_shapes=[pltpu.VMEM((2, t, 1), F32), pltpu.VMEM((2, t, 1), F32), pltpu.VMEM((2, t, LANES), F32)],
        compiler_params=pltpu.CompilerParams(dimension_semantics=("arbitrary",) * 3, vmem_limit_bytes=VMEM_LIMIT),
        name="attn_d",
    )(slopes, dq, dk, dv, lam, subln)


def _out_mlp_kernel(x_ref, ma_ref, mb_ref, wo_ref, g_ref, wu_ref, wd_ref, o_ref, *, tf):
    half = ma_ref.shape[-1]
    x1 = x_ref[...] + _dot(ma_ref[...], wo_ref[0:half, :]) + _dot(mb_ref[...], wo_ref[half:2 * half, :])
    h = (_rms(x1, x1.shape[-1]) * g_ref[...]).astype(BF16)
    o_ref[...] = x1
    for f in range(wu_ref.shape[-1] // tf):
        u = jnp.maximum(_dot(h, wu_ref[:, f * tf:(f + 1) * tf]), 0.0)
        o_ref[...] += _dot((u * u).astype(BF16), wd_ref[f * tf:(f + 1) * tf, :])


def _out_mlp(x, ma, mb, wo, g, wu, wd, tm, tf):
    T, D = x.shape
    full = lambda a: pl.BlockSpec(a.shape, lambda i: (0,) * a.ndim, pipeline_mode=pl.Buffered(1))
    row = lambda c: pl.BlockSpec((tm, c), lambda i: (i, 0))
    return pl.pallas_call(
        functools.partial(_out_mlp_kernel, tf=tf),
        grid=(T // tm,),
        in_specs=[row(D), row(ma.shape[-1]), row(mb.shape[-1]), full(wo), full(g), full(wu), full(wd)],
        out_specs=row(D),
        out_shape=jax.ShapeDtypeStruct((T, D), F32),
        compiler_params=pltpu.CompilerParams(dimension_semantics=("arbitrary",), vmem_limit_bytes=VMEM_LIMIT),
        name="out_mlp",
    )(x, ma, mb, wo, g, wu, wd)


def _alibi_slopes(n):
    return 2.0 ** (-8.0 * jnp.arange(1, n + 1, dtype=F32) / n)


def _tile(n, want):
    t = min(n, want)
    assert n % t == 0, (n, t)
    return t


def kernel(x, positions, norm_mix_g, norm_ffn_g, mlp_w_up, mlp_w_down, ev_w_in, ev_w_out, a_q_norm, a_k_norm, a_sinks, b_cq_norm, b_ckv_norm, b_w_uq, b_w_ukv, b_q_norm, b_k_norm, od_w_in, od_w_out, d_q_norm, d_k_norm, d_lambda, d_subln):
    B, S, D = x.shape
    T = B * S
    depth = norm_mix_g.shape[0]
    tm_proj = _tile(T, 512)
    tm_mlp = _tile(T, 512)
    t_a = _tile(S, 512)
    t_b = _tile(S, 512)
    t_c = _tile(S, 256)
    t_d = _tile(S, 512)
    tf = 512

    xf = x.reshape(T, D)
    pos = positions.reshape(T, 1)
    row2 = lambda a: a.reshape(1, -1).astype(F32)
    pair = lambda a: jnp.concatenate([a, a]).reshape(1, LANES).astype(F32)
    pad_qk = lambda a: jnp.pad(a.astype(F32), (0, LANES - B_QK)).reshape(1, LANES)
    inv = ROPE_THETA ** (-jnp.arange(B_ROPE_HALF, dtype=F32) / B_ROPE_HALF)
    inv_lanes = jnp.zeros((LANES,), F32).at[HEAD:HEAD + B_ROPE_HALF].set(inv).at[HEAD + B_ROPE_HALF:B_QK].set(inv)
    inv_lanes = inv_lanes.reshape(1, LANES)
    slopes_a = _alibi_slopes(A_HEADS)
    slopes_d = _alibi_slopes(D_HEADS)

    for layer in range(depth):
        j = layer // 2
        g_mix = row2(norm_mix_g[layer])
        if layer % 2 == 0:
            w_in = ev_w_in[j]
            w_in = jnp.concatenate([w_in[:, :1152], jnp.zeros((D, HEAD), F32), w_in[:, 1152:],
                                    jnp.zeros((D, LANES - B_QK), F32)], axis=1).astype(BF16)
            wuq = jnp.pad(b_w_uq[j].reshape(-1, B_HEADS, B_QK), ((0, 0), (0, 0), (0, LANES - B_QK)))
            wuq = wuq.reshape(-1, B_HEADS * LANES).astype(BF16)
            qa, ka, va, qb, kb, vb = _even_proj(
                xf, pos, g_mix, w_in, pair(a_q_norm[j]), pair(a_k_norm[j]), row2(b_cq_norm[j]),
                row2(b_ckv_norm[j]), wuq, b_w_ukv[j].astype(BF16), pad_qk(b_q_norm[j]), pad_qk(b_k_norm[j]),
                inv_lanes, tm_proj)
            r3 = lambda a: a.reshape(B, S, a.shape[-1])
            ma = _attn_a(slopes_a, a_sinks[j].astype(F32), r3(qa), r3(ka), r3(va), t_a).reshape(T, -1)
            mb = _attn_b(r3(qb), r3(kb), r3(vb), t_b).reshape(T, -1)
            w_out = ev_w_out[j]
        else:
            lambda_init = 0.8 - 0.6 * math.exp(-0.3 * layer)
            cq, ck, cv, dq, dk, dv = _odd_proj(
                xf, g_mix, od_w_in[j].astype(BF16), d_q_norm[j].reshape(1, LANES).astype(F32),
                d_k_norm[j].reshape(1, LANES).astype(F32), tm_proj)
            r3 = lambda a: a.reshape(B, S, a.shape[-1])
            ma = _attn_c(r3(cq), r3(ck), r3(cv), t_c).reshape(T, -1)
            mb = _attn_d(slopes_d, r3(dq), r3(dk), r3(dv), d_lambda[j].astype(F32), row2(d_subln[j]),
                         lambda_init, t_d).reshape(T, -1)
            w_out = od_w_out[j]
        xf = _out_mlp(xf, ma, mb, w_out.astype(BF16), row2(norm_ffn_g[layer]),
                      mlp_w_up[layer].astype(BF16), mlp_w_down[layer].astype(BF16), tm_mlp, tf)
    return xf.reshape(B, S, D)
```

```python
import functools
import math

import jax
import jax.numpy as jnp
from jax import lax
from jax.experimental import pallas as pl
from jax.experimental.pallas import tpu as pltpu

F32 = jnp.float32
BF16 = jnp.bfloat16

EPS = 1e-6
CHUNK = 64
CHUNK_SHIFT = 6
LANES = 128
HEAD = 64
A_HEADS = 8
A_GROUP = 4
A_WINDOW = 128
A_WINDOW_CHUNKS = A_WINDOW // CHUNK
B_HEADS = 8
B_QK = 96
B_ROPE_HALF = 16
ROPE_THETA = 10000.0
C_HEADS = 8
D_HEADS = 4
NEG = -1e30
LOG2E = math.log2(math.e)
VMEM_LIMIT = 56 * 1024 * 1024


def _dot(a, b):
    return jnp.dot(a, b, preferred_element_type=F32)


def _dot_t(a, b):
    return lax.dot_general(a, b, (((1,), (1,)), ((), ())), preferred_element_type=F32)


def _dot_tl(a, b):
    return lax.dot_general(a, b, (((0,), (0,)), ((), ())), preferred_element_type=F32)


def _rms(x, denom):
    return x * lax.rsqrt(jnp.sum(x * x, axis=-1, keepdims=True) * (1.0 / denom) + EPS)


def _pair_rms(x, lo):
    xx = x * x
    s_lo = jnp.sum(jnp.where(lo, xx, 0.0), axis=-1, keepdims=True)
    s_hi = jnp.sum(jnp.where(lo, 0.0, xx), axis=-1, keepdims=True)
    r = jnp.where(lo, lax.rsqrt(s_lo * (1.0 / HEAD) + EPS), lax.rsqrt(s_hi * (1.0 / HEAD) + EPS))
    return x * r


def _lane_lo(shape):
    return lax.broadcasted_iota(jnp.int32, shape, len(shape) - 1) < HEAD


def _even_proj_kernel(x_ref, pos_ref, g_ref, w_in_ref, aqg_ref, akg_ref, cqg_ref, ckvg_ref,
                      wuq_ref, wukv_ref, bqg_ref, bkg_ref, inv_ref,
                      qa_ref, ka_ref, va_ref, qb_ref, kb_ref, vb_ref):
    x = x_ref[...]
    h = _rms(x, x.shape[-1]) * g_ref[...]
    proj = _dot(h.astype(BF16), w_in_ref[...])
    lo = _lane_lo((1, LANES))
    for p in range(A_HEADS // 2):
        seg = proj[:, p * LANES:(p + 1) * LANES]
        qa_ref[:, p * LANES:(p + 1) * LANES] = (_pair_rms(seg, lo) * aqg_ref[...] * 0.125).astype(BF16)
    ka_ref[...] = (_pair_rms(proj[:, 512:640], lo) * akg_ref[...]).astype(BF16)
    va_ref[...] = proj[:, 640:768].astype(BF16)
    cq = _rms(proj[:, 768:1024], 256) * cqg_ref[...]
    ckv = _rms(proj[:, 1024:1152], 128) * ckvg_ref[...]
    krope = proj[:, 1152:1280]
    qall = _dot(cq.astype(BF16), wuq_ref[...])
    kvall = _dot(ckv.astype(BF16), wukv_ref[...])
    ang = pos_ref[...].astype(F32) * inv_ref[...]
    cosf = jnp.cos(ang)
    sinf = jnp.sin(ang)
    lane = lax.broadcasted_iota(jnp.int32, (1, LANES), 1)
    s_first = jnp.where((lane >= HEAD) & (lane < HEAD + B_ROPE_HALF), -sinf, 0.0)
    s_second = jnp.where((lane >= HEAD + B_ROPE_HALF) & (lane < B_QK), sinf, 0.0)

    def rope(t):
        return (t * cosf + pltpu.roll(t, LANES - B_ROPE_HALF, 1) * s_first
                + pltpu.roll(t, B_ROPE_HALF, 1) * s_second)

    scale_b = B_QK ** -0.5 * LOG2E
    for p in range(B_HEADS // 2):
        vpair = []
        for hh in (2 * p, 2 * p + 1):
            qh = qall[:, hh * LANES:(hh + 1) * LANES]
            qn = _rms(qh, B_QK) * bqg_ref[...]
            qb_ref[:, hh * LANES:(hh + 1) * LANES] = (rope(qn) * scale_b).astype(BF16)
            kvh = kvall[:, hh * LANES:(hh + 1) * LANES]
            kpre = jnp.where(lo, kvh, krope)
            kn = _rms(kpre, B_QK) * bkg_ref[...]
            kb_ref[:, hh * LANES:(hh + 1) * LANES] = rope(kn).astype(BF16)
            vpair.append(kvh)
        vb_ref[:, p * LANES:(p + 1) * LANES] = jnp.where(
            lo, pltpu.roll(vpair[0], HEAD, 1), vpair[1]).astype(BF16)


def _even_proj(x, pos, g, w_in, aqg, akg, cqg, ckvg, wuq, wukv, bqg, bkg, inv, tm):
    T, D = x.shape
    full = lambda a: pl.BlockSpec(a.shape, lambda i: (0,) * a.ndim)
    row = lambda c: pl.BlockSpec((tm, c), lambda i: (i, 0))
    outs = (512, 128, 128, 1024, 1024, 512)
    return pl.pallas_call(
        _even_proj_kernel,
        grid=(T // tm,),
        in_specs=[row(D), row(1)] + [full(a) for a in (g, w_in, aqg, akg, cqg, ckvg, wuq, wukv, bqg, bkg, inv)],
        out_specs=[row(c) for c in outs],
        out_shape=[jax.ShapeDtypeStruct((T, c), BF16) for c in outs],
        compiler_params=pltpu.CompilerParams(dimension_semantics=("arbitrary",), vmem_limit_bytes=VMEM_LIMIT),
        name="even_proj",
    )(x, pos, g, w_in, aqg, akg, cqg, ckvg, wuq, wukv, bqg, bkg, inv)


def _odd_proj_kernel(x_ref, g_ref, w_in_ref, dqg_ref, dkg_ref,
                     cq_ref, ck_ref, cv_ref, dq_ref, dk_ref, dv_ref):
    x = x_ref[...]
    h = _rms(x, x.shape[-1]) * g_ref[...]
    proj = _dot(h.astype(BF16), w_in_ref[...])
    lo = _lane_lo((1, LANES))
    qscale = 0.125 * LOG2E
    cq_ref[...] = (proj[:, 0:512] * qscale).astype(BF16)
    ck_ref[...] = proj[:, 512:1024].astype(BF16)
    cv_ref[...] = proj[:, 1024:1536].astype(BF16)
    for hh in range(D_HEADS):
        sl = slice(hh * LANES, (hh + 1) * LANES)
        dq_ref[:, sl] = (_pair_rms(proj[:, 1536 + hh * LANES:1536 + (hh + 1) * LANES], lo)
                         * dqg_ref[...] * qscale).astype(BF16)
        dk_ref[:, sl] = (_pair_rms(proj[:, 2048 + hh * LANES:2048 + (hh + 1) * LANES], lo)
                         * dkg_ref[...]).astype(BF16)
    dv_ref[...] = proj[:, 2560:3072].astype(BF16)


def _odd_proj(x, g, w_in, dqg, dkg, tm):
    T, D = x.shape
    full = lambda a: pl.BlockSpec(a.shape, lambda i: (0,) * a.ndim)
    row = lambda c: pl.BlockSpec((tm, c), lambda i: (i, 0))
    return pl.pallas_call(
        _odd_proj_kernel,
        grid=(T // tm,),
        in_specs=[row(D)] + [full(a) for a in (g, w_in, dqg, dkg)],
        out_specs=[row(512)] * 6,
        out_shape=[jax.ShapeDtypeStruct((T, 512), BF16)] * 6,
        compiler_params=pltpu.CompilerParams(dimension_semantics=("arbitrary",), vmem_limit_bytes=VMEM_LIMIT),
        name="odd_proj",
    )(x, g, w_in, dqg, dkg)


def _attn_a_kernel(slope_ref, sink_ref, q_ref, k_ref, v_ref, o_ref, *, tq, win):
    p = pl.program_id(1)
    qi = pl.program_id(2)
    q0 = qi * tq
    ks = pl.multiple_of(jnp.maximum(q0 - A_WINDOW, 0), A_WINDOW)
    grp = p // (A_GROUP // 2)
    lane_half = lax.broadcasted_iota(jnp.int32, (1, LANES), 1) // HEAD
    in_grp = lane_half == grp
    k = k_ref[0, pl.ds(ks, win), :]
    v = v_ref[0, pl.ds(ks, win), :]
    qq = q_ref[0].astype(F32)
    qsw = pltpu.roll(qq, HEAD, 1)
    qpos = q0 + lax.broadcasted_iota(jnp.int32, (tq, win), 0)
    kpos = ks + lax.broadcasted_iota(jnp.int32, (tq, win), 1)
    dch = (qpos >> CHUNK_SHIFT) - (kpos >> CHUNK_SHIFT)
    allowed = (dch >= 0) & (dch <= A_WINDOW_CHUNKS)
    dist = jnp.abs(qpos - kpos).astype(F32)
    outs = []
    for e in range(2):
        hidx = 2 * p + e
        qh = jnp.where(in_grp, jnp.where(grp == e, qq, qsw), 0.0).astype(BF16)
        s = _dot_t(qh, k) - slope_ref[hidx] * dist
        s = jnp.where(allowed, s, NEG)
        sink = sink_ref[hidx]
        m = jnp.maximum(jnp.max(s, axis=-1, keepdims=True), sink)
        pr = jnp.exp(s - m)
        den = jnp.sum(pr, axis=-1, keepdims=True) + jnp.exp(sink - m)
        o = _dot(pr.astype(BF16), v) / den
        outs.append(jnp.where(grp == e, o, pltpu.roll(o, HEAD, 1)))
    o_ref[0] = jnp.where(lane_half == 0, outs[0], outs[1]).astype(BF16)


def _attn_a(slopes, sinks, qa, ka, va, tq):
    B, S, _ = qa.shape
    win = tq + A_WINDOW
    smem = pl.BlockSpec(memory_space=pltpu.SMEM)
    return pl.pallas_call(
        functools.partial(_attn_a_kernel, tq=tq, win=win),
        grid=(B, A_HEADS // 2, S // tq),
        in_specs=[smem, smem,
                  pl.BlockSpec((1, tq, LANES), lambda b, p, i: (b, i, p)),
                  pl.BlockSpec((1, S, LANES), lambda b, p, i: (b, 0, 0)),
                  pl.BlockSpec((1, S, LANES), lambda b, p, i: (b, 0, 0))],
        out_specs=pl.BlockSpec((1, tq, LANES), lambda b, p, i: (b, i, p)),
        out_shape=jax.ShapeDtypeStruct((B, S, 512), BF16),
        compiler_params=pltpu.CompilerParams(dimension_semantics=("arbitrary",) * 3, vmem_limit_bytes=VMEM_LIMIT),
        name="attn_a",
    )(slopes, sinks, qa, ka, va)


def _attn_b_kernel(q_ref, k_ref, v_ref, o_ref, m_sc, l_sc, acc_sc, *, t):
    qi = pl.program_id(2)
    krow = lax.broadcasted_iota(jnp.int32, (t, t), 0)
    qcol = lax.broadcasted_iota(jnp.int32, (t, t), 1)
    allowed = (krow >> CHUNK_SHIFT) <= (qcol >> CHUNK_SHIFT)
    qs = [q_ref[0, :, e * LANES:(e + 1) * LANES] for e in range(2)]
    for e in range(2):
        m_sc[e] = jnp.full((1, t), NEG, F32)
        l_sc[e] = jnp.zeros((1, t), F32)
        acc_sc[e] = jnp.zeros((LANES, t), F32)

    def step(j, diag):
        rows = pl.ds(pl.multiple_of(j * t, t), t)
        v = v_ref[0, rows, :]
        ss = [_dot_t(k_ref[0, rows, e * LANES:(e + 1) * LANES], qs[e]) for e in range(2)]
        for e in range(2):
            s = ss[e]
            if diag:
                s = jnp.where(allowed, s, NEG)
            m_prev = m_sc[e]
            m_new = jnp.maximum(m_prev, jnp.max(s, axis=0, keepdims=True))
            alpha = jnp.exp2(m_prev - m_new)
            pr = jnp.exp2(s - m_new)
            l_sc[e] = alpha * l_sc[e] + jnp.sum(pr, axis=0, keepdims=True)
            acc_sc[e] = alpha * acc_sc[e] + _dot_tl(v, pr.astype(BF16))
            m_sc[e] = m_new

    def body(j, c):
        step(j, False)
        return c

    lax.fori_loop(0, qi, body, 0)
    step(qi, True)
    first = lax.broadcasted_iota(jnp.int32, (LANES, 1), 0) < HEAD
    o_t = jnp.where(first, acc_sc[0] / l_sc[0], acc_sc[1] / l_sc[1])
    o_ref[0] = o_t.T.astype(BF16)


def _attn_b(qb, kb, vb, t):
    B, S, _ = qb.shape
    return pl.pallas_call(
        functools.partial(_attn_b_kernel, t=t),
        grid=(B, B_HEADS // 2, S // t),
        in_specs=[pl.BlockSpec((1, t, 2 * LANES), lambda b, p, i: (b, i, p)),
                  pl.BlockSpec((1, S, 2 * LANES), lambda b, p, i: (b, 0, p)),
                  pl.BlockSpec((1, S, LANES), lambda b, p, i: (b, 0, p))],
        out_specs=pl.BlockSpec((1, t, LANES), lambda b, p, i: (b, i, p)),
        out_shape=jax.ShapeDtypeStruct((B, S, 512), BF16),
        scratch_shapes=[pltpu.VMEM((2, 1, t), F32), pltpu.VMEM((2, 1, t), F32), pltpu.VMEM((2, LANES, t), F32)],
        compiler_params=pltpu.CompilerParams(dimension_semantics=("arbitrary",) * 3, vmem_limit_bytes=VMEM_LIMIT),
        name="attn_b",
    )(qb, kb, vb)


def _attn_c_kernel(q_ref, k_ref, v_ref, o_ref, r_sc, acc_sc, *, t):
    qi = pl.program_id(2)
    krow = lax.broadcasted_iota(jnp.int32, (t, t), 0)
    qcol = lax.broadcasted_iota(jnp.int32, (t, t), 1)
    strict = krow < qcol
    later = jnp.where(qcol > krow, 1.0, 0.0).astype(BF16)
    lane_lo = _lane_lo((1, LANES))
    qq = q_ref[0]
    qs = [jnp.where(lane_lo == (e == 0), qq, jnp.zeros_like(qq)) for e in range(2)]
    for e in range(2):
        r_sc[e] = jnp.zeros((1, t), F32)
        acc_sc[e] = jnp.zeros((LANES, t), F32)

    def step(j, diag):
        rows = pl.ds(pl.multiple_of(j * t, t), t)
        k = k_ref[0, rows, :]
        v = v_ref[0, rows, :]
        zs = [_dot_t(k, qs[e]) for e in range(2)]
        for e in range(2):
            z = zs[e]
            sp = jnp.maximum(z, 0.0) + jnp.log2(1.0 + jnp.exp2(-jnp.abs(z)))
            spm = jnp.where(strict, sp, 0.0) if diag else sp
            hi = spm.astype(BF16)
            lo = (spm - hi.astype(F32)).astype(BF16)
            suffix = _dot(later, hi) + _dot(later, lo)
            a = jnp.exp2(z - sp - suffix - r_sc[e])
            if diag:
                a = jnp.where(strict, a, 0.0)
            acc_sc[e] = acc_sc[e] + _dot_tl(v, a.astype(BF16))
            r_sc[e] = r_sc[e] + jnp.sum(spm, axis=0, keepdims=True)

    step(qi, True)

    def body(i, c):
        step(qi - 1 - i, False)
        return c

    lax.fori_loop(0, qi, body, 0)
    first = lax.broadcasted_iota(jnp.int32, (LANES, 1), 0) < HEAD
    o_ref[0] = jnp.where(first, acc_sc[0], acc_sc[1]).T.astype(BF16)


def _attn_c(cq, ck, cv, t):
    B, S, _ = cq.shape
    return pl.pallas_call(
        functools.partial(_attn_c_kernel, t=t),
        grid=(B, C_HEADS // 2, S // t),
        in_specs=[pl.BlockSpec((1, t, LANES), lambda b, p, i: (b, i, p)),
                  pl.BlockSpec((1, S, LANES), lambda b, p, i: (b, 0, p)),
                  pl.BlockSpec((1, S, LANES), lambda b, p, i: (b, 0, p))],
        out_specs=pl.BlockSpec((1, t, LANES), lambda b, p, i: (b, i, p)),
        out_shape=jax.ShapeDtypeStruct((B, S, 512), BF16),
        scratch_shapes=[pltpu.VMEM((2, 1, t), F32), pltpu.VMEM((2, LANES, t), F32)],
        compiler_params=pltpu.CompilerParams(dimension_semantics=("arbitrary",) * 3, vmem_limit_bytes=VMEM_LIMIT),
        name="attn_c",
    )(cq, ck, cv)


def _attn_d_kernel(slope_ref, q_ref, k_ref, v_ref, lam_ref, subln_ref, o_ref, m_sc, l_sc, acc_sc,
                   *, t, lambda_init):
    h = pl.program_id(1)
    qi = pl.program_id(2)
    slope = slope_ref[h] * LOG2E
    lane_lo = _lane_lo((1, LANES))
    qq = q_ref[0]
    qs = [jnp.where(lane_lo == (e == 0), qq, jnp.zeros_like(qq)) for e in range(2)]
    key_bias = slope * lax.broadcasted_iota(jnp.int32, (t, LANES), 0).astype(F32)
    for e in range(2):
        m_sc[e] = jnp.full((1, t), NEG, F32)
        l_sc[e] = jnp.zeros((1, t), F32)
        acc_sc[e] = jnp.zeros((LANES, t), F32)

    def step(j, diag):
        rows = pl.ds(pl.multiple_of(j * t, t), t)
        k = k_ref[0, rows, :]
        v = v_ref[0, rows, :]
        ss = [_dot_t(k, qs[e]) for e in range(2)]
        if diag:
            krow = lax.broadcasted_iota(jnp.int32, (t, t), 0)
            qcol = lax.broadcasted_iota(jnp.int32, (t, t), 1)
            allowed = (krow >> CHUNK_SHIFT) <= (qcol >> CHUNK_SHIFT)
            bias = slope * jnp.minimum(krow, 2 * qcol - krow).astype(F32)
            off = 0.0
        else:
            off = slope * ((qi - j) * t).astype(F32)
        for e in range(2):
            if diag:
                s = jnp.where(allowed, ss[e] + bias, NEG)
            else:
                s = jnp.concatenate([ss[e][:, c * LANES:(c + 1) * LANES] + key_bias
                                     for c in range(t // LANES)], axis=1)
            m_prev = m_sc[e]
            m_new = jnp.maximum(m_prev, jnp.max(s, axis=0, keepdims=True) - off)
            alpha = jnp.exp2(m_prev - m_new)
            pr = jnp.exp2(s - (m_new + off))
            l_sc[e] = alpha * l_sc[e] + jnp.sum(pr, axis=0, keepdims=True)
            acc_sc[e] = alpha * acc_sc[e] + _dot_tl(v, pr.astype(BF16))
            m_sc[e] = m_new

    def body(j, c):
        step(j, False)
        return c

    lax.fori_loop(0, qi, body, 0)
    step(qi, True)
    lf = lam_ref[...]
    lam = (jnp.exp(jnp.sum(lf[0:1] * lf[1:2], axis=-1, keepdims=True))
           - jnp.exp(jnp.sum(lf[2:3] * lf[3:4], axis=-1, keepdims=True)) + lambda_init)
    o = (acc_sc[0] / l_sc[0] - lam * (acc_sc[1] / l_sc[1])).T
    o = _rms(o, LANES) * subln_ref[...] * (1.0 - lambda_init)
    o_ref[0] = o.astype(BF16)


def _attn_d(slopes, dq, dk, dv, lam, subln, lambda_init, t):
    B, S, _ = dq.shape
    smem = pl.BlockSpec(memory_space=pltpu.SMEM)
    return pl.pallas_call(
        functools.partial(_attn_d_kernel, t=t, lambda_init=lambda_init),
        grid=(B, D_HEADS, S // t),
        in_specs=[smem,
                  pl.BlockSpec((1, t, LANES), lambda b, h, i: (b, i, h)),
                  pl.BlockSpec((1, S, LANES), lambda b, h, i: (b, 0, h)),
                  pl.BlockSpec((1, S, LANES), lambda b, h, i: (b, 0, h)),
                  pl.BlockSpec(lam.shape, lambda b, h, i: (0, 0)),
                  pl.BlockSpec(subln.shape, lambda b, h, i: (0, 0))],
        out_specs=pl.BlockSpec((1, t, LANES), lambda b, h, i: (b, i, h)),
        out_shape=jax.ShapeDtypeStruct((B, S, 512), BF16),
        scratch_shapes=[pltpu.VMEM((2, 1, t), F32), pltpu.VMEM((2, 1, t), F32), pltpu.VMEM((2, LANES, t), F32)],
        compiler_params=pltpu.CompilerParams(dimension_semantics=("arbitrary",) * 3, vmem_limit_bytes=VMEM_LIMIT),
        name="attn_d",
    )(slopes, dq, dk, dv, lam, subln)


def _out_mlp_kernel(x_ref, ma_ref, mb_ref, wo_ref, g_ref, wu_ref, wd_ref, o_ref, *, tf):
    half = ma_ref.shape[-1]
    x1 = x_ref[...] + _dot(ma_ref[...], wo_ref[0:half, :]) + _dot(mb_ref[...], wo_ref[half:2 * half, :])
    h = (_rms(x1, x1.shape[-1]) * g_ref[...]).astype(BF16)
    o_ref[...] = x1
    for f in range(wu_ref.shape[-1] // tf):
        u = jnp.maximum(_dot(h, wu_ref[:, f * tf:(f + 1) * tf]), 0.0)
        o_ref[...] += _dot((u * u).astype(BF16), wd_ref[f * tf:(f + 1) * tf, :])


def _out_mlp(x, ma, mb, wo, g, wu, wd, tm, tf):
    T, D = x.shape
    full = lambda a: pl.BlockSpec(a.shape, lambda i: (0,) * a.ndim, pipeline_mode=pl.Buffered(1))
    row = lambda c: pl.BlockSpec((tm, c), lambda i: (i, 0))
    return pl.pallas_call(
        functools.partial(_out_mlp_kernel, tf=tf),
        grid=(T // tm,),
        in_specs=[row(D), row(ma.shape[-1]), row(mb.shape[-1]), full(wo), full(g), full(wu), full(wd)],
        out_specs=row(D),
        out_shape=jax.ShapeDtypeStruct((T, D), F32),
        compiler_params=pltpu.CompilerParams(dimension_semantics=("arbitrary",), vmem_limit_bytes=VMEM_LIMIT),
        name="out_mlp",
    )(x, ma, mb, wo, g, wu, wd)


def _alibi_slopes(n):
    return 2.0 ** (-8.0 * jnp.arange(1, n + 1, dtype=F32) / n)


def _tile(n, want):
    t = min(n, want)
    assert n % t == 0, (n, t)
    return t


def kernel(x, positions, norm_mix_g, norm_ffn_g, mlp_w_up, mlp_w_down, ev_w_in, ev_w_out, a_q_norm, a_k_norm, a_sinks, b_cq_norm, b_ckv_norm, b_w_uq, b_w_ukv, b_q_norm, b_k_norm, od_w_in, od_w_out, d_q_norm, d_k_norm, d_lambda, d_subln):
    B, S, D = x.shape
    T = B * S
    depth = norm_mix_g.shape[0]
    tm_proj = _tile(T, 512)
    tm_mlp = _tile(T, 512)
    t_a = _tile(S, 512)
    t_b = _tile(S, 512)
    t_c = _tile(S, 256)
    t_d = _tile(S, 512)
    tf = 512

    xf = x.reshape(T, D)
    pos = positions.reshape(T, 1)
    row2 = lambda a: a.reshape(1, -1).astype(F32)
    pair = lambda a: jnp.concatenate([a, a]).reshape(1, LANES).astype(F32)
    pad_qk = lambda a: jnp.pad(a.astype(F32), (0, LANES - B_QK)).reshape(1, LANES)
    inv = ROPE_THETA ** (-jnp.arange(B_ROPE_HALF, dtype=F32) / B_ROPE_HALF)
    inv_lanes = jnp.zeros((LANES,), F32).at[HEAD:HEAD + B_ROPE_HALF].set(inv).at[HEAD + B_ROPE_HALF:B_QK].set(inv)
    inv_lanes = inv_lanes.reshape(1, LANES)
    slopes_a = _alibi_slopes(A_HEADS)
    slopes_d = _alibi_slopes(D_HEADS)

    for layer in range(depth):
        j = layer // 2
        g_mix = row2(norm_mix_g[layer])
        if layer % 2 == 0:
            w_in = ev_w_in[j]
            w_in = jnp.concatenate([w_in[:, :1152], jnp.zeros((D, HEAD), F32), w_in[:, 1152:],
                                    jnp.zeros((D, LANES - B_QK), F32)], axis=1).astype(BF16)
            wuq = jnp.pad(b_w_uq[j].reshape(-1, B_HEADS, B_QK), ((0, 0), (0, 0), (0, LANES - B_QK)))
            wuq = wuq.reshape(-1, B_HEADS * LANES).astype(BF16)
            qa, ka, va, qb, kb, vb = _even_proj(
                xf, pos, g_mix, w_in, pair(a_q_norm[j]), pair(a_k_norm[j]), row2(b_cq_norm[j]),
                row2(b_ckv_norm[j]), wuq, b_w_ukv[j].astype(BF16), pad_qk(b_q_norm[j]), pad_qk(b_k_norm[j]),
                inv_lanes, tm_proj)
            r3 = lambda a: a.reshape(B, S, a.shape[-1])
            ma = _attn_a(slopes_a, a_sinks[j].astype(F32), r3(qa), r3(ka), r3(va), t_a).reshape(T, -1)
            mb = _attn_b(r3(qb), r3(kb), r3(vb), t_b).reshape(T, -1)
            w_out = ev_w_out[j]
        else:
            lambda_init = 0.8 - 0.6 * math.exp(-0.3 * layer)
            cq, ck, cv, dq, dk, dv = _odd_proj(
                xf, g_mix, od_w_in[j].astype(BF16), d_q_norm[j].reshape(1, LANES).astype(F32),
                d_k_norm[j].reshape(1, LANES).astype(F32), tm_proj)
            r3 = lambda a: a.reshape(B, S, a.shape[-1])
            ma = _attn_c(r3(cq), r3(ck), r3(cv), t_c).reshape(T, -1)
            mb = _attn_d(slopes_d, r3(dq), r3(dk), r3(dv), d_lambda[j].astype(F32), row2(d_subln[j]),
                         lambda_init, t_d).reshape(T, -1)
            w_out = od_w_out[j]
        xf = _out_mlp(xf, ma, mb, w_out.astype(BF16), row2(norm_ffn_g[layer]),
                      mlp_w_up[layer].astype(BF16), mlp_w_down[layer].astype(BF16), tm_mlp, tf)
    return xf.reshape(B, S, D)
```

```python
import functools
import math

import jax
import jax.numpy as jnp
from jax import lax
from jax.experimental import pallas as pl
from jax.experimental.pallas import tpu as pltpu

F32 = jnp.float32
BF16 = jnp.bfloat16

EPS = 1e-6
CHUNK = 64
CHUNK_SHIFT = 6
LANES = 128
HEAD = 64
A_HEADS = 8
A_GROUP = 4
A_WINDOW = 128
A_WINDOW_CHUNKS = A_WINDOW // CHUNK
B_HEADS = 8
B_QK = 96
B_ROPE_HALF = 16
ROPE_THETA = 10000.0
C_HEADS = 8
D_HEADS = 4
NEG = -1e30
LOG2E = math.log2(math.e)
VMEM_LIMIT = 56 * 1024 * 1024


def _dot(a, b):
    return jnp.dot(a, b, preferred_element_type=F32)


def _dot_t(a, b):
    return lax.dot_general(a, b, (((1,), (1,)), ((), ())), preferred_element_type=F32)


def _dot_tl(a, b):
    return lax.dot_general(a, b, (((0,), (0,)), ((), ())), preferred_element_type=F32)


def _rms(x, denom):
    return x * lax.rsqrt(jnp.sum(x * x, axis=-1, keepdims=True) * (1.0 / denom) + EPS)


def _pair_rms(x, lo):
    xx = x * x
    s_lo = jnp.sum(jnp.where(lo, xx, 0.0), axis=-1, keepdims=True)
    s_hi = jnp.sum(jnp.where(lo, 0.0, xx), axis=-1, keepdims=True)
    r = jnp.where(lo, lax.rsqrt(s_lo * (1.0 / HEAD) + EPS), lax.rsqrt(s_hi * (1.0 / HEAD) + EPS))
    return x * r


def _lane_lo(shape):
    return lax.broadcasted_iota(jnp.int32, shape, len(shape) - 1) < HEAD


def _even_proj_kernel(x_ref, pos_ref, g_ref, w_in_ref, aqg_ref, akg_ref, cqg_ref, ckvg_ref,
                      wuq_ref, wukv_ref, bqg_ref, bkg_ref, inv_ref,
                      qa_ref, ka_ref, va_ref, qb_ref, kb_ref, vb_ref):
    x = x_ref[...]
    h = _rms(x, x.shape[-1]) * g_ref[...]
    proj = _dot(h.astype(BF16), w_in_ref[...])
    lo = _lane_lo((1, LANES))
    for p in range(A_HEADS // 2):
        seg = proj[:, p * LANES:(p + 1) * LANES]
        qa_ref[:, p * LANES:(p + 1) * LANES] = (_pair_rms(seg, lo) * aqg_ref[...] * 0.125).astype(BF16)
    ka_ref[...] = (_pair_rms(proj[:, 512:640], lo) * akg_ref[...]).astype(BF16)
    va_ref[...] = proj[:, 640:768].astype(BF16)
    cq = _rms(proj[:, 768:1024], 256) * cqg_ref[...]
    ckv = _rms(proj[:, 1024:1152], 128) * ckvg_ref[...]
    krope = proj[:, 1152:1280]
    qall = _dot(cq.astype(BF16), wuq_ref[...])
    kvall = _dot(ckv.astype(BF16), wukv_ref[...])
    ang = pos_ref[...].astype(F32) * inv_ref[...]
    cosf = jnp.cos(ang)
    sinf = jnp.sin(ang)
    lane = lax.broadcasted_iota(jnp.int32, (1, LANES), 1)
    s_first = jnp.where((lane >= HEAD) & (lane < HEAD + B_ROPE_HALF), -sinf, 0.0)
    s_second = jnp.where((lane >= HEAD + B_ROPE_HALF) & (lane < B_QK), sinf, 0.0)

    def rope(t):
        return (t * cosf + pltpu.roll(t, LANES - B_ROPE_HALF, 1) * s_first
                + pltpu.roll(t, B_ROPE_HALF, 1) * s_second)

    scale_b = B_QK ** -0.5 * LOG2E
    for p in range(B_HEADS // 2):
        vpair = []
        for hh in (2 * p, 2 * p + 1):
            qh = qall[:, hh * LANES:(hh + 1) * LANES]
            qn = _rms(qh, B_QK) * bqg_ref[...]
            qb_ref[:, hh * LANES:(hh + 1) * LANES] = (rope(qn) * scale_b).astype(BF16)
            kvh = kvall[:, hh * LANES:(hh + 1) * LANES]
            kpre = jnp.where(lo, kvh, krope)
            kn = _rms(kpre, B_QK) * bkg_ref[...]
            kb_ref[:, hh * LANES:(hh + 1) * LANES] = rope(kn).astype(BF16)
            vpair.append(kvh)
        vb_ref[:, p * LANES:(p + 1) * LANES] = jnp.where(
            lo, pltpu.roll(vpair[0], HEAD, 1), vpair[1]).astype(BF16)


def _even_proj(x, pos, g, w_in, aqg, akg, cqg, ckvg, wuq, wukv, bqg, bkg, inv, tm):
    T, D = x.shape
    full = lambda a: pl.BlockSpec(a.shape, lambda i: (0,) * a.ndim)
    row = lambda c: pl.BlockSpec((tm, c), lambda i: (i, 0))
    outs = (512, 128, 128, 1024, 1024, 512)
    return pl.pallas_call(
        _even_proj_kernel,
        grid=(T // tm,),
        in_specs=[row(D), row(1)] + [full(a) for a in (g, w_in, aqg, akg, cqg, ckvg, wuq, wukv, bqg, bkg, inv)],
        out_specs=[row(c) for c in outs],
        out_shape=[jax.ShapeDtypeStruct((T, c), BF16) for c in outs],
        compiler_params=pltpu.CompilerParams(dimension_semantics=("arbitrary",), vmem_limit_bytes=VMEM_LIMIT),
        name="even_proj",
    )(x, pos, g, w_in, aqg, akg, cqg, ckvg, wuq, wukv, bqg, bkg, inv)


def _odd_proj_kernel(x_ref, g_ref, w_in_ref, dqg_ref, dkg_ref,
                     cq_ref, ck_ref, cv_ref, dq_ref, dk_ref, dv_ref):
    x = x_ref[...]
    h = _rms(x, x.shape[-1]) * g_ref[...]
    proj = _dot(h.astype(BF16), w_in_ref[...])
    lo = _lane_lo((1, LANES))
    qscale = 0.125 * LOG2E
    cq_ref[...] = (proj[:, 0:512] * qscale).astype(BF16)
    ck_ref[...] = proj[:, 512:1024].astype(BF16)
    cv_ref[...] = proj[:, 1024:1536].astype(BF16)
    for hh in range(D_HEADS):
        sl = slice(hh * LANES, (hh + 1) * LANES)
        dq_ref[:, sl] = (_pair_rms(proj[:, 1536 + hh * LANES:1536 + (hh + 1) * LANES], lo)
                         * dqg_ref[...] * qscale).astype(BF16)
        dk_ref[:, sl] = (_pair_rms(proj[:, 2048 + hh * LANES:2048 + (hh + 1) * LANES], lo)
                         * dkg_ref[...]).astype(BF16)
    dv_ref[...] = proj[:, 2560:3072].astype(BF16)


def _odd_proj(x, g, w_in, dqg, dkg, tm):
    T, D = x.shape
    full = lambda a: pl.BlockSpec(a.shape, lambda i: (0,) * a.ndim)
    row = lambda c: pl.BlockSpec((tm, c), lambda i: (i, 0))
    return pl.pallas_call(
        _odd_proj_kernel,
        grid=(T // tm,),
        in_specs=[row(D)] + [full(a) for a in (g, w_in, dqg, dkg)],
        out_specs=[row(512)] * 6,
        out_shape=[jax.ShapeDtypeStruct((T, 512), BF16)] * 6,
        compiler_params=pltpu.CompilerParams(dimension_semantics=("arbitrary",), vmem_limit_bytes=VMEM_LIMIT),
        name="odd_proj",
    )(x, g, w_in, dqg, dkg)


def _attn_a_kernel(slope_ref, sink_ref, q_ref, k_ref, v_ref, o_ref, *, tq, win):
    p = pl.program_id(1)
    qi = pl.program_id(2)
    q0 = qi * tq
    ks = pl.multiple_of(jnp.maximum(q0 - A_WINDOW, 0), A_WINDOW)
    grp = p // (A_GROUP // 2)
    lane_half = lax.broadcasted_iota(jnp.int32, (1, LANES), 1) // HEAD
    in_grp = lane_half == grp
    k = k_ref[0, pl.ds(ks, win), :]
    v = v_ref[0, pl.ds(ks, win), :]
    qq = q_ref[0].astype(F32)
    qsw = pltpu.roll(qq, HEAD, 1)
    qpos = q0 + lax.broadcasted_iota(jnp.int32, (tq, win), 0)
    kpos = ks + lax.broadcasted_iota(jnp.int32, (tq, win), 1)
    dch = (qpos >> CHUNK_SHIFT) - (kpos >> CHUNK_SHIFT)
    allowed = (dch >= 0) & (dch <= A_WINDOW_CHUNKS)
    dist = jnp.abs(qpos - kpos).astype(F32)
    outs = []
    for e in range(2):
        hidx = 2 * p + e
        qh = jnp.where(in_grp, jnp.where(grp == e, qq, qsw), 0.0).astype(BF16)
        s = _dot_t(qh, k) - slope_ref[hidx] * dist
        s = jnp.where(allowed, s, NEG)
        sink = sink_ref[hidx]
        m = jnp.maximum(jnp.max(s, axis=-1, keepdims=True), sink)
        pr = jnp.exp(s - m)
        den = jnp.sum(pr, axis=-1, keepdims=True) + jnp.exp(sink - m)
        o = _dot(pr.astype(BF16), v) / den
        outs.append(jnp.where(grp == e, o, pltpu.roll(o, HEAD, 1)))
    o_ref[0] = jnp.where(lane_half == 0, outs[0], outs[1]).astype(BF16)


def _attn_a(slopes, sinks, qa, ka, va, tq):
    B, S, _ = qa.shape
    win = tq + A_WINDOW
    smem = pl.BlockSpec(memory_space=pltpu.SMEM)
    return pl.pallas_call(
        functools.partial(_attn_a_kernel, tq=tq, win=win),
        grid=(B, A_HEADS // 2, S // tq),
        in_specs=[smem, smem,
                  pl.BlockSpec((1, tq, LANES), lambda b, p, i: (b, i, p)),
                  pl.BlockSpec((1, S, LANES), lambda b, p, i: (b, 0, 0)),
                  pl.BlockSpec((1, S, LANES), lambda b, p, i: (b, 0, 0))],
        out_specs=pl.BlockSpec((1, tq, LANES), lambda b, p, i: (b, i, p)),
        out_shape=jax.ShapeDtypeStruct((B, S, 512), BF16),
        compiler_params=pltpu.CompilerParams(dimension_semantics=("arbitrary",) * 3, vmem_limit_bytes=VMEM_LIMIT),
        name="attn_a",
    )(slopes, sinks, qa, ka, va)


def _pipelined(n, stages):
    for step in range(n + len(stages) - 1):
        for si, stage in enumerate(stages):
            c = step - si
            if 0 <= c < n:
                stage(c)


def _walk_key_blocks(qi, nchunk, step, descending):
    first = qi * nchunk

    def full(i, carry):
        step((first - 1 - i) if descending else i, None)
        return carry

    if descending:
        for d in range(nchunk - 1, -1, -1):
            step(first + d, d)
        lax.fori_loop(0, first, full, 0)
    else:
        lax.fori_loop(0, first, full, 0)
        for d in range(nchunk):
            step(first + d, d)


def _flash_specs(S, tq, q_width, k_width, v_width):
    return dict(
        in_specs=[pl.BlockSpec((1, tq, q_width), lambda b, p, i: (b, i, p)),
                  pl.BlockSpec((1, S, k_width), lambda b, p, i: (b, 0, p)),
                  pl.BlockSpec((1, S, v_width), lambda b, p, i: (b, 0, p))],
        out_specs=pl.BlockSpec((1, tq, LANES), lambda b, p, i: (b, i, p)),
        compiler_params=pltpu.CompilerParams(dimension_semantics=("arbitrary",) * 3, vmem_limit_bytes=VMEM_LIMIT),
    )


def _chunk_causal(t):
    krow = lax.broadcasted_iota(jnp.int32, (t, t), 0)
    qcol = lax.broadcasted_iota(jnp.int32, (t, t), 1)
    return (krow >> CHUNK_SHIFT) <= (qcol >> CHUNK_SHIFT)


def _softmax_stage(s, off, lanes, e, m_sc, l_sc):
    m_prev = m_sc[e, :, lanes]
    m_new = jnp.maximum(m_prev, jnp.max(s, axis=0, keepdims=True) - off)
    alpha = jnp.exp2(m_prev - m_new)
    pr = jnp.exp2(s - (m_new + off))
    l_sc[e, :, lanes] = alpha * l_sc[e, :, lanes] + jnp.sum(pr, axis=0, keepdims=True)
    m_sc[e, :, lanes] = m_new
    return pr.astype(BF16), alpha


def _attn_b_kernel(q_ref, k_ref, v_ref, o_ref, m_sc, l_sc, acc_sc, *, tq, t):
    qi = pl.program_id(2)
    nchunk = tq // t
    chains = [(e, c) for c in range(nchunk) for e in range(2)]
    m_sc[...] = jnp.full(m_sc.shape, NEG, F32)
    l_sc[...] = jnp.zeros(l_sc.shape, F32)
    acc_sc[...] = jnp.zeros(acc_sc.shape, F32)

    def step(j, d):
        rows = pl.ds(pl.multiple_of(j * t, t), t)
        v = v_ref[0, rows, :]
        live = [ch for ch in chains if d is None or ch[1] >= d]
        st = [dict() for _ in live]

        def scores(i):
            e, c = live[i]
            st[i]["s"] = _dot_t(k_ref[0, rows, e * LANES:(e + 1) * LANES],
                                q_ref[0, c * t:(c + 1) * t, e * LANES:(e + 1) * LANES])

        def softmax(i):
            e, c = live[i]
            s = st[i].pop("s")
            if d is not None and c == d:
                s = jnp.where(_chunk_causal(t), s, NEG)
            st[i]["p"], st[i]["alpha"] = _softmax_stage(s, 0.0, slice(c * t, (c + 1) * t), e, m_sc, l_sc)

        def values(i):
            e, c = live[i]
            lanes = slice(c * t, (c + 1) * t)
            acc_sc[e, :, lanes] = st[i]["alpha"] * acc_sc[e, :, lanes] + _dot_tl(v, st[i].pop("p"))

        _pipelined(len(live), [scores, softmax, values])

    _walk_key_blocks(qi, nchunk, step, descending=False)
    first = lax.broadcasted_iota(jnp.int32, (LANES, 1), 0) < HEAD
    o_t = jnp.where(first, acc_sc[0] / l_sc[0], acc_sc[1] / l_sc[1])
    o_ref[0] = o_t.T.astype(BF16)


def _attn_b(qb, kb, vb, tq, t):
    B, S, _ = qb.shape
    return pl.pallas_call(
        functools.partial(_attn_b_kernel, tq=tq, t=t),
        grid=(B, B_HEADS // 2, S // tq),
        out_shape=jax.ShapeDtypeStruct((B, S, 512), BF16),
        scratch_shapes=[pltpu.VMEM((2, 1, tq), F32), pltpu.VMEM((2, 1, tq), F32), pltpu.VMEM((2, LANES, tq), F32)],
        name="attn_b",
        **_flash_specs(S, tq, 2 * LANES, 2 * LANES, LANES),
    )(qb, kb, vb)


def _attn_c_kernel(q_ref, k_ref, v_ref, o_ref, r_sc, acc_sc, *, tq, t):
    qi = pl.program_id(2)
    nchunk = tq // t
    chains = [(e, c) for c in range(nchunk) for e in range(2)]
    krow = lax.broadcasted_iota(jnp.int32, (t, t), 0)
    qcol = lax.broadcasted_iota(jnp.int32, (t, t), 1)
    later = jnp.where(qcol > krow, 1.0, 0.0).astype(BF16)
    lane_lo = _lane_lo((1, LANES))
    r_sc[...] = jnp.zeros(r_sc.shape, F32)
    acc_sc[...] = jnp.zeros(acc_sc.shape, F32)

    def step(j, d):
        rows = pl.ds(pl.multiple_of(j * t, t), t)
        k = k_ref[0, rows, :]
        v = v_ref[0, rows, :]
        live = [ch for ch in chains if d is None or ch[1] >= d]
        st = [dict() for _ in live]

        def scores(i):
            e, c = live[i]
            qq = q_ref[0, c * t:(c + 1) * t, :]
            st[i]["z"] = _dot_t(k, jnp.where(lane_lo == (e == 0), qq, jnp.zeros_like(qq)))

        def softplus(i):
            e, c = live[i]
            z = st[i].pop("z")
            sp = jnp.maximum(z, 0.0) + jnp.log2(1.0 + jnp.exp2(-jnp.abs(z)))
            st[i]["zs"] = z - sp
            if d is not None and c == d:
                sp = jnp.where(krow < qcol, sp, 0.0)
            hi = sp.astype(BF16)
            st[i]["hi"] = hi
            st[i]["lo"] = (sp - hi.astype(F32)).astype(BF16)
            st[i]["tot"] = jnp.sum(sp, axis=0, keepdims=True)

        def suffix(i):
            st[i]["suf"] = _dot(later, st[i].pop("hi")) + _dot(later, st[i].pop("lo"))

        def weights(i):
            e, c = live[i]
            lanes = slice(c * t, (c + 1) * t)
            a = jnp.exp2(st[i].pop("zs") - st[i].pop("suf") - r_sc[e, :, lanes])
            if d is not None and c == d:
                a = jnp.where(krow < qcol, a, 0.0)
            st[i]["a"] = a.astype(BF16)
            r_sc[e, :, lanes] = r_sc[e, :, lanes] + st[i].pop("tot")

        def values(i):
            e, c = live[i]
            lanes = slice(c * t, (c + 1) * t)
            acc_sc[e, :, lanes] = acc_sc[e, :, lanes] + _dot_tl(v, st[i].pop("a"))

        _pipelined(len(live), [scores, softplus, suffix, weights, values])

    _walk_key_blocks(qi, nchunk, step, descending=True)
    first = lax.broadcasted_iota(jnp.int32, (LANES, 1), 0) < HEAD
    o_ref[0] = jnp.where(first, acc_sc[0], acc_sc[1]).T.astype(BF16)


def _attn_c(cq, ck, cv, tq, t):
    B, S, _ = cq.shape
    return pl.pallas_call(
        functools.partial(_attn_c_kernel, tq=tq, t=t),
        grid=(B, C_HEADS // 2, S // tq),
        out_shape=jax.ShapeDtypeStruct((B, S, 512), BF16),
        scratch_shapes=[pltpu.VMEM((2, 1, tq), F32), pltpu.VMEM((2, LANES, tq), F32)],
        name="attn_c",
        **_flash_specs(S, tq, LANES, LANES, LANES),
    )(cq, ck, cv)


def _attn_d_kernel(slope_ref, q_ref, k_ref, v_ref, lam_ref, subln_ref, o_ref, m_sc, l_sc, acc_sc,
                   *, tq, t, lambda_init):
    h = pl.program_id(1)
    qi = pl.program_id(2)
    nchunk = tq // t
    chains = [(e, c) for c in range(nchunk) for e in range(2)]
    slope = slope_ref[h] * LOG2E
    lane_lo = _lane_lo((1, LANES))
    key_bias = slope * lax.broadcasted_iota(jnp.int32, (t, LANES), 0).astype(F32)
    m_sc[...] = jnp.full(m_sc.shape, NEG, F32)
    l_sc[...] = jnp.zeros(l_sc.shape, F32)
    acc_sc[...] = jnp.zeros(acc_sc.shape, F32)

    def step(j, d):
        rows = pl.ds(pl.multiple_of(j * t, t), t)
        k = k_ref[0, rows, :]
        v = v_ref[0, rows, :]
        live = [ch for ch in chains if d is None or ch[1] >= d]
        st = [dict() for _ in live]

        def scores(i):
            e, c = live[i]
            qq = q_ref[0, c * t:(c + 1) * t, :]
            st[i]["s"] = _dot_t(k, jnp.where(lane_lo == (e == 0), qq, jnp.zeros_like(qq)))

        def softmax(i):
            e, c = live[i]
            s = st[i].pop("s")
            if d is not None and c == d:
                krow = lax.broadcasted_iota(jnp.int32, (t, t), 0)
                qcol = lax.broadcasted_iota(jnp.int32, (t, t), 1)
                s = s + slope * jnp.minimum(krow, 2 * qcol - krow).astype(F32)
                s = jnp.where(_chunk_causal(t), s, NEG)
                off = 0.0
            else:
                s = jnp.concatenate([s[:, b * LANES:(b + 1) * LANES] + key_bias for b in range(t // LANES)], axis=1)
                off = slope * ((qi * nchunk + c - j) * t).astype(F32) if d is None else slope * float((c - d) * t)
            st[i]["p"], st[i]["alpha"] = _softmax_stage(s, off, slice(c * t, (c + 1) * t), e, m_sc, l_sc)

        def values(i):
            e, c = live[i]
            lanes = slice(c * t, (c + 1) * t)
            acc_sc[e, :, lanes] = st[i]["alpha"] * acc_sc[e, :, lanes] + _dot_tl(v, st[i].pop("p"))

        _pipelined(len(live), [scores, softmax, values])

    _walk_key_blocks(qi, nchunk, step, descending=False)
    lf = lam_ref[...]
    lam = (jnp.exp(jnp.sum(lf[0:1] * lf[1:2], axis=-1, keepdims=True))
           - jnp.exp(jnp.sum(lf[2:3] * lf[3:4], axis=-1, keepdims=True)) + lambda_init)
    o = (acc_sc[0] / l_sc[0] - lam * (acc_sc[1] / l_sc[1])).T
    o = _rms(o, LANES) * subln_ref[...] * (1.0 - lambda_init)
    o_ref[0] = o.astype(BF16)


def _attn_d(slopes, dq, dk, dv, lam, subln, lambda_init, tq, t):
    B, S, _ = dq.shape
    specs = _flash_specs(S, tq, LANES, LANES, LANES)
    whole = lambda a: pl.BlockSpec(a.shape, lambda b, h, i: (0, 0))
    specs["in_specs"] = [pl.BlockSpec(memory_space=pltpu.SMEM)] + specs["in_specs"] + [whole(lam), whole(subln)]
    return pl.pallas_call(
        functools.partial(_attn_d_kernel, tq=tq, t=t, lambda_init=lambda_init),
        grid=(B, D_HEADS, S // tq),
        out_shape=jax.ShapeDtypeStruct((B, S, 512), BF16),
        scratch_shapes=[pltpu.VMEM((2, 1, tq), F32), pltpu.VMEM((2, 1, tq), F32), pltpu.VMEM((2, LANES, tq), F32)],
        name="attn_d",
        **specs,
    )(slopes, dq, dk, dv, lam, subln)


def _out_mlp_kernel(x_ref, ma_ref, mb_ref, wo_ref, g_ref, wu_ref, wd_ref, o_ref, *, tf):
    half = ma_ref.shape[-1]
    x1 = x_ref[...] + _dot(ma_ref[...], wo_ref[0:half, :]) + _dot(mb_ref[...], wo_ref[half:2 * half, :])
    h = (_rms(x1, x1.shape[-1]) * g_ref[...]).astype(BF16)
    o_ref[...] = x1
    for f in range(wu_ref.shape[-1] // tf):
        u = jnp.maximum(_dot(h, wu_ref[:, f * tf:(f + 1) * tf]), 0.0)
        o_ref[...] += _dot((u * u).astype(BF16), wd_ref[f * tf:(f + 1) * tf, :])


def _out_mlp(x, ma, mb, wo, g, wu, wd, tm, tf):
    T, D = x.shape
    full = lambda a: pl.BlockSpec(a.shape, lambda i: (0,) * a.ndim, pipeline_mode=pl.Buffered(1))
    row = lambda c: pl.BlockSpec((tm, c), lambda i: (i, 0))
    return pl.pallas_call(
        functools.partial(_out_mlp_kernel, tf=tf),
        grid=(T // tm,),
        in_specs=[row(D), row(ma.shape[-1]), row(mb.shape[-1]), full(wo), full(g), full(wu), full(wd)],
        out_specs=row(D),
        out_shape=jax.ShapeDtypeStruct((T, D), F32),
        compiler_params=pltpu.CompilerParams(dimension_semantics=("arbitrary",), vmem_limit_bytes=VMEM_LIMIT),
        name="out_mlp",
    )(x, ma, mb, wo, g, wu, wd)


def _alibi_slopes(n):
    return 2.0 ** (-8.0 * jnp.arange(1, n + 1, dtype=F32) / n)


def _tile(n, want):
    t = min(n, want)
    assert n % t == 0, (n, t)
    return t


def kernel(x, positions, norm_mix_g, norm_ffn_g, mlp_w_up, mlp_w_down, ev_w_in, ev_w_out, a_q_norm, a_k_norm, a_sinks, b_cq_norm, b_ckv_norm, b_w_uq, b_w_ukv, b_q_norm, b_k_norm, od_w_in, od_w_out, d_q_norm, d_k_norm, d_lambda, d_subln):
    B, S, D = x.shape
    T = B * S
    depth = norm_mix_g.shape[0]
    tm_proj = _tile(T, 512)
    tm_mlp = _tile(T, 512)
    t_a = _tile(S, 512)
    tq_b, t_b = _tile(S, 1024), _tile(S, 512)
    tq_c, t_c = _tile(S, 1024), _tile(S, 256)
    tq_d, t_d = _tile(S, 1024), _tile(S, 512)
    tf = 512

    xf = x.reshape(T, D)
    pos = positions.reshape(T, 1)
    row2 = lambda a: a.reshape(1, -1).astype(F32)
    pair = lambda a: jnp.concatenate([a, a]).reshape(1, LANES).astype(F32)
    pad_qk = lambda a: jnp.pad(a.astype(F32), (0, LANES - B_QK)).reshape(1, LANES)
    inv = ROPE_THETA ** (-jnp.arange(B_ROPE_HALF, dtype=F32) / B_ROPE_HALF)
    inv_lanes = jnp.zeros((LANES,), F32).at[HEAD:HEAD + B_ROPE_HALF].set(inv).at[HEAD + B_ROPE_HALF:B_QK].set(inv)
    inv_lanes = inv_lanes.reshape(1, LANES)
    slopes_a = _alibi_slopes(A_HEADS)
    slopes_d = _alibi_slopes(D_HEADS)

    for layer in range(depth):
        j = layer // 2
        g_mix = row2(norm_mix_g[layer])
        if layer % 2 == 0:
            w_in = ev_w_in[j]
            w_in = jnp.concatenate([w_in[:, :1152], jnp.zeros((D, HEAD), F32), w_in[:, 1152:],
                                    jnp.zeros((D, LANES - B_QK), F32)], axis=1).astype(BF16)
            wuq = jnp.pad(b_w_uq[j].reshape(-1, B_HEADS, B_QK), ((0, 0), (0, 0), (0, LANES - B_QK)))
            wuq = wuq.reshape(-1, B_HEADS * LANES).astype(BF16)
            qa, ka, va, qb, kb, vb = _even_proj(
                xf, pos, g_mix, w_in, pair(a_q_norm[j]), pair(a_k_norm[j]), row2(b_cq_norm[j]),
                row2(b_ckv_norm[j]), wuq, b_w_ukv[j].astype(BF16), pad_qk(b_q_norm[j]), pad_qk(b_k_norm[j]),
                inv_lanes, tm_proj)
            r3 = lambda a: a.reshape(B, S, a.shape[-1])
            ma = _attn_a(slopes_a, a_sinks[j].astype(F32), r3(qa), r3(ka), r3(va), t_a).reshape(T, -1)
            mb = _attn_b(r3(qb), r3(kb), r3(vb), tq_b, t_b).reshape(T, -1)
            w_out = ev_w_out[j]
        else:
            lambda_init = 0.8 - 0.6 * math.exp(-0.3 * layer)
            cq, ck, cv, dq, dk, dv = _odd_proj(
                xf, g_mix, od_w_in[j].astype(BF16), d_q_norm[j].reshape(1, LANES).astype(F32),
                d_k_norm[j].reshape(1, LANES).astype(F32), tm_proj)
            r3 = lambda a: a.reshape(B, S, a.shape[-1])
            ma = _attn_c(r3(cq), r3(ck), r3(cv), tq_c, t_c).reshape(T, -1)
            mb = _attn_d(slopes_d, r3(dq), r3(dk), r3(dv), d_lambda[j].astype(F32), row2(d_subln[j]),
                         lambda_init, tq_d, t_d).reshape(T, -1)
            w_out = od_w_out[j]
        xf = _out_mlp(xf, ma, mb, w_out.astype(BF16), row2(norm_ffn_g[layer]),
                      mlp_w_up[layer].astype(BF16), mlp_w_down[layer].astype(BF16), tm_mlp, tf)
    return xf.reshape(B, S, D)
```

```python
import functools
import math

import jax
import jax.numpy as jnp
from jax import lax
from jax.experimental import pallas as pl
from jax.experimental.pallas import tpu as pltpu

F32 = jnp.float32
BF16 = jnp.bfloat16

EPS = 1e-6
CHUNK = 64
CHUNK_SHIFT = 6
LANES = 128
HEAD = 64
A_HEADS = 8
A_GROUP = 4
A_WINDOW = 128
A_WINDOW_CHUNKS = A_WINDOW // CHUNK
B_HEADS = 8
B_QK = 96
B_ROPE_HALF = 16
ROPE_THETA = 10000.0
C_HEADS = 8
D_HEADS = 4
NEG = -1e30
LOG2E = math.log2(math.e)
VMEM_LIMIT = 56 * 1024 * 1024


def _dot(a, b):
    return jnp.dot(a, b, preferred_element_type=F32)


def _dot_t(a, b):
    return lax.dot_general(a, b, (((1,), (1,)), ((), ())), preferred_element_type=F32)


def _dot_tl(a, b):
    return lax.dot_general(a, b, (((0,), (0,)), ((), ())), preferred_element_type=F32)


def _rms(x, denom):
    return x * lax.rsqrt(jnp.sum(x * x, axis=-1, keepdims=True) * (1.0 / denom) + EPS)


def _pair_rms(x, lo):
    xx = x * x
    s_lo = jnp.sum(jnp.where(lo, xx, 0.0), axis=-1, keepdims=True)
    s_hi = jnp.sum(jnp.where(lo, 0.0, xx), axis=-1, keepdims=True)
    r = jnp.where(lo, lax.rsqrt(s_lo * (1.0 / HEAD) + EPS), lax.rsqrt(s_hi * (1.0 / HEAD) + EPS))
    return x * r


def _lane_lo(shape):
    return lax.broadcasted_iota(jnp.int32, shape, len(shape) - 1) < HEAD


def _even_proj_kernel(x_ref, pos_ref, g_ref, w_in_ref, aqg_ref, akg_ref, cqg_ref, ckvg_ref,
                      wuq_ref, wukv_ref, bqg_ref, bkg_ref, inv_ref,
                      qa_ref, ka_ref, va_ref, qb_ref, kb_ref, vb_ref):
    x = x_ref[...]
    h = _rms(x, x.shape[-1]) * g_ref[...]
    proj = _dot(h.astype(BF16), w_in_ref[...])
    lo = _lane_lo((1, LANES))
    for p in range(A_HEADS // 2):
        seg = proj[:, p * LANES:(p + 1) * LANES]
        qa_ref[:, p * LANES:(p + 1) * LANES] = (_pair_rms(seg, lo) * aqg_ref[...] * 0.125).astype(BF16)
    ka_ref[...] = (_pair_rms(proj[:, 512:640], lo) * akg_ref[...]).astype(BF16)
    va_ref[...] = proj[:, 640:768].astype(BF16)
    cq = _rms(proj[:, 768:1024], 256) * cqg_ref[...]
    ckv = _rms(proj[:, 1024:1152], 128) * ckvg_ref[...]
    krope = proj[:, 1152:1280]
    qall = _dot(cq.astype(BF16), wuq_ref[...])
    kvall = _dot(ckv.astype(BF16), wukv_ref[...])
    ang = pos_ref[...].astype(F32) * inv_ref[...]
    cosf = jnp.cos(ang)
    sinf = jnp.sin(ang)
    lane = lax.broadcasted_iota(jnp.int32, (1, LANES), 1)
    s_first = jnp.where((lane >= HEAD) & (lane < HEAD + B_ROPE_HALF), -sinf, 0.0)
    s_second = jnp.where((lane >= HEAD + B_ROPE_HALF) & (lane < B_QK), sinf, 0.0)

    def rope(t):
        return (t * cosf + pltpu.roll(t, LANES - B_ROPE_HALF, 1) * s_first
                + pltpu.roll(t, B_ROPE_HALF, 1) * s_second)

    scale_b = B_QK ** -0.5 * LOG2E
    for p in range(B_HEADS // 2):
        vpair = []
        for hh in (2 * p, 2 * p + 1):
            qh = qall[:, hh * LANES:(hh + 1) * LANES]
            qn = _rms(qh, B_QK) * bqg_ref[...]
            qb_ref[:, hh * LANES:(hh + 1) * LANES] = (rope(qn) * scale_b).astype(BF16)
            kvh = kvall[:, hh * LANES:(hh + 1) * LANES]
            kpre = jnp.where(lo, kvh, krope)
            kn = _rms(kpre, B_QK) * bkg_ref[...]
            kb_ref[:, hh * LANES:(hh + 1) * LANES] = rope(kn).astype(BF16)
            vpair.append(kvh)
        vb_ref[:, p * LANES:(p + 1) * LANES] = jnp.where(
            lo, pltpu.roll(vpair[0], HEAD, 1), vpair[1]).astype(BF16)


def _even_proj(x, pos, g, w_in, aqg, akg, cqg, ckvg, wuq, wukv, bqg, bkg, inv, tm):
    T, D = x.shape
    full = lambda a: pl.BlockSpec(a.shape, lambda i: (0,) * a.ndim)
    row = lambda c: pl.BlockSpec((tm, c), lambda i: (i, 0))
    outs = (512, 128, 128, 1024, 1024, 512)
    return pl.pallas_call(
        _even_proj_kernel,
        grid=(T // tm,),
        in_specs=[row(D), row(1)] + [full(a) for a in (g, w_in, aqg, akg, cqg, ckvg, wuq, wukv, bqg, bkg, inv)],
        out_specs=[row(c) for c in outs],
        out_shape=[jax.ShapeDtypeStruct((T, c), BF16) for c in outs],
        compiler_params=pltpu.CompilerParams(dimension_semantics=("arbitrary",), vmem_limit_bytes=VMEM_LIMIT),
        name="even_proj",
    )(x, pos, g, w_in, aqg, akg, cqg, ckvg, wuq, wukv, bqg, bkg, inv)


def _odd_proj_kernel(x_ref, g_ref, w_in_ref, dqg_ref, dkg_ref,
                     cq_ref, ck_ref, cv_ref, dq_ref, dk_ref, dv_ref):
    x = x_ref[...]
    h = _rms(x, x.shape[-1]) * g_ref[...]
    proj = _dot(h.astype(BF16), w_in_ref[...])
    lo = _lane_lo((1, LANES))
    qscale = 0.125 * LOG2E
    cq_ref[...] = (proj[:, 0:512] * qscale).astype(BF16)
    ck_ref[...] = proj[:, 512:1024].astype(BF16)
    cv_ref[...] = proj[:, 1024:1536].astype(BF16)
    for hh in range(D_HEADS):
        sl = slice(hh * LANES, (hh + 1) * LANES)
        dq_ref[:, sl] = (_pair_rms(proj[:, 1536 + hh * LANES:1536 + (hh + 1) * LANES], lo)
                         * dqg_ref[...] * qscale).astype(BF16)
        dk_ref[:, sl] = (_pair_rms(proj[:, 2048 + hh * LANES:2048 + (hh + 1) * LANES], lo)
                         * dkg_ref[...]).astype(BF16)
    dv_ref[...] = proj[:, 2560:3072].astype(BF16)


def _odd_proj(x, g, w_in, dqg, dkg, tm):
    T, D = x.shape
    full = lambda a: pl.BlockSpec(a.shape, lambda i: (0,) * a.ndim)
    row = lambda c: pl.BlockSpec((tm, c), lambda i: (i, 0))
    return pl.pallas_call(
        _odd_proj_kernel,
        grid=(T // tm,),
        in_specs=[row(D)] + [full(a) for a in (g, w_in, dqg, dkg)],
        out_specs=[row(512)] * 6,
        out_shape=[jax.ShapeDtypeStruct((T, 512), BF16)] * 6,
        compiler_params=pltpu.CompilerParams(dimension_semantics=("arbitrary",), vmem_limit_bytes=VMEM_LIMIT),
        name="odd_proj",
    )(x, g, w_in, dqg, dkg)


def _attn_a_kernel(slope_ref, sink_ref, q_ref, k_ref, v_ref, o_ref, *, tq, win):
    p = pl.program_id(1)
    qi = pl.program_id(2)
    q0 = qi * tq
    ks = pl.multiple_of(jnp.maximum(q0 - A_WINDOW, 0), A_WINDOW)
    grp = p // (A_GROUP // 2)
    lane_half = lax.broadcasted_iota(jnp.int32, (1, LANES), 1) // HEAD
    in_grp = lane_half == grp
    k = k_ref[0, pl.ds(ks, win), :]
    v = v_ref[0, pl.ds(ks, win), :]
    qq = q_ref[0].astype(F32)
    qsw = pltpu.roll(qq, HEAD, 1)
    qpos = q0 + lax.broadcasted_iota(jnp.int32, (tq, win), 0)
    kpos = ks + lax.broadcasted_iota(jnp.int32, (tq, win), 1)
    dch = (qpos >> CHUNK_SHIFT) - (kpos >> CHUNK_SHIFT)
    allowed = (dch >= 0) & (dch <= A_WINDOW_CHUNKS)
    dist = jnp.abs(qpos - kpos).astype(F32)
    outs = []
    for e in range(2):
        hidx = 2 * p + e
        qh = jnp.where(in_grp, jnp.where(grp == e, qq, qsw), 0.0).astype(BF16)
        s = _dot_t(qh, k) - slope_ref[hidx] * dist
        s = jnp.where(allowed, s, NEG)
        sink = sink_ref[hidx]
        m = jnp.maximum(jnp.max(s, axis=-1, keepdims=True), sink)
        pr = jnp.exp(s - m)
        den = jnp.sum(pr, axis=-1, keepdims=True) + jnp.exp(sink - m)
        o = _dot(pr.astype(BF16), v) / den
        outs.append(jnp.where(grp == e, o, pltpu.roll(o, HEAD, 1)))
    o_ref[0] = jnp.where(lane_half == 0, outs[0], outs[1]).astype(BF16)


def _attn_a(slopes, sinks, qa, ka, va, tq):
    B, S, _ = qa.shape
    win = tq + A_WINDOW
    smem = pl.BlockSpec(memory_space=pltpu.SMEM)
    return pl.pallas_call(
        functools.partial(_attn_a_kernel, tq=tq, win=win),
        grid=(B, A_HEADS // 2, S // tq),
        in_specs=[smem, smem,
                  pl.BlockSpec((1, tq, LANES), lambda b, p, i: (b, i, p)),
                  pl.BlockSpec((1, S, LANES), lambda b, p, i: (b, 0, 0)),
                  pl.BlockSpec((1, S, LANES), lambda b, p, i: (b, 0, 0))],
        out_specs=pl.BlockSpec((1, tq, LANES), lambda b, p, i: (b, i, p)),
        out_shape=jax.ShapeDtypeStruct((B, S, 512), BF16),
        compiler_params=pltpu.CompilerParams(dimension_semantics=("arbitrary",) * 3, vmem_limit_bytes=VMEM_LIMIT),
        name="attn_a",
    )(slopes, sinks, qa, ka, va)


def _pipelined(n, stages):
    for step in range(n + len(stages) - 1):
        for si, stage in enumerate(stages):
            c = step - si
            if 0 <= c < n:
                stage(c)


def _walk_key_blocks(qi, nchunk, step, group, still_live=None):
    first = qi * nchunk
    if still_live is None:
        def full(i, carry):
            step(i * group, None, group)
            return carry

        lax.fori_loop(0, first // group, full, 0)
        for d in range(nchunk):
            step(first + d, d, 1)
    else:
        for d in range(nchunk - 1, -1, -1):
            step(first + d, d, 1)

        def full(carry):
            step(first - 1 - carry[0] * group, None, group)
            return carry[0] + 1, still_live()

        lax.while_loop(lambda carry: (carry[0] < first // group) & carry[1], full, (0, still_live()))


def _flash_specs(S, tq, q_width, k_width, v_width):
    return dict(
        in_specs=[pl.BlockSpec((1, tq, q_width), lambda b, p, i: (b, i, p)),
                  pl.BlockSpec((1, S, k_width), lambda b, p, i: (b, 0, p)),
                  pl.BlockSpec((1, S, v_width), lambda b, p, i: (b, 0, p))],
        out_specs=pl.BlockSpec((1, tq, LANES), lambda b, p, i: (b, i, p)),
        compiler_params=pltpu.CompilerParams(dimension_semantics=("arbitrary",) * 3, vmem_limit_bytes=VMEM_LIMIT),
    )


def _chunk_causal(t):
    krow = lax.broadcasted_iota(jnp.int32, (t, t), 0)
    qcol = lax.broadcasted_iota(jnp.int32, (t, t), 1)
    return (krow >> CHUNK_SHIFT) <= (qcol >> CHUNK_SHIFT)


def _softmax_stage(s, off, lanes, e, m_sc, l_sc):
    m_prev = m_sc[e, :, lanes]
    m_new = jnp.maximum(m_prev, jnp.max(s, axis=0, keepdims=True) - off)
    alpha = jnp.exp2(m_prev - m_new)
    pr = jnp.exp2(s - (m_new + off))
    l_sc[e, :, lanes] = alpha * l_sc[e, :, lanes] + jnp.sum(pr, axis=0, keepdims=True)
    m_sc[e, :, lanes] = m_new
    return pr.astype(BF16), alpha


def _attn_b_kernel(q_ref, k_ref, v_ref, o_ref, m_sc, l_sc, acc_sc, *, tq, t):
    qi = pl.program_id(2)
    nchunk = tq // t
    chains = [(e, c) for c in range(nchunk) for e in range(2)]
    m_sc[...] = jnp.full(m_sc.shape, NEG, F32)
    l_sc[...] = jnp.zeros(l_sc.shape, F32)
    acc_sc[...] = jnp.zeros(acc_sc.shape, F32)

    def step(j, d, n):
        rows = [pl.ds(pl.multiple_of((j + g) * t, t), t) for g in range(n)]
        live = [(e, c, g) for g in range(n) for (e, c) in chains if d is None or c >= d]
        st = [dict() for _ in live]

        def scores(i):
            e, c, g = live[i]
            st[i]["s"] = _dot_t(k_ref[0, rows[g], e * LANES:(e + 1) * LANES],
                                q_ref[0, c * t:(c + 1) * t, e * LANES:(e + 1) * LANES])

        def softmax(i):
            e, c, g = live[i]
            s = st[i].pop("s")
            if d is not None and c == d:
                s = jnp.where(_chunk_causal(t), s, NEG)
            st[i]["p"], st[i]["alpha"] = _softmax_stage(s, 0.0, slice(c * t, (c + 1) * t), e, m_sc, l_sc)

        def values(i):
            e, c, g = live[i]
            lanes = slice(c * t, (c + 1) * t)
            acc_sc[e, :, lanes] = (st[i]["alpha"] * acc_sc[e, :, lanes]
                                   + _dot_tl(v_ref[0, rows[g], :], st[i].pop("p")))

        _pipelined(len(live), [scores, softmax, values])

    _walk_key_blocks(qi, nchunk, step, group=nchunk)
    first = lax.broadcasted_iota(jnp.int32, (LANES, 1), 0) < HEAD
    o_t = jnp.where(first, acc_sc[0] / l_sc[0], acc_sc[1] / l_sc[1])
    o_ref[0] = o_t.T.astype(BF16)


def _attn_b(qb, kb, vb, tq, t):
    B, S, _ = qb.shape
    return pl.pallas_call(
        functools.partial(_attn_b_kernel, tq=tq, t=t),
        grid=(B, B_HEADS // 2, S // tq),
        out_shape=jax.ShapeDtypeStruct((B, S, 512), BF16),
        scratch_shapes=[pltpu.VMEM((2, 1, tq), F32), pltpu.VMEM((2, 1, tq), F32), pltpu.VMEM((2, LANES, tq), F32)],
        name="attn_b",
        **_flash_specs(S, tq, 2 * LANES, 2 * LANES, LANES),
    )(qb, kb, vb)


def _attn_c_kernel(q_ref, k_ref, v_ref, o_ref, r_sc, acc_sc, *, tq, t):
    qi = pl.program_id(2)
    nchunk = tq // t
    chains = [(e, c) for c in range(nchunk) for e in range(2)]
    krow = lax.broadcasted_iota(jnp.int32, (t, t), 0)
    qcol = lax.broadcasted_iota(jnp.int32, (t, t), 1)
    from_here = jnp.where(qcol >= krow, 1.0, 0.0).astype(BF16)
    from_here = jnp.concatenate([from_here, from_here], axis=1)
    lane_lo = _lane_lo((1, LANES))
    r_sc[...] = jnp.zeros(r_sc.shape, F32)
    acc_sc[...] = jnp.zeros(acc_sc.shape, F32)

    def step(j, d, n):
        rows = [pl.ds(pl.multiple_of((j - g) * t, t), t) for g in range(n)]
        live = [(e, c, g) for g in range(n) for (e, c) in chains if d is None or c >= d]
        st = [dict() for _ in live]

        def scores(i):
            e, c, g = live[i]
            qq = q_ref[0, c * t:(c + 1) * t, :]
            st[i]["z"] = _dot_t(k_ref[0, rows[g], :], jnp.where(lane_lo == (e == 0), qq, jnp.zeros_like(qq)))

        def softplus(i):
            e, c, g = live[i]
            z = st[i]["z"]
            sp = jnp.maximum(z, jnp.log2(1.0 + jnp.exp2(jnp.minimum(z, 126.0))))
            if d is not None and c == d:
                sp = jnp.where(krow < qcol, sp, 0.0)
            hi = sp.astype(BF16)
            lo = (sp - hi.astype(F32)).astype(BF16)
            st[i]["hilo"] = jnp.concatenate([hi, lo], axis=0)

        def suffix(i):
            st[i]["suf"] = _dot(from_here, st[i].pop("hilo"))

        def weights(i):
            e, c, g = live[i]
            lanes = slice(c * t, (c + 1) * t)
            suf = st[i].pop("suf")
            a = jnp.exp2(st[i].pop("z") - suf)
            if d is not None and c == d:
                a = jnp.where(krow < qcol, a, 0.0)
            st[i]["a"] = a.astype(BF16)
            r_prev = r_sc[e, :, lanes]
            st[i]["w"] = jnp.exp2(-r_prev)
            r_sc[e, :, lanes] = r_prev + suf[0:1, :]

        def values(i):
            e, c, g = live[i]
            lanes = slice(c * t, (c + 1) * t)
            acc_sc[e, :, lanes] = (acc_sc[e, :, lanes]
                                   + st[i].pop("w") * _dot_tl(v_ref[0, rows[g], :], st[i].pop("a")))

        _pipelined(len(live), [scores, softplus, suffix, weights, values])

    def still_live():
        return jnp.max(jnp.exp2(-r_sc[...])) > 0.0

    _walk_key_blocks(qi, nchunk, step, group=1, still_live=still_live)
    first = lax.broadcasted_iota(jnp.int32, (LANES, 1), 0) < HEAD
    o_ref[0] = jnp.where(first, acc_sc[0], acc_sc[1]).T.astype(BF16)


def _attn_c(cq, ck, cv, tq, t):
    B, S, _ = cq.shape
    return pl.pallas_call(
        functools.partial(_attn_c_kernel, tq=tq, t=t),
        grid=(B, C_HEADS // 2, S // tq),
        out_shape=jax.ShapeDtypeStruct((B, S, 512), BF16),
        scratch_shapes=[pltpu.VMEM((2, 1, tq), F32), pltpu.VMEM((2, LANES, tq), F32)],
        name="attn_c",
        **_flash_specs(S, tq, LANES, LANES, LANES),
    )(cq, ck, cv)


def _attn_d_kernel(slope_ref, q_ref, k_ref, v_ref, lam_ref, subln_ref, o_ref, m_sc, l_sc, acc_sc,
                   *, tq, t, lambda_init):
    h = pl.program_id(1)
    qi = pl.program_id(2)
    nchunk = tq // t
    chains = [(e, c) for c in range(nchunk) for e in range(2)]
    slope = slope_ref[h] * LOG2E
    lane_lo = _lane_lo((1, LANES))
    key_bias = slope * lax.broadcasted_iota(jnp.int32, (t, LANES), 0).astype(F32)
    m_sc[...] = jnp.full(m_sc.shape, NEG, F32)
    l_sc[...] = jnp.zeros(l_sc.shape, F32)
    acc_sc[...] = jnp.zeros(acc_sc.shape, F32)

    def step(j, d, n):
        rows = [pl.ds(pl.multiple_of((j + g) * t, t), t) for g in range(n)]
        live = [(e, c, g) for g in range(n) for (e, c) in chains if d is None or c >= d]
        st = [dict() for _ in live]

        def scores(i):
            e, c, g = live[i]
            qq = q_ref[0, c * t:(c + 1) * t, :]
            st[i]["s"] = _dot_t(k_ref[0, rows[g], :], jnp.where(lane_lo == (e == 0), qq, jnp.zeros_like(qq)))

        def softmax(i):
            e, c, g = live[i]
            s = st[i].pop("s")
            if d is not None and c == d:
                krow = lax.broadcasted_iota(jnp.int32, (t, t), 0)
                qcol = lax.broadcasted_iota(jnp.int32, (t, t), 1)
                s = s + slope * jnp.minimum(krow, 2 * qcol - krow).astype(F32)
                s = jnp.where(_chunk_causal(t), s, NEG)
                off = 0.0
            else:
                s = jnp.concatenate([s[:, b * LANES:(b + 1) * LANES] + key_bias for b in range(t // LANES)], axis=1)
                off = (slope * ((qi * nchunk + c - j - g) * t).astype(F32) if d is None
                       else slope * float((c - d) * t))
            st[i]["p"], st[i]["alpha"] = _softmax_stage(s, off, slice(c * t, (c + 1) * t), e, m_sc, l_sc)

        def values(i):
            e, c, g = live[i]
            lanes = slice(c * t, (c + 1) * t)
            acc_sc[e, :, lanes] = (st[i]["alpha"] * acc_sc[e, :, lanes]
                                   + _dot_tl(v_ref[0, rows[g], :], st[i].pop("p")))

        _pipelined(len(live), [scores, softmax, values])

    _walk_key_blocks(qi, nchunk, step, group=nchunk)
    lf = lam_ref[...]
    lam = (jnp.exp(jnp.sum(lf[0:1] * lf[1:2], axis=-1, keepdims=True))
           - jnp.exp(jnp.sum(lf[2:3] * lf[3:4], axis=-1, keepdims=True)) + lambda_init)
    o = (acc_sc[0] / l_sc[0] - lam * (acc_sc[1] / l_sc[1])).T
    o = _rms(o, LANES) * subln_ref[...] * (1.0 - lambda_init)
    o_ref[0] = o.astype(BF16)


def _attn_d(slopes, dq, dk, dv, lam, subln, lambda_init, tq, t):
    B, S, _ = dq.shape
    specs = _flash_specs(S, tq, LANES, LANES, LANES)
    whole = lambda a: pl.BlockSpec(a.shape, lambda b, h, i: (0, 0))
    specs["in_specs"] = [pl.BlockSpec(memory_space=pltpu.SMEM)] + specs["in_specs"] + [whole(lam), whole(subln)]
    return pl.pallas_call(
        functools.partial(_attn_d_kernel, tq=tq, t=t, lambda_init=lambda_init),
        grid=(B, D_HEADS, S // tq),
        out_shape=jax.ShapeDtypeStruct((B, S, 512), BF16),
        scratch_shapes=[pltpu.VMEM((2, 1, tq), F32), pltpu.VMEM((2, 1, tq), F32), pltpu.VMEM((2, LANES, tq), F32)],
        name="attn_d",
        **specs,
    )(slopes, dq, dk, dv, lam, subln)


def _out_mlp_kernel(x_ref, ma_ref, mb_ref, wo_ref, g_ref, wu_ref, wd_ref, o_ref, *, tf):
    half = ma_ref.shape[-1]
    x1 = x_ref[...] + _dot(ma_ref[...], wo_ref[0:half, :]) + _dot(mb_ref[...], wo_ref[half:2 * half, :])
    h = (_rms(x1, x1.shape[-1]) * g_ref[...]).astype(BF16)
    o_ref[...] = x1
    for f in range(wu_ref.shape[-1] // tf):
        u = jnp.maximum(_dot(h, wu_ref[:, f * tf:(f + 1) * tf]), 0.0)
        o_ref[...] += _dot((u * u).astype(BF16), wd_ref[f * tf:(f + 1) * tf, :])


def _out_mlp(x, ma, mb, wo, g, wu, wd, tm, tf):
    T, D = x.shape
    full = lambda a: pl.BlockSpec(a.shape, lambda i: (0,) * a.ndim, pipeline_mode=pl.Buffered(1))
    row = lambda c: pl.BlockSpec((tm, c), lambda i: (i, 0))
    return pl.pallas_call(
        functools.partial(_out_mlp_kernel, tf=tf),
        grid=(T // tm,),
        in_specs=[row(D), row(ma.shape[-1]), row(mb.shape[-1]), full(wo), full(g), full(wu), full(wd)],
        out_specs=row(D),
        out_shape=jax.ShapeDtypeStruct((T, D), F32),
        compiler_params=pltpu.CompilerParams(dimension_semantics=("arbitrary",), vmem_limit_bytes=VMEM_LIMIT),
        name="out_mlp",
    )(x, ma, mb, wo, g, wu, wd)


def _alibi_slopes(n):
    return 2.0 ** (-8.0 * jnp.arange(1, n + 1, dtype=F32) / n)


def _tile(n, want):
    t = min(n, want)
    assert n % t == 0, (n, t)
    return t


def kernel(x, positions, norm_mix_g, norm_ffn_g, mlp_w_up, mlp_w_down, ev_w_in, ev_w_out, a_q_norm, a_k_norm, a_sinks, b_cq_norm, b_ckv_norm, b_w_uq, b_w_ukv, b_q_norm, b_k_norm, od_w_in, od_w_out, d_q_norm, d_k_norm, d_lambda, d_subln):
    B, S, D = x.shape
    T = B * S
    depth = norm_mix_g.shape[0]
    tm_proj = _tile(T, 512)
    tm_mlp = _tile(T, 512)
    t_a = _tile(S, 512)
    tq_b, t_b = _tile(S, 1024), _tile(S, 512)
    tq_c, t_c = _tile(S, 1024), _tile(S, 256)
    tq_d, t_d = _tile(S, 1024), _tile(S, 512)
    tf = 512

    xf = x.reshape(T, D)
    pos = positions.reshape(T, 1)
    row2 = lambda a: a.reshape(1, -1).astype(F32)
    pair = lambda a: jnp.concatenate([a, a]).reshape(1, LANES).astype(F32)
    pad_qk = lambda a: jnp.pad(a.astype(F32), (0, LANES - B_QK)).reshape(1, LANES)
    inv = ROPE_THETA ** (-jnp.arange(B_ROPE_HALF, dtype=F32) / B_ROPE_HALF)
    inv_lanes = jnp.zeros((LANES,), F32).at[HEAD:HEAD + B_ROPE_HALF].set(inv).at[HEAD + B_ROPE_HALF:B_QK].set(inv)
    inv_lanes = inv_lanes.reshape(1, LANES)
    slopes_a = _alibi_slopes(A_HEADS)
    slopes_d = _alibi_slopes(D_HEADS)

    for layer in range(depth):
        j = layer // 2
        g_mix = row2(norm_mix_g[layer])
        if layer % 2 == 0:
            w_in = ev_w_in[j]
            w_in = jnp.concatenate([w_in[:, :1152], jnp.zeros((D, HEAD), F32), w_in[:, 1152:],
                                    jnp.zeros((D, LANES - B_QK), F32)], axis=1).astype(BF16)
            wuq = jnp.pad(b_w_uq[j].reshape(-1, B_HEADS, B_QK), ((0, 0), (0, 0), (0, LANES - B_QK)))
            wuq = wuq.reshape(-1, B_HEADS * LANES).astype(BF16)
            qa, ka, va, qb, kb, vb = _even_proj(
                xf, pos, g_mix, w_in, pair(a_q_norm[j]), pair(a_k_norm[j]), row2(b_cq_norm[j]),
                row2(b_ckv_norm[j]), wuq, b_w_ukv[j].astype(BF16), pad_qk(b_q_norm[j]), pad_qk(b_k_norm[j]),
                inv_lanes, tm_proj)
            r3 = lambda a: a.reshape(B, S, a.shape[-1])
            ma = _attn_a(slopes_a, a_sinks[j].astype(F32), r3(qa), r3(ka), r3(va), t_a).reshape(T, -1)
            mb = _attn_b(r3(qb), r3(kb), r3(vb), tq_b, t_b).reshape(T, -1)
            w_out = ev_w_out[j]
        else:
            lambda_init = 0.8 - 0.6 * math.exp(-0.3 * layer)
            cq, ck, cv, dq, dk, dv = _odd_proj(
                xf, g_mix, od_w_in[j].astype(BF16), d_q_norm[j].reshape(1, LANES).astype(F32),
                d_k_norm[j].reshape(1, LANES).astype(F32), tm_proj)
            r3 = lambda a: a.reshape(B, S, a.shape[-1])
            ma = _attn_c(r3(cq), r3(ck), r3(cv), tq_c, t_c).reshape(T, -1)
            mb = _attn_d(slopes_d, r3(dq), r3(dk), r3(dv), d_lambda[j].astype(F32), row2(d_subln[j]),
                         lambda_init, tq_d, t_d).reshape(T, -1)
            w_out = od_w_out[j]
        xf = _out_mlp(xf, ma, mb, w_out.astype(BF16), row2(norm_ffn_g[layer]),
                      mlp_w_up[layer].astype(BF16), mlp_w_down[layer].astype(BF16), tm_mlp, tf)
    return xf.reshape(B, S, D)
```

```python
import functools
import math

import jax
import jax.numpy as jnp
from jax import lax
from jax.experimental import pallas as pl
from jax.experimental.pallas import tpu as pltpu

F32 = jnp.float32
BF16 = jnp.bfloat16

EPS = 1e-6
CHUNK = 64
CHUNK_SHIFT = 6
LANES = 128
HEAD = 64
A_HEADS = 8
A_GROUP = 4
A_WINDOW = 128
A_WINDOW_CHUNKS = A_WINDOW // CHUNK
B_HEADS = 8
B_QK = 96
B_ROPE_HALF = 16
ROPE_THETA = 10000.0
C_HEADS = 8
D_HEADS = 4
NEG = -1e30
LOG2E = math.log2(math.e)
VMEM_LIMIT = 56 * 1024 * 1024


def _dot(a, b):
    return jnp.dot(a, b, preferred_element_type=F32)


def _dot_t(a, b):
    return lax.dot_general(a, b, (((1,), (1,)), ((), ())), preferred_element_type=F32)


def _dot_tl(a, b):
    return lax.dot_general(a, b, (((0,), (0,)), ((), ())), preferred_element_type=F32)


def _rms(x, denom):
    return x * lax.rsqrt(jnp.sum(x * x, axis=-1, keepdims=True) * (1.0 / denom) + EPS)


def _pair_rms(x, lo):
    xx = x * x
    s_lo = jnp.sum(jnp.where(lo, xx, 0.0), axis=-1, keepdims=True)
    s_hi = jnp.sum(jnp.where(lo, 0.0, xx), axis=-1, keepdims=True)
    r = jnp.where(lo, lax.rsqrt(s_lo * (1.0 / HEAD) + EPS), lax.rsqrt(s_hi * (1.0 / HEAD) + EPS))
    return x * r


def _lane_lo(shape):
    return lax.broadcasted_iota(jnp.int32, shape, len(shape) - 1) < HEAD


def _even_proj_kernel(x_ref, pos_ref, g_ref, w_in_ref, aqg_ref, akg_ref, cqg_ref, ckvg_ref,
                      wuq_ref, wukv_ref, bqg_ref, bkg_ref, inv_ref,
                      qa_ref, ka_ref, va_ref, qb_ref, kb_ref, vb_ref):
    x = x_ref[...]
    h = _rms(x, x.shape[-1]) * g_ref[...]
    proj = _dot(h.astype(BF16), w_in_ref[...])
    lo = _lane_lo((1, LANES))
    for p in range(A_HEADS // 2):
        seg = proj[:, p * LANES:(p + 1) * LANES]
        qa_ref[:, p * LANES:(p + 1) * LANES] = (_pair_rms(seg, lo) * aqg_ref[...] * (0.125 * LOG2E)).astype(BF16)
    ka_ref[...] = (_pair_rms(proj[:, 512:640], lo) * akg_ref[...]).astype(BF16)
    va_ref[...] = proj[:, 640:768].astype(BF16)
    cq = _rms(proj[:, 768:1024], 256) * cqg_ref[...]
    ckv = _rms(proj[:, 1024:1152], 128) * ckvg_ref[...]
    krope = proj[:, 1152:1280]
    qall = _dot(cq.astype(BF16), wuq_ref[...])
    kvall = _dot(ckv.astype(BF16), wukv_ref[...])
    ang = pos_ref[...].astype(F32) * inv_ref[...]
    cosf = jnp.cos(ang)
    sinf = jnp.sin(ang)
    lane = lax.broadcasted_iota(jnp.int32, (1, LANES), 1)
    s_first = jnp.where((lane >= HEAD) & (lane < HEAD + B_ROPE_HALF), -sinf, 0.0)
    s_second = jnp.where((lane >= HEAD + B_ROPE_HALF) & (lane < B_QK), sinf, 0.0)

    def rope(t):
        return (t * cosf + pltpu.roll(t, LANES - B_ROPE_HALF, 1) * s_first
                + pltpu.roll(t, B_ROPE_HALF, 1) * s_second)

    scale_b = B_QK ** -0.5 * LOG2E
    for p in range(B_HEADS // 2):
        vpair = []
        for hh in (2 * p, 2 * p + 1):
            qh = qall[:, hh * LANES:(hh + 1) * LANES]
            qn = _rms(qh, B_QK) * bqg_ref[...]
            qb_ref[:, hh * LANES:(hh + 1) * LANES] = (rope(qn) * scale_b).astype(BF16)
            kvh = kvall[:, hh * LANES:(hh + 1) * LANES]
            kpre = jnp.where(lo, kvh, krope)
            kn = _rms(kpre, B_QK) * bkg_ref[...]
            kb_ref[:, hh * LANES:(hh + 1) * LANES] = rope(kn).astype(BF16)
            vpair.append(kvh)
        vb_ref[:, p * LANES:(p + 1) * LANES] = jnp.where(
            lo, pltpu.roll(vpair[0], HEAD, 1), vpair[1]).astype(BF16)


def _even_proj(x, pos, g, w_in, aqg, akg, cqg, ckvg, wuq, wukv, bqg, bkg, inv, tm):
    T, D = x.shape
    full = lambda a: pl.BlockSpec(a.shape, lambda i: (0,) * a.ndim)
    row = lambda c: pl.BlockSpec((tm, c), lambda i: (i, 0))
    outs = (512, 128, 128, 1024, 1024, 512)
    return pl.pallas_call(
        _even_proj_kernel,
        grid=(T // tm,),
        in_specs=[row(D), row(1)] + [full(a) for a in (g, w_in, aqg, akg, cqg, ckvg, wuq, wukv, bqg, bkg, inv)],
        out_specs=[row(c) for c in outs],
        out_shape=[jax.ShapeDtypeStruct((T, c), BF16) for c in outs],
        compiler_params=pltpu.CompilerParams(dimension_semantics=("arbitrary",), vmem_limit_bytes=VMEM_LIMIT),
        name="even_proj",
    )(x, pos, g, w_in, aqg, akg, cqg, ckvg, wuq, wukv, bqg, bkg, inv)


def _odd_proj_kernel(x_ref, g_ref, w_in_ref, dqg_ref, dkg_ref,
                     cq_ref, ck_ref, cv_ref, dq_ref, dk_ref, dv_ref):
    x = x_ref[...]
    h = _rms(x, x.shape[-1]) * g_ref[...]
    proj = _dot(h.astype(BF16), w_in_ref[...])
    lo = _lane_lo((1, LANES))
    qscale = 0.125 * LOG2E
    cq_ref[...] = (proj[:, 0:512] * qscale).astype(BF16)
    ck_ref[...] = proj[:, 512:1024].astype(BF16)
    cv_ref[...] = proj[:, 1024:1536].astype(BF16)
    for hh in range(D_HEADS):
        sl = slice(hh * LANES, (hh + 1) * LANES)
        dq_ref[:, sl] = (_pair_rms(proj[:, 1536 + hh * LANES:1536 + (hh + 1) * LANES], lo)
                         * dqg_ref[...] * qscale).astype(BF16)
        dk_ref[:, sl] = (_pair_rms(proj[:, 2048 + hh * LANES:2048 + (hh + 1) * LANES], lo)
                         * dkg_ref[...]).astype(BF16)
    dv_ref[...] = proj[:, 2560:3072].astype(BF16)


def _odd_proj(x, g, w_in, dqg, dkg, tm):
    T, D = x.shape
    full = lambda a: pl.BlockSpec(a.shape, lambda i: (0,) * a.ndim)
    row = lambda c: pl.BlockSpec((tm, c), lambda i: (i, 0))
    return pl.pallas_call(
        _odd_proj_kernel,
        grid=(T // tm,),
        in_specs=[row(D)] + [full(a) for a in (g, w_in, dqg, dkg)],
        out_specs=[row(512)] * 6,
        out_shape=[jax.ShapeDtypeStruct((T, 512), BF16)] * 6,
        compiler_params=pltpu.CompilerParams(dimension_semantics=("arbitrary",), vmem_limit_bytes=VMEM_LIMIT),
        name="odd_proj",
    )(x, g, w_in, dqg, dkg)


def _pipelined(n, stages):
    for step in range(n + len(stages) - 1):
        for si, stage in enumerate(stages):
            c = step - si
            if 0 <= c < n:
                stage(c)


def _attn_a_kernel(slope_ref, sink_ref, q_ref, k_ref, v_ref, o_ref, *, tq, win):
    g = pl.program_id(1)
    qi = pl.program_id(2)
    q0 = qi * tq
    ks = pl.multiple_of(jnp.maximum(q0 - A_WINDOW, 0), A_WINDOW)
    lane_half = lax.broadcasted_iota(jnp.int32, (1, LANES), 1) // HEAD

    def both_halves(x):
        xf = x.astype(F32)
        return jnp.where(lane_half == g, xf, pltpu.roll(xf, HEAD, 1)).astype(BF16)

    k = both_halves(k_ref[0, pl.ds(ks, win), :])
    v = both_halves(v_ref[0, pl.ds(ks, win), :])
    kpos = ks + lax.broadcasted_iota(jnp.int32, (win, tq), 0)
    qpos = q0 + lax.broadcasted_iota(jnp.int32, (win, tq), 1)
    dch = (qpos >> CHUNK_SHIFT) - (kpos >> CHUNK_SHIFT)
    allowed = (dch >= 0) & (dch <= A_WINDOW_CHUNKS)
    negdist = -jnp.abs(qpos - kpos).astype(F32)
    st = [dict() for _ in range(A_GROUP)]

    def scores(i):
        qq = q_ref[0, :, (i // 2) * LANES:(i // 2 + 1) * LANES]
        qh = jnp.where(lane_half == i % 2, qq, jnp.zeros_like(qq))
        st[i]["s"] = _dot_t(k, qh)

    def softmax(i):
        hidx = A_GROUP * g + i
        s = jnp.where(allowed, st[i].pop("s") + (slope_ref[hidx] * LOG2E) * negdist, NEG)
        sink = sink_ref[hidx] * LOG2E
        m = jnp.maximum(jnp.max(s, axis=0, keepdims=True), sink)
        p = jnp.exp2(s - m)
        st[i]["den"] = jnp.sum(p, axis=0, keepdims=True) + jnp.exp2(sink - m)
        st[i]["p"] = p.astype(BF16)

    def values(i):
        st[i]["o"] = (_dot_tl(v, st[i].pop("p")) / st[i].pop("den"))[0:HEAD, :]

    _pipelined(A_GROUP, [scores, softmax, values])
    outs = [st[i]["o"] for i in range(A_GROUP)]
    o_ref[0] = jnp.concatenate(outs, axis=0).T.astype(BF16)


def _attn_a(slopes, sinks, qa, ka, va, tq):
    B, S, _ = qa.shape
    win = tq + A_WINDOW
    smem = pl.BlockSpec(memory_space=pltpu.SMEM)
    gw = A_GROUP * HEAD
    return pl.pallas_call(
        functools.partial(_attn_a_kernel, tq=tq, win=win),
        grid=(B, A_HEADS // A_GROUP, S // tq),
        in_specs=[smem, smem,
                  pl.BlockSpec((1, tq, gw), lambda b, g, i: (b, i, g)),
                  pl.BlockSpec((1, S, LANES), lambda b, g, i: (b, 0, 0)),
                  pl.BlockSpec((1, S, LANES), lambda b, g, i: (b, 0, 0))],
        out_specs=pl.BlockSpec((1, tq, gw), lambda b, g, i: (b, i, g)),
        out_shape=jax.ShapeDtypeStruct((B, S, 512), BF16),
        compiler_params=pltpu.CompilerParams(dimension_semantics=("arbitrary",) * 3, vmem_limit_bytes=VMEM_LIMIT),
        name="attn_a",
    )(slopes, sinks, qa, ka, va)


def _walk_key_blocks(qi, nchunk, step, group, still_live=None):
    first = qi * nchunk
    if still_live is None:
        def full(i, carry):
            step(i * group, None, group)
            return carry

        lax.fori_loop(0, first // group, full, 0)
        for d in range(nchunk):
            step(first + d, d, 1)
    else:
        for d in range(nchunk - 1, -1, -1):
            step(first + d, d, 1)

        def full(carry):
            step(first - 1 - carry[0] * group, None, group)
            return carry[0] + 1, still_live()

        lax.while_loop(lambda carry: (carry[0] < first // group) & carry[1], full, (0, still_live()))


def _flash_specs(S, tq, q_width, k_width, v_width):
    return dict(
        in_specs=[pl.BlockSpec((1, tq, q_width), lambda b, p, i: (b, i, p)),
                  pl.BlockSpec((1, S, k_width), lambda b, p, i: (b, 0, p)),
                  pl.BlockSpec((1, S, v_width), lambda b, p, i: (b, 0, p))],
        out_specs=pl.BlockSpec((1, tq, LANES), lambda b, p, i: (b, i, p)),
        compiler_params=pltpu.CompilerParams(dimension_semantics=("arbitrary",) * 3, vmem_limit_bytes=VMEM_LIMIT),
    )


def _chunk_causal(t):
    krow = lax.broadcasted_iota(jnp.int32, (t, t), 0)
    qcol = lax.broadcasted_iota(jnp.int32, (t, t), 1)
    return (krow >> CHUNK_SHIFT) <= (qcol >> CHUNK_SHIFT)


def _softmax_stage(s, off, lanes, e, m_sc, l_sc):
    m_prev = m_sc[e, :, lanes]
    m_new = jnp.maximum(m_prev, jnp.max(s, axis=0, keepdims=True) - off)
    alpha = jnp.exp2(m_prev - m_new)
    pr = jnp.exp2(s - (m_new + off))
    if l_sc is not None:
        l_sc[e, :, lanes] = alpha * l_sc[e, :, lanes] + jnp.sum(pr, axis=0, keepdims=True)
    m_sc[e, :, lanes] = m_new
    return pr.astype(BF16), alpha


def _attn_b_kernel(q_ref, k_ref, v_ref, o_ref, m_sc, acc_sc, *, tq, t):
    qi = pl.program_id(2)
    nchunk = tq // t
    chains = [(e, c) for c in range(nchunk) for e in range(2)]
    lane = lax.broadcasted_iota(jnp.int32, (1, LANES), 1)
    ones_row = (HEAD, 0)
    m_sc[...] = jnp.full(m_sc.shape, NEG, F32)
    acc_sc[...] = jnp.zeros(acc_sc.shape, F32)

    def step(j, d, n):
        rows = [pl.ds(pl.multiple_of((j + g) * t, t), t) for g in range(n)]
        live = [(e, c, g) for g in range(n) for (e, c) in chains if d is None or c >= d]
        st = [dict() for _ in live]
        vals = {}

        def values_of(e, g):
            if (e, g) not in vals:
                v = v_ref[0, rows[g], :]
                own = (lane < HEAD) if e == 0 else (lane >= HEAD)
                vals[e, g] = jnp.where(own, v, jnp.where(lane == ones_row[e], 1.0, 0.0).astype(BF16))
            return vals[e, g]

        def scores(i):
            e, c, g = live[i]
            st[i]["s"] = _dot_t(k_ref[0, rows[g], e * LANES:(e + 1) * LANES],
                                q_ref[0, c * t:(c + 1) * t, e * LANES:(e + 1) * LANES])

        def softmax(i):
            e, c, g = live[i]
            s = st[i].pop("s")
            if d is not None and c == d:
                s = jnp.where(_chunk_causal(t), s, NEG)
            st[i]["p"], st[i]["alpha"] = _softmax_stage(s, 0.0, slice(c * t, (c + 1) * t), e, m_sc, None)

        def values(i):
            e, c, g = live[i]
            lanes = slice(c * t, (c + 1) * t)
            acc_sc[e, :, lanes] = (st[i]["alpha"] * acc_sc[e, :, lanes]
                                   + _dot_tl(values_of(e, g), st[i].pop("p")))

        _pipelined(len(live), [scores, softmax, values])

    _walk_key_blocks(qi, nchunk, step, group=nchunk)
    first = lax.broadcasted_iota(jnp.int32, (LANES, 1), 0) < HEAD
    den = [acc_sc[e, ones_row[e]:ones_row[e] + 1, :] for e in range(2)]
    o_t = jnp.where(first, acc_sc[0] / den[0], acc_sc[1] / den[1])
    o_ref[0] = o_t.T.astype(BF16)


def _attn_b(qb, kb, vb, tq, t):
    B, S, _ = qb.shape
    return pl.pallas_call(
        functools.partial(_attn_b_kernel, tq=tq, t=t),
        grid=(B, B_HEADS // 2, S // tq),
        out_shape=jax.ShapeDtypeStruct((B, S, 512), BF16),
        scratch_shapes=[pltpu.VMEM((2, 1, tq), F32), pltpu.VMEM((2, LANES, tq), F32)],
        name="attn_b",
        **_flash_specs(S, tq, 2 * LANES, 2 * LANES, LANES),
    )(qb, kb, vb)


def _attn_c_kernel(q_ref, k_ref, v_ref, o_ref, r_sc, acc_sc, *, tq, t):
    qi = pl.program_id(2)
    nchunk = tq // t
    chains = [(e, c) for c in range(nchunk) for e in range(2)]
    krow = lax.broadcasted_iota(jnp.int32, (t, t), 0)
    qcol = lax.broadcasted_iota(jnp.int32, (t, t), 1)
    from_here = jnp.where(qcol >= krow, 1.0, 0.0).astype(BF16)
    from_here = jnp.concatenate([from_here, from_here], axis=1)
    lane_lo = _lane_lo((1, LANES))
    r_sc[...] = jnp.zeros(r_sc.shape, F32)
    acc_sc[...] = jnp.zeros(acc_sc.shape, F32)

    def step(j, d, n):
        rows = [pl.ds(pl.multiple_of((j - g) * t, t), t) for g in range(n)]
        live = [(e, c, g) for g in range(n) for (e, c) in chains if d is None or c >= d]
        st = [dict() for _ in live]

        def scores(i):
            e, c, g = live[i]
            qq = q_ref[0, c * t:(c + 1) * t, :]
            st[i]["z"] = _dot_t(k_ref[0, rows[g], :], jnp.where(lane_lo == (e == 0), qq, jnp.zeros_like(qq)))

        def softplus(i):
            e, c, g = live[i]
            z = st[i]["z"]
            sp = jnp.maximum(z, jnp.log2(1.0 + jnp.exp2(jnp.minimum(z, 126.0))))
            if d is not None and c == d:
                sp = jnp.where(krow < qcol, sp, 0.0)
            hi = sp.astype(BF16)
            lo = (sp - hi.astype(F32)).astype(BF16)
            st[i]["hilo"] = jnp.concatenate([hi, lo], axis=0)

        def suffix(i):
            st[i]["suf"] = _dot(from_here, st[i].pop("hilo"))

        def weights(i):
            e, c, g = live[i]
            lanes = slice(c * t, (c + 1) * t)
            suf = st[i].pop("suf")
            a = jnp.exp2(st[i].pop("z") - suf)
            if d is not None and c == d:
                a = jnp.where(krow < qcol, a, 0.0)
            st[i]["a"] = a.astype(BF16)
            r_prev = r_sc[e, :, lanes]
            st[i]["w"] = jnp.exp2(-r_prev)
            r_sc[e, :, lanes] = r_prev + suf[0:1, :]

        def values(i):
            e, c, g = live[i]
            lanes = slice(c * t, (c + 1) * t)
            acc_sc[e, :, lanes] = (acc_sc[e, :, lanes]
                                   + st[i].pop("w") * _dot_tl(v_ref[0, rows[g], :], st[i].pop("a")))

        _pipelined(len(live), [scores, softplus, suffix, weights, values])

    def still_live():
        return jnp.max(jnp.exp2(-r_sc[...])) > 0.0

    _walk_key_blocks(qi, nchunk, step, group=1, still_live=still_live)
    first = lax.broadcasted_iota(jnp.int32, (LANES, 1), 0) < HEAD
    o_ref[0] = jnp.where(first, acc_sc[0], acc_sc[1]).T.astype(BF16)


def _attn_c(cq, ck, cv, tq, t):
    B, S, _ = cq.shape
    return pl.pallas_call(
        functools.partial(_attn_c_kernel, tq=tq, t=t),
        grid=(B, C_HEADS // 2, S // tq),
        out_shape=jax.ShapeDtypeStruct((B, S, 512), BF16),
        scratch_shapes=[pltpu.VMEM((2, 1, tq), F32), pltpu.VMEM((2, LANES, tq), F32)],
        name="attn_c",
        **_flash_specs(S, tq, LANES, LANES, LANES),
    )(cq, ck, cv)


def _attn_d_kernel(slope_ref, q_ref, k_ref, v_ref, lam_ref, subln_ref, o_ref, m_sc, l_sc, acc_sc,
                   *, tq, t, lambda_init):
    h = pl.program_id(1)
    qi = pl.program_id(2)
    nchunk = tq // t
    chains = [(e, c) for c in range(nchunk) for e in range(2)]
    slope = slope_ref[h] * LOG2E
    lane_lo = _lane_lo((1, LANES))
    key_bias = slope * lax.broadcasted_iota(jnp.int32, (t, LANES), 0).astype(F32)
    m_sc[...] = jnp.full(m_sc.shape, NEG, F32)
    l_sc[...] = jnp.zeros(l_sc.shape, F32)
    acc_sc[...] = jnp.zeros(acc_sc.shape, F32)

    def step(j, d, n):
        rows = [pl.ds(pl.multiple_of((j + g) * t, t), t) for g in range(n)]
        live = [(e, c, g) for g in range(n) for (e, c) in chains if d is None or c >= d]
        st = [dict() for _ in live]

        def scores(i):
            e, c, g = live[i]
            qq = q_ref[0, c * t:(c + 1) * t, :]
            st[i]["s"] = _dot_t(k_ref[0, rows[g], :], jnp.where(lane_lo == (e == 0), qq, jnp.zeros_like(qq)))

        def softmax(i):
            e, c, g = live[i]
            s = st[i].pop("s")
            if d is not None and c == d:
                krow = lax.broadcasted_iota(jnp.int32, (t, t), 0)
                qcol = lax.broadcasted_iota(jnp.int32, (t, t), 1)
                s = s + slope * jnp.minimum(krow, 2 * qcol - krow).astype(F32)
                s = jnp.where(_chunk_causal(t), s, NEG)
                off = 0.0
            else:
                s = jnp.concatenate([s[:, b * LANES:(b + 1) * LANES] + key_bias for b in range(t // LANES)], axis=1)
                off = (slope * ((qi * nchunk + c - j - g) * t).astype(F32) if d is None
                       else slope * float((c - d) * t))
            st[i]["p"], st[i]["alpha"] = _softmax_stage(s, off, slice(c * t, (c + 1) * t), e, m_sc, l_sc)

        def values(i):
            e, c, g = live[i]
            lanes = slice(c * t, (c + 1) * t)
            acc_sc[e, :, lanes] = (st[i]["alpha"] * acc_sc[e, :, lanes]
                                   + _dot_tl(v_ref[0, rows[g], :], st[i].pop("p")))

        _pipelined(len(live), [scores, softmax, values])

    _walk_key_blocks(qi, nchunk, step, group=nchunk)
    lf = lam_ref[...]
    lam = (jnp.exp(jnp.sum(lf[0:1] * lf[1:2], axis=-1, keepdims=True))
           - jnp.exp(jnp.sum(lf[2:3] * lf[3:4], axis=-1, keepdims=True)) + lambda_init)
    o = (acc_sc[0] / l_sc[0] - lam * (acc_sc[1] / l_sc[1])).T
    o = _rms(o, LANES) * subln_ref[...] * (1.0 - lambda_init)
    o_ref[0] = o.astype(BF16)


def _attn_d(slopes, dq, dk, dv, lam, subln, lambda_init, tq, t):
    B, S, _ = dq.shape
    specs = _flash_specs(S, tq, LANES, LANES, LANES)
    whole = lambda a: pl.BlockSpec(a.shape, lambda b, h, i: (0, 0))
    specs["in_specs"] = [pl.BlockSpec(memory_space=pltpu.SMEM)] + specs["in_specs"] + [whole(lam), whole(subln)]
    return pl.pallas_call(
        functools.partial(_attn_d_kernel, tq=tq, t=t, lambda_init=lambda_init),
        grid=(B, D_HEADS, S // tq),
        out_shape=jax.ShapeDtypeStruct((B, S, 512), BF16),
        scratch_shapes=[pltpu.VMEM((2, 1, tq), F32), pltpu.VMEM((2, 1, tq), F32), pltpu.VMEM((2, LANES, tq), F32)],
        name="attn_d",
        **specs,
    )(slopes, dq, dk, dv, lam, subln)


def _out_mlp_kernel(x_ref, ma_ref, mb_ref, wo_ref, g_ref, wu_ref, wd_ref, o_ref, *, tf):
    half = ma_ref.shape[-1]
    x1 = x_ref[...] + _dot(ma_ref[...], wo_ref[0:half, :]) + _dot(mb_ref[...], wo_ref[half:2 * half, :])
    h = (_rms(x1, x1.shape[-1]) * g_ref[...]).astype(BF16)
    o_ref[...] = x1
    for f in range(wu_ref.shape[-1] // tf):
        u = jnp.maximum(_dot(h, wu_ref[:, f * tf:(f + 1) * tf]), 0.0)
        o_ref[...] += _dot((u * u).astype(BF16), wd_ref[f * tf:(f + 1) * tf, :])


def _out_mlp(x, ma, mb, wo, g, wu, wd, tm, tf):
    T, D = x.shape
    full = lambda a: pl.BlockSpec(a.shape, lambda i: (0,) * a.ndim, pipeline_mode=pl.Buffered(1))
    row = lambda c: pl.BlockSpec((tm, c), lambda i: (i, 0))
    return pl.pallas_call(
        functools.partial(_out_mlp_kernel, tf=tf),
        grid=(T // tm,),
        in_specs=[row(D), row(ma.shape[-1]), row(mb.shape[-1]), full(wo), full(g), full(wu), full(wd)],
        out_specs=row(D),
        out_shape=jax.ShapeDtypeStruct((T, D), F32),
        compiler_params=pltpu.CompilerParams(dimension_semantics=("arbitrary",), vmem_limit_bytes=VMEM_LIMIT),
        name="out_mlp",
    )(x, ma, mb, wo, g, wu, wd)


def _alibi_slopes(n):
    return 2.0 ** (-8.0 * jnp.arange(1, n + 1, dtype=F32) / n)


def _tile(n, want):
    t = min(n, want)
    assert n % t == 0, (n, t)
    return t


def kernel(x, positions, norm_mix_g, norm_ffn_g, mlp_w_up, mlp_w_down, ev_w_in, ev_w_out, a_q_norm, a_k_norm, a_sinks, b_cq_norm, b_ckv_norm, b_w_uq, b_w_ukv, b_q_norm, b_k_norm, od_w_in, od_w_out, d_q_norm, d_k_norm, d_lambda, d_subln):
    B, S, D = x.shape
    T = B * S
    depth = norm_mix_g.shape[0]
    tm_proj = _tile(T, 512)
    tm_mlp = _tile(T, 512)
    t_a = _tile(S, 256)
    tq_b, t_b = _tile(S, 1024), _tile(S, 512)
    tq_c, t_c = _tile(S, 1024), _tile(S, 256)
    tq_d, t_d = _tile(S, 1024), _tile(S, 512)
    tf = 512

    xf = x.reshape(T, D)
    pos = positions.reshape(T, 1)
    row2 = lambda a: a.reshape(1, -1).astype(F32)
    pair = lambda a: jnp.concatenate([a, a]).reshape(1, LANES).astype(F32)
    pad_qk = lambda a: jnp.pad(a.astype(F32), (0, LANES - B_QK)).reshape(1, LANES)
    inv = ROPE_THETA ** (-jnp.arange(B_ROPE_HALF, dtype=F32) / B_ROPE_HALF)
    inv_lanes = jnp.zeros((LANES,), F32).at[HEAD:HEAD + B_ROPE_HALF].set(inv).at[HEAD + B_ROPE_HALF:B_QK].set(inv)
    inv_lanes = inv_lanes.reshape(1, LANES)
    slopes_a = _alibi_slopes(A_HEADS)
    slopes_d = _alibi_slopes(D_HEADS)

    for layer in range(depth):
        j = layer // 2
        g_mix = row2(norm_mix_g[layer])
        if layer % 2 == 0:
            w_in = ev_w_in[j]
            w_in = jnp.concatenate([w_in[:, :1152], jnp.zeros((D, HEAD), F32), w_in[:, 1152:],
                                    jnp.zeros((D, LANES - B_QK), F32)], axis=1).astype(BF16)
            wuq = jnp.pad(b_w_uq[j].reshape(-1, B_HEADS, B_QK), ((0, 0), (0, 0), (0, LANES - B_QK)))
            wuq = wuq.reshape(-1, B_HEADS * LANES).astype(BF16)
            qa, ka, va, qb, kb, vb = _even_proj(
                xf, pos, g_mix, w_in, pair(a_q_norm[j]), pair(a_k_norm[j]), row2(b_cq_norm[j]),
                row2(b_ckv_norm[j]), wuq, b_w_ukv[j].astype(BF16), pad_qk(b_q_norm[j]), pad_qk(b_k_norm[j]),
                inv_lanes, tm_proj)
            r3 = lambda a: a.reshape(B, S, a.shape[-1])
            ma = _attn_a(slopes_a, a_sinks[j].astype(F32), r3(qa), r3(ka), r3(va), t_a).reshape(T, -1)
            mb = _attn_b(r3(qb), r3(kb), r3(vb), tq_b, t_b).reshape(T, -1)
            w_out = ev_w_out[j]
        else:
            lambda_init = 0.8 - 0.6 * math.exp(-0.3 * layer)
            cq, ck, cv, dq, dk, dv = _odd_proj(
                xf, g_mix, od_w_in[j].astype(BF16), d_q_norm[j].reshape(1, LANES).astype(F32),
                d_k_norm[j].reshape(1, LANES).astype(F32), tm_proj)
            r3 = lambda a: a.reshape(B, S, a.shape[-1])
            ma = _attn_c(r3(cq), r3(ck), r3(cv), tq_c, t_c).reshape(T, -1)
            mb = _attn_d(slopes_d, r3(dq), r3(dk), r3(dv), d_lambda[j].astype(F32), row2(d_subln[j]),
                         lambda_init, tq_d, t_d).reshape(T, -1)
            w_out = od_w_out[j]
        xf = _out_mlp(xf, ma, mb, w_out.astype(BF16), row2(norm_ffn_g[layer]),
                      mlp_w_up[layer].astype(BF16), mlp_w_down[layer].astype(BF16), tm_mlp, tf)
    return xf.reshape(B, S, D)
```

```python
import functools
import math

import jax
import jax.numpy as jnp
from jax import lax
from jax.experimental import pallas as pl
from jax.experimental.pallas import tpu as pltpu

F32 = jnp.float32
BF16 = jnp.bfloat16

EPS = 1e-6
CHUNK = 64
CHUNK_SHIFT = 6
LANES = 128
HEAD = 64
A_HEADS = 8
A_GROUP = 4
A_WINDOW = 128
A_WINDOW_CHUNKS = A_WINDOW // CHUNK
B_HEADS = 8
B_QK = 96
B_ROPE_HALF = 16
ROPE_THETA = 10000.0
C_HEADS = 8
D_HEADS = 4
NEG = -1e30
LOG2E = math.log2(math.e)
DEN_ROWS = 16
VMEM_LIMIT = 56 * 1024 * 1024


def _dot(a, b):
    return jnp.dot(a, b, preferred_element_type=F32)


def _dot_t(a, b):
    return lax.dot_general(a, b, (((1,), (1,)), ((), ())), preferred_element_type=F32)


def _dot_tl(a, b):
    return lax.dot_general(a, b, (((0,), (0,)), ((), ())), preferred_element_type=F32)


def _rms(x, denom):
    return x * lax.rsqrt(jnp.sum(x * x, axis=-1, keepdims=True) * (1.0 / denom) + EPS)


def _pair_rms(x, lo):
    xx = x * x
    s_lo = jnp.sum(jnp.where(lo, xx, 0.0), axis=-1, keepdims=True)
    s_hi = jnp.sum(jnp.where(lo, 0.0, xx), axis=-1, keepdims=True)
    r = jnp.where(lo, lax.rsqrt(s_lo * (1.0 / HEAD) + EPS), lax.rsqrt(s_hi * (1.0 / HEAD) + EPS))
    return x * r


def _lane_lo(shape):
    return lax.broadcasted_iota(jnp.int32, shape, len(shape) - 1) < HEAD


def _even_proj_kernel(x_ref, pos_ref, g_ref, w_in_ref, aqg_ref, akg_ref, cqg_ref, ckvg_ref,
                      wuq_ref, wukv_ref, bqg_ref, bkg_ref, inv_ref,
                      qa_ref, ka_ref, va_ref, qb_ref, kb_ref, vb_ref):
    x = x_ref[...]
    h = _rms(x, x.shape[-1]) * g_ref[...]
    proj = _dot(h.astype(BF16), w_in_ref[...])
    lo = _lane_lo((1, LANES))
    for p in range(A_HEADS // 2):
        seg = proj[:, p * LANES:(p + 1) * LANES]
        qa_ref[:, p * LANES:(p + 1) * LANES] = (_pair_rms(seg, lo) * aqg_ref[...] * (0.125 * LOG2E)).astype(BF16)
    ka_ref[...] = (_pair_rms(proj[:, 512:640], lo) * akg_ref[...]).astype(BF16)
    va_ref[...] = proj[:, 640:768].astype(BF16)
    cq = _rms(proj[:, 768:1024], 256) * cqg_ref[...]
    ckv = _rms(proj[:, 1024:1152], 128) * ckvg_ref[...]
    krope = proj[:, 1152:1280]
    qall = _dot(cq.astype(BF16), wuq_ref[...])
    kvall = _dot(ckv.astype(BF16), wukv_ref[...])
    ang = pos_ref[...].astype(F32) * inv_ref[...]
    cosf = jnp.cos(ang)
    sinf = jnp.sin(ang)
    lane = lax.broadcasted_iota(jnp.int32, (1, LANES), 1)
    s_first = jnp.where((lane >= HEAD) & (lane < HEAD + B_ROPE_HALF), -sinf, 0.0)
    s_second = jnp.where((lane >= HEAD + B_ROPE_HALF) & (lane < B_QK), sinf, 0.0)

    def rope(t):
        return (t * cosf + pltpu.roll(t, LANES - B_ROPE_HALF, 1) * s_first
                + pltpu.roll(t, B_ROPE_HALF, 1) * s_second)

    scale_b = B_QK ** -0.5 * LOG2E
    for p in range(B_HEADS // 2):
        vpair = []
        for hh in (2 * p, 2 * p + 1):
            qh = qall[:, hh * LANES:(hh + 1) * LANES]
            qn = _rms(qh, B_QK) * bqg_ref[...]
            qb_ref[:, hh * LANES:(hh + 1) * LANES] = (rope(qn) * scale_b).astype(BF16)
            kvh = kvall[:, hh * LANES:(hh + 1) * LANES]
            kpre = jnp.where(lo, kvh, krope)
            kn = _rms(kpre, B_QK) * bkg_ref[...]
            kb_ref[:, hh * LANES:(hh + 1) * LANES] = rope(kn).astype(BF16)
            vpair.append(kvh)
        vb_ref[:, p * LANES:(p + 1) * LANES] = jnp.where(
            lo, pltpu.roll(vpair[0], HEAD, 1), vpair[1]).astype(BF16)


def _even_proj(x, pos, g, w_in, aqg, akg, cqg, ckvg, wuq, wukv, bqg, bkg, inv, tm):
    T, D = x.shape
    full = lambda a: pl.BlockSpec(a.shape, lambda i: (0,) * a.ndim)
    row = lambda c: pl.BlockSpec((tm, c), lambda i: (i, 0))
    outs = (512, 128, 128, 1024, 1024, 512)
    return pl.pallas_call(
        _even_proj_kernel,
        grid=(T // tm,),
        in_specs=[row(D), row(1)] + [full(a) for a in (g, w_in, aqg, akg, cqg, ckvg, wuq, wukv, bqg, bkg, inv)],
        out_specs=[row(c) for c in outs],
        out_shape=[jax.ShapeDtypeStruct((T, c), BF16) for c in outs],
        compiler_params=pltpu.CompilerParams(dimension_semantics=("arbitrary",), vmem_limit_bytes=VMEM_LIMIT),
        name="even_proj",
    )(x, pos, g, w_in, aqg, akg, cqg, ckvg, wuq, wukv, bqg, bkg, inv)


def _odd_proj_kernel(x_ref, g_ref, w_in_ref, dqg_ref, dkg_ref,
                     cq_ref, ck_ref, cv_ref, dq_ref, dk_ref, dv_ref):
    x = x_ref[...]
    h = _rms(x, x.shape[-1]) * g_ref[...]
    proj = _dot(h.astype(BF16), w_in_ref[...])
    lo = _lane_lo((1, LANES))
    qscale = 0.125 * LOG2E
    cq_ref[...] = (proj[:, 0:512] * qscale).astype(BF16)
    ck_ref[...] = proj[:, 512:1024].astype(BF16)
    cv_ref[...] = proj[:, 1024:1536].astype(BF16)
    for hh in range(D_HEADS):
        sl = slice(hh * LANES, (hh + 1) * LANES)
        dq_ref[:, sl] = (_pair_rms(proj[:, 1536 + hh * LANES:1536 + (hh + 1) * LANES], lo)
                         * dqg_ref[...] * qscale).astype(BF16)
        dk_ref[:, sl] = (_pair_rms(proj[:, 2048 + hh * LANES:2048 + (hh + 1) * LANES], lo)
                         * dkg_ref[...]).astype(BF16)
    dv_ref[...] = proj[:, 2560:3072].astype(BF16)


def _odd_proj(x, g, w_in, dqg, dkg, tm):
    T, D = x.shape
    full = lambda a: pl.BlockSpec(a.shape, lambda i: (0,) * a.ndim)
    row = lambda c: pl.BlockSpec((tm, c), lambda i: (i, 0))
    return pl.pallas_call(
        _odd_proj_kernel,
        grid=(T // tm,),
        in_specs=[row(D)] + [full(a) for a in (g, w_in, dqg, dkg)],
        out_specs=[row(512)] * 6,
        out_shape=[jax.ShapeDtypeStruct((T, 512), BF16)] * 6,
        compiler_params=pltpu.CompilerParams(dimension_semantics=("arbitrary",), vmem_limit_bytes=VMEM_LIMIT),
        name="odd_proj",
    )(x, g, w_in, dqg, dkg)


def _pipelined(n, stages):
    for step in range(n + len(stages) - 1):
        for si, stage in enumerate(stages):
            c = step - si
            if 0 <= c < n:
                stage(c)


def _attn_a_kernel(slope_ref, sink_ref, q_ref, k_ref, v_ref, o_ref, *, tq, win):
    g = pl.program_id(1)
    qi = pl.program_id(2)
    q0 = qi * tq
    ks = pl.multiple_of(jnp.maximum(q0 - A_WINDOW, 0), A_WINDOW)
    lane_half = lax.broadcasted_iota(jnp.int32, (1, LANES), 1) // HEAD

    def both_halves(x):
        xf = x.astype(F32)
        return jnp.where(lane_half == g, xf, pltpu.roll(xf, HEAD, 1)).astype(BF16)

    k = both_halves(k_ref[0, pl.ds(ks, win), :])
    v = both_halves(v_ref[0, pl.ds(ks, win), :])
    kpos = ks + lax.broadcasted_iota(jnp.int32, (win, tq), 0)
    qpos = q0 + lax.broadcasted_iota(jnp.int32, (win, tq), 1)
    dch = (qpos >> CHUNK_SHIFT) - (kpos >> CHUNK_SHIFT)
    allowed = (dch >= 0) & (dch <= A_WINDOW_CHUNKS)
    negdist = -jnp.abs(qpos - kpos).astype(F32)
    st = [dict() for _ in range(A_GROUP)]

    def scores(i):
        qq = q_ref[0, :, (i // 2) * LANES:(i // 2 + 1) * LANES]
        qh = jnp.where(lane_half == i % 2, qq, jnp.zeros_like(qq))
        st[i]["s"] = _dot_t(k, qh)

    def softmax(i):
        hidx = A_GROUP * g + i
        s = jnp.where(allowed, st[i].pop("s") + (slope_ref[hidx] * LOG2E) * negdist, NEG)
        sink = sink_ref[hidx] * LOG2E
        m = jnp.maximum(jnp.max(s, axis=0, keepdims=True), sink)
        p = jnp.exp2(s - m)
        st[i]["den"] = jnp.sum(p, axis=0, keepdims=True) + jnp.exp2(sink - m)
        st[i]["p"] = p.astype(BF16)

    def values(i):
        st[i]["o"] = (_dot_tl(v, st[i].pop("p")) / st[i].pop("den"))[0:HEAD, :]

    _pipelined(A_GROUP, [scores, softmax, values])
    outs = [st[i]["o"] for i in range(A_GROUP)]
    o_ref[0] = jnp.concatenate(outs, axis=0).T.astype(BF16)


def _attn_a(slopes, sinks, qa, ka, va, tq):
    B, S, _ = qa.shape
    win = tq + A_WINDOW
    smem = pl.BlockSpec(memory_space=pltpu.SMEM)
    gw = A_GROUP * HEAD
    return pl.pallas_call(
        functools.partial(_attn_a_kernel, tq=tq, win=win),
        grid=(B, A_HEADS // A_GROUP, S // tq),
        in_specs=[smem, smem,
                  pl.BlockSpec((1, tq, gw), lambda b, g, i: (b, i, g)),
                  pl.BlockSpec((1, S, LANES), lambda b, g, i: (b, 0, 0)),
                  pl.BlockSpec((1, S, LANES), lambda b, g, i: (b, 0, 0))],
        out_specs=pl.BlockSpec((1, tq, gw), lambda b, g, i: (b, i, g)),
        out_shape=jax.ShapeDtypeStruct((B, S, 512), BF16),
        compiler_params=pltpu.CompilerParams(dimension_semantics=("arbitrary",) * 3, vmem_limit_bytes=VMEM_LIMIT),
        name="attn_a",
    )(slopes, sinks, qa, ka, va)


def _walk_key_blocks(qi, nchunk, step, group, still_live=None):
    first = qi * nchunk
    if still_live is None:
        def full(i, carry):
            step(i * group, None, group)
            return carry

        lax.fori_loop(0, first // group, full, 0)
        for d in range(nchunk):
            step(first + d, d, 1)
    else:
        for d in range(nchunk - 1, -1, -1):
            step(first + d, d, 1)

        def full(carry):
            step(first - 1 - carry[0] * group, None, group)
            return carry[0] + 1, still_live()

        lax.while_loop(lambda carry: (carry[0] < first // group) & carry[1], full, (0, still_live()))


def _flash_specs(S, tq, q_width, k_width, v_width):
    return dict(
        in_specs=[pl.BlockSpec((1, tq, q_width), lambda b, p, i: (b, i, p)),
                  pl.BlockSpec((1, S, k_width), lambda b, p, i: (b, 0, p)),
                  pl.BlockSpec((1, S, v_width), lambda b, p, i: (b, 0, p))],
        out_specs=pl.BlockSpec((1, tq, LANES), lambda b, p, i: (b, i, p)),
        compiler_params=pltpu.CompilerParams(dimension_semantics=("arbitrary",) * 3, vmem_limit_bytes=VMEM_LIMIT),
    )


def _chunk_causal(t):
    krow = lax.broadcasted_iota(jnp.int32, (t, t), 0)
    qcol = lax.broadcasted_iota(jnp.int32, (t, t), 1)
    return (krow >> CHUNK_SHIFT) <= (qcol >> CHUNK_SHIFT)


def _softmax_stage(s, off, lanes, e, m_sc):
    m_prev = m_sc[e, :, lanes]
    m_new = jnp.maximum(m_prev, jnp.max(s, axis=0, keepdims=True) - off)
    alpha = jnp.exp2(m_prev - m_new)
    pr = jnp.exp2(s - (m_new + off))
    m_sc[e, :, lanes] = m_new
    return pr.astype(BF16), alpha


def _attn_b_kernel(q_ref, k_ref, v_ref, o_ref, m_sc, acc_sc, *, tq, t):
    qi = pl.program_id(2)
    nchunk = tq // t
    chains = [(e, c) for c in range(nchunk) for e in range(2)]
    lane = lax.broadcasted_iota(jnp.int32, (1, LANES), 1)
    ones_row = (HEAD, 0)
    m_sc[...] = jnp.full(m_sc.shape, NEG, F32)
    acc_sc[...] = jnp.zeros(acc_sc.shape, F32)

    def step(j, d, n):
        rows = [pl.ds(pl.multiple_of((j + g) * t, t), t) for g in range(n)]
        live = [(e, c, g) for g in range(n) for (e, c) in chains if d is None or c >= d]
        st = [dict() for _ in live]
        vals = {}

        def values_of(e, g):
            if (e, g) not in vals:
                v = v_ref[0, rows[g], :]
                own = (lane < HEAD) if e == 0 else (lane >= HEAD)
                vals[e, g] = jnp.where(own, v, jnp.where(lane == ones_row[e], 1.0, 0.0).astype(BF16))
            return vals[e, g]

        def scores(i):
            e, c, g = live[i]
            st[i]["s"] = _dot_t(k_ref[0, rows[g], e * LANES:(e + 1) * LANES],
                                q_ref[0, c * t:(c + 1) * t, e * LANES:(e + 1) * LANES])

        def softmax(i):
            e, c, g = live[i]
            s = st[i].pop("s")
            if d is not None and c == d:
                s = jnp.where(_chunk_causal(t), s, NEG)
            st[i]["p"], st[i]["alpha"] = _softmax_stage(s, 0.0, slice(c * t, (c + 1) * t), e, m_sc)

        def values(i):
            e, c, g = live[i]
            lanes = slice(c * t, (c + 1) * t)
            acc_sc[e, :, lanes] = (st[i]["alpha"] * acc_sc[e, :, lanes]
                                   + _dot_tl(values_of(e, g), st[i].pop("p")))

        _pipelined(len(live), [scores, softmax, values])

    _walk_key_blocks(qi, nchunk, step, group=nchunk)
    first = lax.broadcasted_iota(jnp.int32, (LANES, 1), 0) < HEAD
    den = [acc_sc[e, ones_row[e]:ones_row[e] + 1, :] for e in range(2)]
    o_t = jnp.where(first, acc_sc[0] / den[0], acc_sc[1] / den[1])
    o_ref[0] = o_t.T.astype(BF16)


def _attn_b(qb, kb, vb, tq, t):
    B, S, _ = qb.shape
    return pl.pallas_call(
        functools.partial(_attn_b_kernel, tq=tq, t=t),
        grid=(B, B_HEADS // 2, S // tq),
        out_shape=jax.ShapeDtypeStruct((B, S, 512), BF16),
        scratch_shapes=[pltpu.VMEM((2, 1, tq), F32), pltpu.VMEM((2, LANES, tq), F32)],
        name="attn_b",
        **_flash_specs(S, tq, 2 * LANES, 2 * LANES, LANES),
    )(qb, kb, vb)


def _attn_c_kernel(q_ref, k_ref, v_ref, o_ref, r_sc, acc_sc, *, tq, t):
    qi = pl.program_id(2)
    nchunk = tq // t
    chains = [(e, c) for c in range(nchunk) for e in range(2)]
    krow = lax.broadcasted_iota(jnp.int32, (t, t), 0)
    qcol = lax.broadcasted_iota(jnp.int32, (t, t), 1)
    from_here = jnp.where(qcol >= krow, 1.0, 0.0).astype(BF16)
    from_here = jnp.concatenate([from_here, from_here], axis=1)
    lane_lo = _lane_lo((1, LANES))
    r_sc[...] = jnp.zeros(r_sc.shape, F32)
    acc_sc[...] = jnp.zeros(acc_sc.shape, F32)

    def step(j, d, n):
        rows = [pl.ds(pl.multiple_of((j - g) * t, t), t) for g in range(n)]
        live = [(e, c, g) for g in range(n) for (e, c) in chains if d is None or c >= d]
        st = [dict() for _ in live]

        def scores(i):
            e, c, g = live[i]
            qq = q_ref[0, c * t:(c + 1) * t, :]
            st[i]["z"] = _dot_t(k_ref[0, rows[g], :], jnp.where(lane_lo == (e == 0), qq, jnp.zeros_like(qq)))

        def softplus(i):
            e, c, g = live[i]
            z = st[i]["z"]
            sp = jnp.maximum(z, jnp.log2(1.0 + jnp.exp2(jnp.minimum(z, 126.0))))
            if d is not None and c == d:
                sp = jnp.where(krow < qcol, sp, 0.0)
            hi = sp.astype(BF16)
            lo = (sp - hi.astype(F32)).astype(BF16)
            st[i]["hilo"] = jnp.concatenate([hi, lo], axis=0)

        def suffix(i):
            st[i]["suf"] = _dot(from_here, st[i].pop("hilo"))

        def weights(i):
            e, c, g = live[i]
            lanes = slice(c * t, (c + 1) * t)
            suf = st[i].pop("suf")
            a = jnp.exp2(st[i].pop("z") - suf)
            if d is not None and c == d:
                a = jnp.where(krow < qcol, a, 0.0)
            st[i]["a"] = a.astype(BF16)
            r_prev = r_sc[e, :, lanes]
            st[i]["w"] = jnp.exp2(-r_prev)
            r_sc[e, :, lanes] = r_prev + suf[0:1, :]

        def values(i):
            e, c, g = live[i]
            lanes = slice(c * t, (c + 1) * t)
            acc_sc[e, :, lanes] = (acc_sc[e, :, lanes]
                                   + st[i].pop("w") * _dot_tl(v_ref[0, rows[g], :], st[i].pop("a")))

        _pipelined(len(live), [scores, softplus, suffix, weights, values])

    def still_live():
        return jnp.max(jnp.exp2(-r_sc[...])) > 0.0

    _walk_key_blocks(qi, nchunk, step, group=1, still_live=still_live)
    first = lax.broadcasted_iota(jnp.int32, (LANES, 1), 0) < HEAD
    o_ref[0] = jnp.where(first, acc_sc[0], acc_sc[1]).T.astype(BF16)


def _attn_c(cq, ck, cv, tq, t):
    B, S, _ = cq.shape
    return pl.pallas_call(
        functools.partial(_attn_c_kernel, tq=tq, t=t),
        grid=(B, C_HEADS // 2, S // tq),
        out_shape=jax.ShapeDtypeStruct((B, S, 512), BF16),
        scratch_shapes=[pltpu.VMEM((2, 1, tq), F32), pltpu.VMEM((2, LANES, tq), F32)],
        name="attn_c",
        **_flash_specs(S, tq, LANES, LANES, LANES),
    )(cq, ck, cv)


def _attn_d_kernel(slope_ref, q_ref, k_ref, v_ref, lam_ref, subln_ref, o_ref, m_sc, acc_sc,
                   *, tq, t, lambda_init):
    h = pl.program_id(1)
    qi = pl.program_id(2)
    nchunk = tq // t
    chains = [(e, c) for c in range(nchunk) for e in range(2)]
    slope = slope_ref[h] * LOG2E
    lane_lo = _lane_lo((1, LANES))
    key_bias = slope * lax.broadcasted_iota(jnp.int32, (t, LANES), 0).astype(F32)
    m_sc[...] = jnp.full(m_sc.shape, NEG, F32)
    acc_sc[...] = jnp.zeros(acc_sc.shape, F32)

    def step(j, d, n):
        rows = [pl.ds(pl.multiple_of((j + g) * t, t), t) for g in range(n)]
        live = [(e, c, g) for g in range(n) for (e, c) in chains if d is None or c >= d]
        st = [dict() for _ in live]
        vals = {}

        def values_of(g):
            if g not in vals:
                vals[g] = jnp.concatenate([v_ref[0, rows[g], :].T, jnp.ones((DEN_ROWS, t), BF16)], axis=0)
            return vals[g]

        def scores(i):
            e, c, g = live[i]
            qq = q_ref[0, c * t:(c + 1) * t, :]
            st[i]["s"] = _dot_t(k_ref[0, rows[g], :], jnp.where(lane_lo == (e == 0), qq, jnp.zeros_like(qq)))

        def softmax(i):
            e, c, g = live[i]
            s = st[i].pop("s")
            if d is not None and c == d:
                krow = lax.broadcasted_iota(jnp.int32, (t, t), 0)
                qcol = lax.broadcasted_iota(jnp.int32, (t, t), 1)
                s = s + slope * jnp.minimum(krow, 2 * qcol - krow).astype(F32)
                s = jnp.where(_chunk_causal(t), s, NEG)
                off = 0.0
            else:
                s = jnp.concatenate([s[:, b * LANES:(b + 1) * LANES] + key_bias for b in range(t // LANES)], axis=1)
                off = (slope * ((qi * nchunk + c - j - g) * t).astype(F32) if d is None
                       else slope * float((c - d) * t))
            st[i]["p"], st[i]["alpha"] = _softmax_stage(s, off, slice(c * t, (c + 1) * t), e, m_sc)

        def values(i):
            e, c, g = live[i]
            lanes = slice(c * t, (c + 1) * t)
            acc_sc[e, :, lanes] = (st[i]["alpha"] * acc_sc[e, :, lanes]
                                   + _dot(values_of(g), st[i].pop("p")))

        _pipelined(len(live), [scores, softmax, values])

    _walk_key_blocks(qi, nchunk, step, group=nchunk)
    lf = lam_ref[...]
    lam = (jnp.exp(jnp.sum(lf[0:1] * lf[1:2], axis=-1, keepdims=True))
           - jnp.exp(jnp.sum(lf[2:3] * lf[3:4], axis=-1, keepdims=True)) + lambda_init)
    sm = [acc_sc[e, 0:LANES, :] / acc_sc[e, LANES:LANES + 1, :] for e in range(2)]
    o = (sm[0] - lam * sm[1]).T
    o = _rms(o, LANES) * subln_ref[...] * (1.0 - lambda_init)
    o_ref[0] = o.astype(BF16)


def _attn_d(slopes, dq, dk, dv, lam, subln, lambda_init, tq, t):
    B, S, _ = dq.shape
    specs = _flash_specs(S, tq, LANES, LANES, LANES)
    whole = lambda a: pl.BlockSpec(a.shape, lambda b, h, i: (0, 0))
    specs["in_specs"] = [pl.BlockSpec(memory_space=pltpu.SMEM)] + specs["in_specs"] + [whole(lam), whole(subln)]
    return pl.pallas_call(
        functools.partial(_attn_d_kernel, tq=tq, t=t, lambda_init=lambda_init),
        grid=(B, D_HEADS, S // tq),
        out_shape=jax.ShapeDtypeStruct((B, S, 512), BF16),
        scratch_shapes=[pltpu.VMEM((2, 1, tq), F32), pltpu.VMEM((2, LANES + DEN_ROWS, tq), F32)],
        name="attn_d",
        **specs,
    )(slopes, dq, dk, dv, lam, subln)


def _out_mlp_kernel(x_ref, ma_ref, mb_ref, wo_ref, g_ref, wu_ref, wd_ref, o_ref, *, tf):
    half = ma_ref.shape[-1]
    x1 = x_ref[...] + _dot(ma_ref[...], wo_ref[0:half, :]) + _dot(mb_ref[...], wo_ref[half:2 * half, :])
    h = (_rms(x1, x1.shape[-1]) * g_ref[...]).astype(BF16)
    o_ref[...] = x1
    for f in range(wu_ref.shape[-1] // tf):
        u = jnp.maximum(_dot(h, wu_ref[:, f * tf:(f + 1) * tf]), 0.0)
        o_ref[...] += _dot((u * u).astype(BF16), wd_ref[f * tf:(f + 1) * tf, :])


def _out_mlp(x, ma, mb, wo, g, wu, wd, tm, tf):
    T, D = x.shape
    full = lambda a: pl.BlockSpec(a.shape, lambda i: (0,) * a.ndim, pipeline_mode=pl.Buffered(1))
    row = lambda c: pl.BlockSpec((tm, c), lambda i: (i, 0))
    return pl.pallas_call(
        functools.partial(_out_mlp_kernel, tf=tf),
        grid=(T // tm,),
        in_specs=[row(D), row(ma.shape[-1]), row(mb.shape[-1]), full(wo), full(g), full(wu), full(wd)],
        out_specs=row(D),
        out_shape=jax.ShapeDtypeStruct((T, D), F32),
        compiler_params=pltpu.CompilerParams(dimension_semantics=("arbitrary",), vmem_limit_bytes=VMEM_LIMIT),
        name="out_mlp",
    )(x, ma, mb, wo, g, wu, wd)


def _alibi_slopes(n):
    return 2.0 ** (-8.0 * jnp.arange(1, n + 1, dtype=F32) / n)


def _tile(n, want):
    t = min(n, want)
    assert n % t == 0, (n, t)
    return t


def kernel(x, positions, norm_mix_g, norm_ffn_g, mlp_w_up, mlp_w_down, ev_w_in, ev_w_out, a_q_norm, a_k_norm, a_sinks, b_cq_norm, b_ckv_norm, b_w_uq, b_w_ukv, b_q_norm, b_k_norm, od_w_in, od_w_out, d_q_norm, d_k_norm, d_lambda, d_subln):
    B, S, D = x.shape
    T = B * S
    depth = norm_mix_g.shape[0]
    tm_proj = _tile(T, 512)
    tm_mlp = _tile(T, 512)
    t_a = _tile(S, 256)
    tq_b, t_b = _tile(S, 1024), _tile(S, 512)
    tq_c, t_c = _tile(S, 1024), _tile(S, 256)
    tq_d, t_d = _tile(S, 1024), _tile(S, 512)
    tf = 512

    xf = x.reshape(T, D)
    pos = positions.reshape(T, 1)
    row2 = lambda a: a.reshape(1, -1).astype(F32)
    pair = lambda a: jnp.concatenate([a, a]).reshape(1, LANES).astype(F32)
    pad_qk = lambda a: jnp.pad(a.astype(F32), (0, LANES - B_QK)).reshape(1, LANES)
    inv = ROPE_THETA ** (-jnp.arange(B_ROPE_HALF, dtype=F32) / B_ROPE_HALF)
    inv_lanes = jnp.zeros((LANES,), F32).at[HEAD:HEAD + B_ROPE_HALF].set(inv).at[HEAD + B_ROPE_HALF:B_QK].set(inv)
    inv_lanes = inv_lanes.reshape(1, LANES)
    slopes_a = _alibi_slopes(A_HEADS)
    slopes_d = _alibi_slopes(D_HEADS)

    for layer in range(depth):
        j = layer // 2
        g_mix = row2(norm_mix_g[layer])
        if layer % 2 == 0:
            w_in = ev_w_in[j]
            w_in = jnp.concatenate([w_in[:, :1152], jnp.zeros((D, HEAD), F32), w_in[:, 1152:],
                                    jnp.zeros((D, LANES - B_QK), F32)], axis=1).astype(BF16)
            wuq = jnp.pad(b_w_uq[j].reshape(-1, B_HEADS, B_QK), ((0, 0), (0, 0), (0, LANES - B_QK)))
            wuq = wuq.reshape(-1, B_HEADS * LANES).astype(BF16)
            qa, ka, va, qb, kb, vb = _even_proj(
                xf, pos, g_mix, w_in, pair(a_q_norm[j]), pair(a_k_norm[j]), row2(b_cq_norm[j]),
                row2(b_ckv_norm[j]), wuq, b_w_ukv[j].astype(BF16), pad_qk(b_q_norm[j]), pad_qk(b_k_norm[j]),
                inv_lanes, tm_proj)
            r3 = lambda a: a.reshape(B, S, a.shape[-1])
            ma = _attn_a(slopes_a, a_sinks[j].astype(F32), r3(qa), r3(ka), r3(va), t_a).reshape(T, -1)
            mb = _attn_b(r3(qb), r3(kb), r3(vb), tq_b, t_b).reshape(T, -1)
            w_out = ev_w_out[j]
        else:
            lambda_init = 0.8 - 0.6 * math.exp(-0.3 * layer)
            cq, ck, cv, dq, dk, dv = _odd_proj(
                xf, g_mix, od_w_in[j].astype(BF16), d_q_norm[j].reshape(1, LANES).astype(F32),
                d_k_norm[j].reshape(1, LANES).astype(F32), tm_proj)
            r3 = lambda a: a.reshape(B, S, a.shape[-1])
            ma = _attn_c(r3(cq), r3(ck), r3(cv), tq_c, t_c).reshape(T, -1)
            mb = _attn_d(slopes_d, r3(dq), r3(dk), r3(dv), d_lambda[j].astype(F32), row2(d_subln[j]),
                         lambda_init, tq_d, t_d).reshape(T, -1)
            w_out = od_w_out[j]
        xf = _out_mlp(xf, ma, mb, w_out.astype(BF16), row2(norm_ffn_g[layer]),
                      mlp_w_up[layer].astype(BF16), mlp_w_down[layer].astype(BF16), tm_mlp, tf)
    return xf.reshape(B, S, D)
```

```python
import functools
import math

import jax
import jax.numpy as jnp
from jax import lax
from jax.experimental import pallas as pl
from jax.experimental.pallas import tpu as pltpu

F32 = jnp.float32
BF16 = jnp.bfloat16

EPS = 1e-6
CHUNK = 64
CHUNK_SHIFT = 6
LANES = 128
HEAD = 64
A_HEADS = 8
A_GROUP = 4
A_WINDOW = 128
A_WINDOW_CHUNKS = A_WINDOW // CHUNK
B_HEADS = 8
B_QK = 96
B_ROPE_HALF = 16
ROPE_THETA = 10000.0
C_HEADS = 8
D_HEADS = 4
NEG = -1e30
LOG2E = math.log2(math.e)
DEN_ROWS = 16
VMEM_LIMIT = 56 * 1024 * 1024


def _dot(a, b):
    return jnp.dot(a, b, preferred_element_type=F32)


def _dot_t(a, b):
    return lax.dot_general(a, b, (((1,), (1,)), ((), ())), preferred_element_type=F32)


def _dot_tl(a, b):
    return lax.dot_general(a, b, (((0,), (0,)), ((), ())), preferred_element_type=F32)


def _rms(x, denom):
    return x * lax.rsqrt(jnp.sum(x * x, axis=-1, keepdims=True) * (1.0 / denom) + EPS)


def _pair_rms(x, lo):
    xx = x * x
    s_lo = jnp.sum(jnp.where(lo, xx, 0.0), axis=-1, keepdims=True)
    s_hi = jnp.sum(jnp.where(lo, 0.0, xx), axis=-1, keepdims=True)
    r = jnp.where(lo, lax.rsqrt(s_lo * (1.0 / HEAD) + EPS), lax.rsqrt(s_hi * (1.0 / HEAD) + EPS))
    return x * r


def _lane_lo(shape):
    return lax.broadcasted_iota(jnp.int32, shape, len(shape) - 1) < HEAD


def _even_proj_kernel(x_ref, pos_ref, g_ref, w_in_ref, aqg_ref, akg_ref, cqg_ref, ckvg_ref,
                      wuq_ref, wukv_ref, bqg_ref, bkg_ref, inv_ref,
                      qa_ref, ka_ref, va_ref, qb_ref, kb_ref, vb_ref):
    x = x_ref[...]
    h = _rms(x, x.shape[-1]) * g_ref[...]
    proj = _dot(h.astype(BF16), w_in_ref[...])
    lo = _lane_lo((1, LANES))
    for p in range(A_HEADS // 2):
        seg = proj[:, p * LANES:(p + 1) * LANES]
        qa_ref[:, p * LANES:(p + 1) * LANES] = (_pair_rms(seg, lo) * aqg_ref[...] * (0.125 * LOG2E)).astype(BF16)
    ka_ref[...] = (_pair_rms(proj[:, 512:640], lo) * akg_ref[...]).astype(BF16)
    va_ref[...] = proj[:, 640:768].astype(BF16)
    cq = _rms(proj[:, 768:1024], 256) * cqg_ref[...]
    ckv = _rms(proj[:, 1024:1152], 128) * ckvg_ref[...]
    krope = proj[:, 1152:1280]
    qall = _dot(cq.astype(BF16), wuq_ref[...])
    kvall = _dot(ckv.astype(BF16), wukv_ref[...])
    ang = pos_ref[...].astype(F32) * inv_ref[...]
    cosf = jnp.cos(ang)
    sinf = jnp.sin(ang)
    lane = lax.broadcasted_iota(jnp.int32, (1, LANES), 1)
    s_first = jnp.where((lane >= HEAD) & (lane < HEAD + B_ROPE_HALF), -sinf, 0.0)
    s_second = jnp.where((lane >= HEAD + B_ROPE_HALF) & (lane < B_QK), sinf, 0.0)

    def rope(t):
        return (t * cosf + pltpu.roll(t, LANES - B_ROPE_HALF, 1) * s_first
                + pltpu.roll(t, B_ROPE_HALF, 1) * s_second)

    scale_b = B_QK ** -0.5 * LOG2E
    for p in range(B_HEADS // 2):
        vpair = []
        for hh in (2 * p, 2 * p + 1):
            qh = qall[:, hh * LANES:(hh + 1) * LANES]
            qn = _rms(qh, B_QK) * bqg_ref[...]
            qb_ref[:, hh * LANES:(hh + 1) * LANES] = (rope(qn) * scale_b).astype(BF16)
            kvh = kvall[:, hh * LANES:(hh + 1) * LANES]
            kpre = jnp.where(lo, kvh, krope)
            kn = _rms(kpre, B_QK) * bkg_ref[...]
            kb_ref[:, hh * LANES:(hh + 1) * LANES] = rope(kn).astype(BF16)
            vpair.append(kvh)
        vb_ref[:, p * LANES:(p + 1) * LANES] = jnp.where(
            lo, pltpu.roll(vpair[0], HEAD, 1), vpair[1]).astype(BF16)


def _even_proj(x, pos, g, w_in, aqg, akg, cqg, ckvg, wuq, wukv, bqg, bkg, inv, tm):
    T, D = x.shape
    full = lambda a: pl.BlockSpec(a.shape, lambda i: (0,) * a.ndim)
    row = lambda c: pl.BlockSpec((tm, c), lambda i: (i, 0))
    outs = (512, 128, 128, 1024, 1024, 512)
    return pl.pallas_call(
        _even_proj_kernel,
        grid=(T // tm,),
        in_specs=[row(D), row(1)] + [full(a) for a in (g, w_in, aqg, akg, cqg, ckvg, wuq, wukv, bqg, bkg, inv)],
        out_specs=[row(c) for c in outs],
        out_shape=[jax.ShapeDtypeStruct((T, c), BF16) for c in outs],
        compiler_params=pltpu.CompilerParams(dimension_semantics=("arbitrary",), vmem_limit_bytes=VMEM_LIMIT),
        name="even_proj",
    )(x, pos, g, w_in, aqg, akg, cqg, ckvg, wuq, wukv, bqg, bkg, inv)


def _odd_proj_kernel(x_ref, g_ref, w_in_ref, dqg_ref, dkg_ref,
                     cq_ref, ck_ref, cv_ref, dq_ref, dk_ref, dv_ref):
    x = x_ref[...]
    h = _rms(x, x.shape[-1]) * g_ref[...]
    proj = _dot(h.astype(BF16), w_in_ref[...])
    lo = _lane_lo((1, LANES))
    qscale = 0.125 * LOG2E
    cq_ref[...] = (proj[:, 0:512] * qscale).astype(BF16)
    ck_ref[...] = proj[:, 512:1024].astype(BF16)
    cv_ref[...] = proj[:, 1024:1536].astype(BF16)
    for hh in range(D_HEADS):
        sl = slice(hh * LANES, (hh + 1) * LANES)
        dq_ref[:, sl] = (_pair_rms(proj[:, 1536 + hh * LANES:1536 + (hh + 1) * LANES], lo)
                         * dqg_ref[...] * qscale).astype(BF16)
        dk_ref[:, sl] = (_pair_rms(proj[:, 2048 + hh * LANES:2048 + (hh + 1) * LANES], lo)
                         * dkg_ref[...]).astype(BF16)
    dv_ref[...] = proj[:, 2560:3072].astype(BF16)


def _odd_proj(x, g, w_in, dqg, dkg, tm):
    T, D = x.shape
    full = lambda a: pl.BlockSpec(a.shape, lambda i: (0,) * a.ndim)
    row = lambda c: pl.BlockSpec((tm, c), lambda i: (i, 0))
    return pl.pallas_call(
        _odd_proj_kernel,
        grid=(T // tm,),
        in_specs=[row(D)] + [full(a) for a in (g, w_in, dqg, dkg)],
        out_specs=[row(512)] * 6,
        out_shape=[jax.ShapeDtypeStruct((T, 512), BF16)] * 6,
        compiler_params=pltpu.CompilerParams(dimension_semantics=("arbitrary",), vmem_limit_bytes=VMEM_LIMIT),
        name="odd_proj",
    )(x, g, w_in, dqg, dkg)


def _pipelined(n, stages):
    for step in range(n + len(stages) - 1):
        for si, stage in enumerate(stages):
            c = step - si
            if 0 <= c < n:
                stage(c)


def _attn_a_kernel(slope_ref, sink_ref, q_ref, k_ref, v_ref, o_ref, *, tq, win):
    g = pl.program_id(1)
    qi = pl.program_id(2)
    q0 = qi * tq
    ks = pl.multiple_of(jnp.maximum(q0 - A_WINDOW, 0), A_WINDOW)
    lane_half = lax.broadcasted_iota(jnp.int32, (1, LANES), 1) // HEAD

    def both_halves(x):
        xf = x.astype(F32)
        return jnp.where(lane_half == g, xf, pltpu.roll(xf, HEAD, 1)).astype(BF16)

    k = both_halves(k_ref[0, pl.ds(ks, win), :])
    v = both_halves(v_ref[0, pl.ds(ks, win), :])
    kpos = ks + lax.broadcasted_iota(jnp.int32, (win, tq), 0)
    qpos = q0 + lax.broadcasted_iota(jnp.int32, (win, tq), 1)
    dch = (qpos >> CHUNK_SHIFT) - (kpos >> CHUNK_SHIFT)
    allowed = (dch >= 0) & (dch <= A_WINDOW_CHUNKS)
    negdist = -jnp.abs(qpos - kpos).astype(F32)
    st = [dict() for _ in range(A_GROUP)]

    def scores(i):
        qq = q_ref[0, :, (i // 2) * LANES:(i // 2 + 1) * LANES]
        qh = jnp.where(lane_half == i % 2, qq, jnp.zeros_like(qq))
        st[i]["s"] = _dot_t(k, qh)

    def softmax(i):
        hidx = A_GROUP * g + i
        s = jnp.where(allowed, st[i].pop("s") + (slope_ref[hidx] * LOG2E) * negdist, NEG)
        sink = sink_ref[hidx] * LOG2E
        m = jnp.maximum(jnp.max(s, axis=0, keepdims=True), sink)
        p = jnp.exp2(s - m)
        st[i]["den"] = jnp.sum(p, axis=0, keepdims=True) + jnp.exp2(sink - m)
        st[i]["p"] = p.astype(BF16)

    def values(i):
        st[i]["o"] = (_dot_tl(v, st[i].pop("p")) / st[i].pop("den"))[0:HEAD, :]

    _pipelined(A_GROUP, [scores, softmax, values])
    outs = [st[i]["o"] for i in range(A_GROUP)]
    o_ref[0] = jnp.concatenate(outs, axis=0).T.astype(BF16)


def _attn_a(slopes, sinks, qa, ka, va, tq):
    B, S, _ = qa.shape
    win = tq + A_WINDOW
    smem = pl.BlockSpec(memory_space=pltpu.SMEM)
    gw = A_GROUP * HEAD
    return pl.pallas_call(
        functools.partial(_attn_a_kernel, tq=tq, win=win),
        grid=(B, A_HEADS // A_GROUP, S // tq),
        in_specs=[smem, smem,
                  pl.BlockSpec((1, tq, gw), lambda b, g, i: (b, i, g)),
                  pl.BlockSpec((1, S, LANES), lambda b, g, i: (b, 0, 0)),
                  pl.BlockSpec((1, S, LANES), lambda b, g, i: (b, 0, 0))],
        out_specs=pl.BlockSpec((1, tq, gw), lambda b, g, i: (b, i, g)),
        out_shape=jax.ShapeDtypeStruct((B, S, 512), BF16),
        compiler_params=pltpu.CompilerParams(dimension_semantics=("arbitrary",) * 3, vmem_limit_bytes=VMEM_LIMIT),
        name="attn_a",
    )(slopes, sinks, qa, ka, va)


def _walk_key_blocks(qi, nchunk, step, group, backwards=False, still_live=None, max_blocks=None):
    first = qi * nchunk
    trips = first // group
    if not backwards:
        def full(i, carry):
            step(i * group, None, group)
            return carry

        lax.fori_loop(0, trips, full, 0)
        for d in range(nchunk):
            step(first + d, d, 1)
        return
    for d in range(nchunk - 1, -1, -1):
        step(first + d, d, 1)
    if max_blocks is not None:
        trips = jnp.minimum(trips, (max_blocks + (group - 1)) // group)
    if still_live is None:
        def full(i, carry):
            step(first - 1 - i * group, None, group)
            return carry

        lax.fori_loop(0, trips, full, 0)
    else:
        def full(carry):
            step(first - 1 - carry[0] * group, None, group)
            return carry[0] + 1, still_live()

        lax.while_loop(lambda carry: (carry[0] < trips) & carry[1], full, (0, still_live()))


def _flash_specs(S, tq, q_width, k_width, v_width):
    return dict(
        in_specs=[pl.BlockSpec((1, tq, q_width), lambda b, p, i: (b, i, p)),
                  pl.BlockSpec((1, S, k_width), lambda b, p, i: (b, 0, p)),
                  pl.BlockSpec((1, S, v_width), lambda b, p, i: (b, 0, p))],
        out_specs=pl.BlockSpec((1, tq, LANES), lambda b, p, i: (b, i, p)),
        compiler_params=pltpu.CompilerParams(dimension_semantics=("arbitrary",) * 3, vmem_limit_bytes=VMEM_LIMIT),
    )


def _chunk_causal(t):
    krow = lax.broadcasted_iota(jnp.int32, (t, t), 0)
    qcol = lax.broadcasted_iota(jnp.int32, (t, t), 1)
    return (krow >> CHUNK_SHIFT) <= (qcol >> CHUNK_SHIFT)


def _softmax_stage(s, off, lanes, e, m_sc):
    m_prev = m_sc[e, :, lanes]
    m_new = jnp.maximum(m_prev, jnp.max(s, axis=0, keepdims=True) - off)
    alpha = jnp.exp2(m_prev - m_new)
    pr = jnp.exp2(s - (m_new + off))
    m_sc[e, :, lanes] = m_new
    return pr.astype(BF16), alpha


def _attn_b_kernel(q_ref, k_ref, v_ref, o_ref, m_sc, acc_sc, *, tq, t):
    qi = pl.program_id(2)
    nchunk = tq // t
    chains = [(e, c) for c in range(nchunk) for e in range(2)]
    lane = lax.broadcasted_iota(jnp.int32, (1, LANES), 1)
    ones_row = (HEAD, 0)
    m_sc[...] = jnp.full(m_sc.shape, NEG, F32)
    acc_sc[...] = jnp.zeros(acc_sc.shape, F32)

    def step(j, d, n):
        rows = [pl.ds(pl.multiple_of((j + g) * t, t), t) for g in range(n)]
        live = [(e, c, g) for g in range(n) for (e, c) in chains if d is None or c >= d]
        st = [dict() for _ in live]
        vals = {}

        def values_of(e, g):
            if (e, g) not in vals:
                v = v_ref[0, rows[g], :]
                own = (lane < HEAD) if e == 0 else (lane >= HEAD)
                vals[e, g] = jnp.where(own, v, jnp.where(lane == ones_row[e], 1.0, 0.0).astype(BF16))
            return vals[e, g]

        def scores(i):
            e, c, g = live[i]
            st[i]["s"] = _dot_t(k_ref[0, rows[g], e * LANES:(e + 1) * LANES],
                                q_ref[0, c * t:(c + 1) * t, e * LANES:(e + 1) * LANES])

        def softmax(i):
            e, c, g = live[i]
            s = st[i].pop("s")
            if d is not None and c == d:
                s = jnp.where(_chunk_causal(t), s, NEG)
            st[i]["p"], st[i]["alpha"] = _softmax_stage(s, 0.0, slice(c * t, (c + 1) * t), e, m_sc)

        def values(i):
            e, c, g = live[i]
            lanes = slice(c * t, (c + 1) * t)
            acc_sc[e, :, lanes] = (st[i]["alpha"] * acc_sc[e, :, lanes]
                                   + _dot_tl(values_of(e, g), st[i].pop("p")))

        _pipelined(len(live), [scores, softmax, values])

    _walk_key_blocks(qi, nchunk, step, group=nchunk)
    first = lax.broadcasted_iota(jnp.int32, (LANES, 1), 0) < HEAD
    den = [acc_sc[e, ones_row[e]:ones_row[e] + 1, :] for e in range(2)]
    o_t = jnp.where(first, acc_sc[0] / den[0], acc_sc[1] / den[1])
    o_ref[0] = o_t.T.astype(BF16)


def _attn_b(qb, kb, vb, tq, t):
    B, S, _ = qb.shape
    return pl.pallas_call(
        functools.partial(_attn_b_kernel, tq=tq, t=t),
        grid=(B, B_HEADS // 2, S // tq),
        out_shape=jax.ShapeDtypeStruct((B, S, 512), BF16),
        scratch_shapes=[pltpu.VMEM((2, 1, tq), F32), pltpu.VMEM((2, LANES, tq), F32)],
        name="attn_b",
        **_flash_specs(S, tq, 2 * LANES, 2 * LANES, LANES),
    )(qb, kb, vb)


def _attn_c_kernel(q_ref, k_ref, v_ref, o_ref, r_sc, acc_sc, *, tq, t):
    qi = pl.program_id(2)
    nchunk = tq // t
    chains = [(e, c) for c in range(nchunk) for e in range(2)]
    krow = lax.broadcasted_iota(jnp.int32, (t, t), 0)
    qcol = lax.broadcasted_iota(jnp.int32, (t, t), 1)
    from_here = jnp.where(qcol >= krow, 1.0, 0.0).astype(BF16)
    from_here = jnp.concatenate([from_here, from_here], axis=1)
    lane_lo = _lane_lo((1, LANES))
    r_sc[...] = jnp.zeros(r_sc.shape, F32)
    acc_sc[...] = jnp.zeros(acc_sc.shape, F32)

    def step(j, d, n):
        rows = [pl.ds(pl.multiple_of((j - g) * t, t), t) for g in range(n)]
        live = [(e, c, g) for g in range(n) for (e, c) in chains if d is None or c >= d]
        st = [dict() for _ in live]

        def scores(i):
            e, c, g = live[i]
            qq = q_ref[0, c * t:(c + 1) * t, :]
            st[i]["z"] = _dot_t(k_ref[0, rows[g], :], jnp.where(lane_lo == (e == 0), qq, jnp.zeros_like(qq)))

        def softplus(i):
            e, c, g = live[i]
            z = st[i]["z"]
            sp = jnp.maximum(z, jnp.log2(1.0 + jnp.exp2(jnp.minimum(z, 126.0))))
            if d is not None and c == d:
                sp = jnp.where(krow < qcol, sp, 0.0)
            hi = sp.astype(BF16)
            lo = (sp - hi.astype(F32)).astype(BF16)
            st[i]["hilo"] = jnp.concatenate([hi, lo], axis=0)

        def suffix(i):
            st[i]["suf"] = _dot(from_here, st[i].pop("hilo"))

        def weights(i):
            e, c, g = live[i]
            lanes = slice(c * t, (c + 1) * t)
            suf = st[i].pop("suf")
            a = jnp.exp2(st[i].pop("z") - suf)
            if d is not None and c == d:
                a = jnp.where(krow < qcol, a, 0.0)
            st[i]["a"] = a.astype(BF16)
            r_prev = r_sc[e, :, lanes]
            st[i]["w"] = jnp.exp2(-r_prev)
            r_sc[e, :, lanes] = r_prev + suf[0:1, :]

        def values(i):
            e, c, g = live[i]
            lanes = slice(c * t, (c + 1) * t)
            acc_sc[e, :, lanes] = (acc_sc[e, :, lanes]
                                   + st[i].pop("w") * _dot_tl(v_ref[0, rows[g], :], st[i].pop("a")))

        _pipelined(len(live), [scores, softplus, suffix, weights, values])

    def still_live():
        return jnp.max(jnp.exp2(-r_sc[...])) > 0.0

    _walk_key_blocks(qi, nchunk, step, group=1, backwards=True, still_live=still_live)
    first = lax.broadcasted_iota(jnp.int32, (LANES, 1), 0) < HEAD
    o_ref[0] = jnp.where(first, acc_sc[0], acc_sc[1]).T.astype(BF16)


def _attn_c(cq, ck, cv, tq, t):
    B, S, _ = cq.shape
    return pl.pallas_call(
        functools.partial(_attn_c_kernel, tq=tq, t=t),
        grid=(B, C_HEADS // 2, S // tq),
        out_shape=jax.ShapeDtypeStruct((B, S, 512), BF16),
        scratch_shapes=[pltpu.VMEM((2, 1, tq), F32), pltpu.VMEM((2, LANES, tq), F32)],
        name="attn_c",
        **_flash_specs(S, tq, LANES, LANES, LANES),
    )(cq, ck, cv)


def _attn_d_kernel(slope_ref, reach_ref, q_ref, k_ref, v_ref, lam_ref, subln_ref, o_ref, m_sc, acc_sc,
                   *, tq, t, lambda_init):
    h = pl.program_id(1)
    qi = pl.program_id(2)
    nchunk = tq // t
    chains = [(e, c) for c in range(nchunk) for e in range(2)]
    slope = slope_ref[h] * LOG2E
    lane_lo = _lane_lo((1, LANES))
    key_bias = slope * lax.broadcasted_iota(jnp.int32, (t, LANES), 0).astype(F32)
    m_sc[...] = jnp.full(m_sc.shape, NEG, F32)
    acc_sc[...] = jnp.zeros(acc_sc.shape, F32)

    def step(j, d, n):
        rows = [pl.ds(pl.multiple_of((j - g) * t, t), t) for g in range(n)]
        live = [(e, c, g) for g in range(n) for (e, c) in chains if d is None or c >= d]
        st = [dict() for _ in live]
        vals = {}

        def values_of(g):
            if g not in vals:
                vals[g] = jnp.concatenate([v_ref[0, rows[g], :].T, jnp.ones((DEN_ROWS, t), BF16)], axis=0)
            return vals[g]

        def scores(i):
            e, c, g = live[i]
            qq = q_ref[0, c * t:(c + 1) * t, :]
            st[i]["s"] = _dot_t(k_ref[0, rows[g], :], jnp.where(lane_lo == (e == 0), qq, jnp.zeros_like(qq)))

        def softmax(i):
            e, c, g = live[i]
            s = st[i].pop("s")
            if d is not None and c == d:
                krow = lax.broadcasted_iota(jnp.int32, (t, t), 0)
                qcol = lax.broadcasted_iota(jnp.int32, (t, t), 1)
                s = s + slope * jnp.minimum(krow, 2 * qcol - krow).astype(F32)
                s = jnp.where(_chunk_causal(t), s, NEG)
                off = 0.0
            else:
                s = jnp.concatenate([s[:, b * LANES:(b + 1) * LANES] + key_bias for b in range(t // LANES)], axis=1)
                off = (slope * ((qi * nchunk + c - j + g) * t).astype(F32) if d is None
                       else slope * float((c - d) * t))
            st[i]["p"], st[i]["alpha"] = _softmax_stage(s, off, slice(c * t, (c + 1) * t), e, m_sc)

        def values(i):
            e, c, g = live[i]
            lanes = slice(c * t, (c + 1) * t)
            acc_sc[e, :, lanes] = (st[i]["alpha"] * acc_sc[e, :, lanes]
                                   + _dot(values_of(g), st[i].pop("p")))

        _pipelined(len(live), [scores, softmax, values])

    _walk_key_blocks(qi, nchunk, step, group=nchunk, backwards=True, max_blocks=reach_ref[h])
    lf = lam_ref[...]
    lam = (jnp.exp(jnp.sum(lf[0:1] * lf[1:2], axis=-1, keepdims=True))
           - jnp.exp(jnp.sum(lf[2:3] * lf[3:4], axis=-1, keepdims=True)) + lambda_init)
    sm = [acc_sc[e, 0:LANES, :] / acc_sc[e, LANES:LANES + 1, :] for e in range(2)]
    o = (sm[0] - lam * sm[1]).T
    o = _rms(o, LANES) * subln_ref[...] * (1.0 - lambda_init)
    o_ref[0] = o.astype(BF16)


def _attn_d(slopes, reach, dq, dk, dv, lam, subln, lambda_init, tq, t):
    B, S, _ = dq.shape
    specs = _flash_specs(S, tq, LANES, LANES, LANES)
    whole = lambda a: pl.BlockSpec(a.shape, lambda b, h, i: (0, 0))
    smem = pl.BlockSpec(memory_space=pltpu.SMEM)
    specs["in_specs"] = [smem, smem] + specs["in_specs"] + [whole(lam), whole(subln)]
    return pl.pallas_call(
        functools.partial(_attn_d_kernel, tq=tq, t=t, lambda_init=lambda_init),
        grid=(B, D_HEADS, S // tq),
        out_shape=jax.ShapeDtypeStruct((B, S, 512), BF16),
        scratch_shapes=[pltpu.VMEM((2, 1, tq), F32), pltpu.VMEM((2, LANES + DEN_ROWS, tq), F32)],
        name="attn_d",
        **specs,
    )(slopes, reach, dq, dk, dv, lam, subln)


def _out_mlp_kernel(x_ref, ma_ref, mb_ref, wo_ref, g_ref, wu_ref, wd_ref, o_ref, *, tf):
    half = ma_ref.shape[-1]
    x1 = x_ref[...] + _dot(ma_ref[...], wo_ref[0:half, :]) + _dot(mb_ref[...], wo_ref[half:2 * half, :])
    h = (_rms(x1, x1.shape[-1]) * g_ref[...]).astype(BF16)
    o_ref[...] = x1
    for f in range(wu_ref.shape[-1] // tf):
        u = jnp.maximum(_dot(h, wu_ref[:, f * tf:(f + 1) * tf]), 0.0)
        o_ref[...] += _dot((u * u).astype(BF16), wd_ref[f * tf:(f + 1) * tf, :])


def _out_mlp(x, ma, mb, wo, g, wu, wd, tm, tf):
    T, D = x.shape
    full = lambda a: pl.BlockSpec(a.shape, lambda i: (0,) * a.ndim, pipeline_mode=pl.Buffered(1))
    row = lambda c: pl.BlockSpec((tm, c), lambda i: (i, 0))
    return pl.pallas_call(
        functools.partial(_out_mlp_kernel, tf=tf),
        grid=(T // tm,),
        in_specs=[row(D), row(ma.shape[-1]), row(mb.shape[-1]), full(wo), full(g), full(wu), full(wd)],
        out_specs=row(D),
        out_shape=jax.ShapeDtypeStruct((T, D), F32),
        compiler_params=pltpu.CompilerParams(dimension_semantics=("arbitrary",), vmem_limit_bytes=VMEM_LIMIT),
        name="out_mlp",
    )(x, ma, mb, wo, g, wu, wd)


def _alibi_slopes(n):
    return 2.0 ** (-8.0 * jnp.arange(1, n + 1, dtype=F32) / n)


def _alibi_reach(slopes, q_gain, k_gain, t, S):
    smax = 1.01 * HEAD * 0.125 * LOG2E * jnp.max(jnp.abs(q_gain)) * jnp.max(jnp.abs(k_gain))
    dist = (160.0 + 2.0 * smax) / (slopes * LOG2E)
    blocks = jnp.floor((dist - 1.0) / t) + 1.0
    return jnp.clip(blocks, 0.0, float(S // t)).astype(jnp.int32)


def _tile(n, want):
    t = min(n, want)
    assert n % t == 0, (n, t)
    return t


def kernel(x, positions, norm_mix_g, norm_ffn_g, mlp_w_up, mlp_w_down, ev_w_in, ev_w_out, a_q_norm, a_k_norm, a_sinks, b_cq_norm, b_ckv_norm, b_w_uq, b_w_ukv, b_q_norm, b_k_norm, od_w_in, od_w_out, d_q_norm, d_k_norm, d_lambda, d_subln):
    B, S, D = x.shape
    T = B * S
    depth = norm_mix_g.shape[0]
    tm_proj = _tile(T, 512)
    tm_mlp = _tile(T, 512)
    t_a = _tile(S, 256)
    tq_b, t_b = _tile(S, 1024), _tile(S, 512)
    tq_c, t_c = _tile(S, 1024), _tile(S, 256)
    tq_d, t_d = _tile(S, 1024), _tile(S, 512)
    tf = 512

    xf = x.reshape(T, D)
    pos = positions.reshape(T, 1)
    row2 = lambda a: a.reshape(1, -1).astype(F32)
    pair = lambda a: jnp.concatenate([a, a]).reshape(1, LANES).astype(F32)
    pad_qk = lambda a: jnp.pad(a.astype(F32), (0, LANES - B_QK)).reshape(1, LANES)
    inv = ROPE_THETA ** (-jnp.arange(B_ROPE_HALF, dtype=F32) / B_ROPE_HALF)
    inv_lanes = jnp.zeros((LANES,), F32).at[HEAD:HEAD + B_ROPE_HALF].set(inv).at[HEAD + B_ROPE_HALF:B_QK].set(inv)
    inv_lanes = inv_lanes.reshape(1, LANES)
    slopes_a = _alibi_slopes(A_HEADS)
    slopes_d = _alibi_slopes(D_HEADS)

    for layer in range(depth):
        j = layer // 2
        g_mix = row2(norm_mix_g[layer])
        if layer % 2 == 0:
            w_in = ev_w_in[j]
            w_in = jnp.concatenate([w_in[:, :1152], jnp.zeros((D, HEAD), F32), w_in[:, 1152:],
                                    jnp.zeros((D, LANES - B_QK), F32)], axis=1).astype(BF16)
            wuq = jnp.pad(b_w_uq[j].reshape(-1, B_HEADS, B_QK), ((0, 0), (0, 0), (0, LANES - B_QK)))
            wuq = wuq.reshape(-1, B_HEADS * LANES).astype(BF16)
            qa, ka, va, qb, kb, vb = _even_proj(
                xf, pos, g_mix, w_in, pair(a_q_norm[j]), pair(a_k_norm[j]), row2(b_cq_norm[j]),
                row2(b_ckv_norm[j]), wuq, b_w_ukv[j].astype(BF16), pad_qk(b_q_norm[j]), pad_qk(b_k_norm[j]),
                inv_lanes, tm_proj)
            r3 = lambda a: a.reshape(B, S, a.shape[-1])
            ma = _attn_a(slopes_a, a_sinks[j].astype(F32), r3(qa), r3(ka), r3(va), t_a).reshape(T, -1)
            mb = _attn_b(r3(qb), r3(kb), r3(vb), tq_b, t_b).reshape(T, -1)
            w_out = ev_w_out[j]
        else:
            lambda_init = 0.8 - 0.6 * math.exp(-0.3 * layer)
            cq, ck, cv, dq, dk, dv = _odd_proj(
                xf, g_mix, od_w_in[j].astype(BF16), d_q_norm[j].reshape(1, LANES).astype(F32),
                d_k_norm[j].reshape(1, LANES).astype(F32), tm_proj)
            r3 = lambda a: a.reshape(B, S, a.shape[-1])
            ma = _attn_c(r3(cq), r3(ck), r3(cv), tq_c, t_c).reshape(T, -1)
            reach = _alibi_reach(slopes_d, d_q_norm[j], d_k_norm[j], t_d, S)
            mb = _attn_d(slopes_d, reach, r3(dq), r3(dk), r3(dv), d_lambda[j].astype(F32), row2(d_subln[j]),
                         lambda_init, tq_d, t_d).reshape(T, -1)
            w_out = od_w_out[j]
        xf = _out_mlp(xf, ma, mb, w_out.astype(BF16), row2(norm_ffn_g[layer]),
                      mlp_w_up[layer].astype(BF16), mlp_w_down[layer].astype(BF16), tm_mlp, tf)
    return xf.reshape(B, S, D)
```

```python
import functools
import math

import jax
import jax.numpy as jnp
from jax import lax
from jax.experimental import pallas as pl
from jax.experimental.pallas import tpu as pltpu

F32 = jnp.float32
BF16 = jnp.bfloat16

EPS = 1e-6
CHUNK = 64
CHUNK_SHIFT = 6
LANES = 128
HEAD = 64
A_HEADS = 8
A_GROUP = 4
A_WINDOW = 128
A_WINDOW_CHUNKS = A_WINDOW // CHUNK
B_HEADS = 8
B_QK = 96
B_ROPE_HALF = 16
ROPE_THETA = 10000.0
C_HEADS = 8
D_HEADS = 4
NEG = -1e30
LOG2E = math.log2(math.e)
DEN_ROWS = 16
VMEM_LIMIT = 56 * 1024 * 1024


def _dot(a, b):
    return jnp.dot(a, b, preferred_element_type=F32)


def _dot_t(a, b):
    return lax.dot_general(a, b, (((1,), (1,)), ((), ())), preferred_element_type=F32)


def _dot_tl(a, b):
    return lax.dot_general(a, b, (((0,), (0,)), ((), ())), preferred_element_type=F32)


def _rms(x, denom):
    return x * lax.rsqrt(jnp.sum(x * x, axis=-1, keepdims=True) * (1.0 / denom) + EPS)


def _pair_rms(x, lo):
    xx = x * x
    s_lo = jnp.sum(jnp.where(lo, xx, 0.0), axis=-1, keepdims=True)
    s_hi = jnp.sum(jnp.where(lo, 0.0, xx), axis=-1, keepdims=True)
    r = jnp.where(lo, lax.rsqrt(s_lo * (1.0 / HEAD) + EPS), lax.rsqrt(s_hi * (1.0 / HEAD) + EPS))
    return x * r


def _lane_lo(shape):
    return lax.broadcasted_iota(jnp.int32, shape, len(shape) - 1) < HEAD


def _even_proj_kernel(x_ref, pos_ref, g_ref, w_in_ref, aqg_ref, akg_ref, cqg_ref, ckvg_ref,
                      wuq_ref, wukv_ref, bqg_ref, bkg_ref, inv_ref,
                      qa_ref, ka_ref, va_ref, qb_ref, kb_ref, vb_ref):
    x = x_ref[...]
    h = _rms(x, x.shape[-1]) * g_ref[...]
    proj = _dot(h.astype(BF16), w_in_ref[...])
    lo = _lane_lo((1, LANES))
    for p in range(A_HEADS // 2):
        seg = proj[:, p * LANES:(p + 1) * LANES]
        qa_ref[:, p * LANES:(p + 1) * LANES] = (_pair_rms(seg, lo) * aqg_ref[...] * (0.125 * LOG2E)).astype(BF16)
    ka_ref[...] = (_pair_rms(proj[:, 512:640], lo) * akg_ref[...]).astype(BF16)
    va_ref[...] = proj[:, 640:768].astype(BF16)
    cq = _rms(proj[:, 768:1024], 256) * cqg_ref[...]
    ckv = _rms(proj[:, 1024:1152], 128) * ckvg_ref[...]
    krope = proj[:, 1152:1280]
    qall = _dot(cq.astype(BF16), wuq_ref[...])
    kvall = _dot(ckv.astype(BF16), wukv_ref[...])
    ang = pos_ref[...].astype(F32) * inv_ref[...]
    cosf = jnp.cos(ang)
    sinf = jnp.sin(ang)
    lane = lax.broadcasted_iota(jnp.int32, (1, LANES), 1)
    s_first = jnp.where((lane >= HEAD) & (lane < HEAD + B_ROPE_HALF), -sinf, 0.0)
    s_second = jnp.where((lane >= HEAD + B_ROPE_HALF) & (lane < B_QK), sinf, 0.0)

    def rope(t):
        return (t * cosf + pltpu.roll(t, LANES - B_ROPE_HALF, 1) * s_first
                + pltpu.roll(t, B_ROPE_HALF, 1) * s_second)

    scale_b = B_QK ** -0.5 * LOG2E
    for p in range(B_HEADS // 2):
        vpair = []
        for hh in (2 * p, 2 * p + 1):
            qh = qall[:, hh * LANES:(hh + 1) * LANES]
            qn = _rms(qh, B_QK) * bqg_ref[...]
            qb_ref[:, hh * LANES:(hh + 1) * LANES] = (rope(qn) * scale_b).astype(BF16)
            kvh = kvall[:, hh * LANES:(hh + 1) * LANES]
            kpre = jnp.where(lo, kvh, krope)
            kn = _rms(kpre, B_QK) * bkg_ref[...]
            kb_ref[:, hh * LANES:(hh + 1) * LANES] = rope(kn).astype(BF16)
            vpair.append(kvh)
        vb_ref[:, p * LANES:(p + 1) * LANES] = jnp.where(
            lo, pltpu.roll(vpair[0], HEAD, 1), vpair[1]).astype(BF16)


def _even_proj(x, pos, g, w_in, aqg, akg, cqg, ckvg, wuq, wukv, bqg, bkg, inv, tm):
    T, D = x.shape
    full = lambda a: pl.BlockSpec(a.shape, lambda i: (0,) * a.ndim)
    row = lambda c: pl.BlockSpec((tm, c), lambda i: (i, 0))
    outs = (512, 128, 128, 1024, 1024, 512)
    return pl.pallas_call(
        _even_proj_kernel,
        grid=(T // tm,),
        in_specs=[row(D), row(1)] + [full(a) for a in (g, w_in, aqg, akg, cqg, ckvg, wuq, wukv, bqg, bkg, inv)],
        out_specs=[row(c) for c in outs],
        out_shape=[jax.ShapeDtypeStruct((T, c), BF16) for c in outs],
        compiler_params=pltpu.CompilerParams(dimension_semantics=("arbitrary",), vmem_limit_bytes=VMEM_LIMIT),
        name="even_proj",
    )(x, pos, g, w_in, aqg, akg, cqg, ckvg, wuq, wukv, bqg, bkg, inv)


def _odd_proj_kernel(x_ref, g_ref, w_in_ref, dqg_ref, dkg_ref,
                     cq_ref, ck_ref, cv_ref, dq_ref, dk_ref, dv_ref):
    x = x_ref[...]
    h = _rms(x, x.shape[-1]) * g_ref[...]
    proj = _dot(h.astype(BF16), w_in_ref[...])
    lo = _lane_lo((1, LANES))
    qscale = 0.125 * LOG2E
    cq_ref[...] = (proj[:, 0:512] * qscale).astype(BF16)
    ck_ref[...] = proj[:, 512:1024].astype(BF16)
    cv_ref[...] = proj[:, 1024:1536].astype(BF16)
    for hh in range(D_HEADS):
        sl = slice(hh * LANES, (hh + 1) * LANES)
        dq_ref[:, sl] = (_pair_rms(proj[:, 1536 + hh * LANES:1536 + (hh + 1) * LANES], lo)
                         * dqg_ref[...] * qscale).astype(BF16)
        dk_ref[:, sl] = (_pair_rms(proj[:, 2048 + hh * LANES:2048 + (hh + 1) * LANES], lo)
                         * dkg_ref[...]).astype(BF16)
    dv_ref[...] = proj[:, 2560:3072].astype(BF16)


def _odd_proj(x, g, w_in, dqg, dkg, tm):
    T, D = x.shape
    full = lambda a: pl.BlockSpec(a.shape, lambda i: (0,) * a.ndim)
    row = lambda c: pl.BlockSpec((tm, c), lambda i: (i, 0))
    return pl.pallas_call(
        _odd_proj_kernel,
        grid=(T // tm,),
        in_specs=[row(D)] + [full(a) for a in (g, w_in, dqg, dkg)],
        out_specs=[row(512)] * 6,
        out_shape=[jax.ShapeDtypeStruct((T, 512), BF16)] * 6,
        compiler_params=pltpu.CompilerParams(dimension_semantics=("arbitrary",), vmem_limit_bytes=VMEM_LIMIT),
        name="odd_proj",
    )(x, g, w_in, dqg, dkg)


def _pipelined(n, stages):
    for step in range(n + len(stages) - 1):
        for si, stage in enumerate(stages):
            c = step - si
            if 0 <= c < n:
                stage(c)


def _attn_a_kernel(slope_ref, sink_ref, q_ref, k_ref, v_ref, o_ref, *, tq, win):
    g = pl.program_id(1)
    qi = pl.program_id(2)
    q0 = qi * tq
    ks = pl.multiple_of(jnp.maximum(q0 - A_WINDOW, 0), A_WINDOW)
    lane_half = lax.broadcasted_iota(jnp.int32, (1, LANES), 1) // HEAD

    def both_halves(x):
        xf = x.astype(F32)
        return jnp.where(lane_half == g, xf, pltpu.roll(xf, HEAD, 1)).astype(BF16)

    k = both_halves(k_ref[0, pl.ds(ks, win), :])
    v = both_halves(v_ref[0, pl.ds(ks, win), :])
    kpos = ks + lax.broadcasted_iota(jnp.int32, (win, tq), 0)
    qpos = q0 + lax.broadcasted_iota(jnp.int32, (win, tq), 1)
    dch = (qpos >> CHUNK_SHIFT) - (kpos >> CHUNK_SHIFT)
    allowed = (dch >= 0) & (dch <= A_WINDOW_CHUNKS)
    negdist = -jnp.abs(qpos - kpos).astype(F32)
    st = [dict() for _ in range(A_GROUP)]

    def scores(i):
        qq = q_ref[0, :, (i // 2) * LANES:(i // 2 + 1) * LANES]
        qh = jnp.where(lane_half == i % 2, qq, jnp.zeros_like(qq))
        st[i]["s"] = _dot_t(k, qh)

    def softmax(i):
        hidx = A_GROUP * g + i
        s = jnp.where(allowed, st[i].pop("s") + (slope_ref[hidx] * LOG2E) * negdist, NEG)
        sink = sink_ref[hidx] * LOG2E
        m = jnp.maximum(jnp.max(s, axis=0, keepdims=True), sink)
        p = jnp.exp2(s - m)
        st[i]["den"] = jnp.sum(p, axis=0, keepdims=True) + jnp.exp2(sink - m)
        st[i]["p"] = p.astype(BF16)

    def values(i):
        st[i]["o"] = (_dot_tl(v, st[i].pop("p")) / st[i].pop("den"))[0:HEAD, :]

    _pipelined(A_GROUP, [scores, softmax, values])
    outs = [st[i]["o"] for i in range(A_GROUP)]
    o_ref[0] = jnp.concatenate(outs, axis=0).T.astype(BF16)


def _attn_a(slopes, sinks, qa, ka, va, tq):
    B, S, _ = qa.shape
    win = tq + A_WINDOW
    smem = pl.BlockSpec(memory_space=pltpu.SMEM)
    gw = A_GROUP * HEAD
    return pl.pallas_call(
        functools.partial(_attn_a_kernel, tq=tq, win=win),
        grid=(B, A_HEADS // A_GROUP, S // tq),
        in_specs=[smem, smem,
                  pl.BlockSpec((1, tq, gw), lambda b, g, i: (b, i, g)),
                  pl.BlockSpec((1, S, LANES), lambda b, g, i: (b, 0, 0)),
                  pl.BlockSpec((1, S, LANES), lambda b, g, i: (b, 0, 0))],
        out_specs=pl.BlockSpec((1, tq, gw), lambda b, g, i: (b, i, g)),
        out_shape=jax.ShapeDtypeStruct((B, S, 512), BF16),
        compiler_params=pltpu.CompilerParams(dimension_semantics=("arbitrary",) * 3, vmem_limit_bytes=VMEM_LIMIT),
        name="attn_a",
    )(slopes, sinks, qa, ka, va)


def _walk_key_blocks(qi, nchunk, step, group, backwards=False, still_live=None, max_blocks=None):
    first = qi * nchunk
    trips = first // group
    if not backwards:
        def full(i, carry):
            step(i * group, None, group)
            return carry

        lax.fori_loop(0, trips, full, 0)
        for d in range(nchunk):
            step(first + d, d, 1)
        return
    for d in range(nchunk - 1, -1, -1):
        step(first + d, d, 1)
    if max_blocks is not None:
        trips = jnp.minimum(trips, (max_blocks + (group - 1)) // group)
    if still_live is None:
        def full(i, carry):
            step(first - 1 - i * group, None, group)
            return carry

        lax.fori_loop(0, trips, full, 0)
    else:
        def full(carry):
            step(first - 1 - carry[0] * group, None, group)
            return carry[0] + 1, still_live()

        lax.while_loop(lambda carry: (carry[0] < trips) & carry[1], full, (0, still_live()))


def _flash_specs(S, tq, q_width, k_width, v_width):
    return dict(
        in_specs=[pl.BlockSpec((1, tq, q_width), lambda b, p, i: (b, i, p)),
                  pl.BlockSpec((1, S, k_width), lambda b, p, i: (b, 0, p)),
                  pl.BlockSpec((1, S, v_width), lambda b, p, i: (b, 0, p))],
        out_specs=pl.BlockSpec((1, tq, LANES), lambda b, p, i: (b, i, p)),
        compiler_params=pltpu.CompilerParams(dimension_semantics=("arbitrary",) * 3, vmem_limit_bytes=VMEM_LIMIT),
    )


def _chunk_causal(t):
    krow = lax.broadcasted_iota(jnp.int32, (t, t), 0)
    qcol = lax.broadcasted_iota(jnp.int32, (t, t), 1)
    return (krow >> CHUNK_SHIFT) <= (qcol >> CHUNK_SHIFT)


def _softmax_stage(s, off, lanes, e, m_sc):
    m_prev = m_sc[e, :, lanes]
    m_new = jnp.maximum(m_prev, jnp.max(s, axis=0, keepdims=True) - off)
    alpha = jnp.exp2(m_prev - m_new)
    pr = jnp.exp2(s - (m_new + off))
    m_sc[e, :, lanes] = m_new
    return pr.astype(BF16), alpha


def _attn_b_kernel(q_ref, k_ref, v_ref, o_ref, m_sc, acc_sc, *, tq, t):
    qi = pl.program_id(2)
    nchunk = tq // t
    chains = [(e, c) for c in range(nchunk) for e in range(2)]
    lane = lax.broadcasted_iota(jnp.int32, (1, LANES), 1)
    ones_row = (HEAD, 0)
    m_sc[...] = jnp.full(m_sc.shape, NEG, F32)
    acc_sc[...] = jnp.zeros(acc_sc.shape, F32)

    def step(j, d, n):
        rows = [pl.ds(pl.multiple_of((j + g) * t, t), t) for g in range(n)]
        live = [(e, c, g) for g in range(n) for (e, c) in chains if d is None or c >= d]
        st = [dict() for _ in live]
        vals = {}

        def values_of(e, g):
            if (e, g) not in vals:
                v = v_ref[0, rows[g], :]
                own = (lane < HEAD) if e == 0 else (lane >= HEAD)
                vals[e, g] = jnp.where(own, v, jnp.where(lane == ones_row[e], 1.0, 0.0).astype(BF16))
            return vals[e, g]

        def scores(i):
            e, c, g = live[i]
            st[i]["s"] = _dot_t(k_ref[0, rows[g], e * LANES:(e + 1) * LANES],
                                q_ref[0, c * t:(c + 1) * t, e * LANES:(e + 1) * LANES])

        def softmax(i):
            e, c, g = live[i]
            s = st[i].pop("s")
            if d is not None and c == d:
                s = jnp.where(_chunk_causal(t), s, NEG)
            st[i]["p"], st[i]["alpha"] = _softmax_stage(s, 0.0, slice(c * t, (c + 1) * t), e, m_sc)

        def values(i):
            e, c, g = live[i]
            lanes = slice(c * t, (c + 1) * t)
            acc_sc[e, :, lanes] = (st[i]["alpha"] * acc_sc[e, :, lanes]
                                   + _dot_tl(values_of(e, g), st[i].pop("p")))

        _pipelined(len(live), [scores, softmax, values])

    _walk_key_blocks(qi, nchunk, step, group=nchunk)
    first = lax.broadcasted_iota(jnp.int32, (LANES, 1), 0) < HEAD
    den = [acc_sc[e, ones_row[e]:ones_row[e] + 1, :] for e in range(2)]
    o_t = jnp.where(first, acc_sc[0] / den[0], acc_sc[1] / den[1])
    o_ref[0] = o_t.T.astype(BF16)


def _attn_b(qb, kb, vb, tq, t):
    B, S, _ = qb.shape
    return pl.pallas_call(
        functools.partial(_attn_b_kernel, tq=tq, t=t),
        grid=(B, B_HEADS // 2, S // tq),
        out_shape=jax.ShapeDtypeStruct((B, S, 512), BF16),
        scratch_shapes=[pltpu.VMEM((2, 1, tq), F32), pltpu.VMEM((2, LANES, tq), F32)],
        name="attn_b",
        **_flash_specs(S, tq, 2 * LANES, 2 * LANES, LANES),
    )(qb, kb, vb)


def _attn_c_kernel(q_ref, k_ref, v_ref, o_ref, r_sc, acc_sc, *, tq, t):
    qi = pl.program_id(2)
    nchunk = tq // t
    chains = [(e, c) for c in range(nchunk) for e in range(2)]
    krow = lax.broadcasted_iota(jnp.int32, (t, t), 0)
    qcol = lax.broadcasted_iota(jnp.int32, (t, t), 1)
    from_here = jnp.where(qcol >= krow, 1.0, 0.0).astype(BF16)
    from_here = jnp.concatenate([from_here, from_here], axis=1)
    lane_lo = _lane_lo((1, LANES))
    r_sc[...] = jnp.zeros(r_sc.shape, F32)
    acc_sc[...] = jnp.zeros(acc_sc.shape, F32)

    def step(j, d, n):
        rows = [pl.ds(pl.multiple_of((j - g) * t, t), t) for g in range(n)]
        live = [(e, c, g) for g in range(n) for (e, c) in chains if d is None or c >= d]
        st = [dict() for _ in live]

        def scores(i):
            e, c, g = live[i]
            qq = q_ref[0, c * t:(c + 1) * t, :]
            st[i]["z"] = _dot_t(k_ref[0, rows[g], :], jnp.where(lane_lo == (e == 0), qq, jnp.zeros_like(qq)))

        def softplus(i):
            e, c, g = live[i]
            z = st[i]["z"]
            sp = jnp.maximum(z, jnp.log2(1.0 + jnp.exp2(jnp.minimum(z, 126.0))))
            if d is not None and c == d:
                sp = jnp.where(krow < qcol, sp, 0.0)
            hi = sp.astype(BF16)
            lo = (sp - hi.astype(F32)).astype(BF16)
            st[i]["hilo"] = jnp.concatenate([hi, lo], axis=0)

        def suffix(i):
            st[i]["suf"] = _dot(from_here, st[i].pop("hilo"))

        def weights(i):
            e, c, g = live[i]
            lanes = slice(c * t, (c + 1) * t)
            suf = st[i].pop("suf")
            a = jnp.exp2(st[i].pop("z") - suf)
            if d is not None and c == d:
                a = jnp.where(krow < qcol, a, 0.0)
            st[i]["a"] = a.astype(BF16)
            r_prev = r_sc[e, :, lanes]
            st[i]["w"] = jnp.exp2(-r_prev)
            r_sc[e, :, lanes] = r_prev + suf[0:1, :]

        def values(i):
            e, c, g = live[i]
            lanes = slice(c * t, (c + 1) * t)
            acc_sc[e, :, lanes] = (acc_sc[e, :, lanes]
                                   + st[i].pop("w") * _dot_tl(v_ref[0, rows[g], :], st[i].pop("a")))

        _pipelined(len(live), [scores, softplus, suffix, weights, values])

    def still_live():
        return jnp.max(jnp.exp2(-r_sc[...])) > 0.0

    del still_live
    _walk_key_blocks(qi, nchunk, step, group=2, backwards=True)
    first = lax.broadcasted_iota(jnp.int32, (LANES, 1), 0) < HEAD
    o_ref[0] = jnp.where(first, acc_sc[0], acc_sc[1]).T.astype(BF16)


def _attn_c(cq, ck, cv, tq, t):
    B, S, _ = cq.shape
    return pl.pallas_call(
        functools.partial(_attn_c_kernel, tq=tq, t=t),
        grid=(B, C_HEADS // 2, S // tq),
        out_shape=jax.ShapeDtypeStruct((B, S, 512), BF16),
        scratch_shapes=[pltpu.VMEM((2, 1, tq), F32), pltpu.VMEM((2, LANES, tq), F32)],
        name="attn_c",
        **_flash_specs(S, tq, LANES, LANES, LANES),
    )(cq, ck, cv)


def _attn_d_kernel(slope_ref, reach_ref, q_ref, k_ref, v_ref, lam_ref, subln_ref, o_ref, m_sc, acc_sc,
                   *, tq, t, lambda_init):
    h = pl.program_id(1)
    qi = pl.program_id(2)
    nchunk = tq // t
    chains = [(e, c) for c in range(nchunk) for e in range(2)]
    slope = slope_ref[h] * LOG2E
    lane_lo = _lane_lo((1, LANES))
    key_bias = slope * lax.broadcasted_iota(jnp.int32, (t, LANES), 0).astype(F32)
    m_sc[...] = jnp.full(m_sc.shape, NEG, F32)
    acc_sc[...] = jnp.zeros(acc_sc.shape, F32)

    def step(j, d, n):
        rows = [pl.ds(pl.multiple_of((j - g) * t, t), t) for g in range(n)]
        live = [(e, c, g) for g in range(n) for (e, c) in chains if d is None or c >= d]
        st = [dict() for _ in live]
        vals = {}

        def values_of(g):
            if g not in vals:
                vals[g] = jnp.concatenate([v_ref[0, rows[g], :].T, jnp.ones((DEN_ROWS, t), BF16)], axis=0)
            return vals[g]

        def scores(i):
            e, c, g = live[i]
            qq = q_ref[0, c * t:(c + 1) * t, :]
            st[i]["s"] = _dot_t(k_ref[0, rows[g], :], jnp.where(lane_lo == (e == 0), qq, jnp.zeros_like(qq)))

        def softmax(i):
            e, c, g = live[i]
            s = st[i].pop("s")
            if d is not None and c == d:
                krow = lax.broadcasted_iota(jnp.int32, (t, t), 0)
                qcol = lax.broadcasted_iota(jnp.int32, (t, t), 1)
                s = s + slope * jnp.minimum(krow, 2 * qcol - krow).astype(F32)
                s = jnp.where(_chunk_causal(t), s, NEG)
                off = 0.0
            else:
                s = jnp.concatenate([s[:, b * LANES:(b + 1) * LANES] + key_bias for b in range(t // LANES)], axis=1)
                off = (slope * ((qi * nchunk + c - j + g) * t).astype(F32) if d is None
                       else slope * float((c - d) * t))
            st[i]["p"], st[i]["alpha"] = _softmax_stage(s, off, slice(c * t, (c + 1) * t), e, m_sc)

        def values(i):
            e, c, g = live[i]
            lanes = slice(c * t, (c + 1) * t)
            acc_sc[e, :, lanes] = (st[i]["alpha"] * acc_sc[e, :, lanes]
                                   + _dot(values_of(g), st[i].pop("p")))

        _pipelined(len(live), [scores, softmax, values])

    del reach_ref
    _walk_key_blocks(qi, nchunk, step, group=nchunk, backwards=True)
    lf = lam_ref[...]
    lam = (jnp.exp(jnp.sum(lf[0:1] * lf[1:2], axis=-1, keepdims=True))
           - jnp.exp(jnp.sum(lf[2:3] * lf[3:4], axis=-1, keepdims=True)) + lambda_init)
    sm = [acc_sc[e, 0:LANES, :] / acc_sc[e, LANES:LANES + 1, :] for e in range(2)]
    o = (sm[0] - lam * sm[1]).T
    o = _rms(o, LANES) * subln_ref[...] * (1.0 - lambda_init)
    o_ref[0] = o.astype(BF16)


def _attn_d(slopes, reach, dq, dk, dv, lam, subln, lambda_init, tq, t):
    B, S, _ = dq.shape
    specs = _flash_specs(S, tq, LANES, LANES, LANES)
    whole = lambda a: pl.BlockSpec(a.shape, lambda b, h, i: (0, 0))
    smem = pl.BlockSpec(memory_space=pltpu.SMEM)
    specs["in_specs"] = [smem, smem] + specs["in_specs"] + [whole(lam), whole(subln)]
    return pl.pallas_call(
        functools.partial(_attn_d_kernel, tq=tq, t=t, lambda_init=lambda_init),
        grid=(B, D_HEADS, S // tq),
        out_shape=jax.ShapeDtypeStruct((B, S, 512), BF16),
        scratch_shapes=[pltpu.VMEM((2, 1, tq), F32), pltpu.VMEM((2, LANES + DEN_ROWS, tq), F32)],
        name="attn_d",
        **specs,
    )(slopes, reach, dq, dk, dv, lam, subln)


def _out_mlp_kernel(x_ref, ma_ref, mb_ref, wo_ref, g_ref, wu_ref, wd_ref, o_ref, *, tf):
    half = ma_ref.shape[-1]
    x1 = x_ref[...] + _dot(ma_ref[...], wo_ref[0:half, :]) + _dot(mb_ref[...], wo_ref[half:2 * half, :])
    h = (_rms(x1, x1.shape[-1]) * g_ref[...]).astype(BF16)
    o_ref[...] = x1
    for f in range(wu_ref.shape[-1] // tf):
        u = jnp.maximum(_dot(h, wu_ref[:, f * tf:(f + 1) * tf]), 0.0)
        o_ref[...] += _dot((u * u).astype(BF16), wd_ref[f * tf:(f + 1) * tf, :])


def _out_mlp(x, ma, mb, wo, g, wu, wd, tm, tf):
    T, D = x.shape
    full = lambda a: pl.BlockSpec(a.shape, lambda i: (0,) * a.ndim, pipeline_mode=pl.Buffered(1))
    row = lambda c: pl.BlockSpec((tm, c), lambda i: (i, 0))
    return pl.pallas_call(
        functools.partial(_out_mlp_kernel, tf=tf),
        grid=(T // tm,),
        in_specs=[row(D), row(ma.shape[-1]), row(mb.shape[-1]), full(wo), full(g), full(wu), full(wd)],
        out_specs=row(D),
        out_shape=jax.ShapeDtypeStruct((T, D), F32),
        compiler_params=pltpu.CompilerParams(dimension_semantics=("arbitrary",), vmem_limit_bytes=VMEM_LIMIT),
        name="out_mlp",
    )(x, ma, mb, wo, g, wu, wd)


def _alibi_slopes(n):
    return 2.0 ** (-8.0 * jnp.arange(1, n + 1, dtype=F32) / n)


def _alibi_reach(slopes, q_gain, k_gain, t, S):
    smax = 1.01 * HEAD * 0.125 * LOG2E * jnp.max(jnp.abs(q_gain)) * jnp.max(jnp.abs(k_gain))
    dist = (160.0 + 2.0 * smax) / (slopes * LOG2E)
    blocks = jnp.floor((dist - 1.0) / t) + 1.0
    return jnp.clip(blocks, 0.0, float(S // t)).astype(jnp.int32)


def _tile(n, want):
    t = min(n, want)
    assert n % t == 0, (n, t)
    return t


def kernel(x, positions, norm_mix_g, norm_ffn_g, mlp_w_up, mlp_w_down, ev_w_in, ev_w_out, a_q_norm, a_k_norm, a_sinks, b_cq_norm, b_ckv_norm, b_w_uq, b_w_ukv, b_q_norm, b_k_norm, od_w_in, od_w_out, d_q_norm, d_k_norm, d_lambda, d_subln):
    B, S, D = x.shape
    T = B * S
    depth = norm_mix_g.shape[0]
    tm_proj = _tile(T, 512)
    tm_mlp = _tile(T, 512)
    t_a = _tile(S, 256)
    tq_b, t_b = _tile(S, 1024), _tile(S, 512)
    tq_c, t_c = _tile(S, 1024), _tile(S, 256)
    tq_d, t_d = _tile(S, 1024), _tile(S, 512)
    tf = 512

    xf = x.reshape(T, D)
    pos = positions.reshape(T, 1)
    row2 = lambda a: a.reshape(1, -1).astype(F32)
    pair = lambda a: jnp.concatenate([a, a]).reshape(1, LANES).astype(F32)
    pad_qk = lambda a: jnp.pad(a.astype(F32), (0, LANES - B_QK)).reshape(1, LANES)
    inv = ROPE_THETA ** (-jnp.arange(B_ROPE_HALF, dtype=F32) / B_ROPE_HALF)
    inv_lanes = jnp.zeros((LANES,), F32).at[HEAD:HEAD + B_ROPE_HALF].set(inv).at[HEAD + B_ROPE_HALF:B_QK].set(inv)
    inv_lanes = inv_lanes.reshape(1, LANES)
    slopes_a = _alibi_slopes(A_HEADS)
    slopes_d = _alibi_slopes(D_HEADS)

    for layer in range(depth):
        j = layer // 2
        g_mix = row2(norm_mix_g[layer])
        if layer % 2 == 0:
            w_in = ev_w_in[j]
            w_in = jnp.concatenate([w_in[:, :1152], jnp.zeros((D, HEAD), F32), w_in[:, 1152:],
                                    jnp.zeros((D, LANES - B_QK), F32)], axis=1).astype(BF16)
            wuq = jnp.pad(b_w_uq[j].reshape(-1, B_HEADS, B_QK), ((0, 0), (0, 0), (0, LANES - B_QK)))
            wuq = wuq.reshape(-1, B_HEADS * LANES).astype(BF16)
            qa, ka, va, qb, kb, vb = _even_proj(
                xf, pos, g_mix, w_in, pair(a_q_norm[j]), pair(a_k_norm[j]), row2(b_cq_norm[j]),
                row2(b_ckv_norm[j]), wuq, b_w_ukv[j].astype(BF16), pad_qk(b_q_norm[j]), pad_qk(b_k_norm[j]),
                inv_lanes, tm_proj)
            r3 = lambda a: a.reshape(B, S, a.shape[-1])
            ma = _attn_a(slopes_a, a_sinks[j].astype(F32), r3(qa), r3(ka), r3(va), t_a).reshape(T, -1)
            mb = _attn_b(r3(qb), r3(kb), r3(vb), tq_b, t_b).reshape(T, -1)
            w_out = ev_w_out[j]
        else:
            lambda_init = 0.8 - 0.6 * math.exp(-0.3 * layer)
            cq, ck, cv, dq, dk, dv = _odd_proj(
                xf, g_mix, od_w_in[j].astype(BF16), d_q_norm[j].reshape(1, LANES).astype(F32),
                d_k_norm[j].reshape(1, LANES).astype(F32), tm_proj)
            r3 = lambda a: a.reshape(B, S, a.shape[-1])
            ma = _attn_c(r3(cq), r3(ck), r3(cv), tq_c, t_c).reshape(T, -1)
            reach = _alibi_reach(slopes_d, d_q_norm[j], d_k_norm[j], t_d, S)
            mb = _attn_d(slopes_d, reach, r3(dq), r3(dk), r3(dv), d_lambda[j].astype(F32), row2(d_subln[j]),
                         lambda_init, tq_d, t_d).reshape(T, -1)
            w_out = od_w_out[j]
        xf = _out_mlp(xf, ma, mb, w_out.astype(BF16), row2(norm_ffn_g[layer]),
                      mlp_w_up[layer].astype(BF16), mlp_w_down[layer].astype(BF16), tm_mlp, tf)
    return xf.reshape(B, S, D)
```

```python
import functools
import math

import jax
import jax.numpy as jnp
from jax import lax
from jax.experimental import pallas as pl
from jax.experimental.pallas import tpu as pltpu

F32 = jnp.float32
BF16 = jnp.bfloat16

EPS = 1e-6
CHUNK = 64
CHUNK_SHIFT = 6
LANES = 128
HEAD = 64
A_HEADS = 8
A_GROUP = 4
A_WINDOW = 128
A_WINDOW_CHUNKS = A_WINDOW // CHUNK
B_HEADS = 8
B_QK = 96
B_ROPE_HALF = 16
ROPE_THETA = 10000.0
C_HEADS = 8
D_HEADS = 4
NEG = -1e30
LOG2E = math.log2(math.e)
DEN_ROWS = 16
VMEM_LIMIT = 56 * 1024 * 1024


def _dot(a, b):
    return jnp.dot(a, b, preferred_element_type=F32)


def _dot_t(a, b):
    return lax.dot_general(a, b, (((1,), (1,)), ((), ())), preferred_element_type=F32)


def _dot_tl(a, b):
    return lax.dot_general(a, b, (((0,), (0,)), ((), ())), preferred_element_type=F32)


def _rms(x, denom):
    return x * lax.rsqrt(jnp.sum(x * x, axis=-1, keepdims=True) * (1.0 / denom) + EPS)


def _pair_rms(x, lo):
    xx = x * x
    s_lo = jnp.sum(jnp.where(lo, xx, 0.0), axis=-1, keepdims=True)
    s_hi = jnp.sum(jnp.where(lo, 0.0, xx), axis=-1, keepdims=True)
    r = jnp.where(lo, lax.rsqrt(s_lo * (1.0 / HEAD) + EPS), lax.rsqrt(s_hi * (1.0 / HEAD) + EPS))
    return x * r


def _lane_lo(shape):
    return lax.broadcasted_iota(jnp.int32, shape, len(shape) - 1) < HEAD


def _even_proj_kernel(x_ref, pos_ref, g_ref, w_in_ref, aqg_ref, akg_ref, cqg_ref, ckvg_ref,
                      wuq_ref, wukv_ref, bqg_ref, bkg_ref, inv_ref,
                      qa_ref, ka_ref, va_ref, qb_ref, kb_ref, vb_ref):
    x = x_ref[...]
    h = _rms(x, x.shape[-1]) * g_ref[...]
    proj = _dot(h.astype(BF16), w_in_ref[...])
    lo = _lane_lo((1, LANES))
    for p in range(A_HEADS // 2):
        seg = proj[:, p * LANES:(p + 1) * LANES]
        qa_ref[:, p * LANES:(p + 1) * LANES] = (_pair_rms(seg, lo) * aqg_ref[...] * (0.125 * LOG2E)).astype(BF16)
    ka_ref[...] = (_pair_rms(proj[:, 512:640], lo) * akg_ref[...]).astype(BF16)
    va_ref[...] = proj[:, 640:768].astype(BF16)
    cq = _rms(proj[:, 768:1024], 256) * cqg_ref[...]
    ckv = _rms(proj[:, 1024:1152], 128) * ckvg_ref[...]
    krope = proj[:, 1152:1280]
    qall = _dot(cq.astype(BF16), wuq_ref[...])
    kvall = _dot(ckv.astype(BF16), wukv_ref[...])
    ang = pos_ref[...].astype(F32) * inv_ref[...]
    cosf = jnp.cos(ang)
    sinf = jnp.sin(ang)
    lane = lax.broadcasted_iota(jnp.int32, (1, LANES), 1)
    s_first = jnp.where((lane >= HEAD) & (lane < HEAD + B_ROPE_HALF), -sinf, 0.0)
    s_second = jnp.where((lane >= HEAD + B_ROPE_HALF) & (lane < B_QK), sinf, 0.0)

    def rope(t):
        return (t * cosf + pltpu.roll(t, LANES - B_ROPE_HALF, 1) * s_first
                + pltpu.roll(t, B_ROPE_HALF, 1) * s_second)

    scale_b = B_QK ** -0.5 * LOG2E
    for p in range(B_HEADS // 2):
        vpair = []
        for hh in (2 * p, 2 * p + 1):
            qh = qall[:, hh * LANES:(hh + 1) * LANES]
            qn = _rms(qh, B_QK) * bqg_ref[...]
            qb_ref[:, hh * LANES:(hh + 1) * LANES] = (rope(qn) * scale_b).astype(BF16)
            kvh = kvall[:, hh * LANES:(hh + 1) * LANES]
            kpre = jnp.where(lo, kvh, krope)
            kn = _rms(kpre, B_QK) * bkg_ref[...]
            kb_ref[:, hh * LANES:(hh + 1) * LANES] = rope(kn).astype(BF16)
            vpair.append(kvh)
        vb_ref[:, p * LANES:(p + 1) * LANES] = jnp.where(
            lo, pltpu.roll(vpair[0], HEAD, 1), vpair[1]).astype(BF16)


def _even_proj(x, pos, g, w_in, aqg, akg, cqg, ckvg, wuq, wukv, bqg, bkg, inv, tm):
    T, D = x.shape
    full = lambda a: pl.BlockSpec(a.shape, lambda i: (0,) * a.ndim)
    row = lambda c: pl.BlockSpec((tm, c), lambda i: (i, 0))
    outs = (512, 128, 128, 1024, 1024, 512)
    return pl.pallas_call(
        _even_proj_kernel,
        grid=(T // tm,),
        in_specs=[row(D), row(1)] + [full(a) for a in (g, w_in, aqg, akg, cqg, ckvg, wuq, wukv, bqg, bkg, inv)],
        out_specs=[row(c) for c in outs],
        out_shape=[jax.ShapeDtypeStruct((T, c), BF16) for c in outs],
        compiler_params=pltpu.CompilerParams(dimension_semantics=("arbitrary",), vmem_limit_bytes=VMEM_LIMIT),
        name="even_proj",
    )(x, pos, g, w_in, aqg, akg, cqg, ckvg, wuq, wukv, bqg, bkg, inv)


def _odd_proj_kernel(x_ref, g_ref, w_in_ref, dqg_ref, dkg_ref,
                     cq_ref, ck_ref, cv_ref, dq_ref, dk_ref, dv_ref):
    x = x_ref[...]
    h = _rms(x, x.shape[-1]) * g_ref[...]
    proj = _dot(h.astype(BF16), w_in_ref[...])
    lo = _lane_lo((1, LANES))
    qscale = 0.125 * LOG2E
    cq_ref[...] = (proj[:, 0:512] * qscale).astype(BF16)
    ck_ref[...] = proj[:, 512:1024].astype(BF16)
    cv_ref[...] = proj[:, 1024:1536].astype(BF16)
    for hh in range(D_HEADS):
        sl = slice(hh * LANES, (hh + 1) * LANES)
        dq_ref[:, sl] = (_pair_rms(proj[:, 1536 + hh * LANES:1536 + (hh + 1) * LANES], lo)
                         * dqg_ref[...] * qscale).astype(BF16)
        dk_ref[:, sl] = (_pair_rms(proj[:, 2048 + hh * LANES:2048 + (hh + 1) * LANES], lo)
                         * dkg_ref[...]).astype(BF16)
    dv_ref[...] = proj[:, 2560:3072].astype(BF16)


def _odd_proj(x, g, w_in, dqg, dkg, tm):
    T, D = x.shape
    full = lambda a: pl.BlockSpec(a.shape, lambda i: (0,) * a.ndim)
    row = lambda c: pl.BlockSpec((tm, c), lambda i: (i, 0))
    return pl.pallas_call(
        _odd_proj_kernel,
        grid=(T // tm,),
        in_specs=[row(D)] + [full(a) for a in (g, w_in, dqg, dkg)],
        out_specs=[row(512)] * 6,
        out_shape=[jax.ShapeDtypeStruct((T, 512), BF16)] * 6,
        compiler_params=pltpu.CompilerParams(dimension_semantics=("arbitrary",), vmem_limit_bytes=VMEM_LIMIT),
        name="odd_proj",
    )(x, g, w_in, dqg, dkg)


def _pipelined(n, stages):
    for step in range(n + len(stages) - 1):
        for si, stage in enumerate(stages):
            c = step - si
            if 0 <= c < n:
                stage(c)


def _attn_a_kernel(slope_ref, sink_ref, q_ref, k_ref, v_ref, o_ref, *, tq, win):
    g = pl.program_id(1)
    qi = pl.program_id(2)
    q0 = qi * tq
    ks = pl.multiple_of(jnp.maximum(q0 - A_WINDOW, 0), A_WINDOW)
    lane_half = lax.broadcasted_iota(jnp.int32, (1, LANES), 1) // HEAD

    def both_halves(x):
        xf = x.astype(F32)
        return jnp.where(lane_half == g, xf, pltpu.roll(xf, HEAD, 1)).astype(BF16)

    k = both_halves(k_ref[0, pl.ds(ks, win), :])
    v = both_halves(v_ref[0, pl.ds(ks, win), :])
    kpos = ks + lax.broadcasted_iota(jnp.int32, (win, tq), 0)
    qpos = q0 + lax.broadcasted_iota(jnp.int32, (win, tq), 1)
    dch = (qpos >> CHUNK_SHIFT) - (kpos >> CHUNK_SHIFT)
    allowed = (dch >= 0) & (dch <= A_WINDOW_CHUNKS)
    negdist = -jnp.abs(qpos - kpos).astype(F32)
    st = [dict() for _ in range(A_GROUP)]

    def scores(i):
        qq = q_ref[0, :, (i // 2) * LANES:(i // 2 + 1) * LANES]
        qh = jnp.where(lane_half == i % 2, qq, jnp.zeros_like(qq))
        st[i]["s"] = _dot_t(k, qh)

    def softmax(i):
        hidx = A_GROUP * g + i
        s = jnp.where(allowed, st[i].pop("s") + (slope_ref[hidx] * LOG2E) * negdist, NEG)
        sink = sink_ref[hidx] * LOG2E
        m = jnp.maximum(jnp.max(s, axis=0, keepdims=True), sink)
        p = jnp.exp2(s - m)
        st[i]["den"] = jnp.sum(p, axis=0, keepdims=True) + jnp.exp2(sink - m)
        st[i]["p"] = p.astype(BF16)

    def values(i):
        st[i]["o"] = (_dot_tl(v, st[i].pop("p")) / st[i].pop("den"))[0:HEAD, :]

    _pipelined(A_GROUP, [scores, softmax, values])
    outs = [st[i]["o"] for i in range(A_GROUP)]
    o_ref[0] = jnp.concatenate(outs, axis=0).T.astype(BF16)


def _attn_a(slopes, sinks, qa, ka, va, tq):
    B, S, _ = qa.shape
    win = tq + A_WINDOW
    smem = pl.BlockSpec(memory_space=pltpu.SMEM)
    gw = A_GROUP * HEAD
    return pl.pallas_call(
        functools.partial(_attn_a_kernel, tq=tq, win=win),
        grid=(B, A_HEADS // A_GROUP, S // tq),
        in_specs=[smem, smem,
                  pl.BlockSpec((1, tq, gw), lambda b, g, i: (b, i, g)),
                  pl.BlockSpec((1, S, LANES), lambda b, g, i: (b, 0, 0)),
                  pl.BlockSpec((1, S, LANES), lambda b, g, i: (b, 0, 0))],
        out_specs=pl.BlockSpec((1, tq, gw), lambda b, g, i: (b, i, g)),
        out_shape=jax.ShapeDtypeStruct((B, S, 512), BF16),
        compiler_params=pltpu.CompilerParams(dimension_semantics=("arbitrary",) * 3, vmem_limit_bytes=VMEM_LIMIT),
        name="attn_a",
    )(slopes, sinks, qa, ka, va)


def _walk_key_blocks(qi, nchunk, step, group, max_blocks=None):
    first = qi * nchunk
    trips = first // group
    if max_blocks is None:
        def full(i, carry):
            step(i * group, None, group)
            return carry

        lax.fori_loop(0, trips, full, 0)
        for d in range(nchunk):
            step(first + d, d, 1)
    else:
        for d in range(nchunk - 1, -1, -1):
            step(first + d, d, 1)

        def full(i, carry):
            step(first - 1 - i * group, None, group)
            return carry

        lax.fori_loop(0, jnp.minimum(trips, (max_blocks + (group - 1)) // group), full, 0)


def _flash_specs(S, tq, q_width, k_width, v_width):
    return dict(
        in_specs=[pl.BlockSpec((1, tq, q_width), lambda b, p, i: (b, i, p)),
                  pl.BlockSpec((1, S, k_width), lambda b, p, i: (b, 0, p)),
                  pl.BlockSpec((1, S, v_width), lambda b, p, i: (b, 0, p))],
        out_specs=pl.BlockSpec((1, tq, LANES), lambda b, p, i: (b, i, p)),
        compiler_params=pltpu.CompilerParams(dimension_semantics=("arbitrary",) * 3, vmem_limit_bytes=VMEM_LIMIT),
    )


def _chunk_causal(t):
    krow = lax.broadcasted_iota(jnp.int32, (t, t), 0)
    qcol = lax.broadcasted_iota(jnp.int32, (t, t), 1)
    return (krow >> CHUNK_SHIFT) <= (qcol >> CHUNK_SHIFT)


def _softmax_stage(s, off, lanes, e, m_sc):
    m_prev = m_sc[e, :, lanes]
    m_new = jnp.maximum(m_prev, jnp.max(s, axis=0, keepdims=True) - off)
    alpha = jnp.exp2(m_prev - m_new)
    pr = jnp.exp2(s - (m_new + off))
    m_sc[e, :, lanes] = m_new
    return pr.astype(BF16), alpha


def _attn_b_kernel(q_ref, k_ref, v_ref, o_ref, m_sc, acc_sc, *, tq, t):
    qi = pl.program_id(2)
    nchunk = tq // t
    chains = [(e, c) for c in range(nchunk) for e in range(2)]
    lane = lax.broadcasted_iota(jnp.int32, (1, LANES), 1)
    ones_row = (HEAD, 0)
    m_sc[...] = jnp.full(m_sc.shape, NEG, F32)
    acc_sc[...] = jnp.zeros(acc_sc.shape, F32)

    def step(j, d, n):
        rows = [pl.ds(pl.multiple_of((j + g) * t, t), t) for g in range(n)]
        live = [(e, c, g) for g in range(n) for (e, c) in chains if d is None or c >= d]
        st = [dict() for _ in live]
        vals = {}

        def values_of(e, g):
            if (e, g) not in vals:
                v = v_ref[0, rows[g], :]
                own = (lane < HEAD) if e == 0 else (lane >= HEAD)
                vals[e, g] = jnp.where(own, v, jnp.where(lane == ones_row[e], 1.0, 0.0).astype(BF16))
            return vals[e, g]

        def scores(i):
            e, c, g = live[i]
            st[i]["s"] = _dot_t(k_ref[0, rows[g], e * LANES:(e + 1) * LANES],
                                q_ref[0, c * t:(c + 1) * t, e * LANES:(e + 1) * LANES])

        def softmax(i):
            e, c, g = live[i]
            s = st[i].pop("s")
            if d is not None and c == d:
                s = jnp.where(_chunk_causal(t), s, NEG)
            st[i]["p"], st[i]["alpha"] = _softmax_stage(s, 0.0, slice(c * t, (c + 1) * t), e, m_sc)

        def values(i):
            e, c, g = live[i]
            lanes = slice(c * t, (c + 1) * t)
            acc_sc[e, :, lanes] = (st[i]["alpha"] * acc_sc[e, :, lanes]
                                   + _dot_tl(values_of(e, g), st[i].pop("p")))

        _pipelined(len(live), [scores, softmax, values])

    _walk_key_blocks(qi, nchunk, step, group=nchunk)
    first = lax.broadcasted_iota(jnp.int32, (LANES, 1), 0) < HEAD
    den = [acc_sc[e, ones_row[e]:ones_row[e] + 1, :] for e in range(2)]
    o_t = jnp.where(first, acc_sc[0] / den[0], acc_sc[1] / den[1])
    o_ref[0] = o_t.T.astype(BF16)


def _attn_b(qb, kb, vb, tq, t):
    B, S, _ = qb.shape
    return pl.pallas_call(
        functools.partial(_attn_b_kernel, tq=tq, t=t),
        grid=(B, B_HEADS // 2, S // tq),
        out_shape=jax.ShapeDtypeStruct((B, S, 512), BF16),
        scratch_shapes=[pltpu.VMEM((2, 1, tq), F32), pltpu.VMEM((2, LANES, tq), F32)],
        name="attn_b",
        **_flash_specs(S, tq, 2 * LANES, 2 * LANES, LANES),
    )(qb, kb, vb)


def _attn_c_kernel(q_ref, k_ref, v_ref, o_ref, r_sc, acc_sc, *, tq, t):
    qi = pl.program_id(2)
    nchunk = tq // t
    first = qi * nchunk
    chains = [(e, c) for c in range(nchunk) for e in range(2)]
    krow = lax.broadcasted_iota(jnp.int32, (t, t), 0)
    qcol = lax.broadcasted_iota(jnp.int32, (t, t), 1)
    from_here = jnp.where(qcol >= krow, 1.0, 0.0).astype(BF16)
    from_here = jnp.concatenate([from_here, from_here], axis=1)
    lane_lo = _lane_lo((1, LANES))
    r_sc[...] = jnp.zeros(r_sc.shape, F32)
    acc_sc[...] = jnp.zeros(acc_sc.shape, F32)

    def step(backs):
        live = [(e, c, b) for b in range(len(backs)) for (e, c) in chains]
        st = [dict() for _ in live]

        def block(i):
            e, c, b = live[i]
            return first + c - backs[b]

        def rows(i):
            return pl.ds(pl.multiple_of(jnp.maximum(block(i), 0) * t, t), t)

        def scores(i):
            e, c, b = live[i]
            qq = q_ref[0, c * t:(c + 1) * t, :]
            st[i]["z"] = _dot_t(k_ref[0, rows(i), :], jnp.where(lane_lo == (e == 0), qq, jnp.zeros_like(qq)))

        def softplus(i):
            e, c, b = live[i]
            z = st[i]["z"]
            sp = jnp.maximum(z, jnp.log2(1.0 + jnp.exp2(jnp.minimum(z, 126.0))))
            if isinstance(backs[b], int) and backs[b] == 0:
                sp = jnp.where(krow < qcol, sp, 0.0)
            hi = sp.astype(BF16)
            lo = (sp - hi.astype(F32)).astype(BF16)
            st[i]["hilo"] = jnp.concatenate([hi, lo], axis=0)

        def suffix(i):
            st[i]["suf"] = _dot(from_here, st[i].pop("hilo"))

        def weights(i):
            e, c, b = live[i]
            lanes = slice(c * t, (c + 1) * t)
            suf = st[i].pop("suf")
            a = jnp.exp2(st[i].pop("z") - suf)
            if isinstance(backs[b], int) and backs[b] == 0:
                a = jnp.where(krow < qcol, a, 0.0)
            st[i]["a"] = a.astype(BF16)
            r_prev = r_sc[e, :, lanes]
            w = jnp.exp2(-r_prev)
            tot = suf[0:1, :]
            if not (isinstance(backs[b], int) and backs[b] <= c):
                exists = (block(i) >= 0).astype(F32)
                w, tot = w * exists, tot * exists
            st[i]["w"] = w
            r_sc[e, :, lanes] = r_prev + tot

        def values(i):
            e, c, b = live[i]
            lanes = slice(c * t, (c + 1) * t)
            acc_sc[e, :, lanes] = (acc_sc[e, :, lanes]
                                   + st[i].pop("w") * _dot_tl(v_ref[0, rows(i), :], st[i].pop("a")))

        _pipelined(len(live), [scores, softplus, suffix, weights, values])

    def still_live():
        return jnp.max(jnp.exp2(-r_sc[...])) > 0.0

    step([0, 1])

    def further(carry):
        step([carry[0]])
        return carry[0] + 1, still_live()

    lax.while_loop(lambda carry: (carry[0] < first + nchunk) & carry[1], further, (2, still_live()))
    first_head = lax.broadcasted_iota(jnp.int32, (LANES, 1), 0) < HEAD
    o_ref[0] = jnp.where(first_head, acc_sc[0], acc_sc[1]).T.astype(BF16)


def _attn_c(cq, ck, cv, tq, t):
    B, S, _ = cq.shape
    return pl.pallas_call(
        functools.partial(_attn_c_kernel, tq=tq, t=t),
        grid=(B, C_HEADS // 2, S // tq),
        out_shape=jax.ShapeDtypeStruct((B, S, 512), BF16),
        scratch_shapes=[pltpu.VMEM((2, 1, tq), F32), pltpu.VMEM((2, LANES, tq), F32)],
        name="attn_c",
        **_flash_specs(S, tq, LANES, LANES, LANES),
    )(cq, ck, cv)


def _attn_d_kernel(slope_ref, reach_ref, q_ref, k_ref, v_ref, lam_ref, subln_ref, o_ref, m_sc, acc_sc,
                   *, tq, t, lambda_init):
    h = pl.program_id(1)
    qi = pl.program_id(2)
    nchunk = tq // t
    chains = [(e, c) for c in range(nchunk) for e in range(2)]
    slope = slope_ref[h] * LOG2E
    lane_lo = _lane_lo((1, LANES))
    key_bias = slope * lax.broadcasted_iota(jnp.int32, (t, LANES), 0).astype(F32)
    m_sc[...] = jnp.full(m_sc.shape, NEG, F32)
    acc_sc[...] = jnp.zeros(acc_sc.shape, F32)

    def step(j, d, n):
        rows = [pl.ds(pl.multiple_of((j - g) * t, t), t) for g in range(n)]
        live = [(e, c, g) for g in range(n) for (e, c) in chains if d is None or c >= d]
        st = [dict() for _ in live]
        vals = {}

        def values_of(g):
            if g not in vals:
                vals[g] = jnp.concatenate([v_ref[0, rows[g], :].T, jnp.ones((DEN_ROWS, t), BF16)], axis=0)
            return vals[g]

        def scores(i):
            e, c, g = live[i]
            qq = q_ref[0, c * t:(c + 1) * t, :]
            st[i]["s"] = _dot_t(k_ref[0, rows[g], :], jnp.where(lane_lo == (e == 0), qq, jnp.zeros_like(qq)))

        def softmax(i):
            e, c, g = live[i]
            s = st[i].pop("s")
            if d is not None and c == d:
                krow = lax.broadcasted_iota(jnp.int32, (t, t), 0)
                qcol = lax.broadcasted_iota(jnp.int32, (t, t), 1)
                s = s + slope * jnp.minimum(krow, 2 * qcol - krow).astype(F32)
                s = jnp.where(_chunk_causal(t), s, NEG)
                off = 0.0
            else:
                s = jnp.concatenate([s[:, b * LANES:(b + 1) * LANES] + key_bias for b in range(t // LANES)], axis=1)
                off = (slope * ((qi * nchunk + c - j + g) * t).astype(F32) if d is None
                       else slope * float((c - d) * t))
            st[i]["p"], st[i]["alpha"] = _softmax_stage(s, off, slice(c * t, (c + 1) * t), e, m_sc)

        def values(i):
            e, c, g = live[i]
            lanes = slice(c * t, (c + 1) * t)
            acc_sc[e, :, lanes] = (st[i]["alpha"] * acc_sc[e, :, lanes]
                                   + _dot(values_of(g), st[i].pop("p")))

        _pipelined(len(live), [scores, softmax, values])

    _walk_key_blocks(qi, nchunk, step, group=nchunk, max_blocks=reach_ref[h])
    lf = lam_ref[...]
    lam = (jnp.exp(jnp.sum(lf[0:1] * lf[1:2], axis=-1, keepdims=True))
           - jnp.exp(jnp.sum(lf[2:3] * lf[3:4], axis=-1, keepdims=True)) + lambda_init)
    sm = [acc_sc[e, 0:LANES, :] / acc_sc[e, LANES:LANES + 1, :] for e in range(2)]
    o = (sm[0] - lam * sm[1]).T
    o = _rms(o, LANES) * subln_ref[...] * (1.0 - lambda_init)
    o_ref[0] = o.astype(BF16)


def _attn_d(slopes, reach, dq, dk, dv, lam, subln, lambda_init, tq, t):
    B, S, _ = dq.shape
    specs = _flash_specs(S, tq, LANES, LANES, LANES)
    whole = lambda a: pl.BlockSpec(a.shape, lambda b, h, i: (0, 0))
    smem = pl.BlockSpec(memory_space=pltpu.SMEM)
    specs["in_specs"] = [smem, smem] + specs["in_specs"] + [whole(lam), whole(subln)]
    return pl.pallas_call(
        functools.partial(_attn_d_kernel, tq=tq, t=t, lambda_init=lambda_init),
        grid=(B, D_HEADS, S // tq),
        out_shape=jax.ShapeDtypeStruct((B, S, 512), BF16),
        scratch_shapes=[pltpu.VMEM((2, 1, tq), F32), pltpu.VMEM((2, LANES + DEN_ROWS, tq), F32)],
        name="attn_d",
        **specs,
    )(slopes, reach, dq, dk, dv, lam, subln)


def _out_mlp_kernel(x_ref, ma_ref, mb_ref, wo_ref, g_ref, wu_ref, wd_ref, o_ref, *, tf):
    half = ma_ref.shape[-1]
    x1 = x_ref[...] + _dot(ma_ref[...], wo_ref[0:half, :]) + _dot(mb_ref[...], wo_ref[half:2 * half, :])
    h = (_rms(x1, x1.shape[-1]) * g_ref[...]).astype(BF16)
    o_ref[...] = x1
    for f in range(wu_ref.shape[-1] // tf):
        u = jnp.maximum(_dot(h, wu_ref[:, f * tf:(f + 1) * tf]), 0.0)
        o_ref[...] += _dot((u * u).astype(BF16), wd_ref[f * tf:(f + 1) * tf, :])


def _out_mlp(x, ma, mb, wo, g, wu, wd, tm, tf):
    T, D = x.shape
    full = lambda a: pl.BlockSpec(a.shape, lambda i: (0,) * a.ndim, pipeline_mode=pl.Buffered(1))
    row = lambda c: pl.BlockSpec((tm, c), lambda i: (i, 0))
    return pl.pallas_call(
        functools.partial(_out_mlp_kernel, tf=tf),
        grid=(T // tm,),
        in_specs=[row(D), row(ma.shape[-1]), row(mb.shape[-1]), full(wo), full(g), full(wu), full(wd)],
        out_specs=row(D),
        out_shape=jax.ShapeDtypeStruct((T, D), F32),
        compiler_params=pltpu.CompilerParams(dimension_semantics=("arbitrary",), vmem_limit_bytes=VMEM_LIMIT),
        name="out_mlp",
    )(x, ma, mb, wo, g, wu, wd)


def _alibi_slopes(n):
    return 2.0 ** (-8.0 * jnp.arange(1, n + 1, dtype=F32) / n)


def _alibi_reach(slopes, q_gain, k_gain, t, S):
    smax = 1.01 * HEAD * 0.125 * LOG2E * jnp.max(jnp.abs(q_gain)) * jnp.max(jnp.abs(k_gain))
    dist = (160.0 + 2.0 * smax) / (slopes * LOG2E)
    blocks = jnp.floor((dist - 1.0) / t) + 1.0
    return jnp.clip(blocks, 0.0, float(S // t)).astype(jnp.int32)


def _tile(n, want):
    t = min(n, want)
    assert n % t == 0, (n, t)
    return t


def kernel(x, positions, norm_mix_g, norm_ffn_g, mlp_w_up, mlp_w_down, ev_w_in, ev_w_out, a_q_norm, a_k_norm, a_sinks, b_cq_norm, b_ckv_norm, b_w_uq, b_w_ukv, b_q_norm, b_k_norm, od_w_in, od_w_out, d_q_norm, d_k_norm, d_lambda, d_subln):
    B, S, D = x.shape
    T = B * S
    depth = norm_mix_g.shape[0]
    tm_proj = _tile(T, 512)
    tm_mlp = _tile(T, 512)
    t_a = _tile(S, 256)
    tq_b, t_b = _tile(S, 1024), _tile(S, 512)
    tq_c, t_c = _tile(S, 1024), _tile(S, 256)
    tq_d, t_d = _tile(S, 1024), _tile(S, 512)
    tf = 512

    xf = x.reshape(T, D)
    pos = positions.reshape(T, 1)
    row2 = lambda a: a.reshape(1, -1).astype(F32)
    pair = lambda a: jnp.concatenate([a, a]).reshape(1, LANES).astype(F32)
    pad_qk = lambda a: jnp.pad(a.astype(F32), (0, LANES - B_QK)).reshape(1, LANES)
    inv = ROPE_THETA ** (-jnp.arange(B_ROPE_HALF, dtype=F32) / B_ROPE_HALF)
    inv_lanes = jnp.zeros((LANES,), F32).at[HEAD:HEAD + B_ROPE_HALF].set(inv).at[HEAD + B_ROPE_HALF:B_QK].set(inv)
    inv_lanes = inv_lanes.reshape(1, LANES)
    slopes_a = _alibi_slopes(A_HEADS)
    slopes_d = _alibi_slopes(D_HEADS)

    for layer in range(depth):
        j = layer // 2
        g_mix = row2(norm_mix_g[layer])
        if layer % 2 == 0:
            w_in = ev_w_in[j]
            w_in = jnp.concatenate([w_in[:, :1152], jnp.zeros((D, HEAD), F32), w_in[:, 1152:],
                                    jnp.zeros((D, LANES - B_QK), F32)], axis=1).astype(BF16)
            wuq = jnp.pad(b_w_uq[j].reshape(-1, B_HEADS, B_QK), ((0, 0), (0, 0), (0, LANES - B_QK)))
            wuq = wuq.reshape(-1, B_HEADS * LANES).astype(BF16)
            qa, ka, va, qb, kb, vb = _even_proj(
                xf, pos, g_mix, w_in, pair(a_q_norm[j]), pair(a_k_norm[j]), row2(b_cq_norm[j]),
                row2(b_ckv_norm[j]), wuq, b_w_ukv[j].astype(BF16), pad_qk(b_q_norm[j]), pad_qk(b_k_norm[j]),
                inv_lanes, tm_proj)
            r3 = lambda a: a.reshape(B, S, a.shape[-1])
            ma = _attn_a(slopes_a, a_sinks[j].astype(F32), r3(qa), r3(ka), r3(va), t_a).reshape(T, -1)
            mb = _attn_b(r3(qb), r3(kb), r3(vb), tq_b, t_b).reshape(T, -1)
            w_out = ev_w_out[j]
        else:
            lambda_init = 0.8 - 0.6 * math.exp(-0.3 * layer)
            cq, ck, cv, dq, dk, dv = _odd_proj(
                xf, g_mix, od_w_in[j].astype(BF16), d_q_norm[j].reshape(1, LANES).astype(F32),
                d_k_norm[j].reshape(1, LANES).astype(F32), tm_proj)
            r3 = lambda a: a.reshape(B, S, a.shape[-1])
            ma = _attn_c(r3(cq), r3(ck), r3(cv), tq_c, t_c).reshape(T, -1)
            reach = _alibi_reach(slopes_d, d_q_norm[j], d_k_norm[j], t_d, S)
            mb = _attn_d(slopes_d, reach, r3(dq), r3(dk), r3(dv), d_lambda[j].astype(F32), row2(d_subln[j]),
                         lambda_init, tq_d, t_d).reshape(T, -1)
            w_out = od_w_out[j]
        xf = _out_mlp(xf, ma, mb, w_out.astype(BF16), row2(norm_ffn_g[layer]),
                      mlp_w_up[layer].astype(BF16), mlp_w_down[layer].astype(BF16), tm_mlp, tf)
    return xf.reshape(B, S, D)
```

```python
import functools
import math

import jax
import jax.numpy as jnp
from jax import lax
from jax.experimental import pallas as pl
from jax.experimental.pallas import tpu as pltpu

F32 = jnp.float32
BF16 = jnp.bfloat16

EPS = 1e-6
CHUNK = 64
CHUNK_SHIFT = 6
LANES = 128
HEAD = 64
A_HEADS = 8
A_GROUP = 4
A_WINDOW = 128
A_WINDOW_CHUNKS = A_WINDOW // CHUNK
B_HEADS = 8
B_QK = 96
B_ROPE_HALF = 16
ROPE_THETA = 10000.0
C_HEADS = 8
D_HEADS = 4
B_Q_RANK = 256
B_KV_RANK = 128
MIX_HALF = 512
E_AK = A_HEADS * HEAD
E_AV = E_AK + (A_HEADS // A_GROUP) * HEAD
E_CQ = E_AV + (A_HEADS // A_GROUP) * HEAD
E_CKV = E_CQ + B_Q_RANK
E_ROPE = E_CKV + B_KV_RANK
E_END = E_ROPE + LANES
O_CK, O_CV, O_DQ, O_DK, O_DV, O_END = (i * MIX_HALF for i in range(1, 7))
NEG = -1e30
LOG2E = math.log2(math.e)
DEN_ROWS = 16
VMEM_LIMIT = 56 * 1024 * 1024


def _dot(a, b):
    return jnp.dot(a, b, preferred_element_type=F32)


def _dot_t(a, b):
    return lax.dot_general(a, b, (((1,), (1,)), ((), ())), preferred_element_type=F32)


def _dot_tl(a, b):
    return lax.dot_general(a, b, (((0,), (0,)), ((), ())), preferred_element_type=F32)


def _rms(x, denom):
    return x * lax.rsqrt(jnp.sum(x * x, axis=-1, keepdims=True) * (1.0 / denom) + EPS)


def _pair_rms(x, lo):
    xx = x * x
    s_lo = jnp.sum(jnp.where(lo, xx, 0.0), axis=-1, keepdims=True)
    s_hi = jnp.sum(jnp.where(lo, 0.0, xx), axis=-1, keepdims=True)
    r = jnp.where(lo, lax.rsqrt(s_lo * (1.0 / HEAD) + EPS), lax.rsqrt(s_hi * (1.0 / HEAD) + EPS))
    return x * r


def _lane_lo(shape):
    return lax.broadcasted_iota(jnp.int32, shape, len(shape) - 1) < HEAD


def _even_proj_kernel(x_ref, pos_ref, g_ref, w_in_ref, aqg_ref, akg_ref, cqg_ref, ckvg_ref,
                      wuq_ref, wukv_ref, bqg_ref, bkg_ref, inv_ref, place_ref,
                      qa_ref, ka_ref, va_ref, qb_ref, kb_ref, vb_ref):
    x = x_ref[...]
    h = _rms(x, x.shape[-1]) * g_ref[...]
    proj = _dot(h.astype(BF16), w_in_ref[...])
    lo = _lane_lo((1, LANES))
    for p in range(A_HEADS // 2):
        seg = proj[:, p * LANES:(p + 1) * LANES]
        qa_ref[:, p * LANES:(p + 1) * LANES] = (_pair_rms(seg, lo) * aqg_ref[...] * (0.125 * LOG2E)).astype(BF16)
    ka_ref[...] = (_pair_rms(proj[:, E_AK:E_AV], lo) * akg_ref[...]).astype(BF16)
    va_ref[...] = proj[:, E_AV:E_CQ].astype(BF16)
    cq = _rms(proj[:, E_CQ:E_CKV], B_Q_RANK) * cqg_ref[...]
    ckv = _rms(proj[:, E_CKV:E_ROPE], B_KV_RANK) * ckvg_ref[...]
    krope = proj[:, E_ROPE:E_END]
    qall = _dot(cq.astype(BF16), wuq_ref[...])
    kvall = _dot(ckv.astype(BF16), wukv_ref[...])
    ang = inv_ref[...] * pos_ref[0].astype(F32)
    trig = jnp.concatenate([jnp.cos(ang), jnp.sin(ang)], axis=0)
    trig_hi = trig.astype(BF16)
    trig_lo = (trig - trig_hi.astype(F32)).astype(BF16)
    placed = _dot_tl(jnp.concatenate([trig_hi, trig_lo], axis=0), place_ref[...])
    cosf = placed[:, 0:LANES] + jnp.where(lo, 1.0, 0.0)
    s_first = placed[:, LANES:2 * LANES]
    s_second = placed[:, 2 * LANES:3 * LANES]

    def rope(t):
        return (t * cosf + pltpu.roll(t, LANES - B_ROPE_HALF, 1) * s_first
                + pltpu.roll(t, B_ROPE_HALF, 1) * s_second)

    scale_b = B_QK ** -0.5 * LOG2E
    krope_rot = rope(krope * bkg_ref[...])
    krope_ss = jnp.sum(krope * krope, axis=-1, keepdims=True)
    for p in range(B_HEADS // 2):
        vpair = []
        for hh in (2 * p, 2 * p + 1):
            qh = qall[:, hh * LANES:(hh + 1) * LANES]
            qn = _rms(qh, B_QK) * bqg_ref[...]
            qb_ref[:, hh * LANES:(hh + 1) * LANES] = (rope(qn) * scale_b).astype(BF16)
            kvh = kvall[:, hh * LANES:(hh + 1) * LANES]
            knope = jnp.where(lo, kvh, 0.0)
            ss = jnp.sum(knope * knope, axis=-1, keepdims=True) + krope_ss
            kn = (knope * bkg_ref[...] + krope_rot) * lax.rsqrt(ss * (1.0 / B_QK) + EPS)
            kb_ref[:, hh * LANES:(hh + 1) * LANES] = kn.astype(BF16)
            vpair.append(kvh)
        vb_ref[:, p * LANES:(p + 1) * LANES] = jnp.where(
            lo, pltpu.roll(vpair[0], HEAD, 1), vpair[1]).astype(BF16)


def _even_proj(x, pos, g, w_in, aqg, akg, cqg, ckvg, wuq, wukv, bqg, bkg, inv, place, tm):
    T, D = x.shape
    full = lambda a: pl.BlockSpec(a.shape, lambda i: (0,) * a.ndim)
    row = lambda c: pl.BlockSpec((tm, c), lambda i: (i, 0))
    outs = (E_AK, E_AV - E_AK, E_CQ - E_AV, B_HEADS * LANES, B_HEADS * LANES, MIX_HALF)
    pos = pos.reshape(T // tm, 1, tm)
    return pl.pallas_call(
        _even_proj_kernel,
        grid=(T // tm,),
        in_specs=[row(D), pl.BlockSpec((1, 1, tm), lambda i: (i, 0, 0))]
        + [full(a) for a in (g, w_in, aqg, akg, cqg, ckvg, wuq, wukv, bqg, bkg, inv, place)],
        out_specs=[row(c) for c in outs],
        out_shape=[jax.ShapeDtypeStruct((T, c), BF16) for c in outs],
        compiler_params=pltpu.CompilerParams(dimension_semantics=("arbitrary",), vmem_limit_bytes=VMEM_LIMIT),
        name="even_proj",
    )(x, pos, g, w_in, aqg, akg, cqg, ckvg, wuq, wukv, bqg, bkg, inv, place)


def _odd_proj_kernel(x_ref, g_ref, w_in_ref, dqg_ref, dkg_ref,
                     cq_ref, ck_ref, cv_ref, dq_ref, dk_ref, dv_ref):
    x = x_ref[...]
    h = _rms(x, x.shape[-1]) * g_ref[...]
    proj = _dot(h.astype(BF16), w_in_ref[...])
    lo = _lane_lo((1, LANES))
    qscale = 0.125 * LOG2E
    cq_ref[...] = (proj[:, 0:O_CK] * qscale).astype(BF16)
    ck_ref[...] = proj[:, O_CK:O_CV].astype(BF16)
    cv_ref[...] = proj[:, O_CV:O_DQ].astype(BF16)
    for hh in range(D_HEADS):
        sl = slice(hh * LANES, (hh + 1) * LANES)
        dq_ref[:, sl] = (_pair_rms(proj[:, O_DQ + hh * LANES:O_DQ + (hh + 1) * LANES], lo)
                         * dqg_ref[...] * qscale).astype(BF16)
        dk_ref[:, sl] = (_pair_rms(proj[:, O_DK + hh * LANES:O_DK + (hh + 1) * LANES], lo)
                         * dkg_ref[...]).astype(BF16)
    dv_ref[...] = proj[:, O_DV:O_END].astype(BF16)


def _odd_proj(x, g, w_in, dqg, dkg, tm):
    T, D = x.shape
    full = lambda a: pl.BlockSpec(a.shape, lambda i: (0,) * a.ndim)
    row = lambda c: pl.BlockSpec((tm, c), lambda i: (i, 0))
    return pl.pallas_call(
        _odd_proj_kernel,
        grid=(T // tm,),
        in_specs=[row(D)] + [full(a) for a in (g, w_in, dqg, dkg)],
        out_specs=[row(MIX_HALF)] * 6,
        out_shape=[jax.ShapeDtypeStruct((T, MIX_HALF), BF16)] * 6,
        compiler_params=pltpu.CompilerParams(dimension_semantics=("arbitrary",), vmem_limit_bytes=VMEM_LIMIT),
        name="odd_proj",
    )(x, g, w_in, dqg, dkg)


def _pipelined(n, stages):
    for step in range(n + len(stages) - 1):
        for si, stage in enumerate(stages):
            c = step - si
            if 0 <= c < n:
                stage(c)


def _attn_a_kernel(slope_ref, sink_ref, q_ref, k_ref, v_ref, o_ref, *, tq, win):
    g = pl.program_id(1)
    qi = pl.program_id(2)
    q0 = qi * tq
    ks = pl.multiple_of(jnp.maximum(q0 - A_WINDOW, 0), A_WINDOW)
    lane_half = lax.broadcasted_iota(jnp.int32, (1, LANES), 1) // HEAD

    def both_halves(x):
        xf = x.astype(F32)
        return jnp.where(lane_half == g, xf, pltpu.roll(xf, HEAD, 1)).astype(BF16)

    k = both_halves(k_ref[0, pl.ds(ks, win), :])
    v = both_halves(v_ref[0, pl.ds(ks, win), :])
    kpos = ks + lax.broadcasted_iota(jnp.int32, (win, tq), 0)
    qpos = q0 + lax.broadcasted_iota(jnp.int32, (win, tq), 1)
    dch = (qpos >> CHUNK_SHIFT) - (kpos >> CHUNK_SHIFT)
    allowed = (dch >= 0) & (dch <= A_WINDOW_CHUNKS)
    negdist = -jnp.abs(qpos - kpos).astype(F32)
    st = [dict() for _ in range(A_GROUP)]

    def scores(i):
        qq = q_ref[0, :, (i // 2) * LANES:(i // 2 + 1) * LANES]
        qh = jnp.where(lane_half == i % 2, qq, jnp.zeros_like(qq))
        st[i]["s"] = _dot_t(k, qh)

    def softmax(i):
        hidx = A_GROUP * g + i
        s = jnp.where(allowed, st[i].pop("s") + (slope_ref[hidx] * LOG2E) * negdist, NEG)
        sink = sink_ref[hidx] * LOG2E
        m = jnp.maximum(jnp.max(s, axis=0, keepdims=True), sink)
        p = jnp.exp2(s - m)
        st[i]["den"] = jnp.sum(p, axis=0, keepdims=True) + jnp.exp2(sink - m)
        st[i]["p"] = p.astype(BF16)

    def values(i):
        st[i]["o"] = (_dot_tl(v, st[i].pop("p")) / st[i].pop("den"))[0:HEAD, :]

    _pipelined(A_GROUP, [scores, softmax, values])
    outs = [st[i]["o"] for i in range(A_GROUP)]
    o_ref[0] = jnp.concatenate(outs, axis=0).T.astype(BF16)


def _attn_a(slopes, sinks, qa, ka, va, tq):
    B, S, _ = qa.shape
    win = tq + A_WINDOW
    smem = pl.BlockSpec(memory_space=pltpu.SMEM)
    gw = A_GROUP * HEAD
    return pl.pallas_call(
        functools.partial(_attn_a_kernel, tq=tq, win=win),
        grid=(B, A_HEADS // A_GROUP, S // tq),
        in_specs=[smem, smem,
                  pl.BlockSpec((1, tq, gw), lambda b, g, i: (b, i, g)),
                  pl.BlockSpec((1, S, LANES), lambda b, g, i: (b, 0, 0)),
                  pl.BlockSpec((1, S, LANES), lambda b, g, i: (b, 0, 0))],
        out_specs=pl.BlockSpec((1, tq, gw), lambda b, g, i: (b, i, g)),
        out_shape=jax.ShapeDtypeStruct((B, S, MIX_HALF), BF16),
        compiler_params=pltpu.CompilerParams(dimension_semantics=("arbitrary",) * 3, vmem_limit_bytes=VMEM_LIMIT),
        name="attn_a",
    )(slopes, sinks, qa, ka, va)


def _walk_key_blocks(qi, nchunk, step, group, max_blocks=None):
    first = qi * nchunk
    trips = first // group
    if max_blocks is None:
        def full(i, carry):
            step([(i * group + g, None) for g in range(group)])
            return carry

        lax.fori_loop(0, trips, full, 0)
        step([(first + d, d) for d in range(nchunk)])
    else:
        step([(first + d, d) for d in range(nchunk - 1, -1, -1)])

        def full(i, carry):
            step([(first - 1 - i * group - g, None) for g in range(group)])
            return carry

        lax.fori_loop(0, jnp.minimum(trips, (max_blocks + (group - 1)) // group), full, 0)


def _flash_specs(S, tq, q_width, k_width, v_width):
    return dict(
        in_specs=[pl.BlockSpec((1, tq, q_width), lambda b, p, i: (b, i, p)),
                  pl.BlockSpec((1, S, k_width), lambda b, p, i: (b, 0, p)),
                  pl.BlockSpec((1, S, v_width), lambda b, p, i: (b, 0, p))],
        out_specs=pl.BlockSpec((1, tq, LANES), lambda b, p, i: (b, i, p)),
        compiler_params=pltpu.CompilerParams(dimension_semantics=("arbitrary",) * 3, vmem_limit_bytes=VMEM_LIMIT),
    )


def _chunk_causal(t):
    krow = lax.broadcasted_iota(jnp.int32, (t, t), 0)
    qcol = lax.broadcasted_iota(jnp.int32, (t, t), 1)
    return (krow >> CHUNK_SHIFT) <= (qcol >> CHUNK_SHIFT)


def _softmax_stage(s, off, lanes, e, m_sc):
    m_prev = m_sc[e, :, lanes]
    m_new = jnp.maximum(m_prev, jnp.max(s, axis=0, keepdims=True) - off)
    alpha = jnp.exp2(m_prev - m_new)
    pr = jnp.exp2(s - (m_new + off))
    m_sc[e, :, lanes] = m_new
    return pr.astype(BF16), alpha


def _attn_b_kernel(q_ref, k_ref, v_ref, o_ref, m_sc, acc_sc, *, tq, t):
    qi = pl.program_id(2)
    nchunk = tq // t
    chains = [(e, c) for c in range(nchunk) for e in range(2)]
    lane = lax.broadcasted_iota(jnp.int32, (1, LANES), 1)
    ones_row = (HEAD, 0)
    m_sc[...] = jnp.full(m_sc.shape, NEG, F32)
    acc_sc[...] = jnp.zeros(acc_sc.shape, F32)

    def step(blocks):
        rows = [pl.ds(pl.multiple_of(j * t, t), t) for j, _ in blocks]
        live = [(e, c, g) for g, (_, d) in enumerate(blocks) for (e, c) in chains if d is None or c >= d]
        st = [dict() for _ in live]
        vals = {}

        def values_of(e, g):
            if (e, g) not in vals:
                v = v_ref[0, rows[g], :]
                own = (lane < HEAD) if e == 0 else (lane >= HEAD)
                vals[e, g] = jnp.where(own, v, jnp.where(lane == ones_row[e], 1.0, 0.0).astype(BF16))
            return vals[e, g]

        def scores(i):
            e, c, g = live[i]
            st[i]["s"] = _dot_t(k_ref[0, rows[g], e * LANES:(e + 1) * LANES],
                                q_ref[0, c * t:(c + 1) * t, e * LANES:(e + 1) * LANES])

        def softmax(i):
            e, c, g = live[i]
            s = st[i].pop("s")
            if c == blocks[g][1]:
                s = jnp.where(_chunk_causal(t), s, NEG)
            st[i]["p"], st[i]["alpha"] = _softmax_stage(s, 0.0, slice(c * t, (c + 1) * t), e, m_sc)

        def values(i):
            e, c, g = live[i]
            lanes = slice(c * t, (c + 1) * t)
            acc_sc[e, :, lanes] = (st[i]["alpha"] * acc_sc[e, :, lanes]
                                   + _dot_tl(values_of(e, g), st[i].pop("p")))

        _pipelined(len(live), [scores, softmax, values])

    _walk_key_blocks(qi, nchunk, step, group=nchunk)
    first = lax.broadcasted_iota(jnp.int32, (LANES, 1), 0) < HEAD
    den = [acc_sc[e, ones_row[e]:ones_row[e] + 1, :] for e in range(2)]
    o_t = jnp.where(first, acc_sc[0] / den[0], acc_sc[1] / den[1])
    o_ref[0] = o_t.T.astype(BF16)


def _attn_b(qb, kb, vb, tq, t):
    B, S, _ = qb.shape
    return pl.pallas_call(
        functools.partial(_attn_b_kernel, tq=tq, t=t),
        grid=(B, B_HEADS // 2, S // tq),
        out_shape=jax.ShapeDtypeStruct((B, S, MIX_HALF), BF16),
        scratch_shapes=[pltpu.VMEM((2, 1, tq), F32), pltpu.VMEM((2, LANES, tq), F32)],
        name="attn_b",
        **_flash_specs(S, tq, 2 * LANES, 2 * LANES, LANES),
    )(qb, kb, vb)


def _attn_c_kernel(q_ref, k_ref, v_ref, o_ref, r_sc, acc_sc, *, tq, t):
    qi = pl.program_id(2)
    nchunk = tq // t
    first = qi * nchunk
    chains = [(e, c) for c in range(nchunk) for e in range(2)]
    krow = lax.broadcasted_iota(jnp.int32, (t, t), 0)
    qcol = lax.broadcasted_iota(jnp.int32, (t, t), 1)
    from_here = jnp.where(qcol >= krow, 1.0, 0.0).astype(BF16)
    from_here = jnp.concatenate([from_here, from_here], axis=1)
    lane_lo = _lane_lo((1, LANES))
    r_sc[...] = jnp.zeros(r_sc.shape, F32)
    acc_sc[...] = jnp.zeros(acc_sc.shape, F32)

    def step(backs):
        live = [(e, c, b) for b in range(len(backs)) for (e, c) in chains]
        st = [dict() for _ in live]

        def block(i):
            e, c, b = live[i]
            return first + c - backs[b]

        def rows(i):
            return pl.ds(pl.multiple_of(jnp.maximum(block(i), 0) * t, t), t)

        def scores(i):
            e, c, b = live[i]
            qq = q_ref[0, c * t:(c + 1) * t, :]
            st[i]["z"] = _dot_t(k_ref[0, rows(i), :], jnp.where(lane_lo == (e == 0), qq, jnp.zeros_like(qq)))

        def softplus(i):
            e, c, b = live[i]
            z = st[i]["z"]
            sp = jnp.maximum(z, jnp.log2(1.0 + jnp.exp2(jnp.minimum(z, 126.0))))
            if isinstance(backs[b], int) and backs[b] == 0:
                sp = jnp.where(krow < qcol, sp, 0.0)
            hi = sp.astype(BF16)
            lo = (sp - hi.astype(F32)).astype(BF16)
            st[i]["hilo"] = jnp.concatenate([hi, lo], axis=0)

        def suffix(i):
            st[i]["suf"] = _dot(from_here, st[i].pop("hilo"))

        def weights(i):
            e, c, b = live[i]
            lanes = slice(c * t, (c + 1) * t)
            suf = st[i].pop("suf")
            a = jnp.exp2(st[i].pop("z") - suf)
            if isinstance(backs[b], int) and backs[b] == 0:
                a = jnp.where(krow < qcol, a, 0.0)
            st[i]["a"] = a.astype(BF16)
            r_prev = r_sc[e, :, lanes]
            w = jnp.exp2(-r_prev)
            tot = suf[0:1, :]
            if not (isinstance(backs[b], int) and backs[b] <= c):
                exists = (block(i) >= 0).astype(F32)
                w, tot = w * exists, tot * exists
            st[i]["w"] = w
            r_sc[e, :, lanes] = r_prev + tot

        def values(i):
            e, c, b = live[i]
            lanes = slice(c * t, (c + 1) * t)
            acc_sc[e, :, lanes] = (acc_sc[e, :, lanes]
                                   + st[i].pop("w") * _dot_tl(v_ref[0, rows(i), :], st[i].pop("a")))

        _pipelined(len(live), [scores, softplus, suffix, weights, values])

    def still_live():
        return jnp.max(jnp.exp2(-r_sc[...])) > 0.0

    step([0, 1])

    def further(carry):
        step([carry[0]])
        return carry[0] + 1, still_live()

    lax.while_loop(lambda carry: (carry[0] < first + nchunk) & carry[1], further, (2, still_live()))
    first_head = lax.broadcasted_iota(jnp.int32, (LANES, 1), 0) < HEAD
    o_ref[0] = jnp.where(first_head, acc_sc[0], acc_sc[1]).T.astype(BF16)


def _attn_c(cq, ck, cv, tq, t):
    B, S, _ = cq.shape
    return pl.pallas_call(
        functools.partial(_attn_c_kernel, tq=tq, t=t),
        grid=(B, C_HEADS // 2, S // tq),
        out_shape=jax.ShapeDtypeStruct((B, S, MIX_HALF), BF16),
        scratch_shapes=[pltpu.VMEM((2, 1, tq), F32), pltpu.VMEM((2, LANES, tq), F32)],
        name="attn_c",
        **_flash_specs(S, tq, LANES, LANES, LANES),
    )(cq, ck, cv)


def _attn_d_kernel(slope_ref, reach_ref, q_ref, k_ref, v_ref, lam_ref, subln_ref, o_ref, m_sc, acc_sc,
                   *, tq, t, lambda_init):
    h = pl.program_id(1)
    qi = pl.program_id(2)
    nchunk = tq // t
    chains = [(e, c) for c in range(nchunk) for e in range(2)]
    slope = slope_ref[h] * LOG2E
    lane_lo = _lane_lo((1, LANES))
    key_bias = slope * lax.broadcasted_iota(jnp.int32, (t, LANES), 0).astype(F32)
    m_sc[...] = jnp.full(m_sc.shape, NEG, F32)
    acc_sc[...] = jnp.zeros(acc_sc.shape, F32)

    def step(blocks):
        rows = [pl.ds(pl.multiple_of(j * t, t), t) for j, _ in blocks]
        live = [(e, c, g) for g, (_, d) in enumerate(blocks) for (e, c) in chains if d is None or c >= d]
        st = [dict() for _ in live]
        vals = {}

        def values_of(g):
            if g not in vals:
                vals[g] = jnp.concatenate([v_ref[0, rows[g], :].T, jnp.ones((DEN_ROWS, t), BF16)], axis=0)
            return vals[g]

        def scores(i):
            e, c, g = live[i]
            qq = q_ref[0, c * t:(c + 1) * t, :]
            st[i]["s"] = _dot_t(k_ref[0, rows[g], :], jnp.where(lane_lo == (e == 0), qq, jnp.zeros_like(qq)))

        def softmax(i):
            e, c, g = live[i]
            s = st[i].pop("s")
            j, d = blocks[g]
            if c == d:
                krow = lax.broadcasted_iota(jnp.int32, (t, t), 0)
                qcol = lax.broadcasted_iota(jnp.int32, (t, t), 1)
                s = s + slope * jnp.minimum(krow, 2 * qcol - krow).astype(F32)
                s = jnp.where(_chunk_causal(t), s, NEG)
                off = 0.0
            else:
                s = jnp.concatenate([s[:, b * LANES:(b + 1) * LANES] + key_bias for b in range(t // LANES)], axis=1)
                off = (slope * ((qi * nchunk + c - j) * t).astype(F32) if d is None
                       else slope * float((c - d) * t))
            st[i]["p"], st[i]["alpha"] = _softmax_stage(s, off, slice(c * t, (c + 1) * t), e, m_sc)

        def values(i):
            e, c, g = live[i]
            lanes = slice(c * t, (c + 1) * t)
            acc_sc[e, :, lanes] = (st[i]["alpha"] * acc_sc[e, :, lanes]
                                   + _dot(values_of(g), st[i].pop("p")))

        _pipelined(len(live), [scores, softmax, values])

    _walk_key_blocks(qi, nchunk, step, group=nchunk, max_blocks=reach_ref[h])
    lf = lam_ref[...]
    lam = (jnp.exp(jnp.sum(lf[0:1] * lf[1:2], axis=-1, keepdims=True))
           - jnp.exp(jnp.sum(lf[2:3] * lf[3:4], axis=-1, keepdims=True)) + lambda_init)
    sm = [acc_sc[e, 0:LANES, :] / acc_sc[e, LANES:LANES + 1, :] for e in range(2)]
    o = (sm[0] - lam * sm[1]).T
    o = _rms(o, LANES) * subln_ref[...] * (1.0 - lambda_init)
    o_ref[0] = o.astype(BF16)


def _attn_d(slopes, reach, dq, dk, dv, lam, subln, lambda_init, tq, t):
    B, S, _ = dq.shape
    specs = _flash_specs(S, tq, LANES, LANES, LANES)
    whole = lambda a: pl.BlockSpec(a.shape, lambda b, h, i: (0, 0))
    smem = pl.BlockSpec(memory_space=pltpu.SMEM)
    specs["in_specs"] = [smem, smem] + specs["in_specs"] + [whole(lam), whole(subln)]
    return pl.pallas_call(
        functools.partial(_attn_d_kernel, tq=tq, t=t, lambda_init=lambda_init),
        grid=(B, D_HEADS, S // tq),
        out_shape=jax.ShapeDtypeStruct((B, S, MIX_HALF), BF16),
        scratch_shapes=[pltpu.VMEM((2, 1, tq), F32), pltpu.VMEM((2, LANES + DEN_ROWS, tq), F32)],
        name="attn_d",
        **specs,
    )(slopes, reach, dq, dk, dv, lam, subln)


def _out_mlp_kernel(x_ref, ma_ref, mb_ref, wo_ref, g_ref, wu_ref, wd_ref, o_ref, *, tf):
    half = ma_ref.shape[-1]
    x1 = x_ref[...] + _dot(ma_ref[...], wo_ref[0:half, :]) + _dot(mb_ref[...], wo_ref[half:2 * half, :])
    h = (_rms(x1, x1.shape[-1]) * g_ref[...]).astype(BF16)
    o_ref[...] = x1
    for f in range(wu_ref.shape[-1] // tf):
        u = jnp.maximum(_dot(h, wu_ref[:, f * tf:(f + 1) * tf]), 0.0)
        o_ref[...] += _dot((u * u).astype(BF16), wd_ref[f * tf:(f + 1) * tf, :])


def _out_mlp(x, ma, mb, wo, g, wu, wd, tm, tf):
    T, D = x.shape
    full = lambda a: pl.BlockSpec(a.shape, lambda i: (0,) * a.ndim, pipeline_mode=pl.Buffered(1))
    row = lambda c: pl.BlockSpec((tm, c), lambda i: (i, 0))
    return pl.pallas_call(
        functools.partial(_out_mlp_kernel, tf=tf),
        grid=(T // tm,),
        in_specs=[row(D), row(ma.shape[-1]), row(mb.shape[-1]), full(wo), full(g), full(wu), full(wd)],
        out_specs=row(D),
        out_shape=jax.ShapeDtypeStruct((T, D), F32),
        compiler_params=pltpu.CompilerParams(dimension_semantics=("arbitrary",), vmem_limit_bytes=VMEM_LIMIT),
        name="out_mlp",
    )(x, ma, mb, wo, g, wu, wd)


def _alibi_slopes(n):
    return 2.0 ** (-8.0 * jnp.arange(1, n + 1, dtype=F32) / n)


def _rope_placement():
    f = jnp.arange(B_ROPE_HALF)
    p = jnp.zeros((2 * B_ROPE_HALF, 3 * LANES), F32)
    p = p.at[f, HEAD + f].set(1.0).at[f, HEAD + B_ROPE_HALF + f].set(1.0)
    p = p.at[B_ROPE_HALF + f, LANES + HEAD + f].set(-1.0)
    p = p.at[B_ROPE_HALF + f, 2 * LANES + HEAD + B_ROPE_HALF + f].set(1.0)
    return jnp.concatenate([p, p], axis=0).astype(BF16)


def _alibi_reach(slopes, q_gain, k_gain, t, S):
    smax = 1.01 * HEAD * 0.125 * LOG2E * jnp.max(jnp.abs(q_gain)) * jnp.max(jnp.abs(k_gain))
    dist = (160.0 + 2.0 * smax) / (slopes * LOG2E)
    blocks = jnp.floor((dist - 1.0) / t) + 1.0
    return jnp.clip(blocks, 0.0, float(S // t)).astype(jnp.int32)


def _tile(n, want):
    t = min(n, want)
    assert n % t == 0, (n, t)
    return t


def kernel(x, positions, norm_mix_g, norm_ffn_g, mlp_w_up, mlp_w_down, ev_w_in, ev_w_out, a_q_norm, a_k_norm, a_sinks, b_cq_norm, b_ckv_norm, b_w_uq, b_w_ukv, b_q_norm, b_k_norm, od_w_in, od_w_out, d_q_norm, d_k_norm, d_lambda, d_subln):
    B, S, D = x.shape
    T = B * S
    depth = norm_mix_g.shape[0]
    tm_proj = _tile(T, 512)
    tm_mlp = _tile(T, 512)
    t_a = _tile(S, 256)
    tq_b, t_b = _tile(S, 1024), _tile(S, 512)
    tq_c, t_c = _tile(S, 1024), _tile(S, 256)
    tq_d, t_d = _tile(S, 1024), _tile(S, 512)
    tf = 512

    xf = x.reshape(T, D)
    pos = positions.reshape(T)
    row2 = lambda a: a.reshape(1, -1).astype(F32)
    pair = lambda a: jnp.concatenate([a, a]).reshape(1, LANES).astype(F32)
    pad_qk = lambda a: jnp.pad(a.astype(F32), (0, LANES - B_QK)).reshape(1, LANES)
    inv = (ROPE_THETA ** (-jnp.arange(B_ROPE_HALF, dtype=F32) / B_ROPE_HALF)).reshape(B_ROPE_HALF, 1)
    rope_place = _rope_placement()
    slopes_a = _alibi_slopes(A_HEADS)
    slopes_d = _alibi_slopes(D_HEADS)

    for layer in range(depth):
        j = layer // 2
        g_mix = row2(norm_mix_g[layer])
        if layer % 2 == 0:
            w_in = ev_w_in[j]
            w_in = jnp.concatenate([w_in[:, :E_ROPE], jnp.zeros((D, HEAD), F32), w_in[:, E_ROPE:],
                                    jnp.zeros((D, LANES - B_QK), F32)], axis=1).astype(BF16)
            wuq = jnp.pad(b_w_uq[j].reshape(-1, B_HEADS, B_QK), ((0, 0), (0, 0), (0, LANES - B_QK)))
            wuq = wuq.reshape(-1, B_HEADS * LANES).astype(BF16)
            qa, ka, va, qb, kb, vb = _even_proj(
                xf, pos, g_mix, w_in, pair(a_q_norm[j]), pair(a_k_norm[j]), row2(b_cq_norm[j]),
                row2(b_ckv_norm[j]), wuq, b_w_ukv[j].astype(BF16), pad_qk(b_q_norm[j]), pad_qk(b_k_norm[j]),
                inv, rope_place, tm_proj)
            r3 = lambda a: a.reshape(B, S, a.shape[-1])
            ma = _attn_a(slopes_a, a_sinks[j].astype(F32), r3(qa), r3(ka), r3(va), t_a).reshape(T, -1)
            mb = _attn_b(r3(qb), r3(kb), r3(vb), tq_b, t_b).reshape(T, -1)
            w_out = ev_w_out[j]
        else:
            lambda_init = 0.8 - 0.6 * math.exp(-0.3 * layer)
            cq, ck, cv, dq, dk, dv = _odd_proj(
                xf, g_mix, od_w_in[j].astype(BF16), d_q_norm[j].reshape(1, LANES).astype(F32),
                d_k_norm[j].reshape(1, LANES).astype(F32), tm_proj)
            r3 = lambda a: a.reshape(B, S, a.shape[-1])
            ma = _attn_c(r3(cq), r3(ck), r3(cv), tq_c, t_c).reshape(T, -1)
            reach = _alibi_reach(slopes_d, d_q_norm[j], d_k_norm[j], t_d, S)
            mb = _attn_d(slopes_d, reach, r3(dq), r3(dk), r3(dv), d_lambda[j].astype(F32), row2(d_subln[j]),
                         lambda_init, tq_d, t_d).reshape(T, -1)
            w_out = od_w_out[j]
        xf = _out_mlp(xf, ma, mb, w_out.astype(BF16), row2(norm_ffn_g[layer]),
                      mlp_w_up[layer].astype(BF16), mlp_w_down[layer].astype(BF16), tm_mlp, tf)
    return xf.reshape(B, S, D)
```

```python
import functools
import math

import jax
import jax.numpy as jnp
from jax import lax
from jax.experimental import pallas as pl
from jax.experimental.pallas import tpu as pltpu

F32 = jnp.float32
BF16 = jnp.bfloat16

EPS = 1e-6
CHUNK = 64
CHUNK_SHIFT = 6
LANES = 128
HEAD = 64
A_HEADS = 8
A_GROUP = 4
A_WINDOW = 128
A_WINDOW_CHUNKS = A_WINDOW // CHUNK
B_HEADS = 8
B_QK = 96
B_ROPE_HALF = 16
ROPE_THETA = 10000.0
C_HEADS = 8
D_HEADS = 4
B_Q_RANK = 256
B_KV_RANK = 128
MIX_HALF = 512
E_AK = A_HEADS * HEAD
E_AV = E_AK + (A_HEADS // A_GROUP) * HEAD
E_CQ = E_AV + (A_HEADS // A_GROUP) * HEAD
E_CKV = E_CQ + B_Q_RANK
E_ROPE = E_CKV + B_KV_RANK
E_END = E_ROPE + LANES
O_CK, O_CV, O_DQ, O_DK, O_DV, O_END = (i * MIX_HALF for i in range(1, 7))
NEG = -1e30
LOG2E = math.log2(math.e)
DEN_ROWS = 16
VMEM_LIMIT = 56 * 1024 * 1024


def _dot(a, b):
    return jnp.dot(a, b, preferred_element_type=F32)


def _dot_t(a, b):
    return lax.dot_general(a, b, (((1,), (1,)), ((), ())), preferred_element_type=F32)


def _dot_tl(a, b):
    return lax.dot_general(a, b, (((0,), (0,)), ((), ())), preferred_element_type=F32)


def _rms(x, denom):
    return x * lax.rsqrt(jnp.sum(x * x, axis=-1, keepdims=True) * (1.0 / denom) + EPS)


def _pair_rms(x, lo):
    xx = x * x
    s_lo = jnp.sum(jnp.where(lo, xx, 0.0), axis=-1, keepdims=True)
    s_hi = jnp.sum(jnp.where(lo, 0.0, xx), axis=-1, keepdims=True)
    r = jnp.where(lo, lax.rsqrt(s_lo * (1.0 / HEAD) + EPS), lax.rsqrt(s_hi * (1.0 / HEAD) + EPS))
    return x * r


def _lane_lo(shape):
    return lax.broadcasted_iota(jnp.int32, shape, len(shape) - 1) < HEAD


def _pipelined(n, stages):
    for step in range(n + len(stages) - 1):
        for si, stage in enumerate(stages):
            c = step - si
            if 0 <= c < n:
                stage(c)


def _even_proj_kernel(x_ref, pos_ref, g_ref, w_in_ref, aqg_ref, akg_ref, cqg_ref, ckvg_ref,
                      wuq_ref, wukv_ref, bqg_ref, bkg_ref, inv_ref, place_ref,
                      qa_ref, ka_ref, va_ref, qb_ref, kb_ref, vb_ref):
    x = x_ref[...]
    h = _rms(x, x.shape[-1]) * g_ref[...]
    proj = _dot(h.astype(BF16), w_in_ref[...])
    lo = _lane_lo((1, LANES))
    for p in range(A_HEADS // 2):
        seg = proj[:, p * LANES:(p + 1) * LANES]
        qa_ref[:, p * LANES:(p + 1) * LANES] = (_pair_rms(seg, lo) * aqg_ref[...] * (0.125 * LOG2E)).astype(BF16)
    ka_ref[...] = (_pair_rms(proj[:, E_AK:E_AV], lo) * akg_ref[...]).astype(BF16)
    va_ref[...] = proj[:, E_AV:E_CQ].astype(BF16)
    cq = _rms(proj[:, E_CQ:E_CKV], B_Q_RANK) * cqg_ref[...]
    ckv = _rms(proj[:, E_CKV:E_ROPE], B_KV_RANK) * ckvg_ref[...]
    krope = proj[:, E_ROPE:E_END]
    qall = _dot(cq.astype(BF16), wuq_ref[...])
    kvall = _dot(ckv.astype(BF16), wukv_ref[...])
    ang = inv_ref[...] * pos_ref[0].astype(F32)
    trig = jnp.concatenate([jnp.cos(ang), jnp.sin(ang)], axis=0)
    trig_hi = trig.astype(BF16)
    trig_lo = (trig - trig_hi.astype(F32)).astype(BF16)
    placed = _dot_tl(jnp.concatenate([trig_hi, trig_lo], axis=0), place_ref[...])
    cosf = placed[:, 0:LANES] + jnp.where(lo, 1.0, 0.0)
    s_first = placed[:, LANES:2 * LANES]
    s_second = placed[:, 2 * LANES:3 * LANES]

    def rope(t):
        return (t * cosf + pltpu.roll(t, LANES - B_ROPE_HALF, 1) * s_first
                + pltpu.roll(t, B_ROPE_HALF, 1) * s_second)

    scale_b = B_QK ** -0.5 * LOG2E
    krope_rot = rope(krope * bkg_ref[...])
    krope_ss = jnp.sum(krope * krope, axis=-1, keepdims=True)
    sq = [dict() for _ in range(B_HEADS)]
    sk = [dict() for _ in range(B_HEADS)]

    def q_sumsq(hh):
        qh = qall[:, hh * LANES:(hh + 1) * LANES]
        sq[hh]["x"] = qh
        sq[hh]["ss"] = jnp.sum(qh * qh, axis=-1, keepdims=True)

    def q_norm(hh):
        sq[hh]["n"] = sq[hh].pop("x") * lax.rsqrt(sq[hh].pop("ss") * (1.0 / B_QK) + EPS) * bqg_ref[...]

    def q_store(hh):
        qb_ref[:, hh * LANES:(hh + 1) * LANES] = (rope(sq[hh].pop("n")) * scale_b).astype(BF16)

    def k_sumsq(hh):
        knope = jnp.where(lo, kvall[:, hh * LANES:(hh + 1) * LANES], 0.0)
        sk[hh]["x"] = knope
        sk[hh]["ss"] = jnp.sum(knope * knope, axis=-1, keepdims=True) + krope_ss

    def k_store(hh):
        kn = (sk[hh].pop("x") * bkg_ref[...] + krope_rot) * lax.rsqrt(sk[hh].pop("ss") * (1.0 / B_QK) + EPS)
        kb_ref[:, hh * LANES:(hh + 1) * LANES] = kn.astype(BF16)

    _pipelined(B_HEADS, [q_sumsq, k_sumsq, q_norm, k_store, q_store])
    for p in range(B_HEADS // 2):
        v_even = kvall[:, 2 * p * LANES:(2 * p + 1) * LANES]
        v_odd = kvall[:, (2 * p + 1) * LANES:(2 * p + 2) * LANES]
        vb_ref[:, p * LANES:(p + 1) * LANES] = jnp.where(lo, pltpu.roll(v_even, HEAD, 1), v_odd).astype(BF16)


def _even_proj(x, pos, g, w_in, aqg, akg, cqg, ckvg, wuq, wukv, bqg, bkg, inv, place, tm):
    T, D = x.shape
    full = lambda a: pl.BlockSpec(a.shape, lambda i: (0,) * a.ndim)
    row = lambda c: pl.BlockSpec((tm, c), lambda i: (i, 0))
    outs = (E_AK, E_AV - E_AK, E_CQ - E_AV, B_HEADS * LANES, B_HEADS * LANES, MIX_HALF)
    pos = pos.reshape(T // tm, 1, tm)
    return pl.pallas_call(
        _even_proj_kernel,
        grid=(T // tm,),
        in_specs=[row(D), pl.BlockSpec((1, 1, tm), lambda i: (i, 0, 0))]
        + [full(a) for a in (g, w_in, aqg, akg, cqg, ckvg, wuq, wukv, bqg, bkg, inv, place)],
        out_specs=[row(c) for c in outs],
        out_shape=[jax.ShapeDtypeStruct((T, c), BF16) for c in outs],
        compiler_params=pltpu.CompilerParams(dimension_semantics=("arbitrary",), vmem_limit_bytes=VMEM_LIMIT),
        name="even_proj",
    )(x, pos, g, w_in, aqg, akg, cqg, ckvg, wuq, wukv, bqg, bkg, inv, place)


def _odd_proj_kernel(x_ref, g_ref, w_in_ref, dqg_ref, dkg_ref,
                     cq_ref, ck_ref, cv_ref, dq_ref, dk_ref, dv_ref):
    x = x_ref[...]
    h = _rms(x, x.shape[-1]) * g_ref[...]
    proj = _dot(h.astype(BF16), w_in_ref[...])
    lo = _lane_lo((1, LANES))
    qscale = 0.125 * LOG2E
    cq_ref[...] = (proj[:, 0:O_CK] * qscale).astype(BF16)
    ck_ref[...] = proj[:, O_CK:O_CV].astype(BF16)
    cv_ref[...] = proj[:, O_CV:O_DQ].astype(BF16)
    for hh in range(D_HEADS):
        sl = slice(hh * LANES, (hh + 1) * LANES)
        dq_ref[:, sl] = (_pair_rms(proj[:, O_DQ + hh * LANES:O_DQ + (hh + 1) * LANES], lo)
                         * dqg_ref[...] * qscale).astype(BF16)
        dk_ref[:, sl] = (_pair_rms(proj[:, O_DK + hh * LANES:O_DK + (hh + 1) * LANES], lo)
                         * dkg_ref[...]).astype(BF16)
    dv_ref[...] = proj[:, O_DV:O_END].astype(BF16)


def _odd_proj(x, g, w_in, dqg, dkg, tm):
    T, D = x.shape
    full = lambda a: pl.BlockSpec(a.shape, lambda i: (0,) * a.ndim)
    row = lambda c: pl.BlockSpec((tm, c), lambda i: (i, 0))
    return pl.pallas_call(
        _odd_proj_kernel,
        grid=(T // tm,),
        in_specs=[row(D)] + [full(a) for a in (g, w_in, dqg, dkg)],
        out_specs=[row(MIX_HALF)] * 6,
        out_shape=[jax.ShapeDtypeStruct((T, MIX_HALF), BF16)] * 6,
        compiler_params=pltpu.CompilerParams(dimension_semantics=("arbitrary",), vmem_limit_bytes=VMEM_LIMIT),
        name="odd_proj",
    )(x, g, w_in, dqg, dkg)


def _attn_a_kernel(slope_ref, sink_ref, q_ref, k_ref, v_ref, o_ref, *, tq, win):
    g = pl.program_id(1)
    qi = pl.program_id(2)
    q0 = qi * tq
    ks = pl.multiple_of(jnp.maximum(q0 - A_WINDOW, 0), A_WINDOW)
    lane_half = lax.broadcasted_iota(jnp.int32, (1, LANES), 1) // HEAD

    def both_halves(x):
        xf = x.astype(F32)
        return jnp.where(lane_half == g, xf, pltpu.roll(xf, HEAD, 1)).astype(BF16)

    k = both_halves(k_ref[0, pl.ds(ks, win), :])
    v = both_halves(v_ref[0, pl.ds(ks, win), :])
    kpos = ks + lax.broadcasted_iota(jnp.int32, (win, tq), 0)
    qpos = q0 + lax.broadcasted_iota(jnp.int32, (win, tq), 1)
    dch = (qpos >> CHUNK_SHIFT) - (kpos >> CHUNK_SHIFT)
    allowed = (dch >= 0) & (dch <= A_WINDOW_CHUNKS)
    negdist = -jnp.abs(qpos - kpos).astype(F32)
    st = [dict() for _ in range(A_GROUP)]

    def scores(i):
        qq = q_ref[0, :, (i // 2) * LANES:(i // 2 + 1) * LANES]
        qh = jnp.where(lane_half == i % 2, qq, jnp.zeros_like(qq))
        st[i]["s"] = _dot_t(k, qh)

    def softmax(i):
        hidx = A_GROUP * g + i
        s = jnp.where(allowed, st[i].pop("s") + (slope_ref[hidx] * LOG2E) * negdist, NEG)
        sink = sink_ref[hidx] * LOG2E
        m = jnp.maximum(jnp.max(s, axis=0, keepdims=True), sink)
        p = jnp.exp2(s - m)
        st[i]["den"] = jnp.sum(p, axis=0, keepdims=True) + jnp.exp2(sink - m)
        st[i]["p"] = p.astype(BF16)

    def values(i):
        st[i]["o"] = (_dot_tl(v, st[i].pop("p")) / st[i].pop("den"))[0:HEAD, :]

    _pipelined(A_GROUP, [scores, softmax, values])
    outs = [st[i]["o"] for i in range(A_GROUP)]
    o_ref[0] = jnp.concatenate(outs, axis=0).T.astype(BF16)


def _attn_a(slopes, sinks, qa, ka, va, tq):
    B, S, _ = qa.shape
    win = tq + A_WINDOW
    smem = pl.BlockSpec(memory_space=pltpu.SMEM)
    gw = A_GROUP * HEAD
    return pl.pallas_call(
        functools.partial(_attn_a_kernel, tq=tq, win=win),
        grid=(B, A_HEADS // A_GROUP, S // tq),
        in_specs=[smem, smem,
                  pl.BlockSpec((1, tq, gw), lambda b, g, i: (b, i, g)),
                  pl.BlockSpec((1, S, LANES), lambda b, g, i: (b, 0, 0)),
                  pl.BlockSpec((1, S, LANES), lambda b, g, i: (b, 0, 0))],
        out_specs=pl.BlockSpec((1, tq, gw), lambda b, g, i: (b, i, g)),
        out_shape=jax.ShapeDtypeStruct((B, S, MIX_HALF), BF16),
        compiler_params=pltpu.CompilerParams(dimension_semantics=("arbitrary",) * 3, vmem_limit_bytes=VMEM_LIMIT),
        name="attn_a",
    )(slopes, sinks, qa, ka, va)


def _walk_key_blocks(qi, nchunk, step, group, max_blocks=None):
    first = qi * nchunk
    trips = first // group
    if max_blocks is None:
        def full(i, carry):
            step([(i * group + g, None) for g in range(group)])
            return carry

        lax.fori_loop(0, trips, full, 0)
        step([(first + d, d) for d in range(nchunk)])
    else:
        step([(first + d, d) for d in range(nchunk - 1, -1, -1)])

        def full(i, carry):
            step([(first - 1 - i * group - g, None) for g in range(group)])
            return carry

        lax.fori_loop(0, jnp.minimum(trips, (max_blocks + (group - 1)) // group), full, 0)


def _flash_specs(S, tq, q_width, k_width, v_width):
    return dict(
        in_specs=[pl.BlockSpec((1, tq, q_width), lambda b, p, i: (b, i, p)),
                  pl.BlockSpec((1, S, k_width), lambda b, p, i: (b, 0, p)),
                  pl.BlockSpec((1, S, v_width), lambda b, p, i: (b, 0, p))],
        out_specs=pl.BlockSpec((1, tq, LANES), lambda b, p, i: (b, i, p)),
        compiler_params=pltpu.CompilerParams(dimension_semantics=("arbitrary",) * 3, vmem_limit_bytes=VMEM_LIMIT),
    )


def _chunk_causal(t):
    krow = lax.broadcasted_iota(jnp.int32, (t, t), 0)
    qcol = lax.broadcasted_iota(jnp.int32, (t, t), 1)
    return (krow >> CHUNK_SHIFT) <= (qcol >> CHUNK_SHIFT)


def _softmax_stage(s, off, lanes, e, m_sc):
    m_prev = m_sc[e, :, lanes]
    m_new = jnp.maximum(m_prev, jnp.max(s, axis=0, keepdims=True) - off)
    alpha = jnp.exp2(m_prev - m_new)
    pr = jnp.exp2(s - (m_new + off))
    m_sc[e, :, lanes] = m_new
    return pr.astype(BF16), alpha


def _attn_b_kernel(q_ref, k_ref, v_ref, o_ref, m_sc, acc_sc, *, tq, t):
    qi = pl.program_id(2)
    nchunk = tq // t
    chains = [(e, c) for c in range(nchunk) for e in range(2)]
    lane = lax.broadcasted_iota(jnp.int32, (1, LANES), 1)
    ones_row = (HEAD, 0)
    m_sc[...] = jnp.full(m_sc.shape, NEG, F32)
    acc_sc[...] = jnp.zeros(acc_sc.shape, F32)

    def step(blocks):
        rows = [pl.ds(pl.multiple_of(j * t, t), t) for j, _ in blocks]
        live = [(e, c, g) for g, (_, d) in enumerate(blocks) for (e, c) in chains if d is None or c >= d]
        st = [dict() for _ in live]
        vals = {}

        def values_of(e, g):
            if (e, g) not in vals:
                v = v_ref[0, rows[g], :]
                own = (lane < HEAD) if e == 0 else (lane >= HEAD)
                vals[e, g] = jnp.where(own, v, jnp.where(lane == ones_row[e], 1.0, 0.0).astype(BF16))
            return vals[e, g]

        def scores(i):
            e, c, g = live[i]
            st[i]["s"] = _dot_t(k_ref[0, rows[g], e * LANES:(e + 1) * LANES],
                                q_ref[0, c * t:(c + 1) * t, e * LANES:(e + 1) * LANES])

        def softmax(i):
            e, c, g = live[i]
            s = st[i].pop("s")
            if c == blocks[g][1]:
                s = jnp.where(_chunk_causal(t), s, NEG)
            st[i]["p"], st[i]["alpha"] = _softmax_stage(s, 0.0, slice(c * t, (c + 1) * t), e, m_sc)

        def values(i):
            e, c, g = live[i]
            lanes = slice(c * t, (c + 1) * t)
            acc_sc[e, :, lanes] = (st[i]["alpha"] * acc_sc[e, :, lanes]
                                   + _dot_tl(values_of(e, g), st[i].pop("p")))

        _pipelined(len(live), [scores, softmax, values])

    _walk_key_blocks(qi, nchunk, step, group=nchunk)
    first = lax.broadcasted_iota(jnp.int32, (LANES, 1), 0) < HEAD
    den = [acc_sc[e, ones_row[e]:ones_row[e] + 1, :] for e in range(2)]
    o_t = jnp.where(first, acc_sc[0] / den[0], acc_sc[1] / den[1])
    o_ref[0] = o_t.T.astype(BF16)


def _attn_b(qb, kb, vb, tq, t):
    B, S, _ = qb.shape
    return pl.pallas_call(
        functools.partial(_attn_b_kernel, tq=tq, t=t),
        grid=(B, B_HEADS // 2, S // tq),
        out_shape=jax.ShapeDtypeStruct((B, S, MIX_HALF), BF16),
        scratch_shapes=[pltpu.VMEM((2, 1, tq), F32), pltpu.VMEM((2, LANES, tq), F32)],
        name="attn_b",
        **_flash_specs(S, tq, 2 * LANES, 2 * LANES, LANES),
    )(qb, kb, vb)


def _attn_c_kernel(q_ref, k_ref, v_ref, o_ref, r_sc, acc_sc, *, tq, t):
    qi = pl.program_id(2)
    nchunk = tq // t
    first = qi * nchunk
    chains = [(e, c) for c in range(nchunk) for e in range(2)]
    krow = lax.broadcasted_iota(jnp.int32, (t, t), 0)
    qcol = lax.broadcasted_iota(jnp.int32, (t, t), 1)
    from_here = jnp.where(qcol >= krow, 1.0, 0.0).astype(BF16)
    from_here = jnp.concatenate([from_here, from_here], axis=1)
    lane_lo = _lane_lo((1, LANES))
    r_sc[...] = jnp.zeros(r_sc.shape, F32)
    acc_sc[...] = jnp.zeros(acc_sc.shape, F32)

    def step(backs):
        live = [(e, c, b) for b in range(len(backs)) for (e, c) in chains]
        st = [dict() for _ in live]

        def block(i):
            e, c, b = live[i]
            return first + c - backs[b]

        def rows(i):
            return pl.ds(pl.multiple_of(jnp.maximum(block(i), 0) * t, t), t)

        def scores(i):
            e, c, b = live[i]
            qq = q_ref[0, c * t:(c + 1) * t, :]
            st[i]["z"] = _dot_t(k_ref[0, rows(i), :], jnp.where(lane_lo == (e == 0), qq, jnp.zeros_like(qq)))

        def softplus(i):
            e, c, b = live[i]
            z = st[i]["z"]
            sp = jnp.maximum(z, jnp.log2(1.0 + jnp.exp2(jnp.minimum(z, 126.0))))
            if isinstance(backs[b], int) and backs[b] == 0:
                sp = jnp.where(krow < qcol, sp, 0.0)
            hi = sp.astype(BF16)
            lo = (sp - hi.astype(F32)).astype(BF16)
            st[i]["hilo"] = jnp.concatenate([hi, lo], axis=0)

        def suffix(i):
            st[i]["suf"] = _dot(from_here, st[i].pop("hilo"))

        def weights(i):
            e, c, b = live[i]
            lanes = slice(c * t, (c + 1) * t)
            suf = st[i].pop("suf")
            a = jnp.exp2(st[i].pop("z") - suf)
            if isinstance(backs[b], int) and backs[b] == 0:
                a = jnp.where(krow < qcol, a, 0.0)
            st[i]["a"] = a.astype(BF16)
            r_prev = r_sc[e, :, lanes]
            w = jnp.exp2(-r_prev)
            tot = suf[0:1, :]
            if not (isinstance(backs[b], int) and backs[b] <= c):
                exists = (block(i) >= 0).astype(F32)
                w, tot = w * exists, tot * exists
            st[i]["w"] = w
            r_sc[e, :, lanes] = r_prev + tot

        def values(i):
            e, c, b = live[i]
            lanes = slice(c * t, (c + 1) * t)
            acc_sc[e, :, lanes] = (acc_sc[e, :, lanes]
                                   + st[i].pop("w") * _dot_tl(v_ref[0, rows(i), :], st[i].pop("a")))

        _pipelined(len(live), [scores, softplus, suffix, weights, values])

    def still_live():
        return jnp.max(jnp.exp2(-r_sc[...])) > 0.0

    step([0, 1])

    def further(carry):
        step([carry[0]])
        return carry[0] + 1, still_live()

    lax.while_loop(lambda carry: (carry[0] < first + nchunk) & carry[1], further, (2, still_live()))
    first_head = lax.broadcasted_iota(jnp.int32, (LANES, 1), 0) < HEAD
    o_ref[0] = jnp.where(first_head, acc_sc[0], acc_sc[1]).T.astype(BF16)


def _attn_c(cq, ck, cv, tq, t):
    B, S, _ = cq.shape
    return pl.pallas_call(
        functools.partial(_attn_c_kernel, tq=tq, t=t),
        grid=(B, C_HEADS // 2, S // tq),
        out_shape=jax.ShapeDtypeStruct((B, S, MIX_HALF), BF16),
        scratch_shapes=[pltpu.VMEM((2, 1, tq), F32), pltpu.VMEM((2, LANES, tq), F32)],
        name="attn_c",
        **_flash_specs(S, tq, LANES, LANES, LANES),
    )(cq, ck, cv)


def _attn_d_kernel(slope_ref, reach_ref, q_ref, k_ref, v_ref, lam_ref, subln_ref, o_ref, m_sc, acc_sc,
                   *, tq, t, lambda_init):
    h = pl.program_id(1)
    qi = pl.program_id(2)
    nchunk = tq // t
    chains = [(e, c) for c in range(nchunk) for e in range(2)]
    slope = slope_ref[h] * LOG2E
    lane_lo = _lane_lo((1, LANES))
    key_bias = slope * lax.broadcasted_iota(jnp.int32, (t, LANES), 0).astype(F32)
    m_sc[...] = jnp.full(m_sc.shape, NEG, F32)
    acc_sc[...] = jnp.zeros(acc_sc.shape, F32)

    def step(blocks):
        rows = [pl.ds(pl.multiple_of(j * t, t), t) for j, _ in blocks]
        live = [(e, c, g) for g, (_, d) in enumerate(blocks) for (e, c) in chains if d is None or c >= d]
        st = [dict() for _ in live]
        vals = {}

        def values_of(g):
            if g not in vals:
                vals[g] = jnp.concatenate([v_ref[0, rows[g], :].T, jnp.ones((DEN_ROWS, t), BF16)], axis=0)
            return vals[g]

        def scores(i):
            e, c, g = live[i]
            qq = q_ref[0, c * t:(c + 1) * t, :]
            st[i]["s"] = _dot_t(k_ref[0, rows[g], :], jnp.where(lane_lo == (e == 0), qq, jnp.zeros_like(qq)))

        def softmax(i):
            e, c, g = live[i]
            s = st[i].pop("s")
            j, d = blocks[g]
            if c == d:
                krow = lax.broadcasted_iota(jnp.int32, (t, t), 0)
                qcol = lax.broadcasted_iota(jnp.int32, (t, t), 1)
                s = s + slope * jnp.minimum(krow, 2 * qcol - krow).astype(F32)
                s = jnp.where(_chunk_causal(t), s, NEG)
                off = 0.0
            else:
                s = jnp.concatenate([s[:, b * LANES:(b + 1) * LANES] + key_bias for b in range(t // LANES)], axis=1)
                off = (slope * ((qi * nchunk + c - j) * t).astype(F32) if d is None
                       else slope * float((c - d) * t))
            st[i]["p"], st[i]["alpha"] = _softmax_stage(s, off, slice(c * t, (c + 1) * t), e, m_sc)

        def values(i):
            e, c, g = live[i]
            lanes = slice(c * t, (c + 1) * t)
            acc_sc[e, :, lanes] = (st[i]["alpha"] * acc_sc[e, :, lanes]
                                   + _dot(values_of(g), st[i].pop("p")))

        _pipelined(len(live), [scores, softmax, values])

    _walk_key_blocks(qi, nchunk, step, group=nchunk, max_blocks=reach_ref[h])
    lf = lam_ref[...]
    lam = (jnp.exp(jnp.sum(lf[0:1] * lf[1:2], axis=-1, keepdims=True))
           - jnp.exp(jnp.sum(lf[2:3] * lf[3:4], axis=-1, keepdims=True)) + lambda_init)
    sm = [acc_sc[e, 0:LANES, :] / acc_sc[e, LANES:LANES + 1, :] for e in range(2)]
    o = (sm[0] - lam * sm[1]).T
    o = _rms(o, LANES) * subln_ref[...] * (1.0 - lambda_init)
    o_ref[0] = o.astype(BF16)


def _attn_d(slopes, reach, dq, dk, dv, lam, subln, lambda_init, tq, t):
    B, S, _ = dq.shape
    specs = _flash_specs(S, tq, LANES, LANES, LANES)
    whole = lambda a: pl.BlockSpec(a.shape, lambda b, h, i: (0, 0))
    smem = pl.BlockSpec(memory_space=pltpu.SMEM)
    specs["in_specs"] = [smem, smem] + specs["in_specs"] + [whole(lam), whole(subln)]
    return pl.pallas_call(
        functools.partial(_attn_d_kernel, tq=tq, t=t, lambda_init=lambda_init),
        grid=(B, D_HEADS, S // tq),
        out_shape=jax.ShapeDtypeStruct((B, S, MIX_HALF), BF16),
        scratch_shapes=[pltpu.VMEM((2, 1, tq), F32), pltpu.VMEM((2, LANES + DEN_ROWS, tq), F32)],
        name="attn_d",
        **specs,
    )(slopes, reach, dq, dk, dv, lam, subln)


def _out_mlp_kernel(x_ref, ma_ref, mb_ref, wo_ref, g_ref, wu_ref, wd_ref, o_ref, *, tf):
    half = ma_ref.shape[-1]
    x1 = x_ref[...] + _dot(ma_ref[...], wo_ref[0:half, :]) + _dot(mb_ref[...], wo_ref[half:2 * half, :])
    h = (_rms(x1, x1.shape[-1]) * g_ref[...]).astype(BF16)
    o_ref[...] = x1
    for f in range(wu_ref.shape[-1] // tf):
        u = jnp.maximum(_dot(h, wu_ref[:, f * tf:(f + 1) * tf]), 0.0)
        o_ref[...] += _dot((u * u).astype(BF16), wd_ref[f * tf:(f + 1) * tf, :])


def _out_mlp(x, ma, mb, wo, g, wu, wd, layer, tm, tf):
    T, D = x.shape
    full = lambda a: pl.BlockSpec(a.shape, lambda i: (0,) * a.ndim, pipeline_mode=pl.Buffered(1))
    of_layer = lambda a, l: pl.BlockSpec((None,) + a.shape[1:], lambda i: (l,) + (0,) * (a.ndim - 1),
                                         pipeline_mode=pl.Buffered(1))
    row = lambda c: pl.BlockSpec((tm, c), lambda i: (i, 0))
    return pl.pallas_call(
        functools.partial(_out_mlp_kernel, tf=tf),
        grid=(T // tm,),
        in_specs=[row(D), row(ma.shape[-1]), row(mb.shape[-1]), of_layer(wo, layer // 2), full(g),
                  of_layer(wu, layer), of_layer(wd, layer)],
        out_specs=row(D),
        out_shape=jax.ShapeDtypeStruct((T, D), F32),
        compiler_params=pltpu.CompilerParams(dimension_semantics=("arbitrary",), vmem_limit_bytes=VMEM_LIMIT),
        name="out_mlp",
    )(x, ma, mb, wo, g, wu, wd)


def _alibi_slopes(n):
    return 2.0 ** (-8.0 * jnp.arange(1, n + 1, dtype=F32) / n)


def _rope_placement():
    f = jnp.arange(B_ROPE_HALF)
    p = jnp.zeros((2 * B_ROPE_HALF, 3 * LANES), F32)
    p = p.at[f, HEAD + f].set(1.0).at[f, HEAD + B_ROPE_HALF + f].set(1.0)
    p = p.at[B_ROPE_HALF + f, LANES + HEAD + f].set(-1.0)
    p = p.at[B_ROPE_HALF + f, 2 * LANES + HEAD + B_ROPE_HALF + f].set(1.0)
    return jnp.concatenate([p, p], axis=0).astype(BF16)


def _alibi_reach(slopes, q_gain, k_gain, t, S):
    smax = 1.01 * HEAD * 0.125 * LOG2E * jnp.max(jnp.abs(q_gain)) * jnp.max(jnp.abs(k_gain))
    dist = (160.0 + 2.0 * smax) / (slopes * LOG2E)
    blocks = jnp.floor((dist - 1.0) / t) + 1.0
    return jnp.clip(blocks, 0.0, float(S // t)).astype(jnp.int32)


def _tile(n, want):
    t = min(n, want)
    assert n % t == 0, (n, t)
    return t


def kernel(x, positions, norm_mix_g, norm_ffn_g, mlp_w_up, mlp_w_down, ev_w_in, ev_w_out, a_q_norm, a_k_norm, a_sinks, b_cq_norm, b_ckv_norm, b_w_uq, b_w_ukv, b_q_norm, b_k_norm, od_w_in, od_w_out, d_q_norm, d_k_norm, d_lambda, d_subln):
    B, S, D = x.shape
    T = B * S
    depth = norm_mix_g.shape[0]
    tm_proj = _tile(T, 512)
    tm_mlp = _tile(T, 512)
    t_a = _tile(S, 256)
    tq_b, t_b = _tile(S, 1024), _tile(S, 512)
    tq_c, t_c = _tile(S, 1024), _tile(S, 256)
    tq_d, t_d = _tile(S, 1024), _tile(S, 512)
    tf = 512

    xf = x.reshape(T, D)
    pos = positions.reshape(T)
    row2 = lambda a: a.reshape(1, -1).astype(F32)
    pair = lambda a: jnp.concatenate([a, a]).reshape(1, LANES).astype(F32)
    pad_qk = lambda a: jnp.pad(a.astype(F32), (0, LANES - B_QK)).reshape(1, LANES)
    inv = (ROPE_THETA ** (-jnp.arange(B_ROPE_HALF, dtype=F32) / B_ROPE_HALF)).reshape(B_ROPE_HALF, 1)
    rope_place = _rope_placement()
    slopes_a = _alibi_slopes(A_HEADS)
    slopes_d = _alibi_slopes(D_HEADS)

    w_up, w_down = mlp_w_up.astype(BF16), mlp_w_down.astype(BF16)
    ev_w_out_bf, od_w_out_bf = ev_w_out.astype(BF16), od_w_out.astype(BF16)
    for layer in range(depth):
        j = layer // 2
        g_mix = row2(norm_mix_g[layer])
        if layer % 2 == 0:
            w_in = ev_w_in[j]
            w_in = jnp.concatenate([w_in[:, :E_ROPE], jnp.zeros((D, HEAD), F32), w_in[:, E_ROPE:],
                                    jnp.zeros((D, LANES - B_QK), F32)], axis=1).astype(BF16)
            wuq = jnp.pad(b_w_uq[j].reshape(-1, B_HEADS, B_QK), ((0, 0), (0, 0), (0, LANES - B_QK)))
            wuq = wuq.reshape(-1, B_HEADS * LANES).astype(BF16)
            qa, ka, va, qb, kb, vb = _even_proj(
                xf, pos, g_mix, w_in, pair(a_q_norm[j]), pair(a_k_norm[j]), row2(b_cq_norm[j]),
                row2(b_ckv_norm[j]), wuq, b_w_ukv[j].astype(BF16), pad_qk(b_q_norm[j]), pad_qk(b_k_norm[j]),
                inv, rope_place, tm_proj)
            r3 = lambda a: a.reshape(B, S, a.shape[-1])
            ma = _attn_a(slopes_a, a_sinks[j].astype(F32), r3(qa), r3(ka), r3(va), t_a).reshape(T, -1)
            mb = _attn_b(r3(qb), r3(kb), r3(vb), tq_b, t_b).reshape(T, -1)
            w_out = ev_w_out_bf
        else:
            lambda_init = 0.8 - 0.6 * math.exp(-0.3 * layer)
            cq, ck, cv, dq, dk, dv = _odd_proj(
                xf, g_mix, od_w_in[j].astype(BF16), d_q_norm[j].reshape(1, LANES).astype(F32),
                d_k_norm[j].reshape(1, LANES).astype(F32), tm_proj)
            r3 = lambda a: a.reshape(B, S, a.shape[-1])
            ma = _attn_c(r3(cq), r3(ck), r3(cv), tq_c, t_c).reshape(T, -1)
            reach = _alibi_reach(slopes_d, d_q_norm[j], d_k_norm[j], t_d, S)
            mb = _attn_d(slopes_d, reach, r3(dq), r3(dk), r3(dv), d_lambda[j].astype(F32), row2(d_subln[j]),
                         lambda_init, tq_d, t_d).reshape(T, -1)
            w_out = od_w_out_bf
        xf = _out_mlp(xf, ma, mb, w_out, row2(norm_ffn_g[layer]), w_up, w_down, layer, tm_mlp, tf)
    return xf.reshape(B, S, D)
```

```python
import functools
import math

import jax
import jax.numpy as jnp
from jax import lax
from jax.experimental import pallas as pl
from jax.experimental.pallas import tpu as pltpu

F32 = jnp.float32
BF16 = jnp.bfloat16

EPS = 1e-6
CHUNK = 64
CHUNK_SHIFT = 6
LANES = 128
HEAD = 64
A_HEADS = 8
A_GROUP = 4
A_WINDOW = 128
A_WINDOW_CHUNKS = A_WINDOW // CHUNK
B_HEADS = 8
B_QK = 96
B_ROPE_HALF = 16
ROPE_THETA = 10000.0
C_HEADS = 8
D_HEADS = 4
B_Q_RANK = 256
B_KV_RANK = 128
MIX_HALF = 512
E_AK = A_HEADS * HEAD
E_AV = E_AK + (A_HEADS // A_GROUP) * HEAD
E_CQ = E_AV + (A_HEADS // A_GROUP) * HEAD
E_CKV = E_CQ + B_Q_RANK
E_ROPE = E_CKV + B_KV_RANK
E_END = E_ROPE + LANES
O_CK, O_CV, O_DQ, O_DK, O_DV, O_END = (i * MIX_HALF for i in range(1, 7))
NEG = -1e30
LOG2E = math.log2(math.e)
DEN_ROWS = 16
VMEM_LIMIT = 56 * 1024 * 1024


def _dot(a, b):
    return jnp.dot(a, b, preferred_element_type=F32)


def _dot_t(a, b):
    return lax.dot_general(a, b, (((1,), (1,)), ((), ())), preferred_element_type=F32)


def _dot_tl(a, b):
    return lax.dot_general(a, b, (((0,), (0,)), ((), ())), preferred_element_type=F32)


def _rms(x, denom):
    return x * lax.rsqrt(jnp.sum(x * x, axis=-1, keepdims=True) * (1.0 / denom) + EPS)


def _pair_rms(x, lo):
    xx = x * x
    s_lo = jnp.sum(jnp.where(lo, xx, 0.0), axis=-1, keepdims=True)
    s_hi = jnp.sum(jnp.where(lo, 0.0, xx), axis=-1, keepdims=True)
    r = jnp.where(lo, lax.rsqrt(s_lo * (1.0 / HEAD) + EPS), lax.rsqrt(s_hi * (1.0 / HEAD) + EPS))
    return x * r


def _lane_lo(shape):
    return lax.broadcasted_iota(jnp.int32, shape, len(shape) - 1) < HEAD


def _pipelined(n, stages):
    for step in range(n + len(stages) - 1):
        for si, stage in enumerate(stages):
            c = step - si
            if 0 <= c < n:
                stage(c)


def _even_proj_kernel(x_ref, pos_ref, g_ref, w_in_ref, aqg_ref, akg_ref, cqg_ref, ckvg_ref,
                      wuq_ref, wukv_ref, bqg_ref, bkg_ref, inv_ref, place_ref,
                      qa_ref, ka_ref, va_ref, qb_ref, kb_ref, vb_ref):
    x = x_ref[...]
    h = _rms(x, x.shape[-1]) * g_ref[...]
    proj = _dot(h.astype(BF16), w_in_ref[...])
    lo = _lane_lo((1, LANES))
    for p in range(A_HEADS // 2):
        seg = proj[:, p * LANES:(p + 1) * LANES]
        qa_ref[:, p * LANES:(p + 1) * LANES] = (_pair_rms(seg, lo) * aqg_ref[...] * (0.125 * LOG2E)).astype(BF16)
    ka_ref[...] = (_pair_rms(proj[:, E_AK:E_AV], lo) * akg_ref[...]).astype(BF16)
    va_ref[...] = proj[:, E_AV:E_CQ].astype(BF16)
    cq = _rms(proj[:, E_CQ:E_CKV], B_Q_RANK) * cqg_ref[...]
    ckv = _rms(proj[:, E_CKV:E_ROPE], B_KV_RANK) * ckvg_ref[...]
    krope = proj[:, E_ROPE:E_END]
    qall = _dot(cq.astype(BF16), wuq_ref[...])
    kvall = _dot(ckv.astype(BF16), wukv_ref[...])
    ang = inv_ref[...] * pos_ref[0].astype(F32)
    trig = jnp.concatenate([jnp.cos(ang), jnp.sin(ang)], axis=0)
    trig_hi = trig.astype(BF16)
    trig_lo = (trig - trig_hi.astype(F32)).astype(BF16)
    placed = _dot_tl(jnp.concatenate([trig_hi, trig_lo], axis=0), place_ref[...])
    cosf = placed[:, 0:LANES] + jnp.where(lo, 1.0, 0.0)
    s_first = placed[:, LANES:2 * LANES]
    s_second = placed[:, 2 * LANES:3 * LANES]

    def rope(t):
        return (t * cosf + pltpu.roll(t, LANES - B_ROPE_HALF, 1) * s_first
                + pltpu.roll(t, B_ROPE_HALF, 1) * s_second)

    scale_b = B_QK ** -0.5 * LOG2E
    krope_rot = rope(krope * bkg_ref[...])
    krope_ss = jnp.sum(krope * krope, axis=-1, keepdims=True)
    sq = [dict() for _ in range(B_HEADS)]
    sk = [dict() for _ in range(B_HEADS)]

    def q_sumsq(hh):
        qh = qall[:, hh * LANES:(hh + 1) * LANES]
        sq[hh]["x"] = qh
        sq[hh]["ss"] = jnp.sum(qh * qh, axis=-1, keepdims=True)

    def q_norm(hh):
        sq[hh]["n"] = sq[hh].pop("x") * lax.rsqrt(sq[hh].pop("ss") * (1.0 / B_QK) + EPS) * bqg_ref[...]

    def q_store(hh):
        qb_ref[:, hh * LANES:(hh + 1) * LANES] = (rope(sq[hh].pop("n")) * scale_b).astype(BF16)

    def k_sumsq(hh):
        knope = jnp.where(lo, kvall[:, hh * LANES:(hh + 1) * LANES], 0.0)
        sk[hh]["x"] = knope
        sk[hh]["ss"] = jnp.sum(knope * knope, axis=-1, keepdims=True) + krope_ss

    def k_store(hh):
        kn = (sk[hh].pop("x") * bkg_ref[...] + krope_rot) * lax.rsqrt(sk[hh].pop("ss") * (1.0 / B_QK) + EPS)
        kb_ref[:, hh * LANES:(hh + 1) * LANES] = kn.astype(BF16)

    _pipelined(B_HEADS, [q_sumsq, k_sumsq, q_norm, k_store, q_store])
    for p in range(B_HEADS // 2):
        v_even = kvall[:, 2 * p * LANES:(2 * p + 1) * LANES]
        v_odd = kvall[:, (2 * p + 1) * LANES:(2 * p + 2) * LANES]
        vb_ref[:, p * LANES:(p + 1) * LANES] = jnp.where(lo, pltpu.roll(v_even, HEAD, 1), v_odd).astype(BF16)


def _even_proj(x, pos, g, w_in, aqg, akg, cqg, ckvg, wuq, wukv, bqg, bkg, inv, place, tm):
    T, D = x.shape
    full = lambda a: pl.BlockSpec(a.shape, lambda i: (0,) * a.ndim)
    row = lambda c: pl.BlockSpec((tm, c), lambda i: (i, 0))
    outs = (E_AK, E_AV - E_AK, E_CQ - E_AV, B_HEADS * LANES, B_HEADS * LANES, MIX_HALF)
    pos = pos.reshape(T // tm, 1, tm)
    return pl.pallas_call(
        _even_proj_kernel,
        grid=(T // tm,),
        in_specs=[row(D), pl.BlockSpec((1, 1, tm), lambda i: (i, 0, 0))]
        + [full(a) for a in (g, w_in, aqg, akg, cqg, ckvg, wuq, wukv, bqg, bkg, inv, place)],
        out_specs=[row(c) for c in outs],
        out_shape=[jax.ShapeDtypeStruct((T, c), BF16) for c in outs],
        compiler_params=pltpu.CompilerParams(dimension_semantics=("arbitrary",), vmem_limit_bytes=VMEM_LIMIT),
        name="even_proj",
    )(x, pos, g, w_in, aqg, akg, cqg, ckvg, wuq, wukv, bqg, bkg, inv, place)


def _odd_proj_kernel(x_ref, g_ref, w_in_ref, dqg_ref, dkg_ref,
                     cq_ref, ck_ref, cv_ref, dq_ref, dk_ref, dv_ref):
    x = x_ref[...]
    h = _rms(x, x.shape[-1]) * g_ref[...]
    proj = _dot(h.astype(BF16), w_in_ref[...])
    lo = _lane_lo((1, LANES))
    qscale = 0.125 * LOG2E
    cq_ref[...] = (proj[:, 0:O_CK] * qscale).astype(BF16)
    ck_ref[...] = proj[:, O_CK:O_CV].astype(BF16)
    cv_ref[...] = proj[:, O_CV:O_DQ].astype(BF16)
    for hh in range(D_HEADS):
        sl = slice(hh * LANES, (hh + 1) * LANES)
        dq_ref[:, sl] = (_pair_rms(proj[:, O_DQ + hh * LANES:O_DQ + (hh + 1) * LANES], lo)
                         * dqg_ref[...] * qscale).astype(BF16)
        dk_ref[:, sl] = (_pair_rms(proj[:, O_DK + hh * LANES:O_DK + (hh + 1) * LANES], lo)
                         * dkg_ref[...]).astype(BF16)
    dv_ref[...] = proj[:, O_DV:O_END].astype(BF16)


def _odd_proj(x, g, w_in, dqg, dkg, tm):
    T, D = x.shape
    full = lambda a: pl.BlockSpec(a.shape, lambda i: (0,) * a.ndim)
    row = lambda c: pl.BlockSpec((tm, c), lambda i: (i, 0))
    return pl.pallas_call(
        _odd_proj_kernel,
        grid=(T // tm,),
        in_specs=[row(D)] + [full(a) for a in (g, w_in, dqg, dkg)],
        out_specs=[row(MIX_HALF)] * 6,
        out_shape=[jax.ShapeDtypeStruct((T, MIX_HALF), BF16)] * 6,
        compiler_params=pltpu.CompilerParams(dimension_semantics=("arbitrary",), vmem_limit_bytes=VMEM_LIMIT),
        name="odd_proj",
    )(x, g, w_in, dqg, dkg)


def _attn_a_kernel(slope_ref, sink_ref, q_ref, k_ref, v_ref, o_ref, *, tq, win):
    g = pl.program_id(1)
    qi = pl.program_id(2)
    q0 = qi * tq
    ks = pl.multiple_of(jnp.maximum(q0 - A_WINDOW, 0), A_WINDOW)
    lane_half = lax.broadcasted_iota(jnp.int32, (1, LANES), 1) // HEAD

    def both_halves(x):
        xf = x.astype(F32)
        return jnp.where(lane_half == g, xf, pltpu.roll(xf, HEAD, 1)).astype(BF16)

    k = both_halves(k_ref[0, pl.ds(ks, win), :])
    v = both_halves(v_ref[0, pl.ds(ks, win), :])
    kpos = ks + lax.broadcasted_iota(jnp.int32, (win, tq), 0)
    qpos = q0 + lax.broadcasted_iota(jnp.int32, (win, tq), 1)
    dch = (qpos >> CHUNK_SHIFT) - (kpos >> CHUNK_SHIFT)
    allowed = (dch >= 0) & (dch <= A_WINDOW_CHUNKS)
    negdist = -jnp.abs(qpos - kpos).astype(F32)
    st = [dict() for _ in range(A_GROUP)]

    def scores(i):
        qq = q_ref[0, :, (i // 2) * LANES:(i // 2 + 1) * LANES]
        qh = jnp.where(lane_half == i % 2, qq, jnp.zeros_like(qq))
        st[i]["s"] = _dot_t(k, qh)

    def softmax(i):
        hidx = A_GROUP * g + i
        s = jnp.where(allowed, st[i].pop("s") + (slope_ref[hidx] * LOG2E) * negdist, NEG)
        sink = sink_ref[hidx] * LOG2E
        m = jnp.maximum(jnp.max(s, axis=0, keepdims=True), sink)
        p = jnp.exp2(s - m)
        st[i]["den"] = jnp.sum(p, axis=0, keepdims=True) + jnp.exp2(sink - m)
        st[i]["p"] = p.astype(BF16)

    def values(i):
        st[i]["o"] = (_dot_tl(v, st[i].pop("p")) / st[i].pop("den"))[0:HEAD, :]

    _pipelined(A_GROUP, [scores, softmax, values])
    outs = [st[i]["o"] for i in range(A_GROUP)]
    o_ref[0] = jnp.concatenate(outs, axis=0).T.astype(BF16)


def _attn_a(slopes, sinks, qa, ka, va, tq):
    B, S, _ = qa.shape
    win = tq + A_WINDOW
    smem = pl.BlockSpec(memory_space=pltpu.SMEM)
    gw = A_GROUP * HEAD
    return pl.pallas_call(
        functools.partial(_attn_a_kernel, tq=tq, win=win),
        grid=(B, A_HEADS // A_GROUP, S // tq),
        in_specs=[smem, smem,
                  pl.BlockSpec((1, tq, gw), lambda b, g, i: (b, i, g)),
                  pl.BlockSpec((1, S, LANES), lambda b, g, i: (b, 0, 0)),
                  pl.BlockSpec((1, S, LANES), lambda b, g, i: (b, 0, 0))],
        out_specs=pl.BlockSpec((1, tq, gw), lambda b, g, i: (b, i, g)),
        out_shape=jax.ShapeDtypeStruct((B, S, MIX_HALF), BF16),
        compiler_params=pltpu.CompilerParams(dimension_semantics=("arbitrary",) * 3, vmem_limit_bytes=VMEM_LIMIT),
        name="attn_a",
    )(slopes, sinks, qa, ka, va)


def _walk_key_blocks(qi, nchunk, step, group, max_blocks=None):
    first = qi * nchunk
    trips = first // group
    if max_blocks is None:
        def full(i, carry):
            step([(i * group + g, None) for g in range(group)])
            return carry

        lax.fori_loop(0, trips, full, 0)
        step([(first + d, d) for d in range(nchunk)])
    else:
        step([(first + d, d) for d in range(nchunk - 1, -1, -1)])

        def full(i, carry):
            step([(first - 1 - i * group - g, None) for g in range(group)])
            return carry

        lax.fori_loop(0, jnp.minimum(trips, (max_blocks + (group - 1)) // group), full, 0)


def _flash_specs(S, tq, q_width, k_width, v_width):
    return dict(
        in_specs=[pl.BlockSpec((1, tq, q_width), lambda b, p, i: (b, i, p)),
                  pl.BlockSpec((1, S, k_width), lambda b, p, i: (b, 0, p)),
                  pl.BlockSpec((1, S, v_width), lambda b, p, i: (b, 0, p))],
        out_specs=pl.BlockSpec((1, tq, LANES), lambda b, p, i: (b, i, p)),
        compiler_params=pltpu.CompilerParams(dimension_semantics=("arbitrary",) * 3, vmem_limit_bytes=VMEM_LIMIT),
    )


def _chunk_causal(t):
    krow = lax.broadcasted_iota(jnp.int32, (t, t), 0)
    qcol = lax.broadcasted_iota(jnp.int32, (t, t), 1)
    return (krow >> CHUNK_SHIFT) <= (qcol >> CHUNK_SHIFT)


def _softmax_stage(s, off, lanes, e, m_sc):
    m_prev = m_sc[e, :, lanes]
    m_new = jnp.maximum(m_prev, jnp.max(s, axis=0, keepdims=True) - off)
    alpha = jnp.exp2(m_prev - m_new)
    pr = jnp.exp2(s - (m_new + off))
    m_sc[e, :, lanes] = m_new
    return pr.astype(BF16), alpha


def _attn_b_kernel(q_ref, k_ref, v_ref, o_ref, m_sc, acc_sc, *, tq, t):
    qi = pl.program_id(2)
    nchunk = tq // t
    chains = [(e, c) for c in range(nchunk) for e in range(2)]
    lane = lax.broadcasted_iota(jnp.int32, (1, LANES), 1)
    ones_row = (HEAD, 0)
    m_sc[...] = jnp.full(m_sc.shape, NEG, F32)
    acc_sc[...] = jnp.zeros(acc_sc.shape, F32)

    def step(blocks):
        rows = [pl.ds(pl.multiple_of(j * t, t), t) for j, _ in blocks]
        live = [(e, c, g) for g, (_, d) in enumerate(blocks) for (e, c) in chains if d is None or c >= d]
        st = [dict() for _ in live]
        vals = {}

        def values_of(e, g):
            if (e, g) not in vals:
                v = v_ref[0, rows[g], :]
                own = (lane < HEAD) if e == 0 else (lane >= HEAD)
                vals[e, g] = jnp.where(own, v, jnp.where(lane == ones_row[e], 1.0, 0.0).astype(BF16))
            return vals[e, g]

        def scores(i):
            e, c, g = live[i]
            st[i]["s"] = _dot_t(k_ref[0, rows[g], e * LANES:(e + 1) * LANES],
                                q_ref[0, c * t:(c + 1) * t, e * LANES:(e + 1) * LANES])

        def softmax(i):
            e, c, g = live[i]
            s = st[i].pop("s")
            if c == blocks[g][1]:
                s = jnp.where(_chunk_causal(t), s, NEG)
            st[i]["p"], st[i]["alpha"] = _softmax_stage(s, 0.0, slice(c * t, (c + 1) * t), e, m_sc)

        def values(i):
            e, c, g = live[i]
            lanes = slice(c * t, (c + 1) * t)
            acc_sc[e, :, lanes] = (st[i]["alpha"] * acc_sc[e, :, lanes]
                                   + _dot_tl(values_of(e, g), st[i].pop("p")))

        _pipelined(len(live), [scores, softmax, values])

    _walk_key_blocks(qi, nchunk, step, group=nchunk)
    first = lax.broadcasted_iota(jnp.int32, (LANES, 1), 0) < HEAD
    den = [acc_sc[e, ones_row[e]:ones_row[e] + 1, :] for e in range(2)]
    o_t = jnp.where(first, acc_sc[0] / den[0], acc_sc[1] / den[1])
    o_ref[0] = o_t.T.astype(BF16)


def _attn_b(qb, kb, vb, tq, t):
    B, S, _ = qb.shape
    return pl.pallas_call(
        functools.partial(_attn_b_kernel, tq=tq, t=t),
        grid=(B, B_HEADS // 2, S // tq),
        out_shape=jax.ShapeDtypeStruct((B, S, MIX_HALF), BF16),
        scratch_shapes=[pltpu.VMEM((2, 1, tq), F32), pltpu.VMEM((2, LANES, tq), F32)],
        name="attn_b",
        **_flash_specs(S, tq, 2 * LANES, 2 * LANES, LANES),
    )(qb, kb, vb)


def _attn_c_kernel(q_ref, k_ref, v_ref, o_ref, r_sc, acc_sc, *, tq, t):
    qi = pl.program_id(2)
    nchunk = tq // t
    first = qi * nchunk
    chains = [(e, c) for c in range(nchunk) for e in range(2)]
    krow = lax.broadcasted_iota(jnp.int32, (t, t), 0)
    qcol = lax.broadcasted_iota(jnp.int32, (t, t), 1)
    from_here = jnp.where(qcol >= krow, 1.0, 0.0).astype(BF16)
    from_here = jnp.concatenate([from_here, from_here], axis=1)
    lane_lo = _lane_lo((1, LANES))
    r_sc[...] = jnp.zeros(r_sc.shape, F32)
    acc_sc[...] = jnp.zeros(acc_sc.shape, F32)

    def step(backs):
        live = [(e, c, b) for b in range(len(backs)) for (e, c) in chains]
        st = [dict() for _ in live]

        def block(i):
            e, c, b = live[i]
            return first + c - backs[b]

        def rows(i):
            return pl.ds(pl.multiple_of(jnp.maximum(block(i), 0) * t, t), t)

        def scores(i):
            e, c, b = live[i]
            qq = q_ref[0, c * t:(c + 1) * t, :]
            st[i]["z"] = _dot_t(k_ref[0, rows(i), :], jnp.where(lane_lo == (e == 0), qq, jnp.zeros_like(qq)))

        def softplus(i):
            e, c, b = live[i]
            z = st[i]["z"]
            sp = jnp.maximum(z, jnp.log2(1.0 + jnp.exp2(jnp.minimum(z, 126.0))))
            if isinstance(backs[b], int) and backs[b] == 0:
                sp = jnp.where(krow < qcol, sp, 0.0)
            hi = sp.astype(BF16)
            lo = (sp - hi.astype(F32)).astype(BF16)
            st[i]["hilo"] = jnp.concatenate([hi, lo], axis=0)

        def suffix(i):
            st[i]["suf"] = _dot(from_here, st[i].pop("hilo"))

        def weights(i):
            e, c, b = live[i]
            lanes = slice(c * t, (c + 1) * t)
            suf = st[i].pop("suf")
            a = jnp.exp2(st[i].pop("z") - suf)
            if isinstance(backs[b], int) and backs[b] == 0:
                a = jnp.where(krow < qcol, a, 0.0)
            st[i]["a"] = a.astype(BF16)
            r_prev = r_sc[e, :, lanes]
            w = jnp.exp2(-r_prev)
            tot = suf[0:1, :]
            if not (isinstance(backs[b], int) and backs[b] <= c):
                exists = (block(i) >= 0).astype(F32)
                w, tot = w * exists, tot * exists
            st[i]["w"] = w
            r_sc[e, :, lanes] = r_prev + tot

        def values(i):
            e, c, b = live[i]
            lanes = slice(c * t, (c + 1) * t)
            acc_sc[e, :, lanes] = (acc_sc[e, :, lanes]
                                   + st[i].pop("w") * _dot_tl(v_ref[0, rows(i), :], st[i].pop("a")))

        _pipelined(len(live), [scores, softplus, suffix, weights, values])

    def still_live():
        return jnp.max(jnp.exp2(-r_sc[...])) > 0.0

    step([0, 1])

    def further(carry):
        step([carry[0]])
        return carry[0] + 1, still_live()

    lax.while_loop(lambda carry: (carry[0] < first + nchunk) & carry[1], further, (2, still_live()))
    first_head = lax.broadcasted_iota(jnp.int32, (LANES, 1), 0) < HEAD
    o_ref[0] = jnp.where(first_head, acc_sc[0], acc_sc[1]).T.astype(BF16)


def _attn_c(cq, ck, cv, tq, t):
    B, S, _ = cq.shape
    return pl.pallas_call(
        functools.partial(_attn_c_kernel, tq=tq, t=t),
        grid=(B, C_HEADS // 2, S // tq),
        out_shape=jax.ShapeDtypeStruct((B, S, MIX_HALF), BF16),
        scratch_shapes=[pltpu.VMEM((2, 1, tq), F32), pltpu.VMEM((2, LANES, tq), F32)],
        name="attn_c",
        **_flash_specs(S, tq, LANES, LANES, LANES),
    )(cq, ck, cv)


def _attn_d_kernel(slope_ref, reach_ref, q_ref, k_ref, v_ref, lam_ref, subln_ref, o_ref, m_sc, acc_sc,
                   *, tq, t, lambda_init):
    h = pl.program_id(1)
    qi = pl.program_id(2)
    nchunk = tq // t
    chains = [(e, c) for c in range(nchunk) for e in range(2)]
    slope = slope_ref[h] * LOG2E
    lane_lo = _lane_lo((1, LANES))
    key_bias = slope * lax.broadcasted_iota(jnp.int32, (t, LANES), 0).astype(F32)
    m_sc[...] = jnp.full(m_sc.shape, NEG, F32)
    acc_sc[...] = jnp.zeros(acc_sc.shape, F32)

    def step(blocks):
        rows = [pl.ds(pl.multiple_of(j * t, t), t) for j, _ in blocks]
        live = [(e, c, g) for g, (_, d) in enumerate(blocks) for (e, c) in chains if d is None or c >= d]
        st = [dict() for _ in live]
        vals = {}

        def values_of(g):
            if g not in vals:
                vals[g] = jnp.concatenate([v_ref[0, rows[g], :].T, jnp.ones((DEN_ROWS, t), BF16)], axis=0)
            return vals[g]

        def scores(i):
            e, c, g = live[i]
            qq = q_ref[0, c * t:(c + 1) * t, :]
            st[i]["s"] = _dot_t(k_ref[0, rows[g], :], jnp.where(lane_lo == (e == 0), qq, jnp.zeros_like(qq)))

        def softmax(i):
            e, c, g = live[i]
            s = st[i].pop("s")
            j, d = blocks[g]
            if c == d:
                krow = lax.broadcasted_iota(jnp.int32, (t, t), 0)
                qcol = lax.broadcasted_iota(jnp.int32, (t, t), 1)
                s = s + slope * jnp.minimum(krow, 2 * qcol - krow).astype(F32)
                s = jnp.where(_chunk_causal(t), s, NEG)
                off = 0.0
            else:
                s = jnp.concatenate([s[:, b * LANES:(b + 1) * LANES] + key_bias for b in range(t // LANES)], axis=1)
                off = (slope * ((qi * nchunk + c - j) * t).astype(F32) if d is None
                       else slope * float((c - d) * t))
            st[i]["p"], st[i]["alpha"] = _softmax_stage(s, off, slice(c * t, (c + 1) * t), e, m_sc)

        def values(i):
            e, c, g = live[i]
            lanes = slice(c * t, (c + 1) * t)
            acc_sc[e, :, lanes] = (st[i]["alpha"] * acc_sc[e, :, lanes]
                                   + _dot(values_of(g), st[i].pop("p")))

        _pipelined(len(live), [scores, softmax, values])

    _walk_key_blocks(qi, nchunk, step, group=nchunk, max_blocks=reach_ref[h])
    lf = lam_ref[...]
    lam = (jnp.exp(jnp.sum(lf[0:1] * lf[1:2], axis=-1, keepdims=True))
           - jnp.exp(jnp.sum(lf[2:3] * lf[3:4], axis=-1, keepdims=True)) + lambda_init)
    sm = [acc_sc[e, 0:LANES, :] / acc_sc[e, LANES:LANES + 1, :] for e in range(2)]
    o = (sm[0] - lam * sm[1]).T
    o = _rms(o, LANES) * subln_ref[...] * (1.0 - lambda_init)
    o_ref[0] = o.astype(BF16)


def _attn_d(slopes, reach, dq, dk, dv, lam, subln, lambda_init, tq, t):
    B, S, _ = dq.shape
    specs = _flash_specs(S, tq, LANES, LANES, LANES)
    whole = lambda a: pl.BlockSpec(a.shape, lambda b, h, i: (0, 0))
    smem = pl.BlockSpec(memory_space=pltpu.SMEM)
    specs["in_specs"] = [smem, smem] + specs["in_specs"] + [whole(lam), whole(subln)]
    return pl.pallas_call(
        functools.partial(_attn_d_kernel, tq=tq, t=t, lambda_init=lambda_init),
        grid=(B, D_HEADS, S // tq),
        out_shape=jax.ShapeDtypeStruct((B, S, MIX_HALF), BF16),
        scratch_shapes=[pltpu.VMEM((2, 1, tq), F32), pltpu.VMEM((2, LANES + DEN_ROWS, tq), F32)],
        name="attn_d",
        **specs,
    )(slopes, reach, dq, dk, dv, lam, subln)


def _out_mlp_kernel(x_ref, ma_ref, mb_ref, wo_ref, g_ref, wu_ref, wd_ref, o_ref, *, tf):
    half = ma_ref.shape[-1]
    x1 = x_ref[...] + _dot(ma_ref[...], wo_ref[0:half, :]) + _dot(mb_ref[...], wo_ref[half:2 * half, :])
    h = (_rms(x1, x1.shape[-1]) * g_ref[...]).astype(BF16)
    o_ref[...] = x1
    for f in range(wu_ref.shape[-1] // tf):
        u = jnp.maximum(_dot(h, wu_ref[:, f * tf:(f + 1) * tf]), 0.0)
        o_ref[...] += _dot((u * u).astype(BF16), wd_ref[f * tf:(f + 1) * tf, :])


def _out_mlp(x, ma, mb, wo, g, wu, wd, layer, tm, tf):
    T, D = x.shape
    full = lambda a: pl.BlockSpec(a.shape, lambda i: (0,) * a.ndim, pipeline_mode=pl.Buffered(1))
    of_layer = lambda a, l: pl.BlockSpec((None,) + a.shape[1:], lambda i: (l,) + (0,) * (a.ndim - 1),
                                         pipeline_mode=pl.Buffered(1))
    row = lambda c: pl.BlockSpec((tm, c), lambda i: (i, 0))
    return pl.pallas_call(
        functools.partial(_out_mlp_kernel, tf=tf),
        grid=(T // tm,),
        in_specs=[row(D), row(ma.shape[-1]), row(mb.shape[-1]), of_layer(wo, layer // 2), full(g),
                  of_layer(wu, layer), of_layer(wd, layer)],
        out_specs=row(D),
        out_shape=jax.ShapeDtypeStruct((T, D), F32),
        compiler_params=pltpu.CompilerParams(dimension_semantics=("arbitrary",), vmem_limit_bytes=VMEM_LIMIT),
        name="out_mlp",
    )(x, ma, mb, wo, g, wu, wd)


def _alibi_slopes(n):
    return 2.0 ** (-8.0 * jnp.arange(1, n + 1, dtype=F32) / n)


def _rope_placement():
    f = jnp.arange(B_ROPE_HALF)
    p = jnp.zeros((2 * B_ROPE_HALF, 3 * LANES), F32)
    p = p.at[f, HEAD + f].set(1.0).at[f, HEAD + B_ROPE_HALF + f].set(1.0)
    p = p.at[B_ROPE_HALF + f, LANES + HEAD + f].set(-1.0)
    p = p.at[B_ROPE_HALF + f, 2 * LANES + HEAD + B_ROPE_HALF + f].set(1.0)
    return jnp.concatenate([p, p], axis=0).astype(BF16)


def _alibi_reach(slopes, q_gain, k_gain, t, S):
    smax = 1.01 * HEAD * 0.125 * LOG2E * jnp.max(jnp.abs(q_gain)) * jnp.max(jnp.abs(k_gain))
    dist = (160.0 + 2.0 * smax) / (slopes * LOG2E)
    blocks = jnp.floor((dist - 1.0) / t) + 1.0
    return jnp.clip(blocks, 0.0, float(S // t)).astype(jnp.int32)


def _tile(n, want):
    t = min(n, want)
    assert n % t == 0, (n, t)
    return t


def kernel(x, positions, norm_mix_g, norm_ffn_g, mlp_w_up, mlp_w_down, ev_w_in, ev_w_out, a_q_norm, a_k_norm, a_sinks, b_cq_norm, b_ckv_norm, b_w_uq, b_w_ukv, b_q_norm, b_k_norm, od_w_in, od_w_out, d_q_norm, d_k_norm, d_lambda, d_subln):
    B, S, D = x.shape
    T = B * S
    depth = norm_mix_g.shape[0]
    tm_proj = _tile(T, 512)
    tm_mlp = _tile(T, 512)
    t_a = _tile(S, 256)
    tq_b, t_b = _tile(S, 2048), _tile(S, 512)
    tq_c, t_c = _tile(S, 1024), _tile(S, 256)
    tq_d, t_d = _tile(S, 1024), _tile(S, 512)
    tf = 512

    xf = x.reshape(T, D)
    pos = positions.reshape(T)
    row2 = lambda a: a.reshape(1, -1).astype(F32)
    pair = lambda a: jnp.concatenate([a, a]).reshape(1, LANES).astype(F32)
    pad_qk = lambda a: jnp.pad(a.astype(F32), (0, LANES - B_QK)).reshape(1, LANES)
    inv = (ROPE_THETA ** (-jnp.arange(B_ROPE_HALF, dtype=F32) / B_ROPE_HALF)).reshape(B_ROPE_HALF, 1)
    rope_place = _rope_placement()
    slopes_a = _alibi_slopes(A_HEADS)
    slopes_d = _alibi_slopes(D_HEADS)

    w_up, w_down = mlp_w_up.astype(BF16), mlp_w_down.astype(BF16)
    ev_w_out_bf, od_w_out_bf = ev_w_out.astype(BF16), od_w_out.astype(BF16)
    for layer in range(depth):
        j = layer // 2
        g_mix = row2(norm_mix_g[layer])
        if layer % 2 == 0:
            w_in = ev_w_in[j]
            w_in = jnp.concatenate([w_in[:, :E_ROPE], jnp.zeros((D, HEAD), F32), w_in[:, E_ROPE:],
                                    jnp.zeros((D, LANES - B_QK), F32)], axis=1).astype(BF16)
            wuq = jnp.pad(b_w_uq[j].reshape(-1, B_HEADS, B_QK), ((0, 0), (0, 0), (0, LANES - B_QK)))
            wuq = wuq.reshape(-1, B_HEADS * LANES).astype(BF16)
            qa, ka, va, qb, kb, vb = _even_proj(
                xf, pos, g_mix, w_in, pair(a_q_norm[j]), pair(a_k_norm[j]), row2(b_cq_norm[j]),
                row2(b_ckv_norm[j]), wuq, b_w_ukv[j].astype(BF16), pad_qk(b_q_norm[j]), pad_qk(b_k_norm[j]),
                inv, rope_place, tm_proj)
            r3 = lambda a: a.reshape(B, S, a.shape[-1])
            ma = _attn_a(slopes_a, a_sinks[j].astype(F32), r3(qa), r3(ka), r3(va), t_a).reshape(T, -1)
            mb = _attn_b(r3(qb), r3(kb), r3(vb), tq_b, t_b).reshape(T, -1)
            w_out = ev_w_out_bf
        else:
            lambda_init = 0.8 - 0.6 * math.exp(-0.3 * layer)
            cq, ck, cv, dq, dk, dv = _odd_proj(
                xf, g_mix, od_w_in[j].astype(BF16), d_q_norm[j].reshape(1, LANES).astype(F32),
                d_k_norm[j].reshape(1, LANES).astype(F32), tm_proj)
            r3 = lambda a: a.reshape(B, S, a.shape[-1])
            ma = _attn_c(r3(cq), r3(ck), r3(cv), tq_c, t_c).reshape(T, -1)
            reach = _alibi_reach(slopes_d, d_q_norm[j], d_k_norm[j], t_d, S)
            mb = _attn_d(slopes_d, reach, r3(dq), r3(dk), r3(dv), d_lambda[j].astype(F32), row2(d_subln[j]),
                         lambda_init, tq_d, t_d).reshape(T, -1)
            w_out = od_w_out_bf
        xf = _out_mlp(xf, ma, mb, w_out, row2(norm_ffn_g[layer]), w_up, w_down, layer, tm_mlp, tf)
    return xf.reshape(B, S, D)
```

```python
import functools
import math

import jax
import jax.numpy as jnp
from jax import lax
from jax.experimental import pallas as pl
from jax.experimental.pallas import tpu as pltpu

F32 = jnp.float32
BF16 = jnp.bfloat16

EPS = 1e-6
CHUNK = 64
CHUNK_SHIFT = 6
LANES = 128
HEAD = 64
A_HEADS = 8
A_GROUP = 4
A_WINDOW = 128
A_WINDOW_CHUNKS = A_WINDOW // CHUNK
B_HEADS = 8
B_QK = 96
B_ROPE_HALF = 16
ROPE_THETA = 10000.0
C_HEADS = 8
D_HEADS = 4
B_Q_RANK = 256
B_KV_RANK = 128
MIX_HALF = 512
E_AK = A_HEADS * HEAD
E_AV = E_AK + (A_HEADS // A_GROUP) * HEAD
E_CQ = E_AV + (A_HEADS // A_GROUP) * HEAD
E_CKV = E_CQ + B_Q_RANK
E_ROPE = E_CKV + B_KV_RANK
E_END = E_ROPE + LANES
O_CK, O_CV, O_DQ, O_DK, O_DV, O_END = (i * MIX_HALF for i in range(1, 7))
NEG = -1e30
LOG2E = math.log2(math.e)
DEN_ROWS = 16
VMEM_LIMIT = 56 * 1024 * 1024


def _dot(a, b):
    return jnp.dot(a, b, preferred_element_type=F32)


def _dot_t(a, b):
    return lax.dot_general(a, b, (((1,), (1,)), ((), ())), preferred_element_type=F32)


def _dot_tl(a, b):
    return lax.dot_general(a, b, (((0,), (0,)), ((), ())), preferred_element_type=F32)


def _rms(x, denom):
    return x * lax.rsqrt(jnp.sum(x * x, axis=-1, keepdims=True) * (1.0 / denom) + EPS)


def _pair_rms(x, lo):
    xx = x * x
    s_lo = jnp.sum(jnp.where(lo, xx, 0.0), axis=-1, keepdims=True)
    s_hi = jnp.sum(jnp.where(lo, 0.0, xx), axis=-1, keepdims=True)
    r = jnp.where(lo, lax.rsqrt(s_lo * (1.0 / HEAD) + EPS), lax.rsqrt(s_hi * (1.0 / HEAD) + EPS))
    return x * r


def _lane_lo(shape):
    return lax.broadcasted_iota(jnp.int32, shape, len(shape) - 1) < HEAD


def _pipelined(n, stages):
    for step in range(n + len(stages) - 1):
        for si, stage in enumerate(stages):
            c = step - si
            if 0 <= c < n:
                stage(c)


def _even_proj_kernel(x_ref, pos_ref, g_ref, w_in_ref, aqg_ref, akg_ref, cqg_ref, ckvg_ref,
                      wuq_ref, wukv_ref, bqg_ref, bkg_ref, inv_ref, place_ref,
                      qa_ref, ka_ref, va_ref, qb_ref, kb_ref, vb_ref):
    x = x_ref[...]
    h = _rms(x, x.shape[-1]) * g_ref[...]
    proj = _dot(h.astype(BF16), w_in_ref[...])
    lo = _lane_lo((1, LANES))
    for p in range(A_HEADS // 2):
        seg = proj[:, p * LANES:(p + 1) * LANES]
        qa_ref[:, p * LANES:(p + 1) * LANES] = (_pair_rms(seg, lo) * aqg_ref[...] * (0.125 * LOG2E)).astype(BF16)
    ka_ref[...] = (_pair_rms(proj[:, E_AK:E_AV], lo) * akg_ref[...]).astype(BF16)
    va_ref[...] = proj[:, E_AV:E_CQ].astype(BF16)
    cq = _rms(proj[:, E_CQ:E_CKV], B_Q_RANK) * cqg_ref[...]
    ckv = _rms(proj[:, E_CKV:E_ROPE], B_KV_RANK) * ckvg_ref[...]
    krope = proj[:, E_ROPE:E_END]
    qall = _dot(cq.astype(BF16), wuq_ref[...])
    kvall = _dot(ckv.astype(BF16), wukv_ref[...])
    ang = inv_ref[...] * pos_ref[0].astype(F32)
    trig = jnp.concatenate([jnp.cos(ang), jnp.sin(ang)], axis=0)
    trig_hi = trig.astype(BF16)
    trig_lo = (trig - trig_hi.astype(F32)).astype(BF16)
    placed = _dot_tl(jnp.concatenate([trig_hi, trig_lo], axis=0), place_ref[...])
    cosf = placed[:, 0:LANES] + jnp.where(lo, 1.0, 0.0)
    s_first = placed[:, LANES:2 * LANES]
    s_second = placed[:, 2 * LANES:3 * LANES]

    def rope(t):
        return (t * cosf + pltpu.roll(t, LANES - B_ROPE_HALF, 1) * s_first
                + pltpu.roll(t, B_ROPE_HALF, 1) * s_second)

    scale_b = B_QK ** -0.5 * LOG2E
    krope_rot = rope(krope * bkg_ref[...])
    krope_ss = jnp.sum(krope * krope, axis=-1, keepdims=True)
    sq = [dict() for _ in range(B_HEADS)]
    sk = [dict() for _ in range(B_HEADS)]

    def q_sumsq(hh):
        qh = qall[:, hh * LANES:(hh + 1) * LANES]
        sq[hh]["x"] = qh
        sq[hh]["ss"] = jnp.sum(qh * qh, axis=-1, keepdims=True)

    def q_norm(hh):
        sq[hh]["n"] = sq[hh].pop("x") * lax.rsqrt(sq[hh].pop("ss") * (1.0 / B_QK) + EPS) * bqg_ref[...]

    def q_store(hh):
        qb_ref[:, hh * LANES:(hh + 1) * LANES] = (rope(sq[hh].pop("n")) * scale_b).astype(BF16)

    def k_sumsq(hh):
        knope = jnp.where(lo, kvall[:, hh * LANES:(hh + 1) * LANES], 0.0)
        sk[hh]["x"] = knope
        sk[hh]["ss"] = jnp.sum(knope * knope, axis=-1, keepdims=True) + krope_ss

    def k_store(hh):
        kn = (sk[hh].pop("x") * bkg_ref[...] + krope_rot) * lax.rsqrt(sk[hh].pop("ss") * (1.0 / B_QK) + EPS)
        kb_ref[:, hh * LANES:(hh + 1) * LANES] = kn.astype(BF16)

    _pipelined(B_HEADS, [q_sumsq, k_sumsq, q_norm, k_store, q_store])
    for p in range(B_HEADS // 2):
        v_even = kvall[:, 2 * p * LANES:(2 * p + 1) * LANES]
        v_odd = kvall[:, (2 * p + 1) * LANES:(2 * p + 2) * LANES]
        vb_ref[:, p * LANES:(p + 1) * LANES] = jnp.where(lo, pltpu.roll(v_even, HEAD, 1), v_odd).astype(BF16)


def _even_proj(x, pos, g, w_in, aqg, akg, cqg, ckvg, wuq, wukv, bqg, bkg, inv, place, tm):
    T, D = x.shape
    full = lambda a: pl.BlockSpec(a.shape, lambda i: (0,) * a.ndim)
    row = lambda c: pl.BlockSpec((tm, c), lambda i: (i, 0))
    outs = (E_AK, E_AV - E_AK, E_CQ - E_AV, B_HEADS * LANES, B_HEADS * LANES, MIX_HALF)
    pos = pos.reshape(T // tm, 1, tm)
    return pl.pallas_call(
        _even_proj_kernel,
        grid=(T // tm,),
        in_specs=[row(D), pl.BlockSpec((1, 1, tm), lambda i: (i, 0, 0))]
        + [full(a) for a in (g, w_in, aqg, akg, cqg, ckvg, wuq, wukv, bqg, bkg, inv, place)],
        out_specs=[row(c) for c in outs],
        out_shape=[jax.ShapeDtypeStruct((T, c), BF16) for c in outs],
        compiler_params=pltpu.CompilerParams(dimension_semantics=("arbitrary",), vmem_limit_bytes=VMEM_LIMIT),
        name="even_proj",
    )(x, pos, g, w_in, aqg, akg, cqg, ckvg, wuq, wukv, bqg, bkg, inv, place)


def _odd_proj_kernel(x_ref, g_ref, w_in_ref, dqg_ref, dkg_ref,
                     cq_ref, ck_ref, cv_ref, dq_ref, dk_ref, dv_ref):
    x = x_ref[...]
    h = _rms(x, x.shape[-1]) * g_ref[...]
    proj = _dot(h.astype(BF16), w_in_ref[...])
    lo = _lane_lo((1, LANES))
    qscale = 0.125 * LOG2E
    cq_ref[...] = (proj[:, 0:O_CK] * qscale).astype(BF16)
    ck_ref[...] = proj[:, O_CK:O_CV].astype(BF16)
    cv_ref[...] = proj[:, O_CV:O_DQ].astype(BF16)
    for hh in range(D_HEADS):
        sl = slice(hh * LANES, (hh + 1) * LANES)
        dq_ref[:, sl] = (_pair_rms(proj[:, O_DQ + hh * LANES:O_DQ + (hh + 1) * LANES], lo)
                         * dqg_ref[...] * qscale).astype(BF16)
        dk_ref[:, sl] = (_pair_rms(proj[:, O_DK + hh * LANES:O_DK + (hh + 1) * LANES], lo)
                         * dkg_ref[...]).astype(BF16)
    dv_ref[...] = proj[:, O_DV:O_END].astype(BF16)


def _odd_proj(x, g, w_in, dqg, dkg, tm):
    T, D = x.shape
    full = lambda a: pl.BlockSpec(a.shape, lambda i: (0,) * a.ndim)
    row = lambda c: pl.BlockSpec((tm, c), lambda i: (i, 0))
    return pl.pallas_call(
        _odd_proj_kernel,
        grid=(T // tm,),
        in_specs=[row(D)] + [full(a) for a in (g, w_in, dqg, dkg)],
        out_specs=[row(MIX_HALF)] * 6,
        out_shape=[jax.ShapeDtypeStruct((T, MIX_HALF), BF16)] * 6,
        compiler_params=pltpu.CompilerParams(dimension_semantics=("arbitrary",), vmem_limit_bytes=VMEM_LIMIT),
        name="odd_proj",
    )(x, g, w_in, dqg, dkg)


def _attn_a_kernel(slope_ref, sink_ref, q_ref, k_ref, v_ref, o_ref, *, tq, win):
    g = pl.program_id(1)
    qi = pl.program_id(2)
    q0 = qi * tq
    ks = pl.multiple_of(jnp.maximum(q0 - A_WINDOW, 0), A_WINDOW)
    lane_half = lax.broadcasted_iota(jnp.int32, (1, LANES), 1) // HEAD

    def both_halves(x):
        xf = x.astype(F32)
        return jnp.where(lane_half == g, xf, pltpu.roll(xf, HEAD, 1)).astype(BF16)

    k = both_halves(k_ref[0, pl.ds(ks, win), :])
    v = both_halves(v_ref[0, pl.ds(ks, win), :])
    kpos = ks + lax.broadcasted_iota(jnp.int32, (win, tq), 0)
    qpos = q0 + lax.broadcasted_iota(jnp.int32, (win, tq), 1)
    dch = (qpos >> CHUNK_SHIFT) - (kpos >> CHUNK_SHIFT)
    allowed = (dch >= 0) & (dch <= A_WINDOW_CHUNKS)
    negdist = -jnp.abs(qpos - kpos).astype(F32)
    st = [dict() for _ in range(A_GROUP)]

    def scores(i):
        qq = q_ref[0, :, (i // 2) * LANES:(i // 2 + 1) * LANES]
        qh = jnp.where(lane_half == i % 2, qq, jnp.zeros_like(qq))
        st[i]["s"] = _dot_t(k, qh)

    def softmax(i):
        hidx = A_GROUP * g + i
        s = jnp.where(allowed, st[i].pop("s") + (slope_ref[hidx] * LOG2E) * negdist, NEG)
        sink = sink_ref[hidx] * LOG2E
        m = jnp.maximum(jnp.max(s, axis=0, keepdims=True), sink)
        p = jnp.exp2(s - m)
        st[i]["den"] = jnp.sum(p, axis=0, keepdims=True) + jnp.exp2(sink - m)
        st[i]["p"] = p.astype(BF16)

    def values(i):
        st[i]["o"] = (_dot_tl(v, st[i].pop("p")) / st[i].pop("den"))[0:HEAD, :]

    _pipelined(A_GROUP, [scores, softmax, values])
    outs = [st[i]["o"] for i in range(A_GROUP)]
    o_ref[0] = jnp.concatenate(outs, axis=0).T.astype(BF16)


def _attn_a(slopes, sinks, qa, ka, va, tq):
    B, S, _ = qa.shape
    win = tq + A_WINDOW
    smem = pl.BlockSpec(memory_space=pltpu.SMEM)
    gw = A_GROUP * HEAD
    return pl.pallas_call(
        functools.partial(_attn_a_kernel, tq=tq, win=win),
        grid=(B, A_HEADS // A_GROUP, S // tq),
        in_specs=[smem, smem,
                  pl.BlockSpec((1, tq, gw), lambda b, g, i: (b, i, g)),
                  pl.BlockSpec((1, S, LANES), lambda b, g, i: (b, 0, 0)),
                  pl.BlockSpec((1, S, LANES), lambda b, g, i: (b, 0, 0))],
        out_specs=pl.BlockSpec((1, tq, gw), lambda b, g, i: (b, i, g)),
        out_shape=jax.ShapeDtypeStruct((B, S, MIX_HALF), BF16),
        compiler_params=pltpu.CompilerParams(dimension_semantics=("arbitrary",) * 3, vmem_limit_bytes=VMEM_LIMIT),
        name="attn_a",
    )(slopes, sinks, qa, ka, va)


def _walk_key_blocks(qi, nchunk, step, group, max_blocks=None):
    first = qi * nchunk
    trips = first // group
    if max_blocks is None:
        def full(i, carry):
            step([(i * group + g, None) for g in range(group)])
            return carry

        lax.fori_loop(0, trips, full, 0)
        step([(first + d, d) for d in range(nchunk)])
    else:
        step([(first + d, d) for d in range(nchunk - 1, -1, -1)])

        def full(i, carry):
            step([(first - 1 - i * group - g, None) for g in range(group)])
            return carry

        lax.fori_loop(0, jnp.minimum(trips, (max_blocks + (group - 1)) // group), full, 0)


def _flash_specs(S, tq, q_width, k_width, v_width):
    return dict(
        in_specs=[pl.BlockSpec((1, tq, q_width), lambda b, p, i: (b, i, p)),
                  pl.BlockSpec((1, S, k_width), lambda b, p, i: (b, 0, p)),
                  pl.BlockSpec((1, S, v_width), lambda b, p, i: (b, 0, p))],
        out_specs=pl.BlockSpec((1, tq, LANES), lambda b, p, i: (b, i, p)),
        compiler_params=pltpu.CompilerParams(dimension_semantics=("arbitrary",) * 3, vmem_limit_bytes=VMEM_LIMIT),
    )


def _chunk_causal(t):
    krow = lax.broadcasted_iota(jnp.int32, (t, t), 0)
    qcol = lax.broadcasted_iota(jnp.int32, (t, t), 1)
    return (krow >> CHUNK_SHIFT) <= (qcol >> CHUNK_SHIFT)


def _softmax_stage(s, off, lanes, e, m_sc):
    m_prev = m_sc[e, :, lanes]
    m_new = jnp.maximum(m_prev, jnp.max(s, axis=0, keepdims=True) - off)
    alpha = jnp.exp2(m_prev - m_new)
    pr = jnp.exp2(s - (m_new + off))
    m_sc[e, :, lanes] = m_new
    return pr.astype(BF16), alpha


def _attn_b_kernel(q_ref, k_ref, v_ref, o_ref, m_sc, acc_sc, *, tq, t):
    qi = pl.program_id(2)
    nchunk = tq // t
    chains = [(e, c) for c in range(nchunk) for e in range(2)]
    lane = lax.broadcasted_iota(jnp.int32, (1, LANES), 1)
    ones_row = (HEAD, 0)
    m_sc[...] = jnp.full(m_sc.shape, NEG, F32)
    acc_sc[...] = jnp.zeros(acc_sc.shape, F32)

    def step(blocks):
        rows = [pl.ds(pl.multiple_of(j * t, t), t) for j, _ in blocks]
        live = [(e, c, g) for g, (_, d) in enumerate(blocks) for (e, c) in chains if d is None or c >= d]
        st = [dict() for _ in live]
        vals = {}

        def values_of(e, g):
            if (e, g) not in vals:
                v = v_ref[0, rows[g], :]
                own = (lane < HEAD) if e == 0 else (lane >= HEAD)
                vals[e, g] = jnp.where(own, v, jnp.where(lane == ones_row[e], 1.0, 0.0).astype(BF16))
            return vals[e, g]

        def scores(i):
            e, c, g = live[i]
            st[i]["s"] = _dot_t(k_ref[0, rows[g], e * LANES:(e + 1) * LANES],
                                q_ref[0, c * t:(c + 1) * t, e * LANES:(e + 1) * LANES])

        def softmax(i):
            e, c, g = live[i]
            s = st[i].pop("s")
            if c == blocks[g][1]:
                s = jnp.where(_chunk_causal(t), s, NEG)
            st[i]["p"], st[i]["alpha"] = _softmax_stage(s, 0.0, slice(c * t, (c + 1) * t), e, m_sc)

        def values(i):
            e, c, g = live[i]
            lanes = slice(c * t, (c + 1) * t)
            acc_sc[e, :, lanes] = (st[i]["alpha"] * acc_sc[e, :, lanes]
                                   + _dot_tl(values_of(e, g), st[i].pop("p")))

        _pipelined(len(live), [scores, softmax, values])

    _walk_key_blocks(qi, nchunk, step, group=nchunk)
    first = lax.broadcasted_iota(jnp.int32, (LANES, 1), 0) < HEAD
    den = [acc_sc[e, ones_row[e]:ones_row[e] + 1, :] for e in range(2)]
    o_t = jnp.where(first, acc_sc[0] / den[0], acc_sc[1] / den[1])
    o_ref[0] = o_t.T.astype(BF16)


def _attn_b(qb, kb, vb, tq, t):
    B, S, _ = qb.shape
    return pl.pallas_call(
        functools.partial(_attn_b_kernel, tq=tq, t=t),
        grid=(B, B_HEADS // 2, S // tq),
        out_shape=jax.ShapeDtypeStruct((B, S, MIX_HALF), BF16),
        scratch_shapes=[pltpu.VMEM((2, 1, tq), F32), pltpu.VMEM((2, LANES, tq), F32)],
        name="attn_b",
        **_flash_specs(S, tq, 2 * LANES, 2 * LANES, LANES),
    )(qb, kb, vb)


def _attn_c_kernel(q_ref, k_ref, v_ref, o_ref, r_sc, acc_sc, *, tq, t):
    qi = pl.program_id(2)
    nchunk = tq // t
    first = qi * nchunk
    chains = [(e, c) for c in range(nchunk) for e in range(2)]
    krow = lax.broadcasted_iota(jnp.int32, (t, t), 0)
    qcol = lax.broadcasted_iota(jnp.int32, (t, t), 1)
    from_here = jnp.where(qcol >= krow, 1.0, 0.0).astype(BF16)
    from_here = jnp.concatenate([from_here, from_here], axis=1)
    lane_lo = _lane_lo((1, LANES))
    r_sc[...] = jnp.zeros(r_sc.shape, F32)
    acc_sc[...] = jnp.zeros(acc_sc.shape, F32)

    def step(backs):
        live = [(e, c, b) for b in range(len(backs)) for (e, c) in chains]
        st = [dict() for _ in live]

        def block(i):
            e, c, b = live[i]
            return first + c - backs[b]

        def rows(i):
            return pl.ds(pl.multiple_of(jnp.maximum(block(i), 0) * t, t), t)

        def scores(i):
            e, c, b = live[i]
            qq = q_ref[0, c * t:(c + 1) * t, :]
            st[i]["z"] = _dot_t(k_ref[0, rows(i), :], jnp.where(lane_lo == (e == 0), qq, jnp.zeros_like(qq)))

        def softplus(i):
            e, c, b = live[i]
            z = st[i]["z"]
            sp = jnp.maximum(z, jnp.log2(1.0 + jnp.exp2(jnp.minimum(z, 126.0))))
            if isinstance(backs[b], int) and backs[b] == 0:
                sp = jnp.where(krow < qcol, sp, 0.0)
            hi = sp.astype(BF16)
            lo = (sp - hi.astype(F32)).astype(BF16)
            st[i]["hilo"] = jnp.concatenate([hi, lo], axis=0)

        def suffix(i):
            st[i]["suf"] = _dot(from_here, st[i].pop("hilo"))

        def weights(i):
            e, c, b = live[i]
            lanes = slice(c * t, (c + 1) * t)
            suf = st[i].pop("suf")
            a = jnp.exp2(st[i].pop("z") - suf)
            if isinstance(backs[b], int) and backs[b] == 0:
                a = jnp.where(krow < qcol, a, 0.0)
            st[i]["a"] = a.astype(BF16)
            r_prev = r_sc[e, :, lanes]
            w = jnp.exp2(-r_prev)
            tot = suf[0:1, :]
            if not (isinstance(backs[b], int) and backs[b] <= c):
                exists = (block(i) >= 0).astype(F32)
                w, tot = w * exists, tot * exists
            st[i]["w"] = w
            r_sc[e, :, lanes] = r_prev + tot

        def values(i):
            e, c, b = live[i]
            lanes = slice(c * t, (c + 1) * t)
            acc_sc[e, :, lanes] = (acc_sc[e, :, lanes]
                                   + st[i].pop("w") * _dot_tl(v_ref[0, rows(i), :], st[i].pop("a")))

        _pipelined(len(live), [scores, softplus, suffix, weights, values])

    def still_live():
        return jnp.max(jnp.exp2(-r_sc[...])) > 0.0

    step([0, 1])

    def further(carry):
        step([carry[0]])
        return carry[0] + 1, still_live()

    lax.while_loop(lambda carry: (carry[0] < first + nchunk) & carry[1], further, (2, still_live()))
    first_head = lax.broadcasted_iota(jnp.int32, (LANES, 1), 0) < HEAD
    o_ref[0] = jnp.where(first_head, acc_sc[0], acc_sc[1]).T.astype(BF16)


def _attn_c(cq, ck, cv, tq, t):
    B, S, _ = cq.shape
    return pl.pallas_call(
        functools.partial(_attn_c_kernel, tq=tq, t=t),
        grid=(B, C_HEADS // 2, S // tq),
        out_shape=jax.ShapeDtypeStruct((B, S, MIX_HALF), BF16),
        scratch_shapes=[pltpu.VMEM((2, 1, tq), F32), pltpu.VMEM((2, LANES, tq), F32)],
        name="attn_c",
        **_flash_specs(S, tq, LANES, LANES, LANES),
    )(cq, ck, cv)


def _attn_d_kernel(slope_ref, reach_ref, q_ref, k_ref, v_ref, lam_ref, subln_ref, o_ref, m_sc, acc_sc,
                   *, tq, t, lambda_init):
    h = pl.program_id(1)
    qi = pl.program_id(2)
    nchunk = tq // t
    chains = [(e, c) for c in range(nchunk) for e in range(2)]
    slope = slope_ref[h] * LOG2E
    lane_lo = _lane_lo((1, LANES))
    key_bias = slope * lax.broadcasted_iota(jnp.int32, (t, LANES), 0).astype(F32)
    m_sc[...] = jnp.full(m_sc.shape, NEG, F32)
    acc_sc[...] = jnp.zeros(acc_sc.shape, F32)

    def step(blocks):
        rows = [pl.ds(pl.multiple_of(j * t, t), t) for j, _ in blocks]
        live = [(e, c, g) for g, (_, d) in enumerate(blocks) for (e, c) in chains if d is None or c >= d]
        st = [dict() for _ in live]
        vals = {}

        def values_of(g):
            if g not in vals:
                vals[g] = jnp.concatenate([v_ref[0, rows[g], :].T, jnp.ones((DEN_ROWS, t), BF16)], axis=0)
            return vals[g]

        def scores(i):
            e, c, g = live[i]
            qq = q_ref[0, c * t:(c + 1) * t, :]
            st[i]["s"] = _dot_t(k_ref[0, rows[g], :], jnp.where(lane_lo == (e == 0), qq, jnp.zeros_like(qq)))

        def softmax(i):
            e, c, g = live[i]
            s = st[i].pop("s")
            j, d = blocks[g]
            if c == d:
                krow = lax.broadcasted_iota(jnp.int32, (t, t), 0)
                qcol = lax.broadcasted_iota(jnp.int32, (t, t), 1)
                s = s + slope * jnp.minimum(krow, 2 * qcol - krow).astype(F32)
                s = jnp.where(_chunk_causal(t), s, NEG)
                off = 0.0
            else:
                s = jnp.concatenate([s[:, b * LANES:(b + 1) * LANES] + key_bias for b in range(t // LANES)], axis=1)
                off = (slope * ((qi * nchunk + c - j) * t).astype(F32) if d is None
                       else slope * float((c - d) * t))
            st[i]["p"], st[i]["alpha"] = _softmax_stage(s, off, slice(c * t, (c + 1) * t), e, m_sc)

        def values(i):
            e, c, g = live[i]
            lanes = slice(c * t, (c + 1) * t)
            acc_sc[e, :, lanes] = (st[i]["alpha"] * acc_sc[e, :, lanes]
                                   + _dot(values_of(g), st[i].pop("p")))

        _pipelined(len(live), [scores, softmax, values])

    _walk_key_blocks(qi, nchunk, step, group=2, max_blocks=reach_ref[h])
    lf = lam_ref[...]
    lam = (jnp.exp(jnp.sum(lf[0:1] * lf[1:2], axis=-1, keepdims=True))
           - jnp.exp(jnp.sum(lf[2:3] * lf[3:4], axis=-1, keepdims=True)) + lambda_init)
    sm = [acc_sc[e, 0:LANES, :] / acc_sc[e, LANES:LANES + 1, :] for e in range(2)]
    o = (sm[0] - lam * sm[1]).T
    o = _rms(o, LANES) * subln_ref[...] * (1.0 - lambda_init)
    o_ref[0] = o.astype(BF16)


def _attn_d(slopes, reach, dq, dk, dv, lam, subln, lambda_init, tq, t):
    B, S, _ = dq.shape
    specs = _flash_specs(S, tq, LANES, LANES, LANES)
    whole = lambda a: pl.BlockSpec(a.shape, lambda b, h, i: (0, 0))
    smem = pl.BlockSpec(memory_space=pltpu.SMEM)
    specs["in_specs"] = [smem, smem] + specs["in_specs"] + [whole(lam), whole(subln)]
    return pl.pallas_call(
        functools.partial(_attn_d_kernel, tq=tq, t=t, lambda_init=lambda_init),
        grid=(B, D_HEADS, S // tq),
        out_shape=jax.ShapeDtypeStruct((B, S, MIX_HALF), BF16),
        scratch_shapes=[pltpu.VMEM((2, 1, tq), F32), pltpu.VMEM((2, LANES + DEN_ROWS, tq), F32)],
        name="attn_d",
        **specs,
    )(slopes, reach, dq, dk, dv, lam, subln)


def _out_mlp_kernel(x_ref, ma_ref, mb_ref, wo_ref, g_ref, wu_ref, wd_ref, o_ref, *, tf):
    half = ma_ref.shape[-1]
    x1 = x_ref[...] + _dot(ma_ref[...], wo_ref[0:half, :]) + _dot(mb_ref[...], wo_ref[half:2 * half, :])
    h = (_rms(x1, x1.shape[-1]) * g_ref[...]).astype(BF16)
    o_ref[...] = x1
    for f in range(wu_ref.shape[-1] // tf):
        u = jnp.maximum(_dot(h, wu_ref[:, f * tf:(f + 1) * tf]), 0.0)
        o_ref[...] += _dot((u * u).astype(BF16), wd_ref[f * tf:(f + 1) * tf, :])


def _out_mlp(x, ma, mb, wo, g, wu, wd, layer, tm, tf):
    T, D = x.shape
    full = lambda a: pl.BlockSpec(a.shape, lambda i: (0,) * a.ndim, pipeline_mode=pl.Buffered(1))
    of_layer = lambda a, l: pl.BlockSpec((None,) + a.shape[1:], lambda i: (l,) + (0,) * (a.ndim - 1),
                                         pipeline_mode=pl.Buffered(1))
    row = lambda c: pl.BlockSpec((tm, c), lambda i: (i, 0))
    return pl.pallas_call(
        functools.partial(_out_mlp_kernel, tf=tf),
        grid=(T // tm,),
        in_specs=[row(D), row(ma.shape[-1]), row(mb.shape[-1]), of_layer(wo, layer // 2), full(g),
                  of_layer(wu, layer), of_layer(wd, layer)],
        out_specs=row(D),
        out_shape=jax.ShapeDtypeStruct((T, D), F32),
        compiler_params=pltpu.CompilerParams(dimension_semantics=("arbitrary",), vmem_limit_bytes=VMEM_LIMIT),
        name="out_mlp",
    )(x, ma, mb, wo, g, wu, wd)


def _alibi_slopes(n):
    return 2.0 ** (-8.0 * jnp.arange(1, n + 1, dtype=F32) / n)


def _rope_placement():
    f = jnp.arange(B_ROPE_HALF)
    p = jnp.zeros((2 * B_ROPE_HALF, 3 * LANES), F32)
    p = p.at[f, HEAD + f].set(1.0).at[f, HEAD + B_ROPE_HALF + f].set(1.0)
    p = p.at[B_ROPE_HALF + f, LANES + HEAD + f].set(-1.0)
    p = p.at[B_ROPE_HALF + f, 2 * LANES + HEAD + B_ROPE_HALF + f].set(1.0)
    return jnp.concatenate([p, p], axis=0).astype(BF16)


def _alibi_reach(slopes, q_gain, k_gain, t, S):
    smax = 1.01 * HEAD * 0.125 * LOG2E * jnp.max(jnp.abs(q_gain)) * jnp.max(jnp.abs(k_gain))
    dist = (160.0 + 2.0 * smax) / (slopes * LOG2E)
    blocks = jnp.floor((dist - 1.0) / t) + 1.0
    return jnp.clip(blocks, 0.0, float(S // t)).astype(jnp.int32)


def _tile(n, want):
    t = min(n, want)
    assert n % t == 0, (n, t)
    return t


def kernel(x, positions, norm_mix_g, norm_ffn_g, mlp_w_up, mlp_w_down, ev_w_in, ev_w_out, a_q_norm, a_k_norm, a_sinks, b_cq_norm, b_ckv_norm, b_w_uq, b_w_ukv, b_q_norm, b_k_norm, od_w_in, od_w_out, d_q_norm, d_k_norm, d_lambda, d_subln):
    B, S, D = x.shape
    T = B * S
    depth = norm_mix_g.shape[0]
    tm_proj = _tile(T, 512)
    tm_mlp = _tile(T, 512)
    t_a = _tile(S, 256)
    tq_b, t_b = _tile(S, 2048), _tile(S, 512)
    tq_c, t_c = _tile(S, 1024), _tile(S, 256)
    tq_d, t_d = _tile(S, 2048), _tile(S, 512)
    tf = 512

    xf = x.reshape(T, D)
    pos = positions.reshape(T)
    row2 = lambda a: a.reshape(1, -1).astype(F32)
    pair = lambda a: jnp.concatenate([a, a]).reshape(1, LANES).astype(F32)
    pad_qk = lambda a: jnp.pad(a.astype(F32), (0, LANES - B_QK)).reshape(1, LANES)
    inv = (ROPE_THETA ** (-jnp.arange(B_ROPE_HALF, dtype=F32) / B_ROPE_HALF)).reshape(B_ROPE_HALF, 1)
    rope_place = _rope_placement()
    slopes_a = _alibi_slopes(A_HEADS)
    slopes_d = _alibi_slopes(D_HEADS)

    w_up, w_down = mlp_w_up.astype(BF16), mlp_w_down.astype(BF16)
    ev_w_out_bf, od_w_out_bf = ev_w_out.astype(BF16), od_w_out.astype(BF16)
    for layer in range(depth):
        j = layer // 2
        g_mix = row2(norm_mix_g[layer])
        if layer % 2 == 0:
            w_in = ev_w_in[j]
            w_in = jnp.concatenate([w_in[:, :E_ROPE], jnp.zeros((D, HEAD), F32), w_in[:, E_ROPE:],
                                    jnp.zeros((D, LANES - B_QK), F32)], axis=1).astype(BF16)
            wuq = jnp.pad(b_w_uq[j].reshape(-1, B_HEADS, B_QK), ((0, 0), (0, 0), (0, LANES - B_QK)))
            wuq = wuq.reshape(-1, B_HEADS * LANES).astype(BF16)
            qa, ka, va, qb, kb, vb = _even_proj(
                xf, pos, g_mix, w_in, pair(a_q_norm[j]), pair(a_k_norm[j]), row2(b_cq_norm[j]),
                row2(b_ckv_norm[j]), wuq, b_w_ukv[j].astype(BF16), pad_qk(b_q_norm[j]), pad_qk(b_k_norm[j]),
                inv, rope_place, tm_proj)
            r3 = lambda a: a.reshape(B, S, a.shape[-1])
            ma = _attn_a(slopes_a, a_sinks[j].astype(F32), r3(qa), r3(ka), r3(va), t_a).reshape(T, -1)
            mb = _attn_b(r3(qb), r3(kb), r3(vb), tq_b, t_b).reshape(T, -1)
            w_out = ev_w_out_bf
        else:
            lambda_init = 0.8 - 0.6 * math.exp(-0.3 * layer)
            cq, ck, cv, dq, dk, dv = _odd_proj(
                xf, g_mix, od_w_in[j].astype(BF16), d_q_norm[j].reshape(1, LANES).astype(F32),
                d_k_norm[j].reshape(1, LANES).astype(F32), tm_proj)
            r3 = lambda a: a.reshape(B, S, a.shape[-1])
            ma = _attn_c(r3(cq), r3(ck), r3(cv), tq_c, t_c).reshape(T, -1)
            reach = _alibi_reach(slopes_d, d_q_norm[j], d_k_norm[j], t_d, S)
            mb = _attn_d(slopes_d, reach, r3(dq), r3(dk), r3(dv), d_lambda[j].astype(F32), row2(d_subln[j]),
                         lambda_init, tq_d, t_d).reshape(T, -1)
            w_out = od_w_out_bf
        xf = _out_mlp(xf, ma, mb, w_out, row2(norm_ffn_g[layer]), w_up, w_down, layer, tm_mlp, tf)
    return xf.reshape(B, S, D)
```

```python
import functools
import math

import jax
import jax.numpy as jnp
from jax import lax
from jax.experimental import pallas as pl
from jax.experimental.pallas import tpu as pltpu

F32 = jnp.float32
BF16 = jnp.bfloat16

EPS = 1e-6
CHUNK = 64
CHUNK_SHIFT = 6
LANES = 128
HEAD = 64
A_HEADS = 8
A_GROUP = 4
A_WINDOW = 128
A_WINDOW_CHUNKS = A_WINDOW // CHUNK
B_HEADS = 8
B_QK = 96
B_ROPE_HALF = 16
ROPE_THETA = 10000.0
C_HEADS = 8
D_HEADS = 4
B_Q_RANK = 256
B_KV_RANK = 128
MIX_HALF = 512
E_AK = A_HEADS * HEAD
E_AV = E_AK + (A_HEADS // A_GROUP) * HEAD
E_CQ = E_AV + (A_HEADS // A_GROUP) * HEAD
E_CKV = E_CQ + B_Q_RANK
E_ROPE = E_CKV + B_KV_RANK
E_END = E_ROPE + LANES
O_CK, O_CV, O_DQ, O_DK, O_DV, O_END = (i * MIX_HALF for i in range(1, 7))
NEG = -1e30
LOG2E = math.log2(math.e)
DEN_ROWS = 16
VMEM_LIMIT = 56 * 1024 * 1024


def _dot(a, b):
    return jnp.dot(a, b, preferred_element_type=F32)


def _dot_t(a, b):
    return lax.dot_general(a, b, (((1,), (1,)), ((), ())), preferred_element_type=F32)


def _dot_tl(a, b):
    return lax.dot_general(a, b, (((0,), (0,)), ((), ())), preferred_element_type=F32)


def _rms(x, denom):
    return x * lax.rsqrt(jnp.sum(x * x, axis=-1, keepdims=True) * (1.0 / denom) + EPS)


def _pair_rms(x, lo):
    xx = x * x
    s_lo = jnp.sum(jnp.where(lo, xx, 0.0), axis=-1, keepdims=True)
    s_hi = jnp.sum(jnp.where(lo, 0.0, xx), axis=-1, keepdims=True)
    r = jnp.where(lo, lax.rsqrt(s_lo * (1.0 / HEAD) + EPS), lax.rsqrt(s_hi * (1.0 / HEAD) + EPS))
    return x * r


def _lane_lo(shape):
    return lax.broadcasted_iota(jnp.int32, shape, len(shape) - 1) < HEAD


def _pipelined(n, stages):
    for step in range(n + len(stages) - 1):
        for si, stage in enumerate(stages):
            c = step - si
            if 0 <= c < n:
                stage(c)


def _even_proj_kernel(x_ref, pos_ref, g_ref, w_in_ref, aqg_ref, akg_ref, cqg_ref, ckvg_ref,
                      wuq_ref, wukv_ref, bqg_ref, bkg_ref, inv_ref, place_ref,
                      qa_ref, ka_ref, va_ref, qb_ref, kb_ref, vb_ref):
    x = x_ref[...]
    h = _rms(x, x.shape[-1]) * g_ref[...]
    proj = _dot(h.astype(BF16), w_in_ref[...])
    lo = _lane_lo((1, LANES))
    for p in range(A_HEADS // 2):
        seg = proj[:, p * LANES:(p + 1) * LANES]
        qa_ref[:, p * LANES:(p + 1) * LANES] = (_pair_rms(seg, lo) * aqg_ref[...] * (0.125 * LOG2E)).astype(BF16)
    ka_ref[...] = (_pair_rms(proj[:, E_AK:E_AV], lo) * akg_ref[...]).astype(BF16)
    va_ref[...] = proj[:, E_AV:E_CQ].astype(BF16)
    cq = _rms(proj[:, E_CQ:E_CKV], B_Q_RANK) * cqg_ref[...]
    ckv = _rms(proj[:, E_CKV:E_ROPE], B_KV_RANK) * ckvg_ref[...]
    krope = proj[:, E_ROPE:E_END]
    qall = _dot(cq.astype(BF16), wuq_ref[...])
    kvall = _dot(ckv.astype(BF16), wukv_ref[...])
    ang = inv_ref[...] * pos_ref[0].astype(F32)
    trig = jnp.concatenate([jnp.cos(ang), jnp.sin(ang)], axis=0)
    trig_hi = trig.astype(BF16)
    trig_lo = (trig - trig_hi.astype(F32)).astype(BF16)
    placed = _dot_tl(jnp.concatenate([trig_hi, trig_lo], axis=0), place_ref[...])
    cosf = placed[:, 0:LANES] + jnp.where(lo, 1.0, 0.0)
    s_first = placed[:, LANES:2 * LANES]
    s_second = placed[:, 2 * LANES:3 * LANES]

    def rope(t):
        return (t * cosf + pltpu.roll(t, LANES - B_ROPE_HALF, 1) * s_first
                + pltpu.roll(t, B_ROPE_HALF, 1) * s_second)

    scale_b = B_QK ** -0.5 * LOG2E
    krope_rot = rope(krope * bkg_ref[...])
    krope_ss = jnp.sum(krope * krope, axis=-1, keepdims=True)
    sq = [dict() for _ in range(B_HEADS)]
    sk = [dict() for _ in range(B_HEADS)]

    def q_sumsq(hh):
        qh = qall[:, hh * LANES:(hh + 1) * LANES]
        sq[hh]["x"] = qh
        sq[hh]["ss"] = jnp.sum(qh * qh, axis=-1, keepdims=True)

    def q_norm(hh):
        sq[hh]["n"] = sq[hh].pop("x") * lax.rsqrt(sq[hh].pop("ss") * (1.0 / B_QK) + EPS) * bqg_ref[...]

    def q_store(hh):
        qb_ref[:, hh * LANES:(hh + 1) * LANES] = (rope(sq[hh].pop("n")) * scale_b).astype(BF16)

    def k_sumsq(hh):
        knope = jnp.where(lo, kvall[:, hh * LANES:(hh + 1) * LANES], 0.0)
        sk[hh]["x"] = knope
        sk[hh]["ss"] = jnp.sum(knope * knope, axis=-1, keepdims=True) + krope_ss

    def k_store(hh):
        kn = (sk[hh].pop("x") * bkg_ref[...] + krope_rot) * lax.rsqrt(sk[hh].pop("ss") * (1.0 / B_QK) + EPS)
        kb_ref[:, hh * LANES:(hh + 1) * LANES] = kn.astype(BF16)

    _pipelined(B_HEADS, [q_sumsq, k_sumsq, q_norm, k_store, q_store])
    for p in range(B_HEADS // 2):
        v_even = kvall[:, 2 * p * LANES:(2 * p + 1) * LANES]
        v_odd = kvall[:, (2 * p + 1) * LANES:(2 * p + 2) * LANES]
        vb_ref[:, p * LANES:(p + 1) * LANES] = jnp.where(lo, pltpu.roll(v_even, HEAD, 1), v_odd).astype(BF16)


def _even_proj(x, pos, g, w_in, aqg, akg, cqg, ckvg, wuq, wukv, bqg, bkg, inv, place, tm):
    T, D = x.shape
    full = lambda a: pl.BlockSpec(a.shape, lambda i: (0,) * a.ndim)
    row = lambda c: pl.BlockSpec((tm, c), lambda i: (i, 0))
    outs = (E_AK, E_AV - E_AK, E_CQ - E_AV, B_HEADS * LANES, B_HEADS * LANES, MIX_HALF)
    pos = pos.reshape(T // tm, 1, tm)
    return pl.pallas_call(
        _even_proj_kernel,
        grid=(T // tm,),
        in_specs=[row(D), pl.BlockSpec((1, 1, tm), lambda i: (i, 0, 0))]
        + [full(a) for a in (g, w_in, aqg, akg, cqg, ckvg, wuq, wukv, bqg, bkg, inv, place)],
        out_specs=[row(c) for c in outs],
        out_shape=[jax.ShapeDtypeStruct((T, c), BF16) for c in outs],
        compiler_params=pltpu.CompilerParams(dimension_semantics=("arbitrary",), vmem_limit_bytes=VMEM_LIMIT),
        name="even_proj",
    )(x, pos, g, w_in, aqg, akg, cqg, ckvg, wuq, wukv, bqg, bkg, inv, place)


def _odd_proj_kernel(x_ref, g_ref, w_in_ref, dqg_ref, dkg_ref,
                     cq_ref, ck_ref, cv_ref, dq_ref, dk_ref, dv_ref):
    x = x_ref[...]
    h = _rms(x, x.shape[-1]) * g_ref[...]
    proj = _dot(h.astype(BF16), w_in_ref[...])
    lo = _lane_lo((1, LANES))
    qscale = 0.125 * LOG2E
    cq_ref[...] = (proj[:, 0:O_CK] * qscale).astype(BF16)
    ck_ref[...] = proj[:, O_CK:O_CV].astype(BF16)
    cv_ref[...] = proj[:, O_CV:O_DQ].astype(BF16)
    for hh in range(D_HEADS):
        sl = slice(hh * LANES, (hh + 1) * LANES)
        dq_ref[:, sl] = (_pair_rms(proj[:, O_DQ + hh * LANES:O_DQ + (hh + 1) * LANES], lo)
                         * dqg_ref[...] * qscale).astype(BF16)
        dk_ref[:, sl] = (_pair_rms(proj[:, O_DK + hh * LANES:O_DK + (hh + 1) * LANES], lo)
                         * dkg_ref[...]).astype(BF16)
    dv_ref[...] = proj[:, O_DV:O_END].astype(BF16)


def _odd_proj(x, g, w_in, dqg, dkg, tm):
    T, D = x.shape
    full = lambda a: pl.BlockSpec(a.shape, lambda i: (0,) * a.ndim)
    row = lambda c: pl.BlockSpec((tm, c), lambda i: (i, 0))
    return pl.pallas_call(
        _odd_proj_kernel,
        grid=(T // tm,),
        in_specs=[row(D)] + [full(a) for a in (g, w_in, dqg, dkg)],
        out_specs=[row(MIX_HALF)] * 6,
        out_shape=[jax.ShapeDtypeStruct((T, MIX_HALF), BF16)] * 6,
        compiler_params=pltpu.CompilerParams(dimension_semantics=("arbitrary",), vmem_limit_bytes=VMEM_LIMIT),
        name="odd_proj",
    )(x, g, w_in, dqg, dkg)


def _attn_a_kernel(slope_ref, sink_ref, q_ref, k_ref, v_ref, o_ref, *, tq, win):
    g = pl.program_id(1)
    qi = pl.program_id(2)
    q0 = qi * tq
    ks = pl.multiple_of(jnp.maximum(q0 - A_WINDOW, 0), A_WINDOW)
    lane_half = lax.broadcasted_iota(jnp.int32, (1, LANES), 1) // HEAD

    def both_halves(x):
        xf = x.astype(F32)
        return jnp.where(lane_half == g, xf, pltpu.roll(xf, HEAD, 1)).astype(BF16)

    k = both_halves(k_ref[0, pl.ds(ks, win), :])
    v = both_halves(v_ref[0, pl.ds(ks, win), :])
    kpos = ks + lax.broadcasted_iota(jnp.int32, (win, tq), 0)
    qpos = q0 + lax.broadcasted_iota(jnp.int32, (win, tq), 1)
    dch = (qpos >> CHUNK_SHIFT) - (kpos >> CHUNK_SHIFT)
    allowed = (dch >= 0) & (dch <= A_WINDOW_CHUNKS)
    negdist = -jnp.abs(qpos - kpos).astype(F32)
    st = [dict() for _ in range(A_GROUP)]

    def scores(i):
        qq = q_ref[0, :, (i // 2) * LANES:(i // 2 + 1) * LANES]
        qh = jnp.where(lane_half == i % 2, qq, jnp.zeros_like(qq))
        st[i]["s"] = _dot_t(k, qh)

    def softmax(i):
        hidx = A_GROUP * g + i
        s = jnp.where(allowed, st[i].pop("s") + (slope_ref[hidx] * LOG2E) * negdist, NEG)
        sink = sink_ref[hidx] * LOG2E
        m = jnp.maximum(jnp.max(s, axis=0, keepdims=True), sink)
        p = jnp.exp2(s - m)
        st[i]["den"] = jnp.sum(p, axis=0, keepdims=True) + jnp.exp2(sink - m)
        st[i]["p"] = p.astype(BF16)

    def values(i):
        st[i]["o"] = (_dot_tl(v, st[i].pop("p")) / st[i].pop("den"))[0:HEAD, :]

    _pipelined(A_GROUP, [scores, softmax, values])
    outs = [st[i]["o"] for i in range(A_GROUP)]
    o_ref[0] = jnp.concatenate(outs, axis=0).T.astype(BF16)


def _attn_a(slopes, sinks, qa, ka, va, tq):
    B, S, _ = qa.shape
    win = tq + A_WINDOW
    smem = pl.BlockSpec(memory_space=pltpu.SMEM)
    gw = A_GROUP * HEAD
    return pl.pallas_call(
        functools.partial(_attn_a_kernel, tq=tq, win=win),
        grid=(B, A_HEADS // A_GROUP, S // tq),
        in_specs=[smem, smem,
                  pl.BlockSpec((1, tq, gw), lambda b, g, i: (b, i, g)),
                  pl.BlockSpec((1, S, LANES), lambda b, g, i: (b, 0, 0)),
                  pl.BlockSpec((1, S, LANES), lambda b, g, i: (b, 0, 0))],
        out_specs=pl.BlockSpec((1, tq, gw), lambda b, g, i: (b, i, g)),
        out_shape=jax.ShapeDtypeStruct((B, S, MIX_HALF), BF16),
        compiler_params=pltpu.CompilerParams(dimension_semantics=("arbitrary",) * 3, vmem_limit_bytes=VMEM_LIMIT),
        name="attn_a",
    )(slopes, sinks, qa, ka, va)


def _walk_key_blocks(qi, nchunk, step, group, max_blocks=None):
    first = qi * nchunk
    trips = first // group
    if max_blocks is None:
        def full(i, carry):
            step([(i * group + g, None) for g in range(group)])
            return carry

        lax.fori_loop(0, trips, full, 0)
        step([(first + d, d) for d in range(nchunk)])
    else:
        step([(first + d, d) for d in range(nchunk - 1, -1, -1)])

        def full(i, carry):
            step([(first - 1 - i * group - g, None) for g in range(group)])
            return carry

        lax.fori_loop(0, jnp.minimum(trips, (max_blocks + (group - 1)) // group), full, 0)


def _flash_specs(S, tq, q_width, k_width, v_width):
    return dict(
        in_specs=[pl.BlockSpec((1, tq, q_width), lambda b, p, i: (b, i, p)),
                  pl.BlockSpec((1, S, k_width), lambda b, p, i: (b, 0, p)),
                  pl.BlockSpec((1, S, v_width), lambda b, p, i: (b, 0, p))],
        out_specs=pl.BlockSpec((1, tq, LANES), lambda b, p, i: (b, i, p)),
        compiler_params=pltpu.CompilerParams(dimension_semantics=("arbitrary",) * 3, vmem_limit_bytes=VMEM_LIMIT),
    )


def _chunk_causal(t):
    krow = lax.broadcasted_iota(jnp.int32, (t, t), 0)
    qcol = lax.broadcasted_iota(jnp.int32, (t, t), 1)
    return (krow >> CHUNK_SHIFT) <= (qcol >> CHUNK_SHIFT)


def _softmax_stage(s, off, lanes, e, m_sc):
    m_prev = m_sc[e, :, lanes]
    m_new = jnp.maximum(m_prev, jnp.max(s, axis=0, keepdims=True) - off)
    alpha = jnp.exp2(m_prev - m_new)
    pr = jnp.exp2(s - (m_new + off))
    m_sc[e, :, lanes] = m_new
    return pr.astype(BF16), alpha


def _attn_b_kernel(q_ref, k_ref, v_ref, o_ref, m_sc, acc_sc, *, tq, t):
    qi = pl.program_id(2)
    nchunk = tq // t
    chains = [(e, c) for c in range(nchunk) for e in range(2)]
    lane = lax.broadcasted_iota(jnp.int32, (1, LANES), 1)
    ones_row = (HEAD, 0)
    m_sc[...] = jnp.full(m_sc.shape, NEG, F32)
    acc_sc[...] = jnp.zeros(acc_sc.shape, F32)

    def step(blocks):
        rows = [pl.ds(pl.multiple_of(j * t, t), t) for j, _ in blocks]
        live = [(e, c, g) for g, (_, d) in enumerate(blocks) for (e, c) in chains if d is None or c >= d]
        st = [dict() for _ in live]
        vals = {}

        def values_of(e, g):
            if (e, g) not in vals:
                v = v_ref[0, rows[g], :]
                own = (lane < HEAD) if e == 0 else (lane >= HEAD)
                vals[e, g] = jnp.where(own, v, jnp.where(lane == ones_row[e], 1.0, 0.0).astype(BF16))
            return vals[e, g]

        def scores(i):
            e, c, g = live[i]
            st[i]["s"] = _dot_t(k_ref[0, rows[g], e * LANES:(e + 1) * LANES],
                                q_ref[0, c * t:(c + 1) * t, e * LANES:(e + 1) * LANES])

        def softmax(i):
            e, c, g = live[i]
            s = st[i].pop("s")
            if c == blocks[g][1]:
                s = jnp.where(_chunk_causal(t), s, NEG)
            st[i]["p"], st[i]["alpha"] = _softmax_stage(s, 0.0, slice(c * t, (c + 1) * t), e, m_sc)

        def values(i):
            e, c, g = live[i]
            lanes = slice(c * t, (c + 1) * t)
            acc_sc[e, :, lanes] = (st[i]["alpha"] * acc_sc[e, :, lanes]
                                   + _dot_tl(values_of(e, g), st[i].pop("p")))

        _pipelined(len(live), [scores, softmax, values])

    _walk_key_blocks(qi, nchunk, step, group=nchunk)
    first = lax.broadcasted_iota(jnp.int32, (LANES, 1), 0) < HEAD
    den = [acc_sc[e, ones_row[e]:ones_row[e] + 1, :] for e in range(2)]
    o_t = jnp.where(first, acc_sc[0] / den[0], acc_sc[1] / den[1])
    o_ref[0] = o_t.T.astype(BF16)


def _attn_b(qb, kb, vb, tq, t):
    B, S, _ = qb.shape
    return pl.pallas_call(
        functools.partial(_attn_b_kernel, tq=tq, t=t),
        grid=(B, B_HEADS // 2, S // tq),
        out_shape=jax.ShapeDtypeStruct((B, S, MIX_HALF), BF16),
        scratch_shapes=[pltpu.VMEM((2, 1, tq), F32), pltpu.VMEM((2, LANES, tq), F32)],
        name="attn_b",
        **_flash_specs(S, tq, 2 * LANES, 2 * LANES, LANES),
    )(qb, kb, vb)


def _attn_c_kernel(q_ref, k_ref, v_ref, o_ref, r_sc, acc_sc, *, tq, t):
    qi = pl.program_id(2)
    nchunk = tq // t
    first = qi * nchunk
    chains = [(e, c) for c in range(nchunk) for e in range(2)]
    krow = lax.broadcasted_iota(jnp.int32, (t, t), 0)
    qcol = lax.broadcasted_iota(jnp.int32, (t, t), 1)
    from_here = jnp.where(qcol >= krow, 1.0, 0.0).astype(BF16)
    from_here = jnp.concatenate([from_here, from_here], axis=1)
    lane_lo = _lane_lo((1, LANES))
    r_sc[...] = jnp.zeros(r_sc.shape, F32)
    acc_sc[...] = jnp.zeros(acc_sc.shape, F32)

    def step(backs):
        live = [(e, c, b) for b in range(len(backs)) for (e, c) in chains]
        st = [dict() for _ in live]

        def block(i):
            e, c, b = live[i]
            return first + c - backs[b]

        def rows(i):
            return pl.ds(pl.multiple_of(jnp.maximum(block(i), 0) * t, t), t)

        def scores(i):
            e, c, b = live[i]
            qq = q_ref[0, c * t:(c + 1) * t, :]
            st[i]["z"] = _dot_t(k_ref[0, rows(i), :], jnp.where(lane_lo == (e == 0), qq, jnp.zeros_like(qq)))

        def softplus(i):
            e, c, b = live[i]
            z = st[i]["z"]
            sp = jnp.maximum(z, jnp.log2(1.0 + jnp.exp2(jnp.minimum(z, 126.0))))
            if isinstance(backs[b], int) and backs[b] == 0:
                sp = jnp.where(krow < qcol, sp, 0.0)
            hi = sp.astype(BF16)
            lo = (sp - hi.astype(F32)).astype(BF16)
            st[i]["hilo"] = jnp.concatenate([hi, lo], axis=0)

        def suffix(i):
            st[i]["suf"] = _dot(from_here, st[i].pop("hilo"))

        def weights(i):
            e, c, b = live[i]
            lanes = slice(c * t, (c + 1) * t)
            suf = st[i].pop("suf")
            a = jnp.exp2(st[i].pop("z") - suf)
            if isinstance(backs[b], int) and backs[b] == 0:
                a = jnp.where(krow < qcol, a, 0.0)
            st[i]["a"] = a.astype(BF16)
            r_prev = r_sc[e, :, lanes]
            w = jnp.exp2(-r_prev)
            tot = suf[0:1, :]
            if not (isinstance(backs[b], int) and backs[b] <= c):
                exists = (block(i) >= 0).astype(F32)
                w, tot = w * exists, tot * exists
            st[i]["w"] = w
            r_sc[e, :, lanes] = r_prev + tot

        def values(i):
            e, c, b = live[i]
            lanes = slice(c * t, (c + 1) * t)
            acc_sc[e, :, lanes] = (acc_sc[e, :, lanes]
                                   + st[i].pop("w") * _dot_tl(v_ref[0, rows(i), :], st[i].pop("a")))

        _pipelined(len(live), [scores, softplus, suffix, weights, values])

    def still_live():
        return jnp.max(jnp.exp2(-r_sc[...])) > 0.0

    step([0, 1])

    def further(carry):
        step([carry[0]])
        return carry[0] + 1, still_live()

    lax.while_loop(lambda carry: (carry[0] < first + nchunk) & carry[1], further, (2, still_live()))
    first_head = lax.broadcasted_iota(jnp.int32, (LANES, 1), 0) < HEAD
    o_ref[0] = jnp.where(first_head, acc_sc[0], acc_sc[1]).T.astype(BF16)


def _attn_c(cq, ck, cv, tq, t):
    B, S, _ = cq.shape
    return pl.pallas_call(
        functools.partial(_attn_c_kernel, tq=tq, t=t),
        grid=(B, C_HEADS // 2, S // tq),
        out_shape=jax.ShapeDtypeStruct((B, S, MIX_HALF), BF16),
        scratch_shapes=[pltpu.VMEM((2, 1, tq), F32), pltpu.VMEM((2, LANES, tq), F32)],
        name="attn_c",
        **_flash_specs(S, tq, LANES, LANES, LANES),
    )(cq, ck, cv)


def _attn_d_kernel(slope_ref, reach_ref, q_ref, k_ref, v_ref, lam_ref, subln_ref, o_ref, m_sc, acc_sc,
                   *, tq, t, lambda_init):
    h = pl.program_id(1)
    qi = pl.program_id(2)
    nchunk = tq // t
    chains = [(e, c) for c in range(nchunk) for e in range(2)]
    slope = slope_ref[h] * LOG2E
    lane_lo = _lane_lo((1, LANES))
    key_bias = slope * lax.broadcasted_iota(jnp.int32, (t, LANES), 0).astype(F32)
    m_sc[...] = jnp.full(m_sc.shape, NEG, F32)
    acc_sc[...] = jnp.zeros(acc_sc.shape, F32)

    def step(blocks):
        rows = [pl.ds(pl.multiple_of(j * t, t), t) for j, _ in blocks]
        live = [(e, c, g) for g, (_, d) in enumerate(blocks) for (e, c) in chains if d is None or c >= d]
        st = [dict() for _ in live]
        vals = {}

        def values_of(g):
            if g not in vals:
                vals[g] = jnp.concatenate([v_ref[0, rows[g], :].T, jnp.ones((DEN_ROWS, t), BF16)], axis=0)
            return vals[g]

        def scores(i):
            e, c, g = live[i]
            qq = q_ref[0, c * t:(c + 1) * t, :]
            st[i]["s"] = _dot_t(k_ref[0, rows[g], :], jnp.where(lane_lo == (e == 0), qq, jnp.zeros_like(qq)))

        def softmax(i):
            e, c, g = live[i]
            s = st[i].pop("s")
            j, d = blocks[g]
            if c == d:
                krow = lax.broadcasted_iota(jnp.int32, (t, t), 0)
                qcol = lax.broadcasted_iota(jnp.int32, (t, t), 1)
                s = s + slope * jnp.minimum(krow, 2 * qcol - krow).astype(F32)
                s = jnp.where(_chunk_causal(t), s, NEG)
                off = 0.0
            else:
                s = jnp.concatenate([s[:, b * LANES:(b + 1) * LANES] + key_bias for b in range(t // LANES)], axis=1)
                off = (slope * ((qi * nchunk + c - j) * t).astype(F32) if d is None
                       else slope * float((c - d) * t))
            st[i]["p"], st[i]["alpha"] = _softmax_stage(s, off, slice(c * t, (c + 1) * t), e, m_sc)

        def values(i):
            e, c, g = live[i]
            lanes = slice(c * t, (c + 1) * t)
            acc_sc[e, :, lanes] = (st[i]["alpha"] * acc_sc[e, :, lanes]
                                   + _dot(values_of(g), st[i].pop("p")))

        _pipelined(len(live), [scores, softmax, values])

    _walk_key_blocks(qi, nchunk, step, group=2, max_blocks=reach_ref[h])
    lf = lam_ref[...]
    lam = (jnp.exp(jnp.sum(lf[0:1] * lf[1:2], axis=-1, keepdims=True))
           - jnp.exp(jnp.sum(lf[2:3] * lf[3:4], axis=-1, keepdims=True)) + lambda_init)
    sm = [acc_sc[e, 0:LANES, :] / acc_sc[e, LANES:LANES + 1, :] for e in range(2)]
    o = (sm[0] - lam * sm[1]).T
    o = _rms(o, LANES) * subln_ref[...] * (1.0 - lambda_init)
    o_ref[0] = o.astype(BF16)


def _attn_d(slopes, reach, dq, dk, dv, lam, subln, lambda_init, tq, t):
    B, S, _ = dq.shape
    specs = _flash_specs(S, tq, LANES, LANES, LANES)
    whole = lambda a: pl.BlockSpec(a.shape, lambda b, h, i: (0, 0))
    smem = pl.BlockSpec(memory_space=pltpu.SMEM)
    specs["in_specs"] = [smem, smem] + specs["in_specs"] + [whole(lam), whole(subln)]
    return pl.pallas_call(
        functools.partial(_attn_d_kernel, tq=tq, t=t, lambda_init=lambda_init),
        grid=(B, D_HEADS, S // tq),
        out_shape=jax.ShapeDtypeStruct((B, S, MIX_HALF), BF16),
        scratch_shapes=[pltpu.VMEM((2, 1, tq), F32), pltpu.VMEM((2, LANES + DEN_ROWS, tq), F32)],
        name="attn_d",
        **specs,
    )(slopes, reach, dq, dk, dv, lam, subln)


def _out_mlp_kernel(x_ref, ma_ref, mb_ref, wo_ref, g_ref, wu_ref, wd_ref, o_ref, *, tf):
    half = ma_ref.shape[-1]
    x1 = x_ref[...] + _dot(ma_ref[...], wo_ref[0:half, :]) + _dot(mb_ref[...], wo_ref[half:2 * half, :])
    h = (_rms(x1, x1.shape[-1]) * g_ref[...]).astype(BF16)
    o_ref[...] = x1
    for f in range(wu_ref.shape[-1] // tf):
        u = jnp.maximum(_dot(h, wu_ref[:, f * tf:(f + 1) * tf]), 0.0)
        o_ref[...] += _dot((u * u).astype(BF16), wd_ref[f * tf:(f + 1) * tf, :])


def _out_mlp(x, ma, mb, wo, g, wu, wd, layer, tm, tf):
    T, D = x.shape
    full = lambda a: pl.BlockSpec(a.shape, lambda i: (0,) * a.ndim, pipeline_mode=pl.Buffered(1))
    of_layer = lambda a, l: pl.BlockSpec((None,) + a.shape[1:], lambda i: (l,) + (0,) * (a.ndim - 1),
                                         pipeline_mode=pl.Buffered(1))
    row = lambda c: pl.BlockSpec((tm, c), lambda i: (i, 0))
    return pl.pallas_call(
        functools.partial(_out_mlp_kernel, tf=tf),
        grid=(T // tm,),
        in_specs=[row(D), row(ma.shape[-1]), row(mb.shape[-1]), of_layer(wo, layer // 2), full(g),
                  of_layer(wu, layer), of_layer(wd, layer)],
        out_specs=row(D),
        out_shape=jax.ShapeDtypeStruct((T, D), F32),
        compiler_params=pltpu.CompilerParams(dimension_semantics=("arbitrary",), vmem_limit_bytes=VMEM_LIMIT),
        name="out_mlp",
    )(x, ma, mb, wo, g, wu, wd)


def _alibi_slopes(n):
    return 2.0 ** (-8.0 * jnp.arange(1, n + 1, dtype=F32) / n)


def _rope_placement():
    f = jnp.arange(B_ROPE_HALF)
    p = jnp.zeros((2 * B_ROPE_HALF, 3 * LANES), F32)
    p = p.at[f, HEAD + f].set(1.0).at[f, HEAD + B_ROPE_HALF + f].set(1.0)
    p = p.at[B_ROPE_HALF + f, LANES + HEAD + f].set(-1.0)
    p = p.at[B_ROPE_HALF + f, 2 * LANES + HEAD + B_ROPE_HALF + f].set(1.0)
    return jnp.concatenate([p, p], axis=0).astype(BF16)


def _alibi_reach(slopes, q_gain, k_gain, t, S):
    smax = 1.01 * HEAD * 0.125 * LOG2E * jnp.max(jnp.abs(q_gain)) * jnp.max(jnp.abs(k_gain))
    dist = (160.0 + 2.0 * smax) / (slopes * LOG2E)
    blocks = jnp.floor((dist - 1.0) / t) + 1.0
    return jnp.clip(blocks, 0.0, float(S // t)).astype(jnp.int32)


def _tile(n, want):
    t = min(n, want)
    assert n % t == 0, (n, t)
    return t


def kernel(x, positions, norm_mix_g, norm_ffn_g, mlp_w_up, mlp_w_down, ev_w_in, ev_w_out, a_q_norm, a_k_norm, a_sinks, b_cq_norm, b_ckv_norm, b_w_uq, b_w_ukv, b_q_norm, b_k_norm, od_w_in, od_w_out, d_q_norm, d_k_norm, d_lambda, d_subln):
    B, S, D = x.shape
    T = B * S
    depth = norm_mix_g.shape[0]
    tm_proj = _tile(T, 512)
    tm_mlp = _tile(T, 512)
    t_a = _tile(S, 256)
    tq_b, t_b = _tile(S, 2048), _tile(S, 512)
    tq_c, t_c = _tile(S, 2048), _tile(S, 256)
    tq_d, t_d = _tile(S, 2048), _tile(S, 512)
    tf = 512

    xf = x.reshape(T, D)
    pos = positions.reshape(T)
    row2 = lambda a: a.reshape(1, -1).astype(F32)
    pair = lambda a: jnp.concatenate([a, a]).reshape(1, LANES).astype(F32)
    pad_qk = lambda a: jnp.pad(a.astype(F32), (0, LANES - B_QK)).reshape(1, LANES)
    inv = (ROPE_THETA ** (-jnp.arange(B_ROPE_HALF, dtype=F32) / B_ROPE_HALF)).reshape(B_ROPE_HALF, 1)
    rope_place = _rope_placement()
    slopes_a = _alibi_slopes(A_HEADS)
    slopes_d = _alibi_slopes(D_HEADS)

    w_up, w_down = mlp_w_up.astype(BF16), mlp_w_down.astype(BF16)
    ev_w_out_bf, od_w_out_bf = ev_w_out.astype(BF16), od_w_out.astype(BF16)
    for layer in range(depth):
        j = layer // 2
        g_mix = row2(norm_mix_g[layer])
        if layer % 2 == 0:
            w_in = ev_w_in[j]
            w_in = jnp.concatenate([w_in[:, :E_ROPE], jnp.zeros((D, HEAD), F32), w_in[:, E_ROPE:],
                                    jnp.zeros((D, LANES - B_QK), F32)], axis=1).astype(BF16)
            wuq = jnp.pad(b_w_uq[j].reshape(-1, B_HEADS, B_QK), ((0, 0), (0, 0), (0, LANES - B_QK)))
            wuq = wuq.reshape(-1, B_HEADS * LANES).astype(BF16)
            qa, ka, va, qb, kb, vb = _even_proj(
                xf, pos, g_mix, w_in, pair(a_q_norm[j]), pair(a_k_norm[j]), row2(b_cq_norm[j]),
                row2(b_ckv_norm[j]), wuq, b_w_ukv[j].astype(BF16), pad_qk(b_q_norm[j]), pad_qk(b_k_norm[j]),
                inv, rope_place, tm_proj)
            r3 = lambda a: a.reshape(B, S, a.shape[-1])
            ma = _attn_a(slopes_a, a_sinks[j].astype(F32), r3(qa), r3(ka), r3(va), t_a).reshape(T, -1)
            mb = _attn_b(r3(qb), r3(kb), r3(vb), tq_b, t_b).reshape(T, -1)
            w_out = ev_w_out_bf
        else:
            lambda_init = 0.8 - 0.6 * math.exp(-0.3 * layer)
            cq, ck, cv, dq, dk, dv = _odd_proj(
                xf, g_mix, od_w_in[j].astype(BF16), d_q_norm[j].reshape(1, LANES).astype(F32),
                d_k_norm[j].reshape(1, LANES).astype(F32), tm_proj)
            r3 = lambda a: a.reshape(B, S, a.shape[-1])
            ma = _attn_c(r3(cq), r3(ck), r3(cv), tq_c, t_c).reshape(T, -1)
            reach = _alibi_reach(slopes_d, d_q_norm[j], d_k_norm[j], t_d, S)
            mb = _attn_d(slopes_d, reach, r3(dq), r3(dk), r3(dv), d_lambda[j].astype(F32), row2(d_subln[j]),
                         lambda_init, tq_d, t_d).reshape(T, -1)
            w_out = od_w_out_bf
        xf = _out_mlp(xf, ma, mb, w_out, row2(norm_ffn_g[layer]), w_up, w_down, layer, tm_mlp, tf)
    return xf.reshape(B, S, D)
```

```python
import functools
import math

import jax
import jax.numpy as jnp
from jax import lax
from jax.experimental import pallas as pl
from jax.experimental.pallas import tpu as pltpu

F32 = jnp.float32
BF16 = jnp.bfloat16

EPS = 1e-6
CHUNK = 64
CHUNK_SHIFT = 6
LANES = 128
HEAD = 64
A_HEADS = 8
A_GROUP = 4
A_WINDOW = 128
A_WINDOW_CHUNKS = A_WINDOW // CHUNK
B_HEADS = 8
B_QK = 96
B_ROPE_HALF = 16
ROPE_THETA = 10000.0
C_HEADS = 8
D_HEADS = 4
B_Q_RANK = 256
B_KV_RANK = 128
MIX_HALF = 512
E_AK = A_HEADS * HEAD
E_AV = E_AK + (A_HEADS // A_GROUP) * HEAD
E_CQ = E_AV + (A_HEADS // A_GROUP) * HEAD
E_CKV = E_CQ + B_Q_RANK
E_ROPE = E_CKV + B_KV_RANK
E_END = E_ROPE + LANES
O_CK, O_CV, O_DQ, O_DK, O_DV, O_END = (i * MIX_HALF for i in range(1, 7))
NEG = -1e30
LOG2E = math.log2(math.e)
DEN_ROWS = 16
VMEM_LIMIT = 56 * 1024 * 1024


def _dot(a, b):
    return jnp.dot(a, b, preferred_element_type=F32)


def _dot_t(a, b):
    return lax.dot_general(a, b, (((1,), (1,)), ((), ())), preferred_element_type=F32)


def _dot_tl(a, b):
    return lax.dot_general(a, b, (((0,), (0,)), ((), ())), preferred_element_type=F32)


def _rms(x, denom):
    return x * lax.rsqrt(jnp.sum(x * x, axis=-1, keepdims=True) * (1.0 / denom) + EPS)


def _pair_rms(x, lo):
    xx = x * x
    s_lo = jnp.sum(jnp.where(lo, xx, 0.0), axis=-1, keepdims=True)
    s_hi = jnp.sum(jnp.where(lo, 0.0, xx), axis=-1, keepdims=True)
    r = jnp.where(lo, lax.rsqrt(s_lo * (1.0 / HEAD) + EPS), lax.rsqrt(s_hi * (1.0 / HEAD) + EPS))
    return x * r


def _lane_lo(shape):
    return lax.broadcasted_iota(jnp.int32, shape, len(shape) - 1) < HEAD


def _pipelined(n, stages):
    for step in range(n + len(stages) - 1):
        for si, stage in enumerate(stages):
            c = step - si
            if 0 <= c < n:
                stage(c)


def _even_proj_kernel(x_ref, pos_ref, g_ref, w_in_ref, aqg_ref, akg_ref, cqg_ref, ckvg_ref,
                      wuq_ref, wukv_ref, bqg_ref, bkg_ref, inv_ref, place_ref,
                      qa_ref, ka_ref, va_ref, qb_ref, kb_ref, vb_ref):
    x = x_ref[...]
    h = _rms(x, x.shape[-1]) * g_ref[...]
    proj = _dot(h.astype(BF16), w_in_ref[...])
    lo = _lane_lo((1, LANES))
    for p in range(A_HEADS // 2):
        seg = proj[:, p * LANES:(p + 1) * LANES]
        qa_ref[:, p * LANES:(p + 1) * LANES] = (_pair_rms(seg, lo) * aqg_ref[...] * (0.125 * LOG2E)).astype(BF16)
    ka_ref[...] = (_pair_rms(proj[:, E_AK:E_AV], lo) * akg_ref[...]).astype(BF16)
    va_ref[...] = proj[:, E_AV:E_CQ].astype(BF16)
    cq = _rms(proj[:, E_CQ:E_CKV], B_Q_RANK) * cqg_ref[...]
    ckv = _rms(proj[:, E_CKV:E_ROPE], B_KV_RANK) * ckvg_ref[...]
    krope = proj[:, E_ROPE:E_END]
    qall = _dot(cq.astype(BF16), wuq_ref[...])
    kvall = _dot(ckv.astype(BF16), wukv_ref[...])
    ang = inv_ref[...] * pos_ref[0].astype(F32)
    trig = jnp.concatenate([jnp.cos(ang), jnp.sin(ang)], axis=0)
    trig_hi = trig.astype(BF16)
    trig_lo = (trig - trig_hi.astype(F32)).astype(BF16)
    placed = _dot_tl(jnp.concatenate([trig_hi, trig_lo], axis=0), place_ref[...])
    cosf = placed[:, 0:LANES] + jnp.where(lo, 1.0, 0.0)
    s_first = placed[:, LANES:2 * LANES]
    s_second = placed[:, 2 * LANES:3 * LANES]

    def rope(t):
        return (t * cosf + pltpu.roll(t, LANES - B_ROPE_HALF, 1) * s_first
                + pltpu.roll(t, B_ROPE_HALF, 1) * s_second)

    scale_b = B_QK ** -0.5 * LOG2E
    krope_rot = rope(krope * bkg_ref[...])
    krope_ss = jnp.sum(krope * krope, axis=-1, keepdims=True)
    sq = [dict() for _ in range(B_HEADS)]
    sk = [dict() for _ in range(B_HEADS)]

    def q_sumsq(hh):
        qh = qall[:, hh * LANES:(hh + 1) * LANES]
        sq[hh]["x"] = qh
        sq[hh]["ss"] = jnp.sum(qh * qh, axis=-1, keepdims=True)

    def q_norm(hh):
        sq[hh]["n"] = sq[hh].pop("x") * lax.rsqrt(sq[hh].pop("ss") * (1.0 / B_QK) + EPS) * bqg_ref[...]

    def q_store(hh):
        qb_ref[:, hh * LANES:(hh + 1) * LANES] = (rope(sq[hh].pop("n")) * scale_b).astype(BF16)

    def k_sumsq(hh):
        knope = jnp.where(lo, kvall[:, hh * LANES:(hh + 1) * LANES], 0.0)
        sk[hh]["x"] = knope
        sk[hh]["ss"] = jnp.sum(knope * knope, axis=-1, keepdims=True) + krope_ss

    def k_store(hh):
        kn = (sk[hh].pop("x") * bkg_ref[...] + krope_rot) * lax.rsqrt(sk[hh].pop("ss") * (1.0 / B_QK) + EPS)
        kb_ref[:, hh * LANES:(hh + 1) * LANES] = kn.astype(BF16)

    _pipelined(B_HEADS, [q_sumsq, k_sumsq, q_norm, k_store, q_store])
    for p in range(B_HEADS // 2):
        v_even = kvall[:, 2 * p * LANES:(2 * p + 1) * LANES]
        v_odd = kvall[:, (2 * p + 1) * LANES:(2 * p + 2) * LANES]
        vb_ref[:, p * LANES:(p + 1) * LANES] = jnp.where(lo, pltpu.roll(v_even, HEAD, 1), v_odd).astype(BF16)


def _even_proj(x, pos, g, w_in, aqg, akg, cqg, ckvg, wuq, wukv, bqg, bkg, inv, place, tm):
    T, D = x.shape
    full = lambda a: pl.BlockSpec(a.shape, lambda i: (0,) * a.ndim)
    row = lambda c: pl.BlockSpec((tm, c), lambda i: (i, 0))
    outs = (E_AK, E_AV - E_AK, E_CQ - E_AV, B_HEADS * LANES, B_HEADS * LANES, MIX_HALF)
    pos = pos.reshape(T // tm, 1, tm)
    return pl.pallas_call(
        _even_proj_kernel,
        grid=(T // tm,),
        in_specs=[row(D), pl.BlockSpec((1, 1, tm), lambda i: (i, 0, 0))]
        + [full(a) for a in (g, w_in, aqg, akg, cqg, ckvg, wuq, wukv, bqg, bkg, inv, place)],
        out_specs=[row(c) for c in outs],
        out_shape=[jax.ShapeDtypeStruct((T, c), BF16) for c in outs],
        compiler_params=pltpu.CompilerParams(dimension_semantics=("arbitrary",), vmem_limit_bytes=VMEM_LIMIT),
        name="even_proj",
    )(x, pos, g, w_in, aqg, akg, cqg, ckvg, wuq, wukv, bqg, bkg, inv, place)


def _odd_proj_kernel(x_ref, g_ref, w_in_ref, dqg_ref, dkg_ref,
                     cq_ref, ck_ref, cv_ref, dq_ref, dk_ref, dv_ref):
    x = x_ref[...]
    h = _rms(x, x.shape[-1]) * g_ref[...]
    proj = _dot(h.astype(BF16), w_in_ref[...])
    lo = _lane_lo((1, LANES))
    qscale = 0.125 * LOG2E
    cq_ref[...] = (proj[:, 0:O_CK] * qscale).astype(BF16)
    ck_ref[...] = proj[:, O_CK:O_CV].astype(BF16)
    cv_ref[...] = proj[:, O_CV:O_DQ].astype(BF16)
    for hh in range(D_HEADS):
        sl = slice(hh * LANES, (hh + 1) * LANES)
        dq_ref[:, sl] = (_pair_rms(proj[:, O_DQ + hh * LANES:O_DQ + (hh + 1) * LANES], lo)
                         * dqg_ref[...] * qscale).astype(BF16)
        dk_ref[:, sl] = (_pair_rms(proj[:, O_DK + hh * LANES:O_DK + (hh + 1) * LANES], lo)
                         * dkg_ref[...]).astype(BF16)
    dv_ref[...] = proj[:, O_DV:O_END].astype(BF16)


def _odd_proj(x, g, w_in, dqg, dkg, tm):
    T, D = x.shape
    full = lambda a: pl.BlockSpec(a.shape, lambda i: (0,) * a.ndim)
    row = lambda c: pl.BlockSpec((tm, c), lambda i: (i, 0))
    return pl.pallas_call(
        _odd_proj_kernel,
        grid=(T // tm,),
        in_specs=[row(D)] + [full(a) for a in (g, w_in, dqg, dkg)],
        out_specs=[row(MIX_HALF)] * 6,
        out_shape=[jax.ShapeDtypeStruct((T, MIX_HALF), BF16)] * 6,
        compiler_params=pltpu.CompilerParams(dimension_semantics=("arbitrary",), vmem_limit_bytes=VMEM_LIMIT),
        name="odd_proj",
    )(x, g, w_in, dqg, dkg)


def _attn_a_kernel(slope_ref, sink_ref, q_ref, k_ref, v_ref, o_ref, *, tq, win):
    g = pl.program_id(1)
    qi = pl.program_id(2)
    q0 = qi * tq
    ks = pl.multiple_of(jnp.maximum(q0 - A_WINDOW, 0), A_WINDOW)
    lane_half = lax.broadcasted_iota(jnp.int32, (1, LANES), 1) // HEAD

    def both_halves(x):
        xf = x.astype(F32)
        return jnp.where(lane_half == g, xf, pltpu.roll(xf, HEAD, 1)).astype(BF16)

    k = both_halves(k_ref[0, pl.ds(ks, win), :])
    v = both_halves(v_ref[0, pl.ds(ks, win), :])
    kpos = ks + lax.broadcasted_iota(jnp.int32, (win, tq), 0)
    qpos = q0 + lax.broadcasted_iota(jnp.int32, (win, tq), 1)
    dch = (qpos >> CHUNK_SHIFT) - (kpos >> CHUNK_SHIFT)
    allowed = (dch >= 0) & (dch <= A_WINDOW_CHUNKS)
    negdist = -jnp.abs(qpos - kpos).astype(F32)
    st = [dict() for _ in range(A_GROUP)]

    def scores(i):
        qq = q_ref[0, :, (i // 2) * LANES:(i // 2 + 1) * LANES]
        qh = jnp.where(lane_half == i % 2, qq, jnp.zeros_like(qq))
        st[i]["s"] = _dot_t(k, qh)

    def softmax(i):
        hidx = A_GROUP * g + i
        s = jnp.where(allowed, st[i].pop("s") + (slope_ref[hidx] * LOG2E) * negdist, NEG)
        sink = sink_ref[hidx] * LOG2E
        m = jnp.maximum(jnp.max(s, axis=0, keepdims=True), sink)
        p = jnp.exp2(s - m)
        st[i]["den"] = jnp.sum(p, axis=0, keepdims=True) + jnp.exp2(sink - m)
        st[i]["p"] = p.astype(BF16)

    def values(i):
        st[i]["o"] = (_dot_tl(v, st[i].pop("p")) / st[i].pop("den"))[0:HEAD, :]

    _pipelined(A_GROUP, [scores, softmax, values])
    outs = [st[i]["o"] for i in range(A_GROUP)]
    o_ref[0] = jnp.concatenate(outs, axis=0).T.astype(BF16)


def _attn_a(slopes, sinks, qa, ka, va, tq):
    B, S, _ = qa.shape
    win = tq + A_WINDOW
    smem = pl.BlockSpec(memory_space=pltpu.SMEM)
    gw = A_GROUP * HEAD
    return pl.pallas_call(
        functools.partial(_attn_a_kernel, tq=tq, win=win),
        grid=(B, A_HEADS // A_GROUP, S // tq),
        in_specs=[smem, smem,
                  pl.BlockSpec((1, tq, gw), lambda b, g, i: (b, i, g)),
                  pl.BlockSpec((1, S, LANES), lambda b, g, i: (b, 0, 0)),
                  pl.BlockSpec((1, S, LANES), lambda b, g, i: (b, 0, 0))],
        out_specs=pl.BlockSpec((1, tq, gw), lambda b, g, i: (b, i, g)),
        out_shape=jax.ShapeDtypeStruct((B, S, MIX_HALF), BF16),
        compiler_params=pltpu.CompilerParams(dimension_semantics=("arbitrary",) * 3, vmem_limit_bytes=VMEM_LIMIT),
        name="attn_a",
    )(slopes, sinks, qa, ka, va)


def _walk_key_blocks(qi, nchunk, step, group, max_blocks=None):
    first = qi * nchunk
    trips = first // group
    if max_blocks is None:
        def full(i, carry):
            step([(i * group + g, None) for g in range(group)])
            return carry

        lax.fori_loop(0, trips, full, 0)
        step([(first + d, d) for d in range(nchunk)])
    else:
        step([(first + d, d) for d in range(nchunk - 1, -1, -1)])

        def full(i, carry):
            step([(first - 1 - i * group - g, None) for g in range(group)])
            return carry

        lax.fori_loop(0, jnp.minimum(trips, (max_blocks + (group - 1)) // group), full, 0)


def _flash_specs(S, tq, q_width, k_width, v_width):
    return dict(
        in_specs=[pl.BlockSpec((1, tq, q_width), lambda b, p, i: (b, i, p)),
                  pl.BlockSpec((1, S, k_width), lambda b, p, i: (b, 0, p)),
                  pl.BlockSpec((1, S, v_width), lambda b, p, i: (b, 0, p))],
        out_specs=pl.BlockSpec((1, tq, LANES), lambda b, p, i: (b, i, p)),
        compiler_params=pltpu.CompilerParams(dimension_semantics=("arbitrary",) * 3, vmem_limit_bytes=VMEM_LIMIT),
    )


def _chunk_causal(t):
    krow = lax.broadcasted_iota(jnp.int32, (t, t), 0)
    qcol = lax.broadcasted_iota(jnp.int32, (t, t), 1)
    return (krow >> CHUNK_SHIFT) <= (qcol >> CHUNK_SHIFT)


def _softmax_stage(s, off, lanes, e, m_sc):
    m_prev = m_sc[e, :, lanes]
    m_new = jnp.maximum(m_prev, jnp.max(s, axis=0, keepdims=True) - off)
    alpha = jnp.exp2(m_prev - m_new)
    pr = jnp.exp2(s - (m_new + off))
    m_sc[e, :, lanes] = m_new
    return pr.astype(BF16), alpha


def _attn_b_kernel(q_ref, k_ref, v_ref, o_ref, m_sc, acc_sc, *, tq, t):
    qi = pl.program_id(2)
    nchunk = tq // t
    chains = [(e, c) for c in range(nchunk) for e in range(2)]
    lane = lax.broadcasted_iota(jnp.int32, (1, LANES), 1)
    ones_row = (HEAD, 0)
    m_sc[...] = jnp.full(m_sc.shape, NEG, F32)
    acc_sc[...] = jnp.zeros(acc_sc.shape, F32)

    def step(blocks):
        rows = [pl.ds(pl.multiple_of(j * t, t), t) for j, _ in blocks]
        live = [(e, c, g) for g, (_, d) in enumerate(blocks) for (e, c) in chains if d is None or c >= d]
        st = [dict() for _ in live]
        vals = {}

        def values_of(e, g):
            if (e, g) not in vals:
                v = v_ref[0, rows[g], :]
                own = (lane < HEAD) if e == 0 else (lane >= HEAD)
                vals[e, g] = jnp.where(own, v, jnp.where(lane == ones_row[e], 1.0, 0.0).astype(BF16))
            return vals[e, g]

        def scores(i):
            e, c, g = live[i]
            st[i]["s"] = _dot_t(k_ref[0, rows[g], e * LANES:(e + 1) * LANES],
                                q_ref[0, c * t:(c + 1) * t, e * LANES:(e + 1) * LANES])

        def softmax(i):
            e, c, g = live[i]
            s = st[i].pop("s")
            if c == blocks[g][1]:
                s = jnp.where(_chunk_causal(t), s, NEG)
            st[i]["p"], st[i]["alpha"] = _softmax_stage(s, 0.0, slice(c * t, (c + 1) * t), e, m_sc)

        def values(i):
            e, c, g = live[i]
            lanes = slice(c * t, (c + 1) * t)
            acc_sc[e, :, lanes] = (st[i]["alpha"] * acc_sc[e, :, lanes]
                                   + _dot_tl(values_of(e, g), st[i].pop("p")))

        _pipelined(len(live), [scores, softmax, values])

    _walk_key_blocks(qi, nchunk, step, group=nchunk)
    first = lax.broadcasted_iota(jnp.int32, (LANES, 1), 0) < HEAD
    den = [acc_sc[e, ones_row[e]:ones_row[e] + 1, :] for e in range(2)]
    o_t = jnp.where(first, acc_sc[0] / den[0], acc_sc[1] / den[1])
    o_ref[0] = o_t.T.astype(BF16)


def _attn_b(qb, kb, vb, tq, t):
    B, S, _ = qb.shape
    return pl.pallas_call(
        functools.partial(_attn_b_kernel, tq=tq, t=t),
        grid=(B, B_HEADS // 2, S // tq),
        out_shape=jax.ShapeDtypeStruct((B, S, MIX_HALF), BF16),
        scratch_shapes=[pltpu.VMEM((2, 1, tq), F32), pltpu.VMEM((2, LANES, tq), F32)],
        name="attn_b",
        **_flash_specs(S, tq, 2 * LANES, 2 * LANES, LANES),
    )(qb, kb, vb)


def _attn_c_kernel(q_ref, k_ref, v_ref, o_ref, r_sc, acc_sc, *, tq, t):
    qi = pl.program_id(2)
    nchunk = tq // t
    first = qi * nchunk
    chains = [(e, c) for c in range(nchunk) for e in range(2)]
    krow = lax.broadcasted_iota(jnp.int32, (t, t), 0)
    qcol = lax.broadcasted_iota(jnp.int32, (t, t), 1)
    from_here = jnp.where(qcol >= krow, 1.0, 0.0).astype(BF16)
    from_here = jnp.concatenate([from_here, from_here], axis=1)
    lane_lo = _lane_lo((1, LANES))
    r_sc[...] = jnp.zeros(r_sc.shape, F32)
    acc_sc[...] = jnp.zeros(acc_sc.shape, F32)

    def step(backs):
        live = [(e, c, b) for b in range(len(backs)) for (e, c) in chains]
        st = [dict() for _ in live]

        def block(i):
            e, c, b = live[i]
            return first + c - backs[b]

        def rows(i):
            return pl.ds(pl.multiple_of(jnp.maximum(block(i), 0) * t, t), t)

        def scores(i):
            e, c, b = live[i]
            qq = q_ref[0, c * t:(c + 1) * t, :]
            st[i]["z"] = _dot_t(k_ref[0, rows(i), :], jnp.where(lane_lo == (e == 0), qq, jnp.zeros_like(qq)))

        def softplus(i):
            e, c, b = live[i]
            z = st[i]["z"]
            sp = jnp.maximum(z, jnp.log2(1.0 + jnp.exp2(jnp.minimum(z, 126.0))))
            if isinstance(backs[b], int) and backs[b] == 0:
                sp = jnp.where(krow < qcol, sp, 0.0)
            hi = sp.astype(BF16)
            lo = (sp - hi.astype(F32)).astype(BF16)
            st[i]["hilo"] = jnp.concatenate([hi, lo], axis=0)

        def suffix(i):
            st[i]["suf"] = _dot(from_here, st[i].pop("hilo"))

        def weights(i):
            e, c, b = live[i]
            lanes = slice(c * t, (c + 1) * t)
            suf = st[i].pop("suf")
            a = jnp.exp2(st[i].pop("z") - suf)
            if isinstance(backs[b], int) and backs[b] == 0:
                a = jnp.where(krow < qcol, a, 0.0)
            st[i]["a"] = a.astype(BF16)
            r_prev = r_sc[e, :, lanes]
            w = jnp.exp2(-r_prev)
            tot = suf[0:1, :]
            if not (isinstance(backs[b], int) and backs[b] <= c):
                exists = (block(i) >= 0).astype(F32)
                w, tot = w * exists, tot * exists
            st[i]["w"] = w
            r_sc[e, :, lanes] = r_prev + tot

        def values(i):
            e, c, b = live[i]
            lanes = slice(c * t, (c + 1) * t)
            acc_sc[e, :, lanes] = (acc_sc[e, :, lanes]
                                   + st[i].pop("w") * _dot_tl(v_ref[0, rows(i), :], st[i].pop("a")))

        _pipelined(len(live), [scores, softplus, suffix, weights, values])

    def still_live():
        return jnp.max(jnp.exp2(-r_sc[...])) > 0.0

    step([0, 1])

    def further(carry):
        step([carry[0]])
        return carry[0] + 1, still_live()

    lax.while_loop(lambda carry: (carry[0] < first + nchunk) & carry[1], further, (2, still_live()))
    first_head = lax.broadcasted_iota(jnp.int32, (LANES, 1), 0) < HEAD
    o_ref[0] = jnp.where(first_head, acc_sc[0], acc_sc[1]).T.astype(BF16)


def _attn_c(cq, ck, cv, tq, t):
    B, S, _ = cq.shape
    return pl.pallas_call(
        functools.partial(_attn_c_kernel, tq=tq, t=t),
        grid=(B, C_HEADS // 2, S // tq),
        out_shape=jax.ShapeDtypeStruct((B, S, MIX_HALF), BF16),
        scratch_shapes=[pltpu.VMEM((2, 1, tq), F32), pltpu.VMEM((2, LANES, tq), F32)],
        name="attn_c",
        **_flash_specs(S, tq, LANES, LANES, LANES),
    )(cq, ck, cv)


def _attn_d_kernel(slope_ref, reach_ref, q_ref, k_ref, v_ref, lam_ref, subln_ref, o_ref, m_sc, acc_sc,
                   *, tq, t, lambda_init):
    h = pl.program_id(1)
    qi = pl.program_id(2)
    nchunk = tq // t
    chains = [(e, c) for c in range(nchunk) for e in range(2)]
    slope = slope_ref[h] * LOG2E
    lane_lo = _lane_lo((1, LANES))
    key_bias = slope * lax.broadcasted_iota(jnp.int32, (t, LANES), 0).astype(F32)
    m_sc[...] = jnp.full(m_sc.shape, NEG, F32)
    acc_sc[...] = jnp.zeros(acc_sc.shape, F32)

    def step(blocks):
        rows = [pl.ds(pl.multiple_of(j * t, t), t) for j, _ in blocks]
        live = [(e, c, g) for g, (_, d) in enumerate(blocks) for (e, c) in chains if d is None or c >= d]
        st = [dict() for _ in live]
        vals = {}

        def values_of(g):
            if g not in vals:
                vals[g] = jnp.concatenate([v_ref[0, rows[g], :].T, jnp.ones((DEN_ROWS, t), BF16)], axis=0)
            return vals[g]

        def scores(i):
            e, c, g = live[i]
            qq = q_ref[0, c * t:(c + 1) * t, :]
            st[i]["s"] = _dot_t(k_ref[0, rows[g], :], jnp.where(lane_lo == (e == 0), qq, jnp.zeros_like(qq)))

        def softmax(i):
            e, c, g = live[i]
            s = st[i].pop("s")
            j, d = blocks[g]
            if c == d:
                krow = lax.broadcasted_iota(jnp.int32, (t, t), 0)
                qcol = lax.broadcasted_iota(jnp.int32, (t, t), 1)
                s = s + slope * jnp.minimum(krow, 2 * qcol - krow).astype(F32)
                s = jnp.where(_chunk_causal(t), s, NEG)
                off = 0.0
            else:
                s = jnp.concatenate([s[:, b * LANES:(b + 1) * LANES] + key_bias for b in range(t // LANES)], axis=1)
                off = (slope * ((qi * nchunk + c - j) * t).astype(F32) if d is None
                       else slope * float((c - d) * t))
            st[i]["p"], st[i]["alpha"] = _softmax_stage(s, off, slice(c * t, (c + 1) * t), e, m_sc)

        def values(i):
            e, c, g = live[i]
            lanes = slice(c * t, (c + 1) * t)
            acc_sc[e, :, lanes] = (st[i]["alpha"] * acc_sc[e, :, lanes]
                                   + _dot(values_of(g), st[i].pop("p")))

        _pipelined(len(live), [scores, softmax, values])

    _walk_key_blocks(qi, nchunk, step, group=2, max_blocks=reach_ref[h])
    lf = lam_ref[...]
    lam = (jnp.exp(jnp.sum(lf[0:1] * lf[1:2], axis=-1, keepdims=True))
           - jnp.exp(jnp.sum(lf[2:3] * lf[3:4], axis=-1, keepdims=True)) + lambda_init)
    sm = [acc_sc[e, 0:LANES, :] / acc_sc[e, LANES:LANES + 1, :] for e in range(2)]
    o = (sm[0] - lam * sm[1]).T
    o = _rms(o, LANES) * subln_ref[...] * (1.0 - lambda_init)
    o_ref[0] = o.astype(BF16)


def _attn_d(slopes, reach, dq, dk, dv, lam, subln, lambda_init, tq, t):
    B, S, _ = dq.shape
    specs = _flash_specs(S, tq, LANES, LANES, LANES)
    whole = lambda a: pl.BlockSpec(a.shape, lambda b, h, i: (0, 0))
    smem = pl.BlockSpec(memory_space=pltpu.SMEM)
    specs["in_specs"] = [smem, smem] + specs["in_specs"] + [whole(lam), whole(subln)]
    return pl.pallas_call(
        functools.partial(_attn_d_kernel, tq=tq, t=t, lambda_init=lambda_init),
        grid=(B, D_HEADS, S // tq),
        out_shape=jax.ShapeDtypeStruct((B, S, MIX_HALF), BF16),
        scratch_shapes=[pltpu.VMEM((2, 1, tq), F32), pltpu.VMEM((2, LANES + DEN_ROWS, tq), F32)],
        name="attn_d",
        **specs,
    )(slopes, reach, dq, dk, dv, lam, subln)


def _out_mlp_kernel(x_ref, ma_ref, mb_ref, wo_ref, g_ref, wu_ref, wd_ref, o_ref, *, tf):
    half = ma_ref.shape[-1]
    x1 = x_ref[...] + _dot(ma_ref[...], wo_ref[0:half, :]) + _dot(mb_ref[...], wo_ref[half:2 * half, :])
    h = (_rms(x1, x1.shape[-1]) * g_ref[...]).astype(BF16)
    o_ref[...] = x1
    for f in range(wu_ref.shape[-1] // tf):
        u = jnp.maximum(_dot(h, wu_ref[:, f * tf:(f + 1) * tf]), 0.0)
        o_ref[...] += _dot((u * u).astype(BF16), wd_ref[f * tf:(f + 1) * tf, :])


def _out_mlp(x, ma, mb, wo, g, wu, wd, layer, tm, tf):
    T, D = x.shape
    full = lambda a: pl.BlockSpec(a.shape, lambda i: (0,) * a.ndim, pipeline_mode=pl.Buffered(1))
    of_layer = lambda a, l: pl.BlockSpec((None,) + a.shape[1:], lambda i: (l,) + (0,) * (a.ndim - 1),
                                         pipeline_mode=pl.Buffered(1))
    row = lambda c: pl.BlockSpec((tm, c), lambda i: (i, 0))
    return pl.pallas_call(
        functools.partial(_out_mlp_kernel, tf=tf),
        grid=(T // tm,),
        in_specs=[row(D), row(ma.shape[-1]), row(mb.shape[-1]), of_layer(wo, layer // 2), full(g),
                  of_layer(wu, layer), of_layer(wd, layer)],
        out_specs=row(D),
        out_shape=jax.ShapeDtypeStruct((T, D), F32),
        compiler_params=pltpu.CompilerParams(dimension_semantics=("arbitrary",), vmem_limit_bytes=VMEM_LIMIT),
        name="out_mlp",
    )(x, ma, mb, wo, g, wu, wd)


def _alibi_slopes(n):
    return 2.0 ** (-8.0 * jnp.arange(1, n + 1, dtype=F32) / n)


def _rope_placement():
    f = jnp.arange(B_ROPE_HALF)
    p = jnp.zeros((2 * B_ROPE_HALF, 3 * LANES), F32)
    p = p.at[f, HEAD + f].set(1.0).at[f, HEAD + B_ROPE_HALF + f].set(1.0)
    p = p.at[B_ROPE_HALF + f, LANES + HEAD + f].set(-1.0)
    p = p.at[B_ROPE_HALF + f, 2 * LANES + HEAD + B_ROPE_HALF + f].set(1.0)
    return jnp.concatenate([p, p], axis=0).astype(BF16)


def _alibi_reach(slopes, q_gain, k_gain, t, S):
    smax = 1.01 * HEAD * 0.125 * LOG2E * jnp.max(jnp.abs(q_gain)) * jnp.max(jnp.abs(k_gain))
    dist = (160.0 + 2.0 * smax) / (slopes * LOG2E)
    blocks = jnp.floor((dist - 1.0) / t) + 1.0
    return jnp.clip(blocks, 0.0, float(S // t)).astype(jnp.int32)


def _tile(n, want):
    t = min(n, want)
    assert n % t == 0, (n, t)
    return t


def kernel(x, positions, norm_mix_g, norm_ffn_g, mlp_w_up, mlp_w_down, ev_w_in, ev_w_out, a_q_norm, a_k_norm, a_sinks, b_cq_norm, b_ckv_norm, b_w_uq, b_w_ukv, b_q_norm, b_k_norm, od_w_in, od_w_out, d_q_norm, d_k_norm, d_lambda, d_subln):
    B, S, D = x.shape
    T = B * S
    depth = norm_mix_g.shape[0]
    tm_proj = _tile(T, 1024)
    tm_mlp = _tile(T, 1024)
    t_a = _tile(S, 256)
    tq_b, t_b = _tile(S, 2048), _tile(S, 512)
    tq_c, t_c = _tile(S, 1024), _tile(S, 256)
    tq_d, t_d = _tile(S, 2048), _tile(S, 512)
    tf = 512

    xf = x.reshape(T, D)
    pos = positions.reshape(T)
    row2 = lambda a: a.reshape(1, -1).astype(F32)
    pair = lambda a: jnp.concatenate([a, a]).reshape(1, LANES).astype(F32)
    pad_qk = lambda a: jnp.pad(a.astype(F32), (0, LANES - B_QK)).reshape(1, LANES)
    inv = (ROPE_THETA ** (-jnp.arange(B_ROPE_HALF, dtype=F32) / B_ROPE_HALF)).reshape(B_ROPE_HALF, 1)
    rope_place = _rope_placement()
    slopes_a = _alibi_slopes(A_HEADS)
    slopes_d = _alibi_slopes(D_HEADS)

    w_up, w_down = mlp_w_up.astype(BF16), mlp_w_down.astype(BF16)
    ev_w_out_bf, od_w_out_bf = ev_w_out.astype(BF16), od_w_out.astype(BF16)
    for layer in range(depth):
        j = layer // 2
        g_mix = row2(norm_mix_g[layer])
        if layer % 2 == 0:
            w_in = ev_w_in[j]
            w_in = jnp.concatenate([w_in[:, :E_ROPE], jnp.zeros((D, HEAD), F32), w_in[:, E_ROPE:],
                                    jnp.zeros((D, LANES - B_QK), F32)], axis=1).astype(BF16)
            wuq = jnp.pad(b_w_uq[j].reshape(-1, B_HEADS, B_QK), ((0, 0), (0, 0), (0, LANES - B_QK)))
            wuq = wuq.reshape(-1, B_HEADS * LANES).astype(BF16)
            qa, ka, va, qb, kb, vb = _even_proj(
                xf, pos, g_mix, w_in, pair(a_q_norm[j]), pair(a_k_norm[j]), row2(b_cq_norm[j]),
                row2(b_ckv_norm[j]), wuq, b_w_ukv[j].astype(BF16), pad_qk(b_q_norm[j]), pad_qk(b_k_norm[j]),
                inv, rope_place, tm_proj)
            r3 = lambda a: a.reshape(B, S, a.shape[-1])
            ma = _attn_a(slopes_a, a_sinks[j].astype(F32), r3(qa), r3(ka), r3(va), t_a).reshape(T, -1)
            mb = _attn_b(r3(qb), r3(kb), r3(vb), tq_b, t_b).reshape(T, -1)
            w_out = ev_w_out_bf
        else:
            lambda_init = 0.8 - 0.6 * math.exp(-0.3 * layer)
            cq, ck, cv, dq, dk, dv = _odd_proj(
                xf, g_mix, od_w_in[j].astype(BF16), d_q_norm[j].reshape(1, LANES).astype(F32),
                d_k_norm[j].reshape(1, LANES).astype(F32), tm_proj)
            r3 = lambda a: a.reshape(B, S, a.shape[-1])
            ma = _attn_c(r3(cq), r3(ck), r3(cv), tq_c, t_c).reshape(T, -1)
            reach = _alibi_reach(slopes_d, d_q_norm[j], d_k_norm[j], t_d, S)
            mb = _attn_d(slopes_d, reach, r3(dq), r3(dk), r3(dv), d_lambda[j].astype(F32), row2(d_subln[j]),
                         lambda_init, tq_d, t_d).reshape(T, -1)
            w_out = od_w_out_bf
        xf = _out_mlp(xf, ma, mb, w_out, row2(norm_ffn_g[layer]), w_up, w_down, layer, tm_mlp, tf)
    return xf.reshape(B, S, D)
```

```python
import functools
import math

import jax
import jax.numpy as jnp
from jax import lax
from jax.experimental import pallas as pl
from jax.experimental.pallas import tpu as pltpu

F32 = jnp.float32
BF16 = jnp.bfloat16

EPS = 1e-6
CHUNK = 64
CHUNK_SHIFT = 6
LANES = 128
HEAD = 64
A_HEADS = 8
A_GROUP = 4
A_WINDOW = 128
A_WINDOW_CHUNKS = A_WINDOW // CHUNK
B_HEADS = 8
B_QK = 96
B_ROPE_HALF = 16
ROPE_THETA = 10000.0
C_HEADS = 8
D_HEADS = 4
B_Q_RANK = 256
B_KV_RANK = 128
MIX_HALF = 512
E_AK = A_HEADS * HEAD
E_AV = E_AK + (A_HEADS // A_GROUP) * HEAD
E_CQ = E_AV + (A_HEADS // A_GROUP) * HEAD
E_CKV = E_CQ + B_Q_RANK
E_ROPE = E_CKV + B_KV_RANK
E_END = E_ROPE + LANES
O_CK, O_CV, O_DQ, O_DK, O_DV, O_END = (i * MIX_HALF for i in range(1, 7))
NEG = -1e30
LOG2E = math.log2(math.e)
DEN_ROWS = 16
VMEM_LIMIT = 56 * 1024 * 1024


def _dot(a, b):
    return jnp.dot(a, b, preferred_element_type=F32)


def _dot_t(a, b):
    return lax.dot_general(a, b, (((1,), (1,)), ((), ())), preferred_element_type=F32)


def _dot_tl(a, b):
    return lax.dot_general(a, b, (((0,), (0,)), ((), ())), preferred_element_type=F32)


def _rms(x, denom):
    return x * lax.rsqrt(jnp.sum(x * x, axis=-1, keepdims=True) * (1.0 / denom) + EPS)


def _pair_rms(x, lo):
    xx = x * x
    s_lo = jnp.sum(jnp.where(lo, xx, 0.0), axis=-1, keepdims=True)
    s_hi = jnp.sum(jnp.where(lo, 0.0, xx), axis=-1, keepdims=True)
    r = jnp.where(lo, lax.rsqrt(s_lo * (1.0 / HEAD) + EPS), lax.rsqrt(s_hi * (1.0 / HEAD) + EPS))
    return x * r


def _lane_lo(shape):
    return lax.broadcasted_iota(jnp.int32, shape, len(shape) - 1) < HEAD


def _pipelined(n, stages):
    for step in range(n + len(stages) - 1):
        for si, stage in enumerate(stages):
            c = step - si
            if 0 <= c < n:
                stage(c)


def _even_proj_kernel(x_ref, pos_ref, g_ref, w_in_ref, aqg_ref, akg_ref, cqg_ref, ckvg_ref,
                      wuq_ref, wukv_ref, bqg_ref, bkg_ref, inv_ref, place_ref,
                      qa_ref, ka_ref, va_ref, qb_ref, kb_ref, vb_ref):
    x = x_ref[...]
    h = _rms(x, x.shape[-1]) * g_ref[...]
    proj = _dot(h.astype(BF16), w_in_ref[...])
    lo = _lane_lo((1, LANES))
    for p in range(A_HEADS // 2):
        seg = proj[:, p * LANES:(p + 1) * LANES]
        qa_ref[:, p * LANES:(p + 1) * LANES] = (_pair_rms(seg, lo) * aqg_ref[...] * (0.125 * LOG2E)).astype(BF16)
    ka_ref[...] = (_pair_rms(proj[:, E_AK:E_AV], lo) * akg_ref[...]).astype(BF16)
    va_ref[...] = proj[:, E_AV:E_CQ].astype(BF16)
    cq = _rms(proj[:, E_CQ:E_CKV], B_Q_RANK) * cqg_ref[...]
    ckv = _rms(proj[:, E_CKV:E_ROPE], B_KV_RANK) * ckvg_ref[...]
    krope = proj[:, E_ROPE:E_END]
    qall = _dot(cq.astype(BF16), wuq_ref[...])
    kvall = _dot(ckv.astype(BF16), wukv_ref[...])
    ang = inv_ref[...] * pos_ref[0].astype(F32)
    trig = jnp.concatenate([jnp.cos(ang), jnp.sin(ang)], axis=0)
    trig_hi = trig.astype(BF16)
    trig_lo = (trig - trig_hi.astype(F32)).astype(BF16)
    placed = _dot_tl(jnp.concatenate([trig_hi, trig_lo], axis=0), place_ref[...])
    cosf = placed[:, 0:LANES] + jnp.where(lo, 1.0, 0.0)
    s_first = placed[:, LANES:2 * LANES]
    s_second = placed[:, 2 * LANES:3 * LANES]

    def rope(t):
        return (t * cosf + pltpu.roll(t, LANES - B_ROPE_HALF, 1) * s_first
                + pltpu.roll(t, B_ROPE_HALF, 1) * s_second)

    scale_b = B_QK ** -0.5 * LOG2E
    krope_rot = rope(krope * bkg_ref[...])
    krope_ss = jnp.sum(krope * krope, axis=-1, keepdims=True)
    sq = [dict() for _ in range(B_HEADS)]
    sk = [dict() for _ in range(B_HEADS)]

    def q_sumsq(hh):
        qh = qall[:, hh * LANES:(hh + 1) * LANES]
        sq[hh]["x"] = qh
        sq[hh]["ss"] = jnp.sum(qh * qh, axis=-1, keepdims=True)

    def q_norm(hh):
        sq[hh]["n"] = sq[hh].pop("x") * lax.rsqrt(sq[hh].pop("ss") * (1.0 / B_QK) + EPS) * bqg_ref[...]

    def q_store(hh):
        qb_ref[:, hh * LANES:(hh + 1) * LANES] = (rope(sq[hh].pop("n")) * scale_b).astype(BF16)

    def k_sumsq(hh):
        knope = jnp.where(lo, kvall[:, hh * LANES:(hh + 1) * LANES], 0.0)
        sk[hh]["x"] = knope
        sk[hh]["ss"] = jnp.sum(knope * knope, axis=-1, keepdims=True) + krope_ss

    def k_store(hh):
        kn = (sk[hh].pop("x") * bkg_ref[...] + krope_rot) * lax.rsqrt(sk[hh].pop("ss") * (1.0 / B_QK) + EPS)
        kb_ref[:, hh * LANES:(hh + 1) * LANES] = kn.astype(BF16)

    _pipelined(B_HEADS, [q_sumsq, k_sumsq, q_norm, k_store, q_store])
    for p in range(B_HEADS // 2):
        v_even = kvall[:, 2 * p * LANES:(2 * p + 1) * LANES]
        v_odd = kvall[:, (2 * p + 1) * LANES:(2 * p + 2) * LANES]
        vb_ref[:, p * LANES:(p + 1) * LANES] = jnp.where(lo, pltpu.roll(v_even, HEAD, 1), v_odd).astype(BF16)


def _even_proj(x, pos, g, w_in, aqg, akg, cqg, ckvg, wuq, wukv, bqg, bkg, inv, place, tm):
    T, D = x.shape
    full = lambda a: pl.BlockSpec(a.shape, lambda i: (0,) * a.ndim)
    row = lambda c: pl.BlockSpec((tm, c), lambda i: (i, 0))
    outs = (E_AK, E_AV - E_AK, E_CQ - E_AV, B_HEADS * LANES, B_HEADS * LANES, MIX_HALF)
    pos = pos.reshape(T // tm, 1, tm)
    return pl.pallas_call(
        _even_proj_kernel,
        grid=(T // tm,),
        in_specs=[row(D), pl.BlockSpec((1, 1, tm), lambda i: (i, 0, 0))]
        + [full(a) for a in (g, w_in, aqg, akg, cqg, ckvg, wuq, wukv, bqg, bkg, inv, place)],
        out_specs=[row(c) for c in outs],
        out_shape=[jax.ShapeDtypeStruct((T, c), BF16) for c in outs],
        compiler_params=pltpu.CompilerParams(dimension_semantics=("arbitrary",), vmem_limit_bytes=VMEM_LIMIT),
        name="even_proj",
    )(x, pos, g, w_in, aqg, akg, cqg, ckvg, wuq, wukv, bqg, bkg, inv, place)


def _odd_proj_kernel(x_ref, g_ref, w_in_ref, dqg_ref, dkg_ref,
                     cq_ref, ck_ref, cv_ref, dq_ref, dk_ref, dv_ref):
    x = x_ref[...]
    h = _rms(x, x.shape[-1]) * g_ref[...]
    proj = _dot(h.astype(BF16), w_in_ref[...])
    lo = _lane_lo((1, LANES))
    qscale = 0.125 * LOG2E
    cq_ref[...] = (proj[:, 0:O_CK] * qscale).astype(BF16)
    ck_ref[...] = proj[:, O_CK:O_CV].astype(BF16)
    cv_ref[...] = proj[:, O_CV:O_DQ].astype(BF16)
    for hh in range(D_HEADS):
        sl = slice(hh * LANES, (hh + 1) * LANES)
        dq_ref[:, sl] = (_pair_rms(proj[:, O_DQ + hh * LANES:O_DQ + (hh + 1) * LANES], lo)
                         * dqg_ref[...] * qscale).astype(BF16)
        dk_ref[:, sl] = (_pair_rms(proj[:, O_DK + hh * LANES:O_DK + (hh + 1) * LANES], lo)
                         * dkg_ref[...]).astype(BF16)
    dv_ref[...] = proj[:, O_DV:O_END].astype(BF16)


def _odd_proj(x, g, w_in, dqg, dkg, tm):
    T, D = x.shape
    full = lambda a: pl.BlockSpec(a.shape, lambda i: (0,) * a.ndim)
    row = lambda c: pl.BlockSpec((tm, c), lambda i: (i, 0))
    return pl.pallas_call(
        _odd_proj_kernel,
        grid=(T // tm,),
        in_specs=[row(D)] + [full(a) for a in (g, w_in, dqg, dkg)],
        out_specs=[row(MIX_HALF)] * 6,
        out_shape=[jax.ShapeDtypeStruct((T, MIX_HALF), BF16)] * 6,
        compiler_params=pltpu.CompilerParams(dimension_semantics=("arbitrary",), vmem_limit_bytes=VMEM_LIMIT),
        name="odd_proj",
    )(x, g, w_in, dqg, dkg)


def _attn_a_kernel(slope_ref, sink_ref, q_ref, k_ref, v_ref, o_ref, *, tq, win):
    qi = pl.program_id(1)
    q0 = qi * tq
    ks = pl.multiple_of(jnp.maximum(q0 - A_WINDOW, 0), A_WINDOW)
    lane_half = lax.broadcasted_iota(jnp.int32, (1, LANES), 1) // HEAD

    def both_halves(x, g):
        xf = x.astype(F32)
        return jnp.where(lane_half == g, xf, pltpu.roll(xf, HEAD, 1)).astype(BF16)

    groups = range(A_HEADS // A_GROUP)
    k = [both_halves(k_ref[0, pl.ds(ks, win), :], g) for g in groups]
    v = [both_halves(v_ref[0, pl.ds(ks, win), :], g) for g in groups]
    kpos = ks + lax.broadcasted_iota(jnp.int32, (win, tq), 0)
    qpos = q0 + lax.broadcasted_iota(jnp.int32, (win, tq), 1)
    dch = (qpos >> CHUNK_SHIFT) - (kpos >> CHUNK_SHIFT)
    allowed = (dch >= 0) & (dch <= A_WINDOW_CHUNKS)
    negdist = -jnp.abs(qpos - kpos).astype(F32)
    st = [dict() for _ in range(A_HEADS)]

    def scores(i):
        qq = q_ref[0, :, (i // 2) * LANES:(i // 2 + 1) * LANES]
        qh = jnp.where(lane_half == i % 2, qq, jnp.zeros_like(qq))
        st[i]["s"] = _dot_t(k[i // A_GROUP], qh)

    def softmax(i):
        s = jnp.where(allowed, st[i].pop("s") + (slope_ref[i] * LOG2E) * negdist, NEG)
        sink = sink_ref[i] * LOG2E
        m = jnp.maximum(jnp.max(s, axis=0, keepdims=True), sink)
        p = jnp.exp2(s - m)
        st[i]["den"] = jnp.sum(p, axis=0, keepdims=True) + jnp.exp2(sink - m)
        st[i]["p"] = p.astype(BF16)

    def values(i):
        o = _dot_tl(v[i // A_GROUP], st[i].pop("p")) / st[i].pop("den")
        st[i]["o"] = o[0:HEAD, :]

    _pipelined(A_HEADS, [scores, softmax, values])
    outs = [st[i]["o"] for i in range(A_HEADS)]
    o_ref[0] = jnp.concatenate(outs, axis=0).T.astype(BF16)


def _attn_a(slopes, sinks, qa, ka, va, tq):
    B, S, C = qa.shape
    win = tq + A_WINDOW
    smem = pl.BlockSpec(memory_space=pltpu.SMEM)
    return pl.pallas_call(
        functools.partial(_attn_a_kernel, tq=tq, win=win),
        grid=(B, S // tq),
        in_specs=[smem, smem,
                  pl.BlockSpec((1, tq, C), lambda b, i: (b, i, 0)),
                  pl.BlockSpec((1, S, LANES), lambda b, i: (b, 0, 0)),
                  pl.BlockSpec((1, S, LANES), lambda b, i: (b, 0, 0))],
        out_specs=pl.BlockSpec((1, tq, C), lambda b, i: (b, i, 0)),
        out_shape=jax.ShapeDtypeStruct((B, S, C), BF16),
        compiler_params=pltpu.CompilerParams(dimension_semantics=("arbitrary",) * 2, vmem_limit_bytes=VMEM_LIMIT),
        name="attn_a",
    )(slopes, sinks, qa, ka, va)


def _walk_key_blocks(qi, nchunk, step, group, max_blocks=None):
    first = qi * nchunk
    trips = first // group
    if max_blocks is None:
        def full(i, carry):
            step([(i * group + g, None) for g in range(group)])
            return carry

        lax.fori_loop(0, trips, full, 0)
        step([(first + d, d) for d in range(nchunk)])
    else:
        step([(first + d, d) for d in range(nchunk - 1, -1, -1)])

        def full(i, carry):
            step([(first - 1 - i * group - g, None) for g in range(group)])
            return carry

        lax.fori_loop(0, jnp.minimum(trips, (max_blocks + (group - 1)) // group), full, 0)


def _flash_specs(S, tq, q_width, k_width, v_width):
    return dict(
        in_specs=[pl.BlockSpec((1, tq, q_width), lambda b, p, i: (b, i, p)),
                  pl.BlockSpec((1, S, k_width), lambda b, p, i: (b, 0, p)),
                  pl.BlockSpec((1, S, v_width), lambda b, p, i: (b, 0, p))],
        out_specs=pl.BlockSpec((1, tq, LANES), lambda b, p, i: (b, i, p)),
        compiler_params=pltpu.CompilerParams(dimension_semantics=("arbitrary",) * 3, vmem_limit_bytes=VMEM_LIMIT),
    )


def _chunk_causal(t):
    krow = lax.broadcasted_iota(jnp.int32, (t, t), 0)
    qcol = lax.broadcasted_iota(jnp.int32, (t, t), 1)
    return (krow >> CHUNK_SHIFT) <= (qcol >> CHUNK_SHIFT)


def _softmax_stage(s, off, lanes, e, m_sc):
    m_prev = m_sc[e, :, lanes]
    m_new = jnp.maximum(m_prev, jnp.max(s, axis=0, keepdims=True) - off)
    alpha = jnp.exp2(m_prev - m_new)
    pr = jnp.exp2(s - (m_new + off))
    m_sc[e, :, lanes] = m_new
    return pr.astype(BF16), alpha


def _attn_b_kernel(q_ref, k_ref, v_ref, o_ref, m_sc, acc_sc, *, tq, t):
    qi = pl.program_id(2)
    nchunk = tq // t
    chains = [(e, c) for c in range(nchunk) for e in range(2)]
    lane = lax.broadcasted_iota(jnp.int32, (1, LANES), 1)
    ones_row = (HEAD, 0)
    m_sc[...] = jnp.full(m_sc.shape, NEG, F32)
    acc_sc[...] = jnp.zeros(acc_sc.shape, F32)

    def step(blocks):
        rows = [pl.ds(pl.multiple_of(j * t, t), t) for j, _ in blocks]
        live = [(e, c, g) for g, (_, d) in enumerate(blocks) for (e, c) in chains if d is None or c >= d]
        st = [dict() for _ in live]
        vals = {}

        def values_of(e, g):
            if (e, g) not in vals:
                v = v_ref[0, rows[g], :]
                own = (lane < HEAD) if e == 0 else (lane >= HEAD)
                vals[e, g] = jnp.where(own, v, jnp.where(lane == ones_row[e], 1.0, 0.0).astype(BF16))
            return vals[e, g]

        def scores(i):
            e, c, g = live[i]
            st[i]["s"] = _dot_t(k_ref[0, rows[g], e * LANES:(e + 1) * LANES],
                                q_ref[0, c * t:(c + 1) * t, e * LANES:(e + 1) * LANES])

        def softmax(i):
            e, c, g = live[i]
            s = st[i].pop("s")
            if c == blocks[g][1]:
                s = jnp.where(_chunk_causal(t), s, NEG)
            st[i]["p"], st[i]["alpha"] = _softmax_stage(s, 0.0, slice(c * t, (c + 1) * t), e, m_sc)

        def values(i):
            e, c, g = live[i]
            lanes = slice(c * t, (c + 1) * t)
            acc_sc[e, :, lanes] = (st[i]["alpha"] * acc_sc[e, :, lanes]
                                   + _dot_tl(values_of(e, g), st[i].pop("p")))

        _pipelined(len(live), [scores, softmax, values])

    _walk_key_blocks(qi, nchunk, step, group=nchunk)
    first = lax.broadcasted_iota(jnp.int32, (LANES, 1), 0) < HEAD
    den = [acc_sc[e, ones_row[e]:ones_row[e] + 1, :] for e in range(2)]
    o_t = jnp.where(first, acc_sc[0] / den[0], acc_sc[1] / den[1])
    o_ref[0] = o_t.T.astype(BF16)


def _attn_b(qb, kb, vb, tq, t):
    B, S, _ = qb.shape
    return pl.pallas_call(
        functools.partial(_attn_b_kernel, tq=tq, t=t),
        grid=(B, B_HEADS // 2, S // tq),
        out_shape=jax.ShapeDtypeStruct((B, S, MIX_HALF), BF16),
        scratch_shapes=[pltpu.VMEM((2, 1, tq), F32), pltpu.VMEM((2, LANES, tq), F32)],
        name="attn_b",
        **_flash_specs(S, tq, 2 * LANES, 2 * LANES, LANES),
    )(qb, kb, vb)


def _attn_c_kernel(q_ref, k_ref, v_ref, o_ref, r_sc, acc_sc, *, tq, t):
    qi = pl.program_id(2)
    nchunk = tq // t
    first = qi * nchunk
    chains = [(e, c) for c in range(nchunk) for e in range(2)]
    krow = lax.broadcasted_iota(jnp.int32, (t, t), 0)
    qcol = lax.broadcasted_iota(jnp.int32, (t, t), 1)
    from_here = jnp.where(qcol >= krow, 1.0, 0.0).astype(BF16)
    from_here = jnp.concatenate([from_here, from_here], axis=1)
    lane_lo = _lane_lo((1, LANES))
    r_sc[...] = jnp.zeros(r_sc.shape, F32)
    acc_sc[...] = jnp.zeros(acc_sc.shape, F32)

    def step(backs):
        live = [(e, c, b) for b in range(len(backs)) for (e, c) in chains]
        st = [dict() for _ in live]

        def block(i):
            e, c, b = live[i]
            return first + c - backs[b]

        def rows(i):
            return pl.ds(pl.multiple_of(jnp.maximum(block(i), 0) * t, t), t)

        def scores(i):
            e, c, b = live[i]
            qq = q_ref[0, c * t:(c + 1) * t, :]
            st[i]["z"] = _dot_t(k_ref[0, rows(i), :], jnp.where(lane_lo == (e == 0), qq, jnp.zeros_like(qq)))

        def softplus(i):
            e, c, b = live[i]
            z = st[i]["z"]
            sp = jnp.maximum(z, jnp.log2(1.0 + jnp.exp2(jnp.minimum(z, 126.0))))
            if isinstance(backs[b], int) and backs[b] == 0:
                sp = jnp.where(krow < qcol, sp, 0.0)
            hi = sp.astype(BF16)
            lo = (sp - hi.astype(F32)).astype(BF16)
            st[i]["hilo"] = jnp.concatenate([hi, lo], axis=0)

        def suffix(i):
            st[i]["suf"] = _dot(from_here, st[i].pop("hilo"))

        def weights(i):
            e, c, b = live[i]
            lanes = slice(c * t, (c + 1) * t)
            suf = st[i].pop("suf")
            a = jnp.exp2(st[i].pop("z") - suf)
            if isinstance(backs[b], int) and backs[b] == 0:
                a = jnp.where(krow < qcol, a, 0.0)
            st[i]["a"] = a.astype(BF16)
            r_prev = r_sc[e, :, lanes]
            w = jnp.exp2(-r_prev)
            tot = suf[0:1, :]
            if not (isinstance(backs[b], int) and backs[b] <= c):
                exists = (block(i) >= 0).astype(F32)
                w, tot = w * exists, tot * exists
            st[i]["w"] = w
            r_sc[e, :, lanes] = r_prev + tot

        def values(i):
            e, c, b = live[i]
            lanes = slice(c * t, (c + 1) * t)
            acc_sc[e, :, lanes] = (acc_sc[e, :, lanes]
                                   + st[i].pop("w") * _dot_tl(v_ref[0, rows(i), :], st[i].pop("a")))

        _pipelined(len(live), [scores, softplus, suffix, weights, values])

    def still_live():
        return jnp.max(jnp.exp2(-r_sc[...])) > 0.0

    step([0, 1])

    def further(carry):
        step([carry[0]])
        return carry[0] + 1, still_live()

    lax.while_loop(lambda carry: (carry[0] < first + nchunk) & carry[1], further, (2, still_live()))
    first_head = lax.broadcasted_iota(jnp.int32, (LANES, 1), 0) < HEAD
    o_ref[0] = jnp.where(first_head, acc_sc[0], acc_sc[1]).T.astype(BF16)


def _attn_c(cq, ck, cv, tq, t):
    B, S, _ = cq.shape
    return pl.pallas_call(
        functools.partial(_attn_c_kernel, tq=tq, t=t),
        grid=(B, C_HEADS // 2, S // tq),
        out_shape=jax.ShapeDtypeStruct((B, S, MIX_HALF), BF16),
        scratch_shapes=[pltpu.VMEM((2, 1, tq), F32), pltpu.VMEM((2, LANES, tq), F32)],
        name="attn_c",
        **_flash_specs(S, tq, LANES, LANES, LANES),
    )(cq, ck, cv)


def _attn_d_kernel(slope_ref, reach_ref, q_ref, k_ref, v_ref, lam_ref, subln_ref, o_ref, m_sc, acc_sc,
                   *, tq, t, lambda_init):
    h = pl.program_id(1)
    qi = pl.program_id(2)
    nchunk = tq // t
    chains = [(e, c) for c in range(nchunk) for e in range(2)]
    slope = slope_ref[h] * LOG2E
    lane_lo = _lane_lo((1, LANES))
    key_bias = slope * lax.broadcasted_iota(jnp.int32, (t, LANES), 0).astype(F32)
    m_sc[...] = jnp.full(m_sc.shape, NEG, F32)
    acc_sc[...] = jnp.zeros(acc_sc.shape, F32)

    def step(blocks):
        rows = [pl.ds(pl.multiple_of(j * t, t), t) for j, _ in blocks]
        live = [(e, c, g) for g, (_, d) in enumerate(blocks) for (e, c) in chains if d is None or c >= d]
        st = [dict() for _ in live]
        vals = {}

        def values_of(g):
            if g not in vals:
                vals[g] = jnp.concatenate([v_ref[0, rows[g], :].T, jnp.ones((DEN_ROWS, t), BF16)], axis=0)
            return vals[g]

        def scores(i):
            e, c, g = live[i]
            qq = q_ref[0, c * t:(c + 1) * t, :]
            st[i]["s"] = _dot_t(k_ref[0, rows[g], :], jnp.where(lane_lo == (e == 0), qq, jnp.zeros_like(qq)))

        def softmax(i):
            e, c, g = live[i]
            s = st[i].pop("s")
            j, d = blocks[g]
            if c == d:
                krow = lax.broadcasted_iota(jnp.int32, (t, t), 0)
                qcol = lax.broadcasted_iota(jnp.int32, (t, t), 1)
                s = s + slope * jnp.minimum(krow, 2 * qcol - krow).astype(F32)
                s = jnp.where(_chunk_causal(t), s, NEG)
                off = 0.0
            else:
                s = jnp.concatenate([s[:, b * LANES:(b + 1) * LANES] + key_bias for b in range(t // LANES)], axis=1)
                off = (slope * ((qi * nchunk + c - j) * t).astype(F32) if d is None
                       else slope * float((c - d) * t))
            st[i]["p"], st[i]["alpha"] = _softmax_stage(s, off, slice(c * t, (c + 1) * t), e, m_sc)

        def values(i):
            e, c, g = live[i]
            lanes = slice(c * t, (c + 1) * t)
            acc_sc[e, :, lanes] = (st[i]["alpha"] * acc_sc[e, :, lanes]
                                   + _dot(values_of(g), st[i].pop("p")))

        _pipelined(len(live), [scores, softmax, values])

    _walk_key_blocks(qi, nchunk, step, group=2, max_blocks=reach_ref[h])
    lf = lam_ref[...]
    lam = (jnp.exp(jnp.sum(lf[0:1] * lf[1:2], axis=-1, keepdims=True))
           - jnp.exp(jnp.sum(lf[2:3] * lf[3:4], axis=-1, keepdims=True)) + lambda_init)
    sm = [acc_sc[e, 0:LANES, :] / acc_sc[e, LANES:LANES + 1, :] for e in range(2)]
    o = (sm[0] - lam * sm[1]).T
    o = _rms(o, LANES) * subln_ref[...] * (1.0 - lambda_init)
    o_ref[0] = o.astype(BF16)


def _attn_d(slopes, reach, dq, dk, dv, lam, subln, lambda_init, tq, t):
    B, S, _ = dq.shape
    specs = _flash_specs(S, tq, LANES, LANES, LANES)
    whole = lambda a: pl.BlockSpec(a.shape, lambda b, h, i: (0, 0))
    smem = pl.BlockSpec(memory_space=pltpu.SMEM)
    specs["in_specs"] = [smem, smem] + specs["in_specs"] + [whole(lam), whole(subln)]
    return pl.pallas_call(
        functools.partial(_attn_d_kernel, tq=tq, t=t, lambda_init=lambda_init),
        grid=(B, D_HEADS, S // tq),
        out_shape=jax.ShapeDtypeStruct((B, S, MIX_HALF), BF16),
        scratch_shapes=[pltpu.VMEM((2, 1, tq), F32), pltpu.VMEM((2, LANES + DEN_ROWS, tq), F32)],
        name="attn_d",
        **specs,
    )(slopes, reach, dq, dk, dv, lam, subln)


def _out_mlp_kernel(x_ref, ma_ref, mb_ref, wo_ref, g_ref, wu_ref, wd_ref, o_ref, *, tf):
    half = ma_ref.shape[-1]
    x1 = x_ref[...] + _dot(ma_ref[...], wo_ref[0:half, :]) + _dot(mb_ref[...], wo_ref[half:2 * half, :])
    h = (_rms(x1, x1.shape[-1]) * g_ref[...]).astype(BF16)
    o_ref[...] = x1
    for f in range(wu_ref.shape[-1] // tf):
        u = jnp.maximum(_dot(h, wu_ref[:, f * tf:(f + 1) * tf]), 0.0)
        o_ref[...] += _dot((u * u).astype(BF16), wd_ref[f * tf:(f + 1) * tf, :])


def _out_mlp(x, ma, mb, wo, g, wu, wd, layer, tm, tf):
    T, D = x.shape
    full = lambda a: pl.BlockSpec(a.shape, lambda i: (0,) * a.ndim, pipeline_mode=pl.Buffered(1))
    of_layer = lambda a, l: pl.BlockSpec((None,) + a.shape[1:], lambda i: (l,) + (0,) * (a.ndim - 1),
                                         pipeline_mode=pl.Buffered(1))
    row = lambda c: pl.BlockSpec((tm, c), lambda i: (i, 0))
    return pl.pallas_call(
        functools.partial(_out_mlp_kernel, tf=tf),
        grid=(T // tm,),
        in_specs=[row(D), row(ma.shape[-1]), row(mb.shape[-1]), of_layer(wo, layer // 2), full(g),
                  of_layer(wu, layer), of_layer(wd, layer)],
        out_specs=row(D),
        out_shape=jax.ShapeDtypeStruct((T, D), F32),
        compiler_params=pltpu.CompilerParams(dimension_semantics=("arbitrary",), vmem_limit_bytes=VMEM_LIMIT),
        name="out_mlp",
    )(x, ma, mb, wo, g, wu, wd)


def _alibi_slopes(n):
    return 2.0 ** (-8.0 * jnp.arange(1, n + 1, dtype=F32) / n)


def _rope_placement():
    f = jnp.arange(B_ROPE_HALF)
    p = jnp.zeros((2 * B_ROPE_HALF, 3 * LANES), F32)
    p = p.at[f, HEAD + f].set(1.0).at[f, HEAD + B_ROPE_HALF + f].set(1.0)
    p = p.at[B_ROPE_HALF + f, LANES + HEAD + f].set(-1.0)
    p = p.at[B_ROPE_HALF + f, 2 * LANES + HEAD + B_ROPE_HALF + f].set(1.0)
    return jnp.concatenate([p, p], axis=0).astype(BF16)


def _alibi_reach(slopes, q_gain, k_gain, t, S):
    smax = 1.01 * HEAD * 0.125 * LOG2E * jnp.max(jnp.abs(q_gain)) * jnp.max(jnp.abs(k_gain))
    dist = (160.0 + 2.0 * smax) / (slopes * LOG2E)
    blocks = jnp.floor((dist - 1.0) / t) + 1.0
    return jnp.clip(blocks, 0.0, float(S // t)).astype(jnp.int32)


def _tile(n, want):
    t = min(n, want)
    assert n % t == 0, (n, t)
    return t


def kernel(x, positions, norm_mix_g, norm_ffn_g, mlp_w_up, mlp_w_down, ev_w_in, ev_w_out, a_q_norm, a_k_norm, a_sinks, b_cq_norm, b_ckv_norm, b_w_uq, b_w_ukv, b_q_norm, b_k_norm, od_w_in, od_w_out, d_q_norm, d_k_norm, d_lambda, d_subln):
    B, S, D = x.shape
    T = B * S
    depth = norm_mix_g.shape[0]
    tm_proj = _tile(T, 1024)
    tm_mlp = _tile(T, 1024)
    t_a = _tile(S, 256)
    tq_b, t_b = _tile(S, 2048), _tile(S, 512)
    tq_c, t_c = _tile(S, 1024), _tile(S, 256)
    tq_d, t_d = _tile(S, 2048), _tile(S, 512)
    tf = 512

    xf = x.reshape(T, D)
    pos = positions.reshape(T)
    row2 = lambda a: a.reshape(1, -1).astype(F32)
    pair = lambda a: jnp.concatenate([a, a]).reshape(1, LANES).astype(F32)
    pad_qk = lambda a: jnp.pad(a.astype(F32), (0, LANES - B_QK)).reshape(1, LANES)
    inv = (ROPE_THETA ** (-jnp.arange(B_ROPE_HALF, dtype=F32) / B_ROPE_HALF)).reshape(B_ROPE_HALF, 1)
    rope_place = _rope_placement()
    slopes_a = _alibi_slopes(A_HEADS)
    slopes_d = _alibi_slopes(D_HEADS)

    w_up, w_down = mlp_w_up.astype(BF16), mlp_w_down.astype(BF16)
    ev_w_out_bf, od_w_out_bf = ev_w_out.astype(BF16), od_w_out.astype(BF16)
    for layer in range(depth):
        j = layer // 2
        g_mix = row2(norm_mix_g[layer])
        if layer % 2 == 0:
            w_in = ev_w_in[j]
            w_in = jnp.concatenate([w_in[:, :E_ROPE], jnp.zeros((D, HEAD), F32), w_in[:, E_ROPE:],
                                    jnp.zeros((D, LANES - B_QK), F32)], axis=1).astype(BF16)
            wuq = jnp.pad(b_w_uq[j].reshape(-1, B_HEADS, B_QK), ((0, 0), (0, 0), (0, LANES - B_QK)))
            wuq = wuq.reshape(-1, B_HEADS * LANES).astype(BF16)
            qa, ka, va, qb, kb, vb = _even_proj(
                xf, pos, g_mix, w_in, pair(a_q_norm[j]), pair(a_k_norm[j]), row2(b_cq_norm[j]),
                row2(b_ckv_norm[j]), wuq, b_w_ukv[j].astype(BF16), pad_qk(b_q_norm[j]), pad_qk(b_k_norm[j]),
                inv, rope_place, tm_proj)
            r3 = lambda a: a.reshape(B, S, a.shape[-1])
            ma = _attn_a(slopes_a, a_sinks[j].astype(F32), r3(qa), r3(ka), r3(va), t_a).reshape(T, -1)
            mb = _attn_b(r3(qb), r3(kb), r3(vb), tq_b, t_b).reshape(T, -1)
            w_out = ev_w_out_bf
        else:
            lambda_init = 0.8 - 0.6 * math.exp(-0.3 * layer)
            cq, ck, cv, dq, dk, dv = _odd_proj(
                xf, g_mix, od_w_in[j].astype(BF16), d_q_norm[j].reshape(1, LANES).astype(F32),
                d_k_norm[j].reshape(1, LANES).astype(F32), tm_proj)
            r3 = lambda a: a.reshape(B, S, a.shape[-1])
            ma = _attn_c(r3(cq), r3(ck), r3(cv), tq_c, t_c).reshape(T, -1)
            reach = _alibi_reach(slopes_d, d_q_norm[j], d_k_norm[j], t_d, S)
            mb = _attn_d(slopes_d, reach, r3(dq), r3(dk), r3(dv), d_lambda[j].astype(F32), row2(d_subln[j]),
                         lambda_init, tq_d, t_d).reshape(T, -1)
            w_out = od_w_out_bf
        xf = _out_mlp(xf, ma, mb, w_out, row2(norm_ffn_g[layer]), w_up, w_down, layer, tm_mlp, tf)
    return xf.reshape(B, S, D)
```

```python
import functools
import math

import jax
import jax.numpy as jnp
from jax import lax
from jax.experimental import pallas as pl
from jax.experimental.pallas import tpu as pltpu

F32 = jnp.float32
BF16 = jnp.bfloat16

EPS = 1e-6
CHUNK = 64
CHUNK_SHIFT = 6
LANES = 128
HEAD = 64
A_HEADS = 8
A_GROUP = 4
A_WINDOW = 128
A_WINDOW_CHUNKS = A_WINDOW // CHUNK
B_HEADS = 8
B_QK = 96
B_ROPE_HALF = 16
ROPE_THETA = 10000.0
C_HEADS = 8
D_HEADS = 4
B_Q_RANK = 256
B_KV_RANK = 128
MIX_HALF = 512
E_AK = A_HEADS * HEAD
E_AV = E_AK + (A_HEADS // A_GROUP) * HEAD
E_CQ = E_AV + (A_HEADS // A_GROUP) * HEAD
E_CKV = E_CQ + B_Q_RANK
E_ROPE = E_CKV + B_KV_RANK
E_END = E_ROPE + LANES
O_CK, O_CV, O_DQ, O_DK, O_DV, O_END = (i * MIX_HALF for i in range(1, 7))
NEG = -1e30
LOG2E = math.log2(math.e)
DEN_ROWS = 16
VMEM_LIMIT = 56 * 1024 * 1024


def _dot(a, b):
    return jnp.dot(a, b, preferred_element_type=F32)


def _dot_t(a, b):
    return lax.dot_general(a, b, (((1,), (1,)), ((), ())), preferred_element_type=F32)


def _dot_tl(a, b):
    return lax.dot_general(a, b, (((0,), (0,)), ((), ())), preferred_element_type=F32)


def _rms(x, denom):
    return x * lax.rsqrt(jnp.sum(x * x, axis=-1, keepdims=True) * (1.0 / denom) + EPS)


def _pair_rms(x, lo):
    xx = x * x
    s_lo = jnp.sum(jnp.where(lo, xx, 0.0), axis=-1, keepdims=True)
    s_hi = jnp.sum(jnp.where(lo, 0.0, xx), axis=-1, keepdims=True)
    r = jnp.where(lo, lax.rsqrt(s_lo * (1.0 / HEAD) + EPS), lax.rsqrt(s_hi * (1.0 / HEAD) + EPS))
    return x * r


def _lane_lo(shape):
    return lax.broadcasted_iota(jnp.int32, shape, len(shape) - 1) < HEAD


def _pipelined(n, stages):
    for step in range(n + len(stages) - 1):
        for si, stage in enumerate(stages):
            c = step - si
            if 0 <= c < n:
                stage(c)


def _even_proj_kernel(x_ref, pos_ref, g_ref, w_in_ref, aqg_ref, akg_ref, cqg_ref, ckvg_ref,
                      wuq_ref, wukv_ref, bqg_ref, bkg_ref, inv_ref, place_ref,
                      qa_ref, ka_ref, va_ref, qb_ref, kb_ref, vb_ref):
    x = x_ref[...]
    h = _rms(x, x.shape[-1]) * g_ref[...]
    proj = _dot(h.astype(BF16), w_in_ref[...])
    lo = _lane_lo((1, LANES))
    for p in range(A_HEADS // 2):
        seg = proj[:, p * LANES:(p + 1) * LANES]
        qa_ref[:, p * LANES:(p + 1) * LANES] = (_pair_rms(seg, lo) * aqg_ref[...] * (0.125 * LOG2E)).astype(BF16)
    ka_ref[...] = (_pair_rms(proj[:, E_AK:E_AV], lo) * akg_ref[...]).astype(BF16)
    va_ref[...] = proj[:, E_AV:E_CQ].astype(BF16)
    cq = _rms(proj[:, E_CQ:E_CKV], B_Q_RANK) * cqg_ref[...]
    ckv = _rms(proj[:, E_CKV:E_ROPE], B_KV_RANK) * ckvg_ref[...]
    krope = proj[:, E_ROPE:E_END]
    qall = _dot(cq.astype(BF16), wuq_ref[...])
    kvall = _dot(ckv.astype(BF16), wukv_ref[...])
    ang = inv_ref[...] * pos_ref[0].astype(F32)
    trig = jnp.concatenate([jnp.cos(ang), jnp.sin(ang)], axis=0)
    trig_hi = trig.astype(BF16)
    trig_lo = (trig - trig_hi.astype(F32)).astype(BF16)
    placed = _dot_tl(jnp.concatenate([trig_hi, trig_lo], axis=0), place_ref[...])
    cosf = placed[:, 0:LANES] + jnp.where(lo, 1.0, 0.0)
    s_first = placed[:, LANES:2 * LANES]
    s_second = placed[:, 2 * LANES:3 * LANES]

    def rope(t):
        return (t * cosf + pltpu.roll(t, LANES - B_ROPE_HALF, 1) * s_first
                + pltpu.roll(t, B_ROPE_HALF, 1) * s_second)

    scale_b = B_QK ** -0.5 * LOG2E
    krope_rot = rope(krope * bkg_ref[...])
    krope_ss = jnp.sum(krope * krope, axis=-1, keepdims=True)
    sq = [dict() for _ in range(B_HEADS)]
    sk = [dict() for _ in range(B_HEADS)]

    def q_sumsq(hh):
        qh = qall[:, hh * LANES:(hh + 1) * LANES]
        sq[hh]["x"] = qh
        sq[hh]["ss"] = jnp.sum(qh * qh, axis=-1, keepdims=True)

    def q_norm(hh):
        sq[hh]["n"] = sq[hh].pop("x") * lax.rsqrt(sq[hh].pop("ss") * (1.0 / B_QK) + EPS) * bqg_ref[...]

    def q_store(hh):
        qb_ref[:, hh * LANES:(hh + 1) * LANES] = (rope(sq[hh].pop("n")) * scale_b).astype(BF16)

    def k_sumsq(hh):
        knope = jnp.where(lo, kvall[:, hh * LANES:(hh + 1) * LANES], 0.0)
        sk[hh]["x"] = knope
        sk[hh]["ss"] = jnp.sum(knope * knope, axis=-1, keepdims=True) + krope_ss

    def k_store(hh):
        kn = (sk[hh].pop("x") * bkg_ref[...] + krope_rot) * lax.rsqrt(sk[hh].pop("ss") * (1.0 / B_QK) + EPS)
        kb_ref[:, hh * LANES:(hh + 1) * LANES] = kn.astype(BF16)

    _pipelined(B_HEADS, [q_sumsq, k_sumsq, q_norm, k_store, q_store])
    for p in range(B_HEADS // 2):
        v_even = kvall[:, 2 * p * LANES:(2 * p + 1) * LANES]
        v_odd = kvall[:, (2 * p + 1) * LANES:(2 * p + 2) * LANES]
        vb_ref[:, p * LANES:(p + 1) * LANES] = jnp.where(lo, pltpu.roll(v_even, HEAD, 1), v_odd).astype(BF16)


def _even_proj(x, pos, g, w_in, aqg, akg, cqg, ckvg, wuq, wukv, bqg, bkg, inv, place, tm):
    T, D = x.shape
    full = lambda a: pl.BlockSpec(a.shape, lambda i: (0,) * a.ndim)
    row = lambda c: pl.BlockSpec((tm, c), lambda i: (i, 0))
    outs = (E_AK, E_AV - E_AK, E_CQ - E_AV, B_HEADS * LANES, B_HEADS * LANES, MIX_HALF)
    pos = pos.reshape(T // tm, 1, tm)
    return pl.pallas_call(
        _even_proj_kernel,
        grid=(T // tm,),
        in_specs=[row(D), pl.BlockSpec((1, 1, tm), lambda i: (i, 0, 0))]
        + [full(a) for a in (g, w_in, aqg, akg, cqg, ckvg, wuq, wukv, bqg, bkg, inv, place)],
        out_specs=[row(c) for c in outs],
        out_shape=[jax.ShapeDtypeStruct((T, c), BF16) for c in outs],
        compiler_params=pltpu.CompilerParams(dimension_semantics=("arbitrary",), vmem_limit_bytes=VMEM_LIMIT),
        name="even_proj",
    )(x, pos, g, w_in, aqg, akg, cqg, ckvg, wuq, wukv, bqg, bkg, inv, place)


def _odd_proj_kernel(x_ref, g_ref, w_in_ref, dqg_ref, dkg_ref,
                     cq_ref, ck_ref, cv_ref, dq_ref, dk_ref, dv_ref):
    x = x_ref[...]
    h = _rms(x, x.shape[-1]) * g_ref[...]
    proj = _dot(h.astype(BF16), w_in_ref[...])
    lo = _lane_lo((1, LANES))
    qscale = 0.125 * LOG2E
    cq_ref[...] = (proj[:, 0:O_CK] * qscale).astype(BF16)
    ck_ref[...] = proj[:, O_CK:O_CV].astype(BF16)
    cv_ref[...] = proj[:, O_CV:O_DQ].astype(BF16)
    for hh in range(D_HEADS):
        sl = slice(hh * LANES, (hh + 1) * LANES)
        dq_ref[:, sl] = (_pair_rms(proj[:, O_DQ + hh * LANES:O_DQ + (hh + 1) * LANES], lo)
                         * dqg_ref[...] * qscale).astype(BF16)
        dk_ref[:, sl] = (_pair_rms(proj[:, O_DK + hh * LANES:O_DK + (hh + 1) * LANES], lo)
                         * dkg_ref[...]).astype(BF16)
    dv_ref[...] = proj[:, O_DV:O_END].astype(BF16)


def _odd_proj(x, g, w_in, dqg, dkg, tm):
    T, D = x.shape
    full = lambda a: pl.BlockSpec(a.shape, lambda i: (0,) * a.ndim)
    row = lambda c: pl.BlockSpec((tm, c), lambda i: (i, 0))
    return pl.pallas_call(
        _odd_proj_kernel,
        grid=(T // tm,),
        in_specs=[row(D)] + [full(a) for a in (g, w_in, dqg, dkg)],
        out_specs=[row(MIX_HALF)] * 6,
        out_shape=[jax.ShapeDtypeStruct((T, MIX_HALF), BF16)] * 6,
        compiler_params=pltpu.CompilerParams(dimension_semantics=("arbitrary",), vmem_limit_bytes=VMEM_LIMIT),
        name="odd_proj",
    )(x, g, w_in, dqg, dkg)


def _attn_a_kernel(slope_ref, sink_ref, q_ref, k_ref, v_ref, o_ref, *, tq, win):
    qi = pl.program_id(1)
    q0 = qi * tq
    ks = pl.multiple_of(jnp.maximum(q0 - A_WINDOW, 0), A_WINDOW)
    lane_half = lax.broadcasted_iota(jnp.int32, (1, LANES), 1) // HEAD

    def both_halves(x, g):
        xf = x.astype(F32)
        return jnp.where(lane_half == g, xf, pltpu.roll(xf, HEAD, 1)).astype(BF16)

    groups = range(A_HEADS // A_GROUP)
    k = [both_halves(k_ref[0, pl.ds(ks, win), :], g) for g in groups]
    v = [both_halves(v_ref[0, pl.ds(ks, win), :], g) for g in groups]
    kpos = ks + lax.broadcasted_iota(jnp.int32, (win, tq), 0)
    qpos = q0 + lax.broadcasted_iota(jnp.int32, (win, tq), 1)
    dch = (qpos >> CHUNK_SHIFT) - (kpos >> CHUNK_SHIFT)
    allowed = (dch >= 0) & (dch <= A_WINDOW_CHUNKS)
    negdist = -jnp.abs(qpos - kpos).astype(F32)
    st = [dict() for _ in range(A_HEADS)]

    def scores(i):
        qq = q_ref[0, :, (i // 2) * LANES:(i // 2 + 1) * LANES]
        qh = jnp.where(lane_half == i % 2, qq, jnp.zeros_like(qq))
        st[i]["s"] = _dot_t(k[i // A_GROUP], qh)

    def softmax(i):
        s = jnp.where(allowed, st[i].pop("s") + (slope_ref[i] * LOG2E) * negdist, NEG)
        sink = sink_ref[i] * LOG2E
        m = jnp.maximum(jnp.max(s, axis=0, keepdims=True), sink)
        p = jnp.exp2(s - m)
        st[i]["den"] = jnp.sum(p, axis=0, keepdims=True) + jnp.exp2(sink - m)
        st[i]["p"] = p.astype(BF16)

    def values(i):
        o = _dot_tl(v[i // A_GROUP], st[i].pop("p")) / st[i].pop("den")
        st[i]["o"] = o[0:HEAD, :]

    _pipelined(A_HEADS, [scores, softmax, values])
    outs = [st[i]["o"] for i in range(A_HEADS)]
    o_ref[0] = jnp.concatenate(outs, axis=0).T.astype(BF16)


def _attn_a(slopes, sinks, qa, ka, va, tq):
    B, S, C = qa.shape
    win = tq + A_WINDOW
    smem = pl.BlockSpec(memory_space=pltpu.SMEM)
    return pl.pallas_call(
        functools.partial(_attn_a_kernel, tq=tq, win=win),
        grid=(B, S // tq),
        in_specs=[smem, smem,
                  pl.BlockSpec((1, tq, C), lambda b, i: (b, i, 0)),
                  pl.BlockSpec((1, S, LANES), lambda b, i: (b, 0, 0)),
                  pl.BlockSpec((1, S, LANES), lambda b, i: (b, 0, 0))],
        out_specs=pl.BlockSpec((1, tq, C), lambda b, i: (b, i, 0)),
        out_shape=jax.ShapeDtypeStruct((B, S, C), BF16),
        compiler_params=pltpu.CompilerParams(dimension_semantics=("arbitrary",) * 2, vmem_limit_bytes=VMEM_LIMIT),
        name="attn_a",
    )(slopes, sinks, qa, ka, va)


def _walk_key_blocks(qi, nchunk, step, group, max_blocks=None):
    first = qi * nchunk
    trips = first // group
    if max_blocks is None:
        def full(i, carry):
            step([(i * group + g, None) for g in range(group)])
            return carry

        lax.fori_loop(0, trips, full, 0)
        step([(first + d, d) for d in range(nchunk)])
    else:
        step([(first + d, d) for d in range(nchunk - 1, -1, -1)])

        def full(i, carry):
            step([(first - 1 - i * group - g, None) for g in range(group)])
            return carry

        lax.fori_loop(0, jnp.minimum(trips, (max_blocks + (group - 1)) // group), full, 0)


def _flash_specs(S, tq, q_width, k_width, v_width):
    return dict(
        in_specs=[pl.BlockSpec((1, tq, q_width), lambda b, p, i: (b, i, p)),
                  pl.BlockSpec((1, S, k_width), lambda b, p, i: (b, 0, p)),
                  pl.BlockSpec((1, S, v_width), lambda b, p, i: (b, 0, p))],
        out_specs=pl.BlockSpec((1, tq, LANES), lambda b, p, i: (b, i, p)),
        compiler_params=pltpu.CompilerParams(dimension_semantics=("arbitrary",) * 3, vmem_limit_bytes=VMEM_LIMIT),
    )


def _chunk_causal(t):
    krow = lax.broadcasted_iota(jnp.int32, (t, t), 0)
    qcol = lax.broadcasted_iota(jnp.int32, (t, t), 1)
    return (krow >> CHUNK_SHIFT) <= (qcol >> CHUNK_SHIFT)


def _softmax_stage(s, off, lanes, e, m_sc):
    m_prev = m_sc[e, :, lanes]
    m_new = jnp.maximum(m_prev, jnp.max(s, axis=0, keepdims=True) - off)
    alpha = jnp.exp2(m_prev - m_new)
    pr = jnp.exp2(s - (m_new + off))
    m_sc[e, :, lanes] = m_new
    return pr.astype(BF16), alpha


def _attn_b_kernel(q_ref, k_ref, v_ref, o_ref, m_sc, acc_sc, *, tq, t):
    qi = pl.program_id(2)
    nchunk = tq // t
    chains = [(e, c) for c in range(nchunk) for e in range(2)]
    lane = lax.broadcasted_iota(jnp.int32, (1, LANES), 1)
    ones_row = (HEAD, 0)
    m_sc[...] = jnp.full(m_sc.shape, NEG, F32)
    acc_sc[...] = jnp.zeros(acc_sc.shape, F32)

    def step(blocks):
        rows = [pl.ds(pl.multiple_of(j * t, t), t) for j, _ in blocks]
        live = [(e, c, g) for g, (_, d) in enumerate(blocks) for (e, c) in chains if d is None or c >= d]
        st = [dict() for _ in live]
        vals = {}

        def values_of(e, g):
            if (e, g) not in vals:
                v = v_ref[0, rows[g], :]
                own = (lane < HEAD) if e == 0 else (lane >= HEAD)
                vals[e, g] = jnp.where(own, v, jnp.where(lane == ones_row[e], 1.0, 0.0).astype(BF16))
            return vals[e, g]

        def scores(i):
            e, c, g = live[i]
            st[i]["s"] = _dot_t(k_ref[0, rows[g], e * LANES:(e + 1) * LANES],
                                q_ref[0, c * t:(c + 1) * t, e * LANES:(e + 1) * LANES])

        def softmax(i):
            e, c, g = live[i]
            s = st[i].pop("s")
            if c == blocks[g][1]:
                s = jnp.where(_chunk_causal(t), s, NEG)
            st[i]["p"], st[i]["alpha"] = _softmax_stage(s, 0.0, slice(c * t, (c + 1) * t), e, m_sc)

        def values(i):
            e, c, g = live[i]
            lanes = slice(c * t, (c + 1) * t)
            acc_sc[e, :, lanes] = (st[i]["alpha"] * acc_sc[e, :, lanes]
                                   + _dot_tl(values_of(e, g), st[i].pop("p")))

        _pipelined(len(live), [scores, softmax, values])

    _walk_key_blocks(qi, nchunk, step, group=nchunk)
    first = lax.broadcasted_iota(jnp.int32, (LANES, 1), 0) < HEAD
    den = [acc_sc[e, ones_row[e]:ones_row[e] + 1, :] for e in range(2)]
    o_t = jnp.where(first, acc_sc[0] / den[0], acc_sc[1] / den[1])
    o_ref[0] = o_t.T.astype(BF16)


def _attn_b(qb, kb, vb, tq, t):
    B, S, _ = qb.shape
    return pl.pallas_call(
        functools.partial(_attn_b_kernel, tq=tq, t=t),
        grid=(B, B_HEADS // 2, S // tq),
        out_shape=jax.ShapeDtypeStruct((B, S, MIX_HALF), BF16),
        scratch_shapes=[pltpu.VMEM((2, 1, tq), F32), pltpu.VMEM((2, LANES, tq), F32)],
        name="attn_b",
        **_flash_specs(S, tq, 2 * LANES, 2 * LANES, LANES),
    )(qb, kb, vb)


def _attn_c_kernel(q_ref, k_ref, v_ref, o_ref, r_sc, acc_sc, *, tq, t, pairs):
    qi = pl.program_id(2)
    nchunk = tq // t
    first = qi * nchunk
    krow = lax.broadcasted_iota(jnp.int32, (t, t), 0)
    qcol = lax.broadcasted_iota(jnp.int32, (t, t), 1)
    from_here = jnp.where(qcol >= krow, 1.0, 0.0).astype(BF16)
    from_here = jnp.concatenate([from_here, from_here], axis=1)
    lane_lo = _lane_lo((1, LANES))
    r_sc[...] = jnp.zeros(r_sc.shape, F32)
    acc_sc[...] = jnp.zeros(acc_sc.shape, F32)

    def step(backs, which):
        live = [(p, e, c, b) for b in range(len(backs)) for p in which for c in range(nchunk) for e in range(2)]
        st = [dict() for _ in live]

        def block(i):
            p, e, c, b = live[i]
            return first + c - backs[b]

        def rows(i):
            return pl.ds(pl.multiple_of(jnp.maximum(block(i), 0) * t, t), t)

        def cols(i):
            return slice(live[i][0] * LANES, (live[i][0] + 1) * LANES)

        def scores(i):
            p, e, c, b = live[i]
            qq = q_ref[0, c * t:(c + 1) * t, cols(i)]
            st[i]["z"] = _dot_t(k_ref[0, rows(i), cols(i)], jnp.where(lane_lo == (e == 0), qq, jnp.zeros_like(qq)))

        def softplus(i):
            p, e, c, b = live[i]
            z = st[i]["z"]
            sp = jnp.maximum(z, jnp.log2(1.0 + jnp.exp2(jnp.minimum(z, 126.0))))
            if isinstance(backs[b], int) and backs[b] == 0:
                sp = jnp.where(krow < qcol, sp, 0.0)
            hi = sp.astype(BF16)
            lo = (sp - hi.astype(F32)).astype(BF16)
            st[i]["hilo"] = jnp.concatenate([hi, lo], axis=0)

        def suffix(i):
            st[i]["suf"] = _dot(from_here, st[i].pop("hilo"))

        def weights(i):
            p, e, c, b = live[i]
            lanes = slice(c * t, (c + 1) * t)
            suf = st[i].pop("suf")
            a = jnp.exp2(st[i].pop("z") - suf)
            if isinstance(backs[b], int) and backs[b] == 0:
                a = jnp.where(krow < qcol, a, 0.0)
            st[i]["a"] = a.astype(BF16)
            r_prev = r_sc[2 * p + e, :, lanes]
            w = jnp.exp2(-r_prev)
            tot = suf[0:1, :]
            if not (isinstance(backs[b], int) and backs[b] <= c):
                exists = (block(i) >= 0).astype(F32)
                w, tot = w * exists, tot * exists
            st[i]["w"] = w
            r_sc[2 * p + e, :, lanes] = r_prev + tot

        def values(i):
            p, e, c, b = live[i]
            lanes = slice(c * t, (c + 1) * t)
            acc_sc[2 * p + e, :, lanes] = (acc_sc[2 * p + e, :, lanes]
                                           + st[i].pop("w") * _dot_tl(v_ref[0, rows(i), cols(i)], st[i].pop("a")))

        _pipelined(len(live), [scores, softplus, suffix, weights, values])

    step([0, 1], range(pairs))
    first_head = lax.broadcasted_iota(jnp.int32, (LANES, 1), 0) < HEAD
    for p in range(pairs):
        def still_live(p=p):
            return jnp.max(jnp.exp2(-r_sc[2 * p:2 * p + 2])) > 0.0

        def further(carry, p=p, still_live=still_live):
            step([carry[0]], [p])
            return carry[0] + 1, still_live()

        lax.while_loop(lambda carry: (carry[0] < first + nchunk) & carry[1], further, (2, still_live()))
        o_ref[0, :, p * LANES:(p + 1) * LANES] = jnp.where(
            first_head, acc_sc[2 * p], acc_sc[2 * p + 1]).T.astype(BF16)


def _attn_c(cq, ck, cv, tq, t, pairs):
    B, S, C = cq.shape
    w = pairs * LANES
    return pl.pallas_call(
        functools.partial(_attn_c_kernel, tq=tq, t=t, pairs=pairs),
        grid=(B, C // w, S // tq),
        in_specs=[pl.BlockSpec((1, tq, w), lambda b, p, i: (b, i, p)),
                  pl.BlockSpec((1, S, w), lambda b, p, i: (b, 0, p)),
                  pl.BlockSpec((1, S, w), lambda b, p, i: (b, 0, p))],
        out_specs=pl.BlockSpec((1, tq, w), lambda b, p, i: (b, i, p)),
        out_shape=jax.ShapeDtypeStruct((B, S, C), BF16),
        scratch_shapes=[pltpu.VMEM((2 * pairs, 1, tq), F32), pltpu.VMEM((2 * pairs, LANES, tq), F32)],
        compiler_params=pltpu.CompilerParams(dimension_semantics=("arbitrary",) * 3, vmem_limit_bytes=VMEM_LIMIT),
        name="attn_c",
    )(cq, ck, cv)


def _attn_d_kernel(slope_ref, reach_ref, q_ref, k_ref, v_ref, lam_ref, subln_ref, o_ref, m_sc, acc_sc,
                   *, tq, t, lambda_init):
    h = pl.program_id(1)
    qi = pl.program_id(2)
    nchunk = tq // t
    chains = [(e, c) for c in range(nchunk) for e in range(2)]
    slope = slope_ref[h] * LOG2E
    lane_lo = _lane_lo((1, LANES))
    key_bias = slope * lax.broadcasted_iota(jnp.int32, (t, LANES), 0).astype(F32)
    m_sc[...] = jnp.full(m_sc.shape, NEG, F32)
    acc_sc[...] = jnp.zeros(acc_sc.shape, F32)

    def step(blocks):
        rows = [pl.ds(pl.multiple_of(j * t, t), t) for j, _ in blocks]
        live = [(e, c, g) for g, (_, d) in enumerate(blocks) for (e, c) in chains if d is None or c >= d]
        st = [dict() for _ in live]
        vals = {}

        def values_of(g):
            if g not in vals:
                vals[g] = jnp.concatenate([v_ref[0, rows[g], :].T, jnp.ones((DEN_ROWS, t), BF16)], axis=0)
            return vals[g]

        def scores(i):
            e, c, g = live[i]
            qq = q_ref[0, c * t:(c + 1) * t, :]
            st[i]["s"] = _dot_t(k_ref[0, rows[g], :], jnp.where(lane_lo == (e == 0), qq, jnp.zeros_like(qq)))

        def softmax(i):
            e, c, g = live[i]
            s = st[i].pop("s")
            j, d = blocks[g]
            if c == d:
                krow = lax.broadcasted_iota(jnp.int32, (t, t), 0)
                qcol = lax.broadcasted_iota(jnp.int32, (t, t), 1)
                s = s + slope * jnp.minimum(krow, 2 * qcol - krow).astype(F32)
                s = jnp.where(_chunk_causal(t), s, NEG)
                off = 0.0
            else:
                s = jnp.concatenate([s[:, b * LANES:(b + 1) * LANES] + key_bias for b in range(t // LANES)], axis=1)
                off = (slope * ((qi * nchunk + c - j) * t).astype(F32) if d is None
                       else slope * float((c - d) * t))
            st[i]["p"], st[i]["alpha"] = _softmax_stage(s, off, slice(c * t, (c + 1) * t), e, m_sc)

        def values(i):
            e, c, g = live[i]
            lanes = slice(c * t, (c + 1) * t)
            acc_sc[e, :, lanes] = (st[i]["alpha"] * acc_sc[e, :, lanes]
                                   + _dot(values_of(g), st[i].pop("p")))

        _pipelined(len(live), [scores, softmax, values])

    _walk_key_blocks(qi, nchunk, step, group=2, max_blocks=reach_ref[h])
    lf = lam_ref[...]
    lam = (jnp.exp(jnp.sum(lf[0:1] * lf[1:2], axis=-1, keepdims=True))
           - jnp.exp(jnp.sum(lf[2:3] * lf[3:4], axis=-1, keepdims=True)) + lambda_init)
    sm = [acc_sc[e, 0:LANES, :] / acc_sc[e, LANES:LANES + 1, :] for e in range(2)]
    o = (sm[0] - lam * sm[1]).T
    o = _rms(o, LANES) * subln_ref[...] * (1.0 - lambda_init)
    o_ref[0] = o.astype(BF16)


def _attn_d(slopes, reach, dq, dk, dv, lam, subln, lambda_init, tq, t):
    B, S, _ = dq.shape
    specs = _flash_specs(S, tq, LANES, LANES, LANES)
    whole = lambda a: pl.BlockSpec(a.shape, lambda b, h, i: (0, 0))
    smem = pl.BlockSpec(memory_space=pltpu.SMEM)
    specs["in_specs"] = [smem, smem] + specs["in_specs"] + [whole(lam), whole(subln)]
    return pl.pallas_call(
        functools.partial(_attn_d_kernel, tq=tq, t=t, lambda_init=lambda_init),
        grid=(B, D_HEADS, S // tq),
        out_shape=jax.ShapeDtypeStruct((B, S, MIX_HALF), BF16),
        scratch_shapes=[pltpu.VMEM((2, 1, tq), F32), pltpu.VMEM((2, LANES + DEN_ROWS, tq), F32)],
        name="attn_d",
        **specs,
    )(slopes, reach, dq, dk, dv, lam, subln)


def _out_mlp_kernel(x_ref, ma_ref, mb_ref, wo_ref, g_ref, wu_ref, wd_ref, o_ref, *, tf):
    half = ma_ref.shape[-1]
    x1 = x_ref[...] + _dot(ma_ref[...], wo_ref[0:half, :]) + _dot(mb_ref[...], wo_ref[half:2 * half, :])
    h = (_rms(x1, x1.shape[-1]) * g_ref[...]).astype(BF16)
    o_ref[...] = x1
    for f in range(wu_ref.shape[-1] // tf):
        u = jnp.maximum(_dot(h, wu_ref[:, f * tf:(f + 1) * tf]), 0.0)
        o_ref[...] += _dot((u * u).astype(BF16), wd_ref[f * tf:(f + 1) * tf, :])


def _out_mlp(x, ma, mb, wo, g, wu, wd, layer, tm, tf):
    T, D = x.shape
    full = lambda a: pl.BlockSpec(a.shape, lambda i: (0,) * a.ndim, pipeline_mode=pl.Buffered(1))
    of_layer = lambda a, l: pl.BlockSpec((None,) + a.shape[1:], lambda i: (l,) + (0,) * (a.ndim - 1),
                                         pipeline_mode=pl.Buffered(1))
    row = lambda c: pl.BlockSpec((tm, c), lambda i: (i, 0))
    return pl.pallas_call(
        functools.partial(_out_mlp_kernel, tf=tf),
        grid=(T // tm,),
        in_specs=[row(D), row(ma.shape[-1]), row(mb.shape[-1]), of_layer(wo, layer // 2), full(g),
                  of_layer(wu, layer), of_layer(wd, layer)],
        out_specs=row(D),
        out_shape=jax.ShapeDtypeStruct((T, D), F32),
        compiler_params=pltpu.CompilerParams(dimension_semantics=("arbitrary",), vmem_limit_bytes=VMEM_LIMIT),
        name="out_mlp",
    )(x, ma, mb, wo, g, wu, wd)


def _alibi_slopes(n):
    return 2.0 ** (-8.0 * jnp.arange(1, n + 1, dtype=F32) / n)


def _rope_placement():
    f = jnp.arange(B_ROPE_HALF)
    p = jnp.zeros((2 * B_ROPE_HALF, 3 * LANES), F32)
    p = p.at[f, HEAD + f].set(1.0).at[f, HEAD + B_ROPE_HALF + f].set(1.0)
    p = p.at[B_ROPE_HALF + f, LANES + HEAD + f].set(-1.0)
    p = p.at[B_ROPE_HALF + f, 2 * LANES + HEAD + B_ROPE_HALF + f].set(1.0)
    return jnp.concatenate([p, p], axis=0).astype(BF16)


def _alibi_reach(slopes, q_gain, k_gain, t, S):
    smax = 1.01 * HEAD * 0.125 * LOG2E * jnp.max(jnp.abs(q_gain)) * jnp.max(jnp.abs(k_gain))
    dist = (160.0 + 2.0 * smax) / (slopes * LOG2E)
    blocks = jnp.floor((dist - 1.0) / t) + 1.0
    return jnp.clip(blocks, 0.0, float(S // t)).astype(jnp.int32)


def _tile(n, want):
    t = min(n, want)
    assert n % t == 0, (n, t)
    return t


def kernel(x, positions, norm_mix_g, norm_ffn_g, mlp_w_up, mlp_w_down, ev_w_in, ev_w_out, a_q_norm, a_k_norm, a_sinks, b_cq_norm, b_ckv_norm, b_w_uq, b_w_ukv, b_q_norm, b_k_norm, od_w_in, od_w_out, d_q_norm, d_k_norm, d_lambda, d_subln):
    B, S, D = x.shape
    T = B * S
    depth = norm_mix_g.shape[0]
    tm_proj = _tile(T, 1024)
    tm_mlp = _tile(T, 1024)
    t_a = _tile(S, 256)
    tq_b, t_b = _tile(S, 2048), _tile(S, 512)
    tq_c, t_c = _tile(S, 1024), _tile(S, 256)
    tq_d, t_d = _tile(S, 2048), _tile(S, 512)
    tf = 512

    xf = x.reshape(T, D)
    pos = positions.reshape(T)
    row2 = lambda a: a.reshape(1, -1).astype(F32)
    pair = lambda a: jnp.concatenate([a, a]).reshape(1, LANES).astype(F32)
    pad_qk = lambda a: jnp.pad(a.astype(F32), (0, LANES - B_QK)).reshape(1, LANES)
    inv = (ROPE_THETA ** (-jnp.arange(B_ROPE_HALF, dtype=F32) / B_ROPE_HALF)).reshape(B_ROPE_HALF, 1)
    rope_place = _rope_placement()
    slopes_a = _alibi_slopes(A_HEADS)
    slopes_d = _alibi_slopes(D_HEADS)

    w_up, w_down = mlp_w_up.astype(BF16), mlp_w_down.astype(BF16)
    ev_w_out_bf, od_w_out_bf = ev_w_out.astype(BF16), od_w_out.astype(BF16)
    for layer in range(depth):
        j = layer // 2
        g_mix = row2(norm_mix_g[layer])
        if layer % 2 == 0:
            w_in = ev_w_in[j]
            w_in = jnp.concatenate([w_in[:, :E_ROPE], jnp.zeros((D, HEAD), F32), w_in[:, E_ROPE:],
                                    jnp.zeros((D, LANES - B_QK), F32)], axis=1).astype(BF16)
            wuq = jnp.pad(b_w_uq[j].reshape(-1, B_HEADS, B_QK), ((0, 0), (0, 0), (0, LANES - B_QK)))
            wuq = wuq.reshape(-1, B_HEADS * LANES).astype(BF16)
            qa, ka, va, qb, kb, vb = _even_proj(
                xf, pos, g_mix, w_in, pair(a_q_norm[j]), pair(a_k_norm[j]), row2(b_cq_norm[j]),
                row2(b_ckv_norm[j]), wuq, b_w_ukv[j].astype(BF16), pad_qk(b_q_norm[j]), pad_qk(b_k_norm[j]),
                inv, rope_place, tm_proj)
            r3 = lambda a: a.reshape(B, S, a.shape[-1])
            ma = _attn_a(slopes_a, a_sinks[j].astype(F32), r3(qa), r3(ka), r3(va), t_a).reshape(T, -1)
            mb = _attn_b(r3(qb), r3(kb), r3(vb), tq_b, t_b).reshape(T, -1)
            w_out = ev_w_out_bf
        else:
            lambda_init = 0.8 - 0.6 * math.exp(-0.3 * layer)
            cq, ck, cv, dq, dk, dv = _odd_proj(
                xf, g_mix, od_w_in[j].astype(BF16), d_q_norm[j].reshape(1, LANES).astype(F32),
                d_k_norm[j].reshape(1, LANES).astype(F32), tm_proj)
            r3 = lambda a: a.reshape(B, S, a.shape[-1])
            ma = _attn_c(r3(cq), r3(ck), r3(cv), tq_c, t_c, pairs=2).reshape(T, -1)
            reach = _alibi_reach(slopes_d, d_q_norm[j], d_k_norm[j], t_d, S)
            mb = _attn_d(slopes_d, reach, r3(dq), r3(dk), r3(dv), d_lambda[j].astype(F32), row2(d_subln[j]),
                         lambda_init, tq_d, t_d).reshape(T, -1)
            w_out = od_w_out_bf
        xf = _out_mlp(xf, ma, mb, w_out, row2(norm_ffn_g[layer]), w_up, w_down, layer, tm_mlp, tf)
    return xf.reshape(B, S, D)
```

```python
import functools
import math

import jax
import jax.numpy as jnp
from jax import lax
from jax.experimental import pallas as pl
from jax.experimental.pallas import tpu as pltpu

F32 = jnp.float32
BF16 = jnp.bfloat16

EPS = 1e-6
CHUNK = 64
CHUNK_SHIFT = 6
LANES = 128
HEAD = 64
A_HEADS = 8
A_GROUP = 4
A_WINDOW = 128
A_WINDOW_CHUNKS = A_WINDOW // CHUNK
B_HEADS = 8
B_QK = 96
B_ROPE_HALF = 16
ROPE_THETA = 10000.0
C_HEADS = 8
D_HEADS = 4
B_Q_RANK = 256
B_KV_RANK = 128
MIX_HALF = 512
E_AK = A_HEADS * HEAD
E_AV = E_AK + (A_HEADS // A_GROUP) * HEAD
E_CQ = E_AV + (A_HEADS // A_GROUP) * HEAD
E_CKV = E_CQ + B_Q_RANK
E_ROPE = E_CKV + B_KV_RANK
E_END = E_ROPE + LANES
O_CK, O_CV, O_DQ, O_DK, O_DV, O_END = (i * MIX_HALF for i in range(1, 7))
NEG = -1e30
LOG2E = math.log2(math.e)
DEN_ROWS = 16
VMEM_LIMIT = 56 * 1024 * 1024


def _dot(a, b):
    return jnp.dot(a, b, preferred_element_type=F32)


def _dot_t(a, b):
    return lax.dot_general(a, b, (((1,), (1,)), ((), ())), preferred_element_type=F32)


def _dot_tl(a, b):
    return lax.dot_general(a, b, (((0,), (0,)), ((), ())), preferred_element_type=F32)


def _rms(x, denom):
    return x * lax.rsqrt(jnp.sum(x * x, axis=-1, keepdims=True) * (1.0 / denom) + EPS)


def _pair_rms(x, lo):
    xx = x * x
    s_lo = jnp.sum(jnp.where(lo, xx, 0.0), axis=-1, keepdims=True)
    s_hi = jnp.sum(jnp.where(lo, 0.0, xx), axis=-1, keepdims=True)
    r = jnp.where(lo, lax.rsqrt(s_lo * (1.0 / HEAD) + EPS), lax.rsqrt(s_hi * (1.0 / HEAD) + EPS))
    return x * r


def _lane_lo(shape):
    return lax.broadcasted_iota(jnp.int32, shape, len(shape) - 1) < HEAD


def _pipelined(n, stages):
    for step in range(n + len(stages) - 1):
        for si, stage in enumerate(stages):
            c = step - si
            if 0 <= c < n:
                stage(c)


def _even_proj_kernel(x_ref, pos_ref, g_ref, w_in_ref, aqg_ref, akg_ref, cqg_ref, ckvg_ref,
                      wuq_ref, wukv_ref, bqg_ref, bkg_ref, inv_ref, place_ref,
                      qa_ref, ka_ref, va_ref, qb_ref, kb_ref, vb_ref):
    x = x_ref[...]
    h = _rms(x, x.shape[-1]) * g_ref[...]
    proj = _dot(h.astype(BF16), w_in_ref[...])
    lo = _lane_lo((1, LANES))
    for p in range(A_HEADS // 2):
        seg = proj[:, p * LANES:(p + 1) * LANES]
        qa_ref[:, p * LANES:(p + 1) * LANES] = (_pair_rms(seg, lo) * aqg_ref[...] * (0.125 * LOG2E)).astype(BF16)
    ka_ref[...] = (_pair_rms(proj[:, E_AK:E_AV], lo) * akg_ref[...]).astype(BF16)
    va_ref[...] = proj[:, E_AV:E_CQ].astype(BF16)
    cq = _rms(proj[:, E_CQ:E_CKV], B_Q_RANK) * cqg_ref[...]
    ckv = _rms(proj[:, E_CKV:E_ROPE], B_KV_RANK) * ckvg_ref[...]
    krope = proj[:, E_ROPE:E_END]
    qall = _dot(cq.astype(BF16), wuq_ref[...])
    kvall = _dot(ckv.astype(BF16), wukv_ref[...])
    ang = inv_ref[...] * pos_ref[0].astype(F32)
    trig = jnp.concatenate([jnp.cos(ang), jnp.sin(ang)], axis=0)
    trig_hi = trig.astype(BF16)
    trig_lo = (trig - trig_hi.astype(F32)).astype(BF16)
    placed = _dot_tl(jnp.concatenate([trig_hi, trig_lo], axis=0), place_ref[...])
    cosf = placed[:, 0:LANES] + jnp.where(lo, 1.0, 0.0)
    s_first = placed[:, LANES:2 * LANES]
    s_second = placed[:, 2 * LANES:3 * LANES]

    def rope(t):
        return (t * cosf + pltpu.roll(t, LANES - B_ROPE_HALF, 1) * s_first
                + pltpu.roll(t, B_ROPE_HALF, 1) * s_second)

    scale_b = B_QK ** -0.5 * LOG2E
    krope_rot = rope(krope * bkg_ref[...])
    krope_ss = jnp.sum(krope * krope, axis=-1, keepdims=True)
    sq = [dict() for _ in range(B_HEADS)]
    sk = [dict() for _ in range(B_HEADS)]

    def q_sumsq(hh):
        qh = qall[:, hh * LANES:(hh + 1) * LANES]
        sq[hh]["x"] = qh
        sq[hh]["ss"] = jnp.sum(qh * qh, axis=-1, keepdims=True)

    def q_norm(hh):
        sq[hh]["n"] = sq[hh].pop("x") * lax.rsqrt(sq[hh].pop("ss") * (1.0 / B_QK) + EPS) * bqg_ref[...]

    def q_store(hh):
        qb_ref[:, hh * LANES:(hh + 1) * LANES] = (rope(sq[hh].pop("n")) * scale_b).astype(BF16)

    def k_sumsq(hh):
        knope = jnp.where(lo, kvall[:, hh * LANES:(hh + 1) * LANES], 0.0)
        sk[hh]["x"] = knope
        sk[hh]["ss"] = jnp.sum(knope * knope, axis=-1, keepdims=True) + krope_ss

    def k_store(hh):
        kn = (sk[hh].pop("x") * bkg_ref[...] + krope_rot) * lax.rsqrt(sk[hh].pop("ss") * (1.0 / B_QK) + EPS)
        kb_ref[:, hh * LANES:(hh + 1) * LANES] = kn.astype(BF16)

    _pipelined(B_HEADS, [q_sumsq, k_sumsq, q_norm, k_store, q_store])
    for p in range(B_HEADS // 2):
        v_even = kvall[:, 2 * p * LANES:(2 * p + 1) * LANES]
        v_odd = kvall[:, (2 * p + 1) * LANES:(2 * p + 2) * LANES]
        vb_ref[:, p * LANES:(p + 1) * LANES] = jnp.where(lo, pltpu.roll(v_even, HEAD, 1), v_odd).astype(BF16)


def _even_proj(x, pos, g, w_in, aqg, akg, cqg, ckvg, wuq, wukv, bqg, bkg, inv, place, tm):
    T, D = x.shape
    full = lambda a: pl.BlockSpec(a.shape, lambda i: (0,) * a.ndim)
    row = lambda c: pl.BlockSpec((tm, c), lambda i: (i, 0))
    outs = (E_AK, E_AV - E_AK, E_CQ - E_AV, B_HEADS * LANES, B_HEADS * LANES, MIX_HALF)
    pos = pos.reshape(T // tm, 1, tm)
    return pl.pallas_call(
        _even_proj_kernel,
        grid=(T // tm,),
        in_specs=[row(D), pl.BlockSpec((1, 1, tm), lambda i: (i, 0, 0))]
        + [full(a) for a in (g, w_in, aqg, akg, cqg, ckvg, wuq, wukv, bqg, bkg, inv, place)],
        out_specs=[row(c) for c in outs],
        out_shape=[jax.ShapeDtypeStruct((T, c), BF16) for c in outs],
        compiler_params=pltpu.CompilerParams(dimension_semantics=("arbitrary",), vmem_limit_bytes=VMEM_LIMIT),
        name="even_proj",
    )(x, pos, g, w_in, aqg, akg, cqg, ckvg, wuq, wukv, bqg, bkg, inv, place)


def _odd_proj_kernel(x_ref, g_ref, w_in_ref, dqg_ref, dkg_ref,
                     cq_ref, ck_ref, cv_ref, dq_ref, dk_ref, dv_ref):
    x = x_ref[...]
    h = _rms(x, x.shape[-1]) * g_ref[...]
    proj = _dot(h.astype(BF16), w_in_ref[...])
    lo = _lane_lo((1, LANES))
    qscale = 0.125 * LOG2E
    cq_ref[...] = (proj[:, 0:O_CK] * qscale).astype(BF16)
    ck_ref[...] = proj[:, O_CK:O_CV].astype(BF16)
    cv_ref[...] = proj[:, O_CV:O_DQ].astype(BF16)
    for hh in range(D_HEADS):
        sl = slice(hh * LANES, (hh + 1) * LANES)
        dq_ref[:, sl] = (_pair_rms(proj[:, O_DQ + hh * LANES:O_DQ + (hh + 1) * LANES], lo)
                         * dqg_ref[...] * qscale).astype(BF16)
        dk_ref[:, sl] = (_pair_rms(proj[:, O_DK + hh * LANES:O_DK + (hh + 1) * LANES], lo)
                         * dkg_ref[...]).astype(BF16)
    dv_ref[...] = proj[:, O_DV:O_END].astype(BF16)


def _odd_proj(x, g, w_in, dqg, dkg, tm):
    T, D = x.shape
    full = lambda a: pl.BlockSpec(a.shape, lambda i: (0,) * a.ndim)
    row = lambda c: pl.BlockSpec((tm, c), lambda i: (i, 0))
    return pl.pallas_call(
        _odd_proj_kernel,
        grid=(T // tm,),
        in_specs=[row(D)] + [full(a) for a in (g, w_in, dqg, dkg)],
        out_specs=[row(MIX_HALF)] * 6,
        out_shape=[jax.ShapeDtypeStruct((T, MIX_HALF), BF16)] * 6,
        compiler_params=pltpu.CompilerParams(dimension_semantics=("arbitrary",), vmem_limit_bytes=VMEM_LIMIT),
        name="odd_proj",
    )(x, g, w_in, dqg, dkg)


def _attn_a_kernel(slope_ref, sink_ref, q_ref, k_ref, v_ref, o_ref, *, tq, win):
    qi = pl.program_id(1)
    q0 = qi * tq
    ks = pl.multiple_of(jnp.maximum(q0 - A_WINDOW, 0), A_WINDOW)
    lane_half = lax.broadcasted_iota(jnp.int32, (1, LANES), 1) // HEAD

    def both_halves(x, g):
        xf = x.astype(F32)
        return jnp.where(lane_half == g, xf, pltpu.roll(xf, HEAD, 1)).astype(BF16)

    groups = range(A_HEADS // A_GROUP)
    k = [both_halves(k_ref[0, pl.ds(ks, win), :], g) for g in groups]
    v = [both_halves(v_ref[0, pl.ds(ks, win), :], g) for g in groups]
    kpos = ks + lax.broadcasted_iota(jnp.int32, (win, tq), 0)
    qpos = q0 + lax.broadcasted_iota(jnp.int32, (win, tq), 1)
    dch = (qpos >> CHUNK_SHIFT) - (kpos >> CHUNK_SHIFT)
    allowed = (dch >= 0) & (dch <= A_WINDOW_CHUNKS)
    negdist = -jnp.abs(qpos - kpos).astype(F32)
    st = [dict() for _ in range(A_HEADS)]

    def scores(i):
        qq = q_ref[0, :, (i // 2) * LANES:(i // 2 + 1) * LANES]
        qh = jnp.where(lane_half == i % 2, qq, jnp.zeros_like(qq))
        st[i]["s"] = _dot_t(k[i // A_GROUP], qh)

    def softmax(i):
        s = jnp.where(allowed, st[i].pop("s") + (slope_ref[i] * LOG2E) * negdist, NEG)
        sink = sink_ref[i] * LOG2E
        m = jnp.maximum(jnp.max(s, axis=0, keepdims=True), sink)
        p = jnp.exp2(s - m)
        st[i]["den"] = jnp.sum(p, axis=0, keepdims=True) + jnp.exp2(sink - m)
        st[i]["p"] = p.astype(BF16)

    def values(i):
        o = _dot_tl(v[i // A_GROUP], st[i].pop("p")) / st[i].pop("den")
        st[i]["o"] = o[0:HEAD, :]

    _pipelined(A_HEADS, [scores, softmax, values])
    outs = [st[i]["o"] for i in range(A_HEADS)]
    o_ref[0] = jnp.concatenate(outs, axis=0).T.astype(BF16)


def _attn_a(slopes, sinks, qa, ka, va, tq):
    B, S, C = qa.shape
    win = tq + A_WINDOW
    smem = pl.BlockSpec(memory_space=pltpu.SMEM)
    return pl.pallas_call(
        functools.partial(_attn_a_kernel, tq=tq, win=win),
        grid=(B, S // tq),
        in_specs=[smem, smem,
                  pl.BlockSpec((1, tq, C), lambda b, i: (b, i, 0)),
                  pl.BlockSpec((1, S, LANES), lambda b, i: (b, 0, 0)),
                  pl.BlockSpec((1, S, LANES), lambda b, i: (b, 0, 0))],
        out_specs=pl.BlockSpec((1, tq, C), lambda b, i: (b, i, 0)),
        out_shape=jax.ShapeDtypeStruct((B, S, C), BF16),
        compiler_params=pltpu.CompilerParams(dimension_semantics=("arbitrary",) * 2, vmem_limit_bytes=VMEM_LIMIT),
        name="attn_a",
    )(slopes, sinks, qa, ka, va)


def _walk_key_blocks(qi, nchunk, step, group, max_blocks=None):
    first = qi * nchunk
    trips = first // group
    if max_blocks is None:
        def full(i, carry):
            step([(i * group + g, None) for g in range(group)])
            return carry

        lax.fori_loop(0, trips, full, 0)
        step([(first + d, d) for d in range(nchunk)])
    else:
        step([(first + d, d) for d in range(nchunk - 1, -1, -1)])

        def full(i, carry):
            step([(first - 1 - i * group - g, None) for g in range(group)])
            return carry

        lax.fori_loop(0, jnp.minimum(trips, (max_blocks + (group - 1)) // group), full, 0)


def _flash_specs(S, tq, q_width, k_width, v_width):
    return dict(
        in_specs=[pl.BlockSpec((1, tq, q_width), lambda b, p, i: (b, i, p)),
                  pl.BlockSpec((1, S, k_width), lambda b, p, i: (b, 0, p)),
                  pl.BlockSpec((1, S, v_width), lambda b, p, i: (b, 0, p))],
        out_specs=pl.BlockSpec((1, tq, LANES), lambda b, p, i: (b, i, p)),
        compiler_params=pltpu.CompilerParams(dimension_semantics=("arbitrary",) * 3, vmem_limit_bytes=VMEM_LIMIT),
    )


def _chunk_causal(t):
    krow = lax.broadcasted_iota(jnp.int32, (t, t), 0)
    qcol = lax.broadcasted_iota(jnp.int32, (t, t), 1)
    return (krow >> CHUNK_SHIFT) <= (qcol >> CHUNK_SHIFT)


def _softmax_stage(s, off, lanes, e, m_sc):
    m_prev = m_sc[e, :, lanes]
    m_new = jnp.maximum(m_prev, jnp.max(s, axis=0, keepdims=True) - off)
    alpha = jnp.exp2(m_prev - m_new)
    pr = jnp.exp2(s - (m_new + off))
    m_sc[e, :, lanes] = m_new
    return pr.astype(BF16), alpha


def _attn_b_kernel(q_ref, k_ref, v_ref, o_ref, m_sc, acc_sc, *, tq, t):
    qi = pl.program_id(2)
    nchunk = tq // t
    chains = [(e, c) for c in range(nchunk) for e in range(2)]
    lane = lax.broadcasted_iota(jnp.int32, (1, LANES), 1)
    ones_row = (HEAD, 0)
    m_sc[...] = jnp.full(m_sc.shape, NEG, F32)
    acc_sc[...] = jnp.zeros(acc_sc.shape, F32)

    def step(blocks):
        rows = [pl.ds(pl.multiple_of(j * t, t), t) for j, _ in blocks]
        live = [(e, c, g) for g, (_, d) in enumerate(blocks) for (e, c) in chains if d is None or c >= d]
        st = [dict() for _ in live]
        vals = {}

        def values_of(e, g):
            if (e, g) not in vals:
                v = v_ref[0, rows[g], :]
                own = (lane < HEAD) if e == 0 else (lane >= HEAD)
                vals[e, g] = jnp.where(own, v, jnp.where(lane == ones_row[e], 1.0, 0.0).astype(BF16))
            return vals[e, g]

        def scores(i):
            e, c, g = live[i]
            st[i]["s"] = _dot_t(k_ref[0, rows[g], e * LANES:(e + 1) * LANES],
                                q_ref[0, c * t:(c + 1) * t, e * LANES:(e + 1) * LANES])

        def softmax(i):
            e, c, g = live[i]
            s = st[i].pop("s")
            if c == blocks[g][1]:
                s = jnp.where(_chunk_causal(t), s, NEG)
            st[i]["p"], st[i]["alpha"] = _softmax_stage(s, 0.0, slice(c * t, (c + 1) * t), e, m_sc)

        def values(i):
            e, c, g = live[i]
            lanes = slice(c * t, (c + 1) * t)
            acc_sc[e, :, lanes] = (st[i]["alpha"] * acc_sc[e, :, lanes]
                                   + _dot_tl(values_of(e, g), st[i].pop("p")))

        _pipelined(len(live), [scores, softmax, values])

    _walk_key_blocks(qi, nchunk, step, group=nchunk)
    first = lax.broadcasted_iota(jnp.int32, (LANES, 1), 0) < HEAD
    den = [acc_sc[e, ones_row[e]:ones_row[e] + 1, :] for e in range(2)]
    o_t = jnp.where(first, acc_sc[0] / den[0], acc_sc[1] / den[1])
    o_ref[0] = o_t.T.astype(BF16)


def _attn_b(qb, kb, vb, tq, t):
    B, S, _ = qb.shape
    return pl.pallas_call(
        functools.partial(_attn_b_kernel, tq=tq, t=t),
        grid=(B, B_HEADS // 2, S // tq),
        out_shape=jax.ShapeDtypeStruct((B, S, MIX_HALF), BF16),
        scratch_shapes=[pltpu.VMEM((2, 1, tq), F32), pltpu.VMEM((2, LANES, tq), F32)],
        name="attn_b",
        **_flash_specs(S, tq, 2 * LANES, 2 * LANES, LANES),
    )(qb, kb, vb)


def _attn_c_kernel(q_ref, k_ref, v_ref, o_ref, r_sc, acc_sc, *, tq, t, pairs):
    qi = pl.program_id(2)
    nchunk = tq // t
    first = qi * nchunk
    krow = lax.broadcasted_iota(jnp.int32, (t, t), 0)
    qcol = lax.broadcasted_iota(jnp.int32, (t, t), 1)
    from_here = jnp.where(qcol >= krow, 1.0, 0.0).astype(BF16)
    from_here = jnp.concatenate([from_here, from_here], axis=1)
    lane_lo = _lane_lo((1, LANES))
    r_sc[...] = jnp.zeros(r_sc.shape, F32)
    acc_sc[...] = jnp.zeros(acc_sc.shape, F32)

    def step(backs, which):
        live = [(p, e, c, b) for b in range(len(backs)) for p in which for c in range(nchunk) for e in range(2)]
        st = [dict() for _ in live]

        def block(i):
            p, e, c, b = live[i]
            return first + c - backs[b]

        def rows(i):
            return pl.ds(pl.multiple_of(jnp.maximum(block(i), 0) * t, t), t)

        def cols(i):
            return slice(live[i][0] * LANES, (live[i][0] + 1) * LANES)

        def scores(i):
            p, e, c, b = live[i]
            qq = q_ref[0, c * t:(c + 1) * t, cols(i)]
            st[i]["z"] = _dot_t(k_ref[0, rows(i), cols(i)], jnp.where(lane_lo == (e == 0), qq, jnp.zeros_like(qq)))

        def softplus(i):
            p, e, c, b = live[i]
            z = st[i]["z"]
            sp = jnp.maximum(z, jnp.log2(1.0 + jnp.exp2(jnp.minimum(z, 126.0))))
            if isinstance(backs[b], int) and backs[b] == 0:
                sp = jnp.where(krow < qcol, sp, 0.0)
            hi = sp.astype(BF16)
            lo = (sp - hi.astype(F32)).astype(BF16)
            st[i]["hilo"] = jnp.concatenate([hi, lo], axis=0)

        def suffix(i):
            st[i]["suf"] = _dot(from_here, st[i].pop("hilo"))

        def weights(i):
            p, e, c, b = live[i]
            lanes = slice(c * t, (c + 1) * t)
            suf = st[i].pop("suf")
            a = jnp.exp2(st[i].pop("z") - suf)
            if isinstance(backs[b], int) and backs[b] == 0:
                a = jnp.where(krow < qcol, a, 0.0)
            st[i]["a"] = a.astype(BF16)
            r_prev = r_sc[2 * p + e, :, lanes]
            w = jnp.exp2(-r_prev)
            tot = suf[0:1, :]
            if not (isinstance(backs[b], int) and backs[b] <= c):
                exists = (block(i) >= 0).astype(F32)
                w, tot = w * exists, tot * exists
            st[i]["w"] = w
            r_sc[2 * p + e, :, lanes] = r_prev + tot

        def values(i):
            p, e, c, b = live[i]
            lanes = slice(c * t, (c + 1) * t)
            acc_sc[2 * p + e, :, lanes] = (acc_sc[2 * p + e, :, lanes]
                                           + st[i].pop("w") * _dot_tl(v_ref[0, rows(i), cols(i)], st[i].pop("a")))

        _pipelined(len(live), [scores, softplus, suffix, weights, values])

    step([0, 1], range(pairs))
    first_head = lax.broadcasted_iota(jnp.int32, (LANES, 1), 0) < HEAD
    for p in range(pairs):
        def still_live(p=p):
            return jnp.max(jnp.exp2(-r_sc[2 * p:2 * p + 2])) > 0.0

        def further(carry, p=p, still_live=still_live):
            step([carry[0]], [p])
            return carry[0] + 1, still_live()

        lax.while_loop(lambda carry: (carry[0] < first + nchunk) & carry[1], further, (2, still_live()))
        o_ref[0, :, p * LANES:(p + 1) * LANES] = jnp.where(
            first_head, acc_sc[2 * p], acc_sc[2 * p + 1]).T.astype(BF16)


def _attn_c(cq, ck, cv, tq, t, pairs):
    B, S, C = cq.shape
    w = pairs * LANES
    return pl.pallas_call(
        functools.partial(_attn_c_kernel, tq=tq, t=t, pairs=pairs),
        grid=(B, C // w, S // tq),
        in_specs=[pl.BlockSpec((1, tq, w), lambda b, p, i: (b, i, p)),
                  pl.BlockSpec((1, S, w), lambda b, p, i: (b, 0, p)),
                  pl.BlockSpec((1, S, w), lambda b, p, i: (b, 0, p))],
        out_specs=pl.BlockSpec((1, tq, w), lambda b, p, i: (b, i, p)),
        out_shape=jax.ShapeDtypeStruct((B, S, C), BF16),
        scratch_shapes=[pltpu.VMEM((2 * pairs, 1, tq), F32), pltpu.VMEM((2 * pairs, LANES, tq), F32)],
        compiler_params=pltpu.CompilerParams(dimension_semantics=("arbitrary",) * 3, vmem_limit_bytes=VMEM_LIMIT),
        name="attn_c",
    )(cq, ck, cv)


def _attn_d_kernel(slope_ref, reach_ref, q_ref, k_ref, v_ref, lam_ref, subln_ref, o_ref, m_sc, acc_sc,
                   *, tq, t, lambda_init):
    h = pl.program_id(1)
    qi = pl.program_id(2)
    nchunk = tq // t
    chains = [(e, c) for c in range(nchunk) for e in range(2)]
    slope = slope_ref[h] * LOG2E
    lane_lo = _lane_lo((1, LANES))
    key_bias = slope * lax.broadcasted_iota(jnp.int32, (t, LANES), 0).astype(F32)
    m_sc[...] = jnp.full(m_sc.shape, NEG, F32)
    acc_sc[...] = jnp.zeros(acc_sc.shape, F32)

    def step(blocks):
        rows = [pl.ds(pl.multiple_of(j * t, t), t) for j, _ in blocks]
        live = [(e, c, g) for g, (_, d) in enumerate(blocks) for (e, c) in chains if d is None or c >= d]
        st = [dict() for _ in live]
        vals = {}

        def values_of(g):
            if g not in vals:
                vals[g] = jnp.concatenate([v_ref[0, rows[g], :].T, jnp.ones((DEN_ROWS, t), BF16)], axis=0)
            return vals[g]

        def scores(i):
            e, c, g = live[i]
            qq = q_ref[0, c * t:(c + 1) * t, :]
            st[i]["s"] = _dot_t(k_ref[0, rows[g], :], jnp.where(lane_lo == (e == 0), qq, jnp.zeros_like(qq)))

        def softmax(i):
            e, c, g = live[i]
            s = st[i].pop("s")
            j, d = blocks[g]
            if c == d:
                krow = lax.broadcasted_iota(jnp.int32, (t, t), 0)
                qcol = lax.broadcasted_iota(jnp.int32, (t, t), 1)
                s = s + slope * jnp.minimum(krow, 2 * qcol - krow).astype(F32)
                s = jnp.where(_chunk_causal(t), s, NEG)
                off = 0.0
            else:
                s = jnp.concatenate([s[:, b * LANES:(b + 1) * LANES] + key_bias for b in range(t // LANES)], axis=1)
                off = (slope * ((qi * nchunk + c - j) * t).astype(F32) if d is None
                       else slope * float((c - d) * t))
            st[i]["p"], st[i]["alpha"] = _softmax_stage(s, off, slice(c * t, (c + 1) * t), e, m_sc)

        def values(i):
            e, c, g = live[i]
            lanes = slice(c * t, (c + 1) * t)
            acc_sc[e, :, lanes] = (st[i]["alpha"] * acc_sc[e, :, lanes]
                                   + _dot(values_of(g), st[i].pop("p")))

        _pipelined(len(live), [scores, softmax, values])

    _walk_key_blocks(qi, nchunk, step, group=2, max_blocks=reach_ref[h])
    lf = lam_ref[...]
    lam = (jnp.exp(jnp.sum(lf[0:1] * lf[1:2], axis=-1, keepdims=True))
           - jnp.exp(jnp.sum(lf[2:3] * lf[3:4], axis=-1, keepdims=True)) + lambda_init)
    sm = [acc_sc[e, 0:LANES, :] / acc_sc[e, LANES:LANES + 1, :] for e in range(2)]
    o = (sm[0] - lam * sm[1]).T
    o = _rms(o, LANES) * subln_ref[...] * (1.0 - lambda_init)
    o_ref[0] = o.astype(BF16)


def _attn_d(slopes, reach, dq, dk, dv, lam, subln, lambda_init, tq, t):
    B, S, _ = dq.shape
    specs = _flash_specs(S, tq, LANES, LANES, LANES)
    whole = lambda a: pl.BlockSpec(a.shape, lambda b, h, i: (0, 0))
    smem = pl.BlockSpec(memory_space=pltpu.SMEM)
    specs["in_specs"] = [smem, smem] + specs["in_specs"] + [whole(lam), whole(subln)]
    return pl.pallas_call(
        functools.partial(_attn_d_kernel, tq=tq, t=t, lambda_init=lambda_init),
        grid=(B, D_HEADS, S // tq),
        out_shape=jax.ShapeDtypeStruct((B, S, MIX_HALF), BF16),
        scratch_shapes=[pltpu.VMEM((2, 1, tq), F32), pltpu.VMEM((2, LANES + DEN_ROWS, tq), F32)],
        name="attn_d",
        **specs,
    )(slopes, reach, dq, dk, dv, lam, subln)


def _out_mlp_kernel(x_ref, ma_ref, mb_ref, wo_ref, g_ref, wu_ref, wd_ref, o_ref, *, tf):
    half = ma_ref.shape[-1]
    x1 = x_ref[...] + _dot(ma_ref[...], wo_ref[0:half, :]) + _dot(mb_ref[...], wo_ref[half:2 * half, :])
    h = (_rms(x1, x1.shape[-1]) * g_ref[...]).astype(BF16)
    o_ref[...] = x1
    for f in range(wu_ref.shape[-1] // tf):
        u = jnp.maximum(_dot(h, wu_ref[:, f * tf:(f + 1) * tf]), 0.0)
        o_ref[...] += _dot((u * u).astype(BF16), wd_ref[f * tf:(f + 1) * tf, :])


def _out_mlp(x, ma, mb, wo, g, wu, wd, layer, tm, tf):
    T, D = x.shape
    full = lambda a: pl.BlockSpec(a.shape, lambda i: (0,) * a.ndim, pipeline_mode=pl.Buffered(1))
    of_layer = lambda a, l: pl.BlockSpec((None,) + a.shape[1:], lambda i: (l,) + (0,) * (a.ndim - 1),
                                         pipeline_mode=pl.Buffered(1))
    row = lambda c: pl.BlockSpec((tm, c), lambda i: (i, 0))
    return pl.pallas_call(
        functools.partial(_out_mlp_kernel, tf=tf),
        grid=(T // tm,),
        in_specs=[row(D), row(ma.shape[-1]), row(mb.shape[-1]), of_layer(wo, layer // 2), full(g),
                  of_layer(wu, layer), of_layer(wd, layer)],
        out_specs=row(D),
        out_shape=jax.ShapeDtypeStruct((T, D), F32),
        compiler_params=pltpu.CompilerParams(dimension_semantics=("arbitrary",), vmem_limit_bytes=VMEM_LIMIT),
        name="out_mlp",
    )(x, ma, mb, wo, g, wu, wd)


def _alibi_slopes(n):
    return 2.0 ** (-8.0 * jnp.arange(1, n + 1, dtype=F32) / n)


def _rope_placement():
    f = jnp.arange(B_ROPE_HALF)
    p = jnp.zeros((2 * B_ROPE_HALF, 3 * LANES), F32)
    p = p.at[f, HEAD + f].set(1.0).at[f, HEAD + B_ROPE_HALF + f].set(1.0)
    p = p.at[B_ROPE_HALF + f, LANES + HEAD + f].set(-1.0)
    p = p.at[B_ROPE_HALF + f, 2 * LANES + HEAD + B_ROPE_HALF + f].set(1.0)
    return jnp.concatenate([p, p], axis=0).astype(BF16)


def _alibi_reach(slopes, q_gain, k_gain, t, S):
    smax = 1.01 * HEAD * 0.125 * LOG2E * jnp.max(jnp.abs(q_gain)) * jnp.max(jnp.abs(k_gain))
    dist = (160.0 + 2.0 * smax) / (slopes * LOG2E)
    blocks = jnp.floor((dist - 1.0) / t) + 1.0
    return jnp.clip(blocks, 0.0, float(S // t)).astype(jnp.int32)


def _tile(n, want):
    t = min(n, want)
    assert n % t == 0, (n, t)
    return t


def kernel(x, positions, norm_mix_g, norm_ffn_g, mlp_w_up, mlp_w_down, ev_w_in, ev_w_out, a_q_norm, a_k_norm, a_sinks, b_cq_norm, b_ckv_norm, b_w_uq, b_w_ukv, b_q_norm, b_k_norm, od_w_in, od_w_out, d_q_norm, d_k_norm, d_lambda, d_subln):
    B, S, D = x.shape
    T = B * S
    depth = norm_mix_g.shape[0]
    tm_proj = _tile(T, 1024)
    tm_mlp = _tile(T, 1024)
    t_a = _tile(S, 256)
    tq_b, t_b = _tile(S, 2048), _tile(S, 512)
    tq_c, t_c = _tile(S, 1024), _tile(S, 256)
    tq_d, t_d = _tile(S, 2048), _tile(S, 512)
    tf = 512

    xf = x.reshape(T, D)
    pos = positions.reshape(T)
    row2 = lambda a: a.reshape(1, -1).astype(F32)
    pair = lambda a: jnp.concatenate([a, a]).reshape(1, LANES).astype(F32)
    pad_qk = lambda a: jnp.pad(a.astype(F32), (0, LANES - B_QK)).reshape(1, LANES)
    inv = (ROPE_THETA ** (-jnp.arange(B_ROPE_HALF, dtype=F32) / B_ROPE_HALF)).reshape(B_ROPE_HALF, 1)
    rope_place = _rope_placement()
    slopes_a = _alibi_slopes(A_HEADS)
    slopes_d = _alibi_slopes(D_HEADS)

    w_up, w_down = mlp_w_up.astype(BF16), mlp_w_down.astype(BF16)
    ev_w_out_bf, od_w_out_bf = ev_w_out.astype(BF16), od_w_out.astype(BF16)
    for layer in range(depth):
        j = layer // 2
        g_mix = row2(norm_mix_g[layer])
        if layer % 2 == 0:
            w_in = ev_w_in[j]
            w_in = jnp.concatenate([w_in[:, :E_ROPE], jnp.zeros((D, HEAD), F32), w_in[:, E_ROPE:],
                                    jnp.zeros((D, LANES - B_QK), F32)], axis=1).astype(BF16)
            wuq = jnp.pad(b_w_uq[j].reshape(-1, B_HEADS, B_QK), ((0, 0), (0, 0), (0, LANES - B_QK)))
            wuq = wuq.reshape(-1, B_HEADS * LANES).astype(BF16)
            qa, ka, va, qb, kb, vb = _even_proj(
                xf, pos, g_mix, w_in, pair(a_q_norm[j]), pair(a_k_norm[j]), row2(b_cq_norm[j]),
                row2(b_ckv_norm[j]), wuq, b_w_ukv[j].astype(BF16), pad_qk(b_q_norm[j]), pad_qk(b_k_norm[j]),
                inv, rope_place, tm_proj)
            r3 = lambda a: a.reshape(B, S, a.shape[-1])
            ma = _attn_a(slopes_a, a_sinks[j].astype(F32), r3(qa), r3(ka), r3(va), t_a).reshape(T, -1)
            mb = _attn_b(r3(qb), r3(kb), r3(vb), tq_b, t_b).reshape(T, -1)
            w_out = ev_w_out_bf
        else:
            lambda_init = 0.8 - 0.6 * math.exp(-0.3 * layer)
            cq, ck, cv, dq, dk, dv = _odd_proj(
                xf, g_mix, od_w_in[j].astype(BF16), d_q_norm[j].reshape(1, LANES).astype(F32),
                d_k_norm[j].reshape(1, LANES).astype(F32), tm_proj)
            r3 = lambda a: a.reshape(B, S, a.shape[-1])
            ma = _attn_c(r3(cq), r3(ck), r3(cv), tq_c, t_c, pairs=4).reshape(T, -1)
            reach = _alibi_reach(slopes_d, d_q_norm[j], d_k_norm[j], t_d, S)
            mb = _attn_d(slopes_d, reach, r3(dq), r3(dk), r3(dv), d_lambda[j].astype(F32), row2(d_subln[j]),
                         lambda_init, tq_d, t_d).reshape(T, -1)
            w_out = od_w_out_bf
        xf = _out_mlp(xf, ma, mb, w_out, row2(norm_ffn_g[layer]), w_up, w_down, layer, tm_mlp, tf)
    return xf.reshape(B, S, D)
```

```python
import functools
import math

import jax
import jax.numpy as jnp
from jax import lax
from jax.experimental import pallas as pl
from jax.experimental.pallas import tpu as pltpu

F32 = jnp.float32
BF16 = jnp.bfloat16

EPS = 1e-6
CHUNK = 64
CHUNK_SHIFT = 6
LANES = 128
HEAD = 64
A_HEADS = 8
A_GROUP = 4
A_WINDOW = 128
A_WINDOW_CHUNKS = A_WINDOW // CHUNK
B_HEADS = 8
B_QK = 96
B_ROPE_HALF = 16
ROPE_THETA = 10000.0
C_HEADS = 8
D_HEADS = 4
B_Q_RANK = 256
B_KV_RANK = 128
MIX_HALF = 512
E_AK = A_HEADS * HEAD
E_AV = E_AK + (A_HEADS // A_GROUP) * HEAD
E_CQ = E_AV + (A_HEADS // A_GROUP) * HEAD
E_CKV = E_CQ + B_Q_RANK
E_ROPE = E_CKV + B_KV_RANK
E_END = E_ROPE + LANES
O_CK, O_CV, O_DQ, O_DK, O_DV, O_END = (i * MIX_HALF for i in range(1, 7))
NEG = -1e30
LOG2E = math.log2(math.e)
DEN_ROWS = 16
VMEM_LIMIT = 56 * 1024 * 1024


def _dot(a, b):
    return jnp.dot(a, b, preferred_element_type=F32)


def _dot_t(a, b):
    return lax.dot_general(a, b, (((1,), (1,)), ((), ())), preferred_element_type=F32)


def _dot_tl(a, b):
    return lax.dot_general(a, b, (((0,), (0,)), ((), ())), preferred_element_type=F32)


def _rms(x, denom):
    return x * lax.rsqrt(jnp.sum(x * x, axis=-1, keepdims=True) * (1.0 / denom) + EPS)


def _pair_rms(x, lo):
    xx = x * x
    s_lo = jnp.sum(jnp.where(lo, xx, 0.0), axis=-1, keepdims=True)
    s_hi = jnp.sum(jnp.where(lo, 0.0, xx), axis=-1, keepdims=True)
    r = jnp.where(lo, lax.rsqrt(s_lo * (1.0 / HEAD) + EPS), lax.rsqrt(s_hi * (1.0 / HEAD) + EPS))
    return x * r


def _lane_lo(shape):
    return lax.broadcasted_iota(jnp.int32, shape, len(shape) - 1) < HEAD


def _pipelined(n, stages):
    for step in range(n + len(stages) - 1):
        for si, stage in enumerate(stages):
            c = step - si
            if 0 <= c < n:
                stage(c)


def _even_proj_kernel(x_ref, pos_ref, g_ref, w_in_ref, aqg_ref, akg_ref, cqg_ref, ckvg_ref,
                      wuq_ref, wukv_ref, bqg_ref, bkg_ref, inv_ref, place_ref,
                      qa_ref, ka_ref, va_ref, qb_ref, kb_ref, vb_ref):
    x = x_ref[...]
    h = _rms(x, x.shape[-1]) * g_ref[...]
    proj = _dot(h.astype(BF16), w_in_ref[...])
    lo = _lane_lo((1, LANES))
    for p in range(A_HEADS // 2):
        seg = proj[:, p * LANES:(p + 1) * LANES]
        qa_ref[:, p * LANES:(p + 1) * LANES] = (_pair_rms(seg, lo) * aqg_ref[...] * (0.125 * LOG2E)).astype(BF16)
    ka_ref[...] = (_pair_rms(proj[:, E_AK:E_AV], lo) * akg_ref[...]).astype(BF16)
    va_ref[...] = proj[:, E_AV:E_CQ].astype(BF16)
    cq = _rms(proj[:, E_CQ:E_CKV], B_Q_RANK) * cqg_ref[...]
    ckv = _rms(proj[:, E_CKV:E_ROPE], B_KV_RANK) * ckvg_ref[...]
    krope = proj[:, E_ROPE:E_END]
    qall = _dot(cq.astype(BF16), wuq_ref[...])
    kvall = _dot(ckv.astype(BF16), wukv_ref[...])
    ang = inv_ref[...] * pos_ref[0].astype(F32)
    trig = jnp.concatenate([jnp.cos(ang), jnp.sin(ang)], axis=0)
    trig_hi = trig.astype(BF16)
    trig_lo = (trig - trig_hi.astype(F32)).astype(BF16)
    placed = _dot_tl(jnp.concatenate([trig_hi, trig_lo], axis=0), place_ref[...])
    cosf = placed[:, 0:LANES] + jnp.where(lo, 1.0, 0.0)
    s_first = placed[:, LANES:2 * LANES]
    s_second = placed[:, 2 * LANES:3 * LANES]

    def rope(t):
        return (t * cosf + pltpu.roll(t, LANES - B_ROPE_HALF, 1) * s_first
                + pltpu.roll(t, B_ROPE_HALF, 1) * s_second)

    scale_b = B_QK ** -0.5 * LOG2E
    krope_rot = rope(krope * bkg_ref[...])
    krope_ss = jnp.sum(krope * krope, axis=-1, keepdims=True)
    sq = [dict() for _ in range(B_HEADS)]
    sk = [dict() for _ in range(B_HEADS)]

    def q_sumsq(hh):
        qh = qall[:, hh * LANES:(hh + 1) * LANES]
        sq[hh]["x"] = qh
        sq[hh]["ss"] = jnp.sum(qh * qh, axis=-1, keepdims=True)

    def q_norm(hh):
        sq[hh]["n"] = sq[hh].pop("x") * lax.rsqrt(sq[hh].pop("ss") * (1.0 / B_QK) + EPS) * bqg_ref[...]

    def q_store(hh):
        qb_ref[:, hh * LANES:(hh + 1) * LANES] = (rope(sq[hh].pop("n")) * scale_b).astype(BF16)

    def k_sumsq(hh):
        knope = jnp.where(lo, kvall[:, hh * LANES:(hh + 1) * LANES], 0.0)
        sk[hh]["x"] = knope
        sk[hh]["ss"] = jnp.sum(knope * knope, axis=-1, keepdims=True) + krope_ss

    def k_store(hh):
        kn = (sk[hh].pop("x") * bkg_ref[...] + krope_rot) * lax.rsqrt(sk[hh].pop("ss") * (1.0 / B_QK) + EPS)
        kb_ref[:, hh * LANES:(hh + 1) * LANES] = kn.astype(BF16)

    _pipelined(B_HEADS, [q_sumsq, k_sumsq, q_norm, k_store, q_store])
    for p in range(B_HEADS // 2):
        v_even = kvall[:, 2 * p * LANES:(2 * p + 1) * LANES]
        v_odd = kvall[:, (2 * p + 1) * LANES:(2 * p + 2) * LANES]
        vb_ref[:, p * LANES:(p + 1) * LANES] = jnp.where(lo, pltpu.roll(v_even, HEAD, 1), v_odd).astype(BF16)


def _even_proj(x, pos, g, w_in, aqg, akg, cqg, ckvg, wuq, wukv, bqg, bkg, inv, place, tm):
    T, D = x.shape
    full = lambda a: pl.BlockSpec(a.shape, lambda i: (0,) * a.ndim)
    row = lambda c: pl.BlockSpec((tm, c), lambda i: (i, 0))
    outs = (E_AK, E_AV - E_AK, E_CQ - E_AV, B_HEADS * LANES, B_HEADS * LANES, MIX_HALF)
    pos = pos.reshape(T // tm, 1, tm)
    return pl.pallas_call(
        _even_proj_kernel,
        grid=(T // tm,),
        in_specs=[row(D), pl.BlockSpec((1, 1, tm), lambda i: (i, 0, 0))]
        + [full(a) for a in (g, w_in, aqg, akg, cqg, ckvg, wuq, wukv, bqg, bkg, inv, place)],
        out_specs=[row(c) for c in outs],
        out_shape=[jax.ShapeDtypeStruct((T, c), BF16) for c in outs],
        compiler_params=pltpu.CompilerParams(dimension_semantics=("arbitrary",), vmem_limit_bytes=VMEM_LIMIT),
        name="even_proj",
    )(x, pos, g, w_in, aqg, akg, cqg, ckvg, wuq, wukv, bqg, bkg, inv, place)


def _odd_proj_kernel(x_ref, g_ref, w_in_ref, dqg_ref, dkg_ref,
                     cq_ref, ck_ref, cv_ref, dq_ref, dk_ref, dv_ref):
    x = x_ref[...]
    h = _rms(x, x.shape[-1]) * g_ref[...]
    proj = _dot(h.astype(BF16), w_in_ref[...])
    lo = _lane_lo((1, LANES))
    qscale = 0.125 * LOG2E
    cq_ref[...] = (proj[:, 0:O_CK] * qscale).astype(BF16)
    ck_ref[...] = proj[:, O_CK:O_CV].astype(BF16)
    cv_ref[...] = proj[:, O_CV:O_DQ].astype(BF16)
    for hh in range(D_HEADS):
        sl = slice(hh * LANES, (hh + 1) * LANES)
        dq_ref[:, sl] = (_pair_rms(proj[:, O_DQ + hh * LANES:O_DQ + (hh + 1) * LANES], lo)
                         * dqg_ref[...] * qscale).astype(BF16)
        dk_ref[:, sl] = (_pair_rms(proj[:, O_DK + hh * LANES:O_DK + (hh + 1) * LANES], lo)
                         * dkg_ref[...]).astype(BF16)
    dv_ref[...] = proj[:, O_DV:O_END].astype(BF16)


def _odd_proj(x, g, w_in, dqg, dkg, tm):
    T, D = x.shape
    full = lambda a: pl.BlockSpec(a.shape, lambda i: (0,) * a.ndim)
    row = lambda c: pl.BlockSpec((tm, c), lambda i: (i, 0))
    return pl.pallas_call(
        _odd_proj_kernel,
        grid=(T // tm,),
        in_specs=[row(D)] + [full(a) for a in (g, w_in, dqg, dkg)],
        out_specs=[row(MIX_HALF)] * 6,
        out_shape=[jax.ShapeDtypeStruct((T, MIX_HALF), BF16)] * 6,
        compiler_params=pltpu.CompilerParams(dimension_semantics=("arbitrary",), vmem_limit_bytes=VMEM_LIMIT),
        name="odd_proj",
    )(x, g, w_in, dqg, dkg)


def _attn_a_kernel(slope_ref, sink_ref, q_ref, k_ref, v_ref, o_ref, *, tq, win):
    qi = pl.program_id(1)
    q0 = qi * tq
    ks = pl.multiple_of(jnp.maximum(q0 - A_WINDOW, 0), A_WINDOW)
    lane_half = lax.broadcasted_iota(jnp.int32, (1, LANES), 1) // HEAD

    def both_halves(x, g):
        xf = x.astype(F32)
        return jnp.where(lane_half == g, xf, pltpu.roll(xf, HEAD, 1)).astype(BF16)

    groups = range(A_HEADS // A_GROUP)
    k = [both_halves(k_ref[0, pl.ds(ks, win), :], g) for g in groups]
    v = [both_halves(v_ref[0, pl.ds(ks, win), :], g) for g in groups]
    kpos = ks + lax.broadcasted_iota(jnp.int32, (win, tq), 0)
    qpos = q0 + lax.broadcasted_iota(jnp.int32, (win, tq), 1)
    dch = (qpos >> CHUNK_SHIFT) - (kpos >> CHUNK_SHIFT)
    allowed = (dch >= 0) & (dch <= A_WINDOW_CHUNKS)
    negdist = -jnp.abs(qpos - kpos).astype(F32)
    st = [dict() for _ in range(A_HEADS)]

    def scores(i):
        qq = q_ref[0, :, (i // 2) * LANES:(i // 2 + 1) * LANES]
        qh = jnp.where(lane_half == i % 2, qq, jnp.zeros_like(qq))
        st[i]["s"] = _dot_t(k[i // A_GROUP], qh)

    def softmax(i):
        s = jnp.where(allowed, st[i].pop("s") + (slope_ref[i] * LOG2E) * negdist, NEG)
        sink = sink_ref[i] * LOG2E
        m = jnp.maximum(jnp.max(s, axis=0, keepdims=True), sink)
        p = jnp.exp2(s - m)
        st[i]["den"] = jnp.sum(p, axis=0, keepdims=True) + jnp.exp2(sink - m)
        st[i]["p"] = p.astype(BF16)

    def values(i):
        o = _dot_tl(v[i // A_GROUP], st[i].pop("p")) / st[i].pop("den")
        st[i]["o"] = o[0:HEAD, :]

    _pipelined(A_HEADS, [scores, softmax, values])
    outs = [st[i]["o"] for i in range(A_HEADS)]
    o_ref[0] = jnp.concatenate(outs, axis=0).T.astype(BF16)


def _attn_a(slopes, sinks, qa, ka, va, tq):
    B, S, C = qa.shape
    win = tq + A_WINDOW
    smem = pl.BlockSpec(memory_space=pltpu.SMEM)
    return pl.pallas_call(
        functools.partial(_attn_a_kernel, tq=tq, win=win),
        grid=(B, S // tq),
        in_specs=[smem, smem,
                  pl.BlockSpec((1, tq, C), lambda b, i: (b, i, 0)),
                  pl.BlockSpec((1, S, LANES), lambda b, i: (b, 0, 0)),
                  pl.BlockSpec((1, S, LANES), lambda b, i: (b, 0, 0))],
        out_specs=pl.BlockSpec((1, tq, C), lambda b, i: (b, i, 0)),
        out_shape=jax.ShapeDtypeStruct((B, S, C), BF16),
        compiler_params=pltpu.CompilerParams(dimension_semantics=("arbitrary",) * 2, vmem_limit_bytes=VMEM_LIMIT),
        name="attn_a",
    )(slopes, sinks, qa, ka, va)


def _walk_key_blocks(qi, nchunk, step, group, max_blocks=None):
    first = qi * nchunk
    trips = first // group
    if max_blocks is None:
        def full(i, carry):
            step([(i * group + g, None) for g in range(group)])
            return carry

        lax.fori_loop(0, trips, full, 0)
        step([(first + d, d) for d in range(nchunk)])
    else:
        step([(first + d, d) for d in range(nchunk - 1, -1, -1)])

        def full(i, carry):
            step([(first - 1 - i * group - g, None) for g in range(group)])
            return carry

        lax.fori_loop(0, jnp.minimum(trips, (max_blocks + (group - 1)) // group), full, 0)


def _flash_specs(S, tq, q_width, k_width, v_width):
    return dict(
        in_specs=[pl.BlockSpec((1, tq, q_width), lambda b, p, i: (b, i, p)),
                  pl.BlockSpec((1, S, k_width), lambda b, p, i: (b, 0, p)),
                  pl.BlockSpec((1, S, v_width), lambda b, p, i: (b, 0, p))],
        out_specs=pl.BlockSpec((1, tq, LANES), lambda b, p, i: (b, i, p)),
        compiler_params=pltpu.CompilerParams(dimension_semantics=("arbitrary",) * 3, vmem_limit_bytes=VMEM_LIMIT),
    )


def _chunk_causal(t):
    krow = lax.broadcasted_iota(jnp.int32, (t, t), 0)
    qcol = lax.broadcasted_iota(jnp.int32, (t, t), 1)
    return (krow >> CHUNK_SHIFT) <= (qcol >> CHUNK_SHIFT)


def _softmax_stage(s, off, lanes, e, m_sc, narrow_exp=False):
    m_prev = m_sc[e, :, lanes]
    m_new = jnp.maximum(m_prev, jnp.max(s, axis=0, keepdims=True) - off)
    alpha = jnp.exp2(m_prev - m_new)
    arg = s - (m_new + off)
    pr = jnp.exp2(arg.astype(BF16)) if narrow_exp else jnp.exp2(arg).astype(BF16)
    m_sc[e, :, lanes] = m_new
    return pr, alpha


def _attn_b_kernel(q_ref, k_ref, v_ref, o_ref, m_sc, acc_sc, *, tq, t):
    qi = pl.program_id(2)
    nchunk = tq // t
    chains = [(e, c) for c in range(nchunk) for e in range(2)]
    lane = lax.broadcasted_iota(jnp.int32, (1, LANES), 1)
    ones_row = (HEAD, 0)
    m_sc[...] = jnp.full(m_sc.shape, NEG, F32)
    acc_sc[...] = jnp.zeros(acc_sc.shape, F32)

    def step(blocks):
        rows = [pl.ds(pl.multiple_of(j * t, t), t) for j, _ in blocks]
        live = [(e, c, g) for g, (_, d) in enumerate(blocks) for (e, c) in chains if d is None or c >= d]
        st = [dict() for _ in live]
        vals = {}

        def values_of(e, g):
            if (e, g) not in vals:
                v = v_ref[0, rows[g], :]
                own = (lane < HEAD) if e == 0 else (lane >= HEAD)
                vals[e, g] = jnp.where(own, v, jnp.where(lane == ones_row[e], 1.0, 0.0).astype(BF16))
            return vals[e, g]

        def scores(i):
            e, c, g = live[i]
            st[i]["s"] = _dot_t(k_ref[0, rows[g], e * LANES:(e + 1) * LANES],
                                q_ref[0, c * t:(c + 1) * t, e * LANES:(e + 1) * LANES])

        def softmax(i):
            e, c, g = live[i]
            s = st[i].pop("s")
            if c == blocks[g][1]:
                s = jnp.where(_chunk_causal(t), s, NEG)
            st[i]["p"], st[i]["alpha"] = _softmax_stage(s, 0.0, slice(c * t, (c + 1) * t), e, m_sc)

        def values(i):
            e, c, g = live[i]
            lanes = slice(c * t, (c + 1) * t)
            acc_sc[e, :, lanes] = (st[i]["alpha"] * acc_sc[e, :, lanes]
                                   + _dot_tl(values_of(e, g), st[i].pop("p")))

        _pipelined(len(live), [scores, softmax, values])

    _walk_key_blocks(qi, nchunk, step, group=nchunk)
    first = lax.broadcasted_iota(jnp.int32, (LANES, 1), 0) < HEAD
    den = [acc_sc[e, ones_row[e]:ones_row[e] + 1, :] for e in range(2)]
    o_t = jnp.where(first, acc_sc[0] / den[0], acc_sc[1] / den[1])
    o_ref[0] = o_t.T.astype(BF16)


def _attn_b(qb, kb, vb, tq, t):
    B, S, _ = qb.shape
    return pl.pallas_call(
        functools.partial(_attn_b_kernel, tq=tq, t=t),
        grid=(B, B_HEADS // 2, S // tq),
        out_shape=jax.ShapeDtypeStruct((B, S, MIX_HALF), BF16),
        scratch_shapes=[pltpu.VMEM((2, 1, tq), F32), pltpu.VMEM((2, LANES, tq), F32)],
        name="attn_b",
        **_flash_specs(S, tq, 2 * LANES, 2 * LANES, LANES),
    )(qb, kb, vb)


def _attn_c_kernel(q_ref, k_ref, v_ref, o_ref, r_sc, acc_sc, *, tq, t, pairs):
    qi = pl.program_id(2)
    nchunk = tq // t
    first = qi * nchunk
    krow = lax.broadcasted_iota(jnp.int32, (t, t), 0)
    qcol = lax.broadcasted_iota(jnp.int32, (t, t), 1)
    from_here = jnp.where(qcol >= krow, 1.0, 0.0).astype(BF16)
    from_here = jnp.concatenate([from_here, from_here], axis=1)
    lane_lo = _lane_lo((1, LANES))
    r_sc[...] = jnp.zeros(r_sc.shape, F32)
    acc_sc[...] = jnp.zeros(acc_sc.shape, F32)

    def step(backs, which):
        live = [(p, e, c, b) for b in range(len(backs)) for p in which for c in range(nchunk) for e in range(2)]
        st = [dict() for _ in live]

        def block(i):
            p, e, c, b = live[i]
            return first + c - backs[b]

        def rows(i):
            return pl.ds(pl.multiple_of(jnp.maximum(block(i), 0) * t, t), t)

        def cols(i):
            return slice(live[i][0] * LANES, (live[i][0] + 1) * LANES)

        def scores(i):
            p, e, c, b = live[i]
            qq = q_ref[0, c * t:(c + 1) * t, cols(i)]
            st[i]["z"] = _dot_t(k_ref[0, rows(i), cols(i)], jnp.where(lane_lo == (e == 0), qq, jnp.zeros_like(qq)))

        def softplus(i):
            p, e, c, b = live[i]
            z = st[i]["z"]
            sp = jnp.maximum(z, jnp.log2(1.0 + jnp.exp2(jnp.minimum(z, 126.0))))
            if isinstance(backs[b], int) and backs[b] == 0:
                sp = jnp.where(krow < qcol, sp, 0.0)
            hi = sp.astype(BF16)
            lo = (sp - hi.astype(F32)).astype(BF16)
            st[i]["hilo"] = jnp.concatenate([hi, lo], axis=0)

        def suffix(i):
            st[i]["suf"] = _dot(from_here, st[i].pop("hilo"))

        def weights(i):
            p, e, c, b = live[i]
            lanes = slice(c * t, (c + 1) * t)
            suf = st[i].pop("suf")
            a = jnp.exp2(st[i].pop("z") - suf)
            if isinstance(backs[b], int) and backs[b] == 0:
                a = jnp.where(krow < qcol, a, 0.0)
            st[i]["a"] = a.astype(BF16)
            r_prev = r_sc[2 * p + e, :, lanes]
            w = jnp.exp2(-r_prev)
            tot = suf[0:1, :]
            if not (isinstance(backs[b], int) and backs[b] <= c):
                exists = (block(i) >= 0).astype(F32)
                w, tot = w * exists, tot * exists
            st[i]["w"] = w
            r_sc[2 * p + e, :, lanes] = r_prev + tot

        def values(i):
            p, e, c, b = live[i]
            lanes = slice(c * t, (c + 1) * t)
            acc_sc[2 * p + e, :, lanes] = (acc_sc[2 * p + e, :, lanes]
                                           + st[i].pop("w") * _dot_tl(v_ref[0, rows(i), cols(i)], st[i].pop("a")))

        _pipelined(len(live), [scores, softplus, suffix, weights, values])

    step([0, 1], range(pairs))
    first_head = lax.broadcasted_iota(jnp.int32, (LANES, 1), 0) < HEAD
    for p in range(pairs):
        def still_live(p=p):
            return jnp.max(jnp.exp2(-r_sc[2 * p:2 * p + 2])) > 0.0

        def further(carry, p=p, still_live=still_live):
            step([carry[0]], [p])
            return carry[0] + 1, still_live()

        lax.while_loop(lambda carry: (carry[0] < first + nchunk) & carry[1], further, (2, still_live()))
        o_ref[0, :, p * LANES:(p + 1) * LANES] = jnp.where(
            first_head, acc_sc[2 * p], acc_sc[2 * p + 1]).T.astype(BF16)


def _attn_c(cq, ck, cv, tq, t, pairs):
    B, S, C = cq.shape
    w = pairs * LANES
    return pl.pallas_call(
        functools.partial(_attn_c_kernel, tq=tq, t=t, pairs=pairs),
        grid=(B, C // w, S // tq),
        in_specs=[pl.BlockSpec((1, tq, w), lambda b, p, i: (b, i, p)),
                  pl.BlockSpec((1, S, w), lambda b, p, i: (b, 0, p)),
                  pl.BlockSpec((1, S, w), lambda b, p, i: (b, 0, p))],
        out_specs=pl.BlockSpec((1, tq, w), lambda b, p, i: (b, i, p)),
        out_shape=jax.ShapeDtypeStruct((B, S, C), BF16),
        scratch_shapes=[pltpu.VMEM((2 * pairs, 1, tq), F32), pltpu.VMEM((2 * pairs, LANES, tq), F32)],
        compiler_params=pltpu.CompilerParams(dimension_semantics=("arbitrary",) * 3, vmem_limit_bytes=VMEM_LIMIT),
        name="attn_c",
    )(cq, ck, cv)


def _attn_d_kernel(slope_ref, reach_ref, q_ref, k_ref, v_ref, lam_ref, subln_ref, o_ref, m_sc, acc_sc,
                   *, tq, t, lambda_init):
    h = pl.program_id(1)
    qi = pl.program_id(2)
    nchunk = tq // t
    chains = [(e, c) for c in range(nchunk) for e in range(2)]
    slope = slope_ref[h] * LOG2E
    lane_lo = _lane_lo((1, LANES))
    key_bias = slope * lax.broadcasted_iota(jnp.int32, (t, LANES), 0).astype(F32)
    m_sc[...] = jnp.full(m_sc.shape, NEG, F32)
    acc_sc[...] = jnp.zeros(acc_sc.shape, F32)

    def step(blocks):
        rows = [pl.ds(pl.multiple_of(j * t, t), t) for j, _ in blocks]
        live = [(e, c, g) for g, (_, d) in enumerate(blocks) for (e, c) in chains if d is None or c >= d]
        st = [dict() for _ in live]
        vals = {}

        def values_of(g):
            if g not in vals:
                vals[g] = jnp.concatenate([v_ref[0, rows[g], :].T, jnp.ones((DEN_ROWS, t), BF16)], axis=0)
            return vals[g]

        def scores(i):
            e, c, g = live[i]
            qq = q_ref[0, c * t:(c + 1) * t, :]
            st[i]["s"] = _dot_t(k_ref[0, rows[g], :], jnp.where(lane_lo == (e == 0), qq, jnp.zeros_like(qq)))

        def softmax(i):
            e, c, g = live[i]
            s = st[i].pop("s")
            j, d = blocks[g]
            if c == d:
                krow = lax.broadcasted_iota(jnp.int32, (t, t), 0)
                qcol = lax.broadcasted_iota(jnp.int32, (t, t), 1)
                s = s + slope * jnp.minimum(krow, 2 * qcol - krow).astype(F32)
                s = jnp.where(_chunk_causal(t), s, NEG)
                off = 0.0
            else:
                s = jnp.concatenate([s[:, b * LANES:(b + 1) * LANES] + key_bias for b in range(t // LANES)], axis=1)
                off = (slope * ((qi * nchunk + c - j) * t).astype(F32) if d is None
                       else slope * float((c - d) * t))
            st[i]["p"], st[i]["alpha"] = _softmax_stage(s, off, slice(c * t, (c + 1) * t), e, m_sc,
                                                        narrow_exp=True)

        def values(i):
            e, c, g = live[i]
            lanes = slice(c * t, (c + 1) * t)
            acc_sc[e, :, lanes] = (st[i]["alpha"] * acc_sc[e, :, lanes]
                                   + _dot(values_of(g), st[i].pop("p")))

        _pipelined(len(live), [scores, softmax, values])

    _walk_key_blocks(qi, nchunk, step, group=2, max_blocks=reach_ref[h])
    lf = lam_ref[...]
    lam = (jnp.exp(jnp.sum(lf[0:1] * lf[1:2], axis=-1, keepdims=True))
           - jnp.exp(jnp.sum(lf[2:3] * lf[3:4], axis=-1, keepdims=True)) + lambda_init)
    sm = [acc_sc[e, 0:LANES, :] / acc_sc[e, LANES:LANES + 1, :] for e in range(2)]
    o = (sm[0] - lam * sm[1]).T
    o = _rms(o, LANES) * subln_ref[...] * (1.0 - lambda_init)
    o_ref[0] = o.astype(BF16)


def _attn_d(slopes, reach, dq, dk, dv, lam, subln, lambda_init, tq, t):
    B, S, _ = dq.shape
    specs = _flash_specs(S, tq, LANES, LANES, LANES)
    whole = lambda a: pl.BlockSpec(a.shape, lambda b, h, i: (0, 0))
    smem = pl.BlockSpec(memory_space=pltpu.SMEM)
    specs["in_specs"] = [smem, smem] + specs["in_specs"] + [whole(lam), whole(subln)]
    return pl.pallas_call(
        functools.partial(_attn_d_kernel, tq=tq, t=t, lambda_init=lambda_init),
        grid=(B, D_HEADS, S // tq),
        out_shape=jax.ShapeDtypeStruct((B, S, MIX_HALF), BF16),
        scratch_shapes=[pltpu.VMEM((2, 1, tq), F32), pltpu.VMEM((2, LANES + DEN_ROWS, tq), F32)],
        name="attn_d",
        **specs,
    )(slopes, reach, dq, dk, dv, lam, subln)


def _out_mlp_kernel(x_ref, ma_ref, mb_ref, wo_ref, g_ref, wu_ref, wd_ref, o_ref, *, tf):
    half = ma_ref.shape[-1]
    x1 = x_ref[...] + _dot(ma_ref[...], wo_ref[0:half, :]) + _dot(mb_ref[...], wo_ref[half:2 * half, :])
    h = (_rms(x1, x1.shape[-1]) * g_ref[...]).astype(BF16)
    o_ref[...] = x1
    for f in range(wu_ref.shape[-1] // tf):
        u = jnp.maximum(_dot(h, wu_ref[:, f * tf:(f + 1) * tf]), 0.0)
        o_ref[...] += _dot((u * u).astype(BF16), wd_ref[f * tf:(f + 1) * tf, :])


def _out_mlp(x, ma, mb, wo, g, wu, wd, layer, tm, tf):
    T, D = x.shape
    full = lambda a: pl.BlockSpec(a.shape, lambda i: (0,) * a.ndim, pipeline_mode=pl.Buffered(1))
    of_layer = lambda a, l: pl.BlockSpec((None,) + a.shape[1:], lambda i: (l,) + (0,) * (a.ndim - 1),
                                         pipeline_mode=pl.Buffered(1))
    row = lambda c: pl.BlockSpec((tm, c), lambda i: (i, 0))
    return pl.pallas_call(
        functools.partial(_out_mlp_kernel, tf=tf),
        grid=(T // tm,),
        in_specs=[row(D), row(ma.shape[-1]), row(mb.shape[-1]), of_layer(wo, layer // 2), full(g),
                  of_layer(wu, layer), of_layer(wd, layer)],
        out_specs=row(D),
        out_shape=jax.ShapeDtypeStruct((T, D), F32),
        compiler_params=pltpu.CompilerParams(dimension_semantics=("arbitrary",), vmem_limit_bytes=VMEM_LIMIT),
        name="out_mlp",
    )(x, ma, mb, wo, g, wu, wd)


def _alibi_slopes(n):
    return 2.0 ** (-8.0 * jnp.arange(1, n + 1, dtype=F32) / n)


def _rope_placement():
    f = jnp.arange(B_ROPE_HALF)
    p = jnp.zeros((2 * B_ROPE_HALF, 3 * LANES), F32)
    p = p.at[f, HEAD + f].set(1.0).at[f, HEAD + B_ROPE_HALF + f].set(1.0)
    p = p.at[B_ROPE_HALF + f, LANES + HEAD + f].set(-1.0)
    p = p.at[B_ROPE_HALF + f, 2 * LANES + HEAD + B_ROPE_HALF + f].set(1.0)
    return jnp.concatenate([p, p], axis=0).astype(BF16)


def _alibi_reach(slopes, q_gain, k_gain, t, S):
    smax = 1.01 * HEAD * 0.125 * LOG2E * jnp.max(jnp.abs(q_gain)) * jnp.max(jnp.abs(k_gain))
    dist = (160.0 + 2.0 * smax) / (slopes * LOG2E)
    blocks = jnp.floor((dist - 1.0) / t) + 1.0
    return jnp.clip(blocks, 0.0, float(S // t)).astype(jnp.int32)


def _tile(n, want):
    t = min(n, want)
    assert n % t == 0, (n, t)
    return t


def kernel(x, positions, norm_mix_g, norm_ffn_g, mlp_w_up, mlp_w_down, ev_w_in, ev_w_out, a_q_norm, a_k_norm, a_sinks, b_cq_norm, b_ckv_norm, b_w_uq, b_w_ukv, b_q_norm, b_k_norm, od_w_in, od_w_out, d_q_norm, d_k_norm, d_lambda, d_subln):
    B, S, D = x.shape
    T = B * S
    depth = norm_mix_g.shape[0]
    tm_proj = _tile(T, 1024)
    tm_mlp = _tile(T, 1024)
    t_a = _tile(S, 256)
    tq_b, t_b = _tile(S, 2048), _tile(S, 512)
    tq_c, t_c = _tile(S, 1024), _tile(S, 256)
    tq_d, t_d = _tile(S, 2048), _tile(S, 512)
    tf = 512

    xf = x.reshape(T, D)
    pos = positions.reshape(T)
    row2 = lambda a: a.reshape(1, -1).astype(F32)
    pair = lambda a: jnp.concatenate([a, a]).reshape(1, LANES).astype(F32)
    pad_qk = lambda a: jnp.pad(a.astype(F32), (0, LANES - B_QK)).reshape(1, LANES)
    inv = (ROPE_THETA ** (-jnp.arange(B_ROPE_HALF, dtype=F32) / B_ROPE_HALF)).reshape(B_ROPE_HALF, 1)
    rope_place = _rope_placement()
    slopes_a = _alibi_slopes(A_HEADS)
    slopes_d = _alibi_slopes(D_HEADS)

    w_up, w_down = mlp_w_up.astype(BF16), mlp_w_down.astype(BF16)
    ev_w_out_bf, od_w_out_bf = ev_w_out.astype(BF16), od_w_out.astype(BF16)
    for layer in range(depth):
        j = layer // 2
        g_mix = row2(norm_mix_g[layer])
        if layer % 2 == 0:
            w_in = ev_w_in[j]
            w_in = jnp.concatenate([w_in[:, :E_ROPE], jnp.zeros((D, HEAD), F32), w_in[:, E_ROPE:],
                                    jnp.zeros((D, LANES - B_QK), F32)], axis=1).astype(BF16)
            wuq = jnp.pad(b_w_uq[j].reshape(-1, B_HEADS, B_QK), ((0, 0), (0, 0), (0, LANES - B_QK)))
            wuq = wuq.reshape(-1, B_HEADS * LANES).astype(BF16)
            qa, ka, va, qb, kb, vb = _even_proj(
                xf, pos, g_mix, w_in, pair(a_q_norm[j]), pair(a_k_norm[j]), row2(b_cq_norm[j]),
                row2(b_ckv_norm[j]), wuq, b_w_ukv[j].astype(BF16), pad_qk(b_q_norm[j]), pad_qk(b_k_norm[j]),
                inv, rope_place, tm_proj)
            r3 = lambda a: a.reshape(B, S, a.shape[-1])
            ma = _attn_a(slopes_a, a_sinks[j].astype(F32), r3(qa), r3(ka), r3(va), t_a).reshape(T, -1)
            mb = _attn_b(r3(qb), r3(kb), r3(vb), tq_b, t_b).reshape(T, -1)
            w_out = ev_w_out_bf
        else:
            lambda_init = 0.8 - 0.6 * math.exp(-0.3 * layer)
            cq, ck, cv, dq, dk, dv = _odd_proj(
                xf, g_mix, od_w_in[j].astype(BF16), d_q_norm[j].reshape(1, LANES).astype(F32),
                d_k_norm[j].reshape(1, LANES).astype(F32), tm_proj)
            r3 = lambda a: a.reshape(B, S, a.shape[-1])
            ma = _attn_c(r3(cq), r3(ck), r3(cv), tq_c, t_c, pairs=2).reshape(T, -1)
            reach = _alibi_reach(slopes_d, d_q_norm[j], d_k_norm[j], t_d, S)
            mb = _attn_d(slopes_d, reach, r3(dq), r3(dk), r3(dv), d_lambda[j].astype(F32), row2(d_subln[j]),
                         lambda_init, tq_d, t_d).reshape(T, -1)
            w_out = od_w_out_bf
        xf = _out_mlp(xf, ma, mb, w_out, row2(norm_ffn_g[layer]), w_up, w_down, layer, tm_mlp, tf)
    return xf.reshape(B, S, D)
```
